```python
import jax, jax.numpy as jnp
from jax import lax
import numpy as np

D_MODEL = 2048
BATCH = 2
SEQ = 16384
DEPTH = 1

GRID_W = 64
CTX_LEN = 256
D_MIX = D_MODEL
MLSTM_WIDTH = D_MIX // 2
N_MLSTM_HEADS = 4
DV_MLSTM = MLSTM_WIDTH // N_MLSTM_HEADS
DK_MLSTM = DV_MLSTM // 2
QK_COLS = N_MLSTM_HEADS * DK_MLSTM
N_GATE_COLS = 4 * N_MLSTM_HEADS
CONV_WIDTH = D_MIX - MLSTM_WIDTH
CONV_HALF = CONV_WIDTH // 2
CONV_K = 3
CHUNK = 128
GATE_SOFT_CAP = 15.0
FGATE_BIAS_OFFSET = 3.0
P_IN = 2 * QK_COLS + 2 * MLSTM_WIDTH + N_GATE_COLS + 3 * CONV_WIDTH
N_EXPERTS = 32
TOP_K = 4
D_FF_EXPERT = D_MODEL
SWIGLU_LIMIT = 7.0
SWIGLU_ALPHA = 1.702
EXPERT_BLOCK = 512
N_MOD = 6
EPS = 1e-6

kernel_name = "hymba_mlstm_shortconv_moe_dit"


def rmsnorm(x, g):
    xf = x.astype(jnp.float32)
    y = xf * lax.rsqrt(jnp.mean(xf * xf, axis=-1, keepdims=True) + EPS)
    return (y * g.astype(jnp.float32)).astype(x.dtype)


def modulate(x, g, shift, scale):
    return rmsnorm(x, g) * (1 + scale) + shift


def adaln(cond, w, b):
    return jnp.split(jax.nn.silu(cond) @ w + b, N_MOD, axis=-1)


def soft_cap(x):
    return GATE_SOFT_CAP * jnp.tanh(x / GATE_SOFT_CAP)


def project(xn, w_in_l, b_if_l):
    B, S, _ = xn.shape
    sizes = (QK_COLS, QK_COLS, MLSTM_WIDTH, MLSTM_WIDTH, N_GATE_COLS, CONV_WIDTH, CONV_WIDTH, CONV_WIDTH)
    points = np.cumsum(sizes)[:-1].tolist()
    q, k, v, o, g, cb, cc, cx = jnp.split(xn @ w_in_l, points, axis=-1)
    heads = lambda t, dh: t.reshape(B, S, N_MLSTM_HEADS, dh).transpose(0, 2, 1, 3).astype(jnp.float32)
    q = heads(q, DK_MLSTM) * (DK_MLSTM ** -0.5)
    k = heads(k, DK_MLSTM)
    v = heads(v, DV_MLSTM)
    gp = soft_cap((g.astype(jnp.float32) + b_if_l.astype(jnp.float32))
                  .reshape(B, S, 4, N_MLSTM_HEADS).transpose(2, 0, 3, 1))
    gates = (gp[0], jax.nn.log_sigmoid(gp[1]), gp[2], jax.nn.log_sigmoid(gp[3]))
    return q, k, v, o, gates, cb, cc, cx


def _chunk_stats(k, v, log_i, log_f):
    B, H, S, dk = k.shape
    nc = S // CHUNK
    kc = k.reshape(B, H, nc, CHUNK, dk)
    vc = v.reshape(B, H, nc, CHUNK, -1)
    ic = log_i.reshape(B, H, nc, CHUNK)
    b = jnp.cumsum(log_f.reshape(B, H, nc, CHUNK), axis=-1)
    b_last = b[..., -1]
    w_end = b_last[..., None] - b + ic
    m_loc = jnp.max(w_end, axis=-1)
    ke = kc * jnp.exp(w_end - m_loc[..., None])[..., None]
    C_loc = jnp.einsum('bhcld,bhcle->bhcde', ke, vc)
    n_loc = ke.sum(axis=-2)
    return kc, vc, ic, b, b_last, m_loc, C_loc, n_loc


def _scan_chunks(b_last, m_loc, C_loc, n_loc, state0):
    def step(carry, inp):
        C, n, m = carry
        bl, ml, Cl, nl = inp
        m_new = jnp.maximum(bl + m, ml)
        a = jnp.exp(bl + m - m_new)
        g = jnp.exp(ml - m_new)
        new = (a[..., None, None] * C + g[..., None, None] * Cl, a[..., None] * n + g[..., None] * nl, m_new)
        return new, (C, n, m)
    xs = tuple(jnp.moveaxis(t, 2, 0) for t in (b_last, m_loc, C_loc, n_loc))
    final, entries = lax.scan(step, state0, xs)
    return tuple(jnp.moveaxis(t, 0, 2) for t in entries), final


def mlstm_final_state(k, v, log_i, log_f, state0):
    _, _, _, _, bl, ml, Cl, nl = _chunk_stats(k, v, log_i, log_f)
    return _scan_chunks(bl, ml, Cl, nl, state0)[1]


def mlstm_direction(q, k, v, log_i, log_f, state0):
    kc, vc, ic, b, bl, ml, Cl, nl = _chunk_stats(k, v, log_i, log_f)
    (C0, n0, m0), final = _scan_chunks(bl, ml, Cl, nl, state0)
    B, H, S, _ = q.shape
    qc = q.reshape(kc.shape)
    a = b + m0[..., None]
    lower = jnp.tril(jnp.ones((CHUNK, CHUNK), dtype=bool))
    d = jnp.where(lower, b[..., :, None] - b[..., None, :] + ic[..., None, :], -jnp.inf)
    m = jnp.maximum(a, jnp.max(d, axis=-1))
    w_intra = jnp.exp(d - m[..., None])
    w_state = jnp.exp(a - m)
    s = jnp.einsum('bhcjd,bhcsd->bhcjs', qc, kc) * w_intra
    num = jnp.einsum('bhcjs,bhcse->bhcje', s, vc) + w_state[..., None] * jnp.einsum('bhcjd,bhcde->bhcje', qc, C0)
    den = s.sum(axis=-1) + w_state * jnp.einsum('bhcjd,bhcd->bhcj', qc, n0)
    h = num / jnp.maximum(jnp.abs(den), jnp.exp(-m))[..., None]
    return h.reshape(B, H, S, -1), final


def _flip(t):
    return jnp.flip(t, axis=2)


def zero_state(B):
    return (jnp.zeros((B, N_MLSTM_HEADS, DK_MLSTM, DV_MLSTM), jnp.float32),
            jnp.zeros((B, N_MLSTM_HEADS, DK_MLSTM), jnp.float32),
            jnp.zeros((B, N_MLSTM_HEADS), jnp.float32))


def mlstm_merge(h, o, g):
    B, H, S, dv = h.shape
    h = h.transpose(0, 2, 1, 3)
    h = h * lax.rsqrt(jnp.mean(h * h, axis=-1, keepdims=True) + EPS)
    h = h.reshape(B, S, H * dv) * g.astype(jnp.float32) * jax.nn.sigmoid(o.astype(jnp.float32))
    return h.astype(o.dtype)


def shift_conv3(u, w, axis):
    pad = [(0, 0)] * u.ndim
    pad[axis] = (1, 1)
    up = jnp.pad(u, pad)
    n = u.shape[axis]
    left = lax.slice_in_dim(up, 0, n, axis=axis)
    mid = lax.slice_in_dim(up, 1, n + 1, axis=axis)
    right = lax.slice_in_dim(up, 2, n + 2, axis=axis)
    return w[0] * left + w[1] * mid + w[2] * right


def conv_grid(u, w, rows):
    B, S, CW = u.shape
    gr = u.reshape(B, rows, GRID_W, CW)
    yh = shift_conv3(gr[..., :CONV_HALF], w[:, :CONV_HALF], axis=2)
    yv = shift_conv3(gr[..., CONV_HALF:], w[:, CONV_HALF:], axis=1)
    return jnp.concatenate([yh, yv], axis=-1).reshape(B, S, CW)


def mixer_latent(xn, rows, w_in_l, b_if_l, conv_w_l, norm_g_l, w_out_l, st_f, st_b):
    q, k, v, o, (i_f, f_f, i_b, f_b), cb, cc, cx = project(xn, w_in_l, b_if_l)
    h_f, _ = mlstm_direction(q, k, v, i_f, f_f, st_f)
    h_b, _ = mlstm_direction(_flip(q), _flip(k), _flip(v), _flip(i_b), _flip(f_b), st_b)
    h_m = mlstm_merge(h_f + _flip(h_b), o, norm_g_l)
    y_c = cb * conv_grid(cc * cx, conv_w_l, rows)
    return jnp.concatenate([h_m, y_c], axis=-1) @ w_out_l


def mixer_context(cn, w_in_l, b_if_l, conv_w_l, norm_g_l, w_out_l, need_outputs):
    q, k, v, o, (i_f, f_f, i_b, f_b), cb, cc, cx = project(cn, w_in_l, b_if_l)
    z = zero_state(cn.shape[0])
    if need_outputs:
        h_f, st_f = mlstm_direction(q, k, v, i_f, f_f, z)
        h_b, st_b = mlstm_direction(_flip(q), _flip(k), _flip(v), _flip(i_b), _flip(f_b), z)
        h_m = mlstm_merge(h_f + _flip(h_b), o, norm_g_l)
        y_c = cb * shift_conv3(cc * cx, conv_w_l, axis=1)
        return jnp.concatenate([h_m, y_c], axis=-1) @ w_out_l, st_f, st_b
    st_f = mlstm_final_state(k, v, i_f, f_f, z)
    st_b = mlstm_final_state(_flip(k), _flip(v), _flip(i_b), _flip(f_b), z)
    return None, st_f, st_b


def moe(h, w_router_l, b_router_l, w_gu_l, b_gu_l, w_down_l, b_down_l):
    T, D = h.shape
    logits = (h @ w_router_l + b_router_l).astype(jnp.float32)
    top_vals, top_idx = lax.top_k(logits, TOP_K)
    top_w = jax.nn.softmax(top_vals, axis=-1)
    e_flat = top_idx.reshape(-1)
    w_flat = top_w.reshape(-1)
    tok_flat = jnp.arange(T * TOP_K, dtype=jnp.int32) // TOP_K
    onehot = (e_flat[:, None] == jnp.arange(N_EXPERTS, dtype=jnp.int32)[None, :]).astype(jnp.int32)
    counts = onehot.sum(axis=0)
    rank = jnp.take_along_axis(jnp.cumsum(onehot, axis=0), e_flat[:, None], axis=1)[:, 0] - 1
    padded = (counts + EXPERT_BLOCK - 1) // EXPERT_BLOCK * EXPERT_BLOCK
    pend = jnp.cumsum(padded)
    pstart = pend - padded
    dest = pstart[e_flat] + rank
    n_blocks = -(-(T * TOP_K) // EXPERT_BLOCK) + N_EXPERTS
    n_rows = n_blocks * EXPERT_BLOCK
    row_tok = jnp.zeros((n_rows,), jnp.int32).at[dest].set(tok_flat)
    row_w = jnp.zeros((n_rows,), jnp.float32).at[dest].set(w_flat)
    blk_start = jnp.arange(n_blocks, dtype=jnp.int32) * EXPERT_BLOCK
    blk_expert = jnp.clip(jnp.searchsorted(pend, blk_start, side='right'), 0, N_EXPERTS - 1)

    def expert_block(args):
        tok, wt, e = args
        xb = h[tok]
        gu = xb @ w_gu_l[e] + b_gu_l[e]
        gate = jnp.minimum(gu[:, :D_FF_EXPERT], SWIGLU_LIMIT)
        up = jnp.clip(gu[:, D_FF_EXPERT:], -SWIGLU_LIMIT, SWIGLU_LIMIT)
        act = (up + 1) * gate * jax.nn.sigmoid(SWIGLU_ALPHA * gate)
        y = act @ w_down_l[e] + b_down_l[e]
        return y * wt[:, None].astype(y.dtype)

    ys = lax.map(expert_block, (row_tok.reshape(n_blocks, EXPERT_BLOCK),
                                row_w.reshape(n_blocks, EXPERT_BLOCK), blk_expert))
    return jax.ops.segment_sum(ys.reshape(n_rows, D), row_tok, num_segments=T)


def setup_inputs(seed: int = 0) -> dict:
    key = jax.random.key(seed)
    ks = jax.random.split(key, 20)
    nrm = lambda k, shape, s: jax.random.normal(k, shape, jnp.float32) * s
    gate_offset = jnp.tile(jnp.repeat(jnp.array([0.0, FGATE_BIAS_OFFSET], jnp.float32), N_MLSTM_HEADS), 2)
    return {
        "x": nrm(ks[0], (BATCH, SEQ, D_MODEL), 1.0),
        "c": nrm(ks[1], (BATCH, D_MODEL), 1.0),
        "ctx": nrm(ks[2], (BATCH, CTX_LEN, D_MODEL), 1.0),
        "c_ctx": nrm(ks[3], (D_MODEL,), 1.0),
        "w_ada": nrm(ks[4], (DEPTH, D_MODEL, N_MOD * D_MODEL), 0.5 * D_MODEL ** -0.5),
        "b_ada": nrm(ks[5], (DEPTH, N_MOD * D_MODEL), 0.02),
        "g_mix": 1.0 + nrm(ks[6], (DEPTH, D_MODEL), 0.1),
        "w_in": nrm(ks[7], (DEPTH, D_MODEL, P_IN), D_MODEL ** -0.5),
        "b_if": gate_offset + nrm(ks[8], (DEPTH, N_GATE_COLS), 0.5),
        "conv_w": nrm(ks[9], (DEPTH, CONV_K, CONV_WIDTH), 0.5),
        "mlstm_norm_g": 1.0 + nrm(ks[10], (DEPTH, MLSTM_WIDTH), 0.1),
        "w_out": nrm(ks[11], (DEPTH, D_MIX, D_MODEL), D_MIX ** -0.5),
        "g_ffn": 1.0 + nrm(ks[12], (DEPTH, D_MODEL), 0.1),
        "w_router": nrm(ks[13], (DEPTH, D_MODEL, N_EXPERTS), D_MODEL ** -0.5),
        "b_router": nrm(ks[14], (DEPTH, N_EXPERTS), 0.01),
        "w_gu": nrm(ks[15], (DEPTH, N_EXPERTS, D_MODEL, 2 * D_FF_EXPERT), D_MODEL ** -0.5),
        "b_gu": nrm(ks[16], (DEPTH, N_EXPERTS, 2 * D_FF_EXPERT), 0.01),
        "w_down": nrm(ks[17], (DEPTH, N_EXPERTS, D_FF_EXPERT, D_MODEL), D_FF_EXPERT ** -0.5),
        "b_down": nrm(ks[18], (DEPTH, N_EXPERTS, D_MODEL), 0.01),
        "g_final": 1.0 + nrm(ks[19], (D_MODEL,), 0.1),
    }


def reference(x, c, ctx, c_ctx, w_ada, b_ada, g_mix, w_in, b_if, conv_w, mlstm_norm_g, w_out,
              g_ffn, w_router, b_router, w_gu, b_gu, w_down, b_down, g_final):
    B, S, D = x.shape
    rows = S // GRID_W
    for l in range(DEPTH):
        last = l == DEPTH - 1
        sh_m, sc_m, gt_m, sh_f, sc_f, gt_f = [t[:, None, :] for t in adaln(c, w_ada[l], b_ada[l])]
        csh_m, csc_m, cgt_m, csh_f, csc_f, cgt_f = adaln(c_ctx, w_ada[l], b_ada[l])
        ctx_mix, st_f, st_b = mixer_context(modulate(ctx, g_mix[l], csh_m, csc_m), w_in[l], b_if[l],
                                            conv_w[l], mlstm_norm_g[l], w_out[l], need_outputs=not last)
        xn = modulate(x, g_mix[l], sh_m, sc_m)
        x = x + gt_m * mixer_latent(xn, rows, w_in[l], b_if[l], conv_w[l], mlstm_norm_g[l], w_out[l], st_f, st_b)
        xn = modulate(x, g_ffn[l], sh_f, sc_f)
        x = x + gt_f * moe(xn.reshape(-1, D), w_router[l], b_router[l], w_gu[l], b_gu[l],
                           w_down[l], b_down[l]).reshape(B, S, D)
        if not last:
            ctx = ctx + cgt_m * ctx_mix
            cn = modulate(ctx, g_ffn[l], csh_f, csc_f)
            ctx = ctx + cgt_f * moe(cn.reshape(-1, D), w_router[l], b_router[l], w_gu[l], b_gu[l],
                                    w_down[l], b_down[l]).reshape(ctx.shape)
    return rmsnorm(x, g_final)
```

```python
import functools

import jax
import jax.numpy as jnp
from jax import lax
from jax.experimental import pallas as pl
from jax.experimental.pallas import tpu as pltpu

F32 = jnp.float32
BF16 = jnp.bfloat16

N_HEADS = 4
DK = 128
DV = 256
QK_COLS = N_HEADS * DK
MLSTM_WIDTH = N_HEADS * DV
CONV_WIDTH = 1024
CONV_HALF = CONV_WIDTH // 2
N_GATE_COLS = 4 * N_HEADS
GRID_W = 64
CHUNK = 128
GATE_SOFT_CAP = 15.0
N_EXPERTS = 32
TOP_K = 4
SWIGLU_LIMIT = 7.0
SWIGLU_ALPHA = 1.702
N_MOD = 6
EPS = 1e-6
LANES = 128
SUBLANES = 8
ROW_BLOCK = 512
NEG_BIG = -1e30
VMEM_LIMIT = 56 * 1024 * 1024


def _cparams(sem):
    return pltpu.CompilerParams(dimension_semantics=sem, vmem_limit_bytes=VMEM_LIMIT)


def _adaln_kernel(c_ref, w_ref, b_ref, o_ref):
    s = c_ref[...]
    s = s * jax.nn.sigmoid(s)
    o_ref[...] = jnp.dot(s.astype(BF16), w_ref[...].astype(BF16),
                         preferred_element_type=F32) + b_ref[...]


def _adaln(cond, w, b):
    d, n = w.shape
    tn = 1024
    return pl.pallas_call(
        _adaln_kernel,
        grid=(n // tn,),
        in_specs=[pl.BlockSpec((8, d), lambda j: (0, 0)),
                  pl.BlockSpec((d, tn), lambda j: (0, j)),
                  pl.BlockSpec((1, tn), lambda j: (0, j))],
        out_specs=pl.BlockSpec((8, tn), lambda j: (0, j)),
        out_shape=jax.ShapeDtypeStruct((8, n), F32),
        compiler_params=_cparams(("arbitrary",)),
        name="adaln",
    )(cond, w, b)


def _inproj_kernel(x_ref, g_ref, sh_ref, sc_ref, w_ref, wg_ref, proj_ref, gate_ref, xn_scr):
    @pl.when(pl.program_id(2) == 0)
    def _():
        x = x_ref[...]
        y = x * lax.rsqrt(jnp.mean(x * x, axis=-1, keepdims=True) + EPS) * g_ref[...]
        xn = y * (1.0 + sc_ref[...]) + sh_ref[...]
        xn_scr[...] = xn.astype(BF16)
        gate_ref[...] = jnp.dot(xn_scr[...], wg_ref[...], preferred_element_type=F32)

    proj_ref[...] = jnp.dot(xn_scr[...], w_ref[...], preferred_element_type=F32).astype(BF16)


def _inproj(x, g, sh, sc, w, wg, tm):
    bsz, s, d = x.shape
    p = w.shape[1]
    tn = 1024
    nt = s // tm
    x2 = x.reshape(bsz * s, d)
    return pl.pallas_call(
        _inproj_kernel,
        grid=(bsz, nt, p // tn),
        in_specs=[pl.BlockSpec((tm, d), lambda b, i, j: (b * nt + i, 0)),
                  pl.BlockSpec((1, d), lambda b, i, j: (0, 0)),
                  pl.BlockSpec((None, 1, d), lambda b, i, j: (b, 0, 0)),
                  pl.BlockSpec((None, 1, d), lambda b, i, j: (b, 0, 0)),
                  pl.BlockSpec((d, tn), lambda b, i, j: (0, j)),
                  pl.BlockSpec((d, LANES), lambda b, i, j: (0, 0))],
        out_specs=[pl.BlockSpec((tm, tn), lambda b, i, j: (b * nt + i, j)),
                   pl.BlockSpec((tm, LANES), lambda b, i, j: (b * nt + i, 0))],
        out_shape=[jax.ShapeDtypeStruct((bsz * s, p), BF16),
                   jax.ShapeDtypeStruct((bsz * s, LANES), F32)],
        scratch_shapes=[pltpu.VMEM((tm, d), BF16)],
        compiler_params=_cparams(("arbitrary", "arbitrary", "arbitrary")),
        name="inproj",
    )(x2, g, sh, sc, w, wg)


def _log_sigmoid(x):
    return jnp.minimum(x, 0.0) - jnp.log1p(jnp.exp(-jnp.abs(x)))


def _gates_kernel(g_ref, b_ref, gc_ref, gr_ref):
    row = lax.broadcasted_iota(jnp.int32, (CHUNK, LANES), 0)
    lane = lax.broadcasted_iota(jnp.int32, (CHUNK, LANES), 1)
    gp = GATE_SOFT_CAP * jnp.tanh((g_ref[...] + b_ref[...]) / GATE_SOFT_CAP)
    is_f = ((lane >> 2) & 1) == 1
    lf = jnp.where(is_f, _log_sigmoid(gp), 0.0)
    lower = (row >= lane).astype(F32)
    upper = (row <= lane).astype(F32)
    cf = jnp.dot(lower, lf, precision=lax.Precision.HIGHEST, preferred_element_type=F32)
    cb = jnp.dot(upper, lf, precision=lax.Precision.HIGHEST, preferred_element_type=F32)
    cdir = jnp.where(lane < 2 * N_HEADS, cf, cb)
    gc = jnp.where(is_f, cdir, gp - pltpu.roll(cdir, LANES - N_HEADS, 1))
    gc_ref[...] = gc
    gr_ref[...] = gc.T[:N_GATE_COLS, :]


def _gates(gpre, b_if):
    t = gpre.shape[0]
    return pl.pallas_call(
        _gates_kernel,
        grid=(t // CHUNK,),
        in_specs=[pl.BlockSpec((CHUNK, LANES), lambda i: (i, 0)),
                  pl.BlockSpec((1, LANES), lambda i: (0, 0))],
        out_specs=[pl.BlockSpec((CHUNK, LANES), lambda i: (i, 0)),
                   pl.BlockSpec((N_GATE_COLS, CHUNK), lambda i: (0, i))],
        out_shape=[jax.ShapeDtypeStruct((t, LANES), F32),
                   jax.ShapeDtypeStruct((N_GATE_COLS, t), F32)],
        compiler_params=_cparams(("arbitrary",)),
        name="gates",
    )(gpre, b_if)


def _mlstm_chunk(q, k, v, b_col, r_col, r_row, b_last, mask, c_st, n_st, m_st):
    scale = DK ** -0.5
    a_col = b_col + m_st
    d = jnp.where(mask, b_col + r_row, -jnp.inf)
    m_t = jnp.maximum(a_col, jnp.max(d, axis=-1, keepdims=True))
    w_intra = jnp.exp(d - m_t)
    w_state = jnp.exp(a_col - m_t)
    qk = lax.dot_general(q, k, (((1,), (1,)), ((), ())), preferred_element_type=F32)
    s = qk * (w_intra * scale)
    qf = q.astype(F32)
    den = jnp.sum(s, axis=-1, keepdims=True) + w_state * (jnp.sum(qf * n_st, axis=-1, keepdims=True) * scale)
    lhs = jnp.concatenate([s.astype(BF16), (qf * (w_state * scale)).astype(BF16)], axis=1)
    rhs = jnp.concatenate([v, c_st.astype(BF16)], axis=0)
    num = jnp.dot(lhs, rhs, preferred_element_type=F32)
    h = num / jnp.maximum(jnp.abs(den), jnp.exp(-m_t))
    w_end = b_last + r_col
    m_loc = jnp.max(w_end, axis=0, keepdims=True)
    ke = k.astype(F32) * jnp.exp(w_end - m_loc)
    n_loc = jnp.sum(ke, axis=0, keepdims=True)
    c_loc = lax.dot_general(ke.astype(BF16), v, (((0,), (0,)), ((), ())), preferred_element_type=F32)
    m_new = jnp.maximum(b_last + m_st, m_loc)
    a_s = jnp.exp(b_last + m_st - m_new)
    g_s = jnp.exp(m_loc - m_new)
    return h, a_s * c_st + g_s * c_loc, a_s * n_st + g_s * n_loc, m_new


def _mlstm_kernel(qf_ref, kf_ref, vf_ref, gcf_ref, grf_ref, qb_ref, kb_ref, vb_ref, gcb_ref, grb_ref,
                  c0_ref, n0_ref, m0_ref, hf_ref, hb_ref, cout_ref, nout_ref, mout_ref,
                  c_scr, n_scr, m_scr):
    c = pl.program_id(1)

    @pl.when(c == 0)
    def _():
        c_scr[...] = c0_ref[...]
        n_scr[...] = n0_ref[...]
        m_scr[...] = m0_ref[...]

    row = lax.broadcasted_iota(jnp.int32, (CHUNK, CHUNK), 0)
    col = lax.broadcasted_iota(jnp.int32, (CHUNK, CHUNK), 1)
    dirs = ((qf_ref, kf_ref, vf_ref, gcf_ref, grf_ref, hf_ref, 0, CHUNK - 1, col <= row),
            (qb_ref, kb_ref, vb_ref, gcb_ref, grb_ref, hb_ref, 2 * N_HEADS, 0, col >= row))
    for di, (q_ref, k_ref, v_ref, gc_ref, gr_ref, h_ref, off, last, mask) in enumerate(dirs):
        for hd in range(N_HEADS):
            idx = di * N_HEADS + hd
            lr, lb = off + hd, off + N_HEADS + hd
            h, c_new, n_new, m_new = _mlstm_chunk(
                q_ref[:, hd * DK:(hd + 1) * DK], k_ref[:, hd * DK:(hd + 1) * DK],
                v_ref[:, hd * DV:(hd + 1) * DV],
                gc_ref[:, lb:lb + 1], gc_ref[:, lr:lr + 1], gr_ref[lr:lr + 1, :],
                gc_ref[last:last + 1, lb:lb + 1], mask,
                c_scr[idx], n_scr[idx], m_scr[idx][:, 0:1])
            h_ref[:, hd * DV:(hd + 1) * DV] = h
            c_scr[idx] = c_new
            n_scr[idx] = n_new
            m_scr[idx] = jnp.broadcast_to(m_new, (1, LANES))

    @pl.when(c == pl.num_programs(1) - 1)
    def _():
        cout_ref[...] = c_scr[...]
        nout_ref[...] = n_scr[...]
        mout_ref[...] = m_scr[...]


def _mlstm(proj, gcol, grow, bsz, s, c0, n0, m0):
    nc = s // CHUNK
    t = bsz * s
    fwd = lambda b, c: b * nc + c
    bwd = lambda b, c: b * nc + (nc - 1 - c)

    def specs(ci):
        return [pl.BlockSpec((CHUNK, QK_COLS), lambda b, c: (ci(b, c), 0)),
                pl.BlockSpec((CHUNK, QK_COLS), lambda b, c: (ci(b, c), 1)),
                pl.BlockSpec((CHUNK, MLSTM_WIDTH), lambda b, c: (ci(b, c), 1)),
                pl.BlockSpec((CHUNK, LANES), lambda b, c: (ci(b, c), 0)),
                pl.BlockSpec((N_GATE_COLS, CHUNK), lambda b, c: (0, ci(b, c)))]

    st_specs = [pl.BlockSpec((None, 2 * N_HEADS, DK, DV), lambda b, c: (b, 0, 0, 0)),
                pl.BlockSpec((None, 2 * N_HEADS, 1, DK), lambda b, c: (b, 0, 0, 0)),
                pl.BlockSpec((None, 2 * N_HEADS, 1, LANES), lambda b, c: (b, 0, 0, 0))]
    return pl.pallas_call(
        _mlstm_kernel,
        grid=(bsz, nc),
        in_specs=specs(fwd) + specs(bwd) + st_specs,
        out_specs=[pl.BlockSpec((CHUNK, MLSTM_WIDTH), lambda b, c: (fwd(b, c), 0)),
                   pl.BlockSpec((CHUNK, MLSTM_WIDTH), lambda b, c: (bwd(b, c), 0))] + st_specs,
        out_shape=[jax.ShapeDtypeStruct((t, MLSTM_WIDTH), F32),
                   jax.ShapeDtypeStruct((t, MLSTM_WIDTH), F32),
                   jax.ShapeDtypeStruct(c0.shape, F32),
                   jax.ShapeDtypeStruct(n0.shape, F32),
                   jax.ShapeDtypeStruct(m0.shape, F32)],
        scratch_shapes=[pltpu.VMEM((2 * N_HEADS, DK, DV), F32),
                        pltpu.VMEM((2 * N_HEADS, 1, DK), F32),
                        pltpu.VMEM((2 * N_HEADS, 1, LANES), F32)],
        compiler_params=_cparams(("arbitrary", "arbitrary")),
        name="mlstm",
    )(proj, proj, proj, gcol, grow, proj, proj, proj, gcol, grow, c0, n0, m0)


def _mixout_kernel(o_ref, cb_ref, cc_ref, cx_ref, ccp_ref, cxp_ref, ccn_ref, cxn_ref, hf_ref, hb_ref, x_ref,
                   cw_ref, ng_ref, wout_ref, gt_ref, gffn_ref, shf_ref, scf_ref, wr_ref, br_ref,
                   x1_ref, xn2_ref, idx_ref, tw_ref):
    i = pl.program_id(1)
    tm = x_ref.shape[0]
    cw = cw_ref[...]
    u = cc_ref[...].astype(F32) * cx_ref[...].astype(F32)

    uh = u[:, :CONV_HALF]
    pos = lax.broadcasted_iota(jnp.int32, (tm, CONV_HALF), 0) & (GRID_W - 1)
    left = jnp.where(pos == 0, 0.0, pltpu.roll(uh, 1, 0))
    right = jnp.where(pos == GRID_W - 1, 0.0, pltpu.roll(uh, tm - 1, 0))
    yh = cw[0:1, :CONV_HALF] * left + cw[1:2, :CONV_HALF] * uh + cw[2:3, :CONV_HALF] * right

    has_prev = jnp.where(i > 0, 1.0, 0.0)
    has_next = jnp.where(i < pl.num_programs(1) - 1, 1.0, 0.0)
    up = ccp_ref[...].astype(F32) * cxp_ref[...].astype(F32) * has_prev
    un = ccn_ref[...].astype(F32) * cxn_ref[...].astype(F32) * has_next
    ext = jnp.concatenate([up, u[:, CONV_HALF:], un], axis=0)
    yv = (cw[0:1, CONV_HALF:] * ext[0:tm] + cw[1:2, CONV_HALF:] * ext[GRID_W:GRID_W + tm]
          + cw[2:3, CONV_HALF:] * ext[2 * GRID_W:2 * GRID_W + tm])
    yc = cb_ref[...].astype(F32) * jnp.concatenate([yh, yv], axis=1)

    hs = hf_ref[...] + hb_ref[...]
    parts = []
    for hd in range(N_HEADS):
        seg = hs[:, hd * DV:(hd + 1) * DV]
        parts.append(seg * lax.rsqrt(jnp.mean(seg * seg, axis=-1, keepdims=True) + EPS))
    hm = jnp.concatenate(parts, axis=1) * ng_ref[...] * jax.nn.sigmoid(o_ref[...].astype(F32))

    z = jnp.concatenate([hm.astype(BF16), yc.astype(BF16)], axis=1)
    x1 = x_ref[...] + gt_ref[...] * jnp.dot(z, wout_ref[...], preferred_element_type=F32)
    x1_ref[...] = x1

    y = x1 * lax.rsqrt(jnp.mean(x1 * x1, axis=-1, keepdims=True) + EPS) * gffn_ref[...]
    xn2 = y * (1.0 + scf_ref[...]) + shf_ref[...]
    xn2_ref[...] = xn2

    logits = jnp.dot(xn2.astype(BF16), wr_ref[...], preferred_element_type=F32) + br_ref[...]
    lane = lax.broadcasted_iota(jnp.int32, (tm, LANES), 1)
    lane_f = lane.astype(F32)
    vals, idxs = [], []
    for _ in range(TOP_K):
        mx = jnp.max(logits, axis=-1, keepdims=True)
        ik = jnp.min(jnp.where(logits == mx, lane_f, float(LANES)), axis=-1, keepdims=True)
        vals.append(mx)
        idxs.append(ik)
        logits = jnp.where(lane_f == ik, -jnp.inf, logits)
    es = [jnp.exp(v - vals[0]) for v in vals]
    tot = es[0] + es[1] + es[2] + es[3]
    lane4 = lax.broadcasted_iota(jnp.int32, (tm, TOP_K), 1)
    idx_out = jnp.zeros((tm, TOP_K), F32)
    tw_out = jnp.zeros((tm, TOP_K), F32)
    for kk in range(TOP_K):
        idx_out = jnp.where(lane4 == kk, idxs[kk], idx_out)
        tw_out = jnp.where(lane4 == kk, es[kk] / tot, tw_out)
    idx_ref[...] = idx_out.astype(jnp.int32)
    tw_ref[...] = tw_out


def _mixout(proj, hf, hb, x2, conv_w, norm_g, w_out, gt, g_ffn, sh_f, sc_f, w_r, b_r, bsz, s, tm):
    t, d = x2.shape
    nt = s // tm
    rb = tm // GRID_W
    last_rb = t // GRID_W - 1
    row = lambda b, i: b * nt + i
    w = MLSTM_WIDTH
    vec = lambda n: pl.BlockSpec((1, n), lambda b, i: (0, 0))
    per_b = pl.BlockSpec((None, 1, d), lambda b, i: (b, 0, 0))
    halo_prev = lambda cblk: pl.BlockSpec(
        (GRID_W, CONV_HALF), lambda b, i: (jnp.maximum(row(b, i) * rb - 1, 0), cblk))
    halo_next = lambda cblk: pl.BlockSpec(
        (GRID_W, CONV_HALF), lambda b, i: (jnp.minimum((row(b, i) + 1) * rb, last_rb), cblk))
    return pl.pallas_call(
        _mixout_kernel,
        grid=(bsz, nt),
        in_specs=[pl.BlockSpec((tm, w), lambda b, i: (row(b, i), 2)),
                  pl.BlockSpec((tm, w), lambda b, i: (row(b, i), 3)),
                  pl.BlockSpec((tm, w), lambda b, i: (row(b, i), 4)),
                  pl.BlockSpec((tm, w), lambda b, i: (row(b, i), 5)),
                  halo_prev(9), halo_prev(11), halo_next(9), halo_next(11),
                  pl.BlockSpec((tm, w), lambda b, i: (row(b, i), 0)),
                  pl.BlockSpec((tm, w), lambda b, i: (row(b, i), 0)),
                  pl.BlockSpec((tm, d), lambda b, i: (row(b, i), 0)),
                  pl.BlockSpec((3, CONV_WIDTH), lambda b, i: (0, 0)),
                  vec(w),
                  pl.BlockSpec((d, d), lambda b, i: (0, 0)),
                  per_b, vec(d), per_b, per_b,
                  pl.BlockSpec((d, LANES), lambda b, i: (0, 0)),
                  vec(LANES)],
        out_specs=[pl.BlockSpec((tm, d), lambda b, i: (row(b, i), 0)),
                   pl.BlockSpec((tm, d), lambda b, i: (row(b, i), 0)),
                   pl.BlockSpec((tm, TOP_K), lambda b, i: (row(b, i), 0)),
                   pl.BlockSpec((tm, TOP_K), lambda b, i: (row(b, i), 0))],
        out_shape=[jax.ShapeDtypeStruct((t, d), F32),
                   jax.ShapeDtypeStruct((t, d), F32),
                   jax.ShapeDtypeStruct((t, TOP_K), jnp.int32),
                   jax.ShapeDtypeStruct((t, TOP_K), F32)],
        compiler_params=_cparams(("arbitrary", "arbitrary")),
        name="mixout",
    )(proj, proj, proj, proj, proj, proj, proj, proj, hf, hb, x2,
      conv_w, norm_g, w_out, gt, g_ffn, sh_f, sc_f, w_r, b_r)


def _rank_kernel(idx_ref, rank_ref, cnt_ref, run_scr):
    @pl.when(pl.program_id(0) == 0)
    def _():
        run_scr[...] = jnp.zeros_like(run_scr)

    tm = idx_ref.shape[0]
    idx = idx_ref[...]
    lane = lax.broadcasted_iota(jnp.int32, (tm, LANES), 1)
    hits = [lane == idx[:, kk:kk + 1] for kk in range(TOP_K)]
    onehot = jnp.zeros((tm, LANES), F32)
    for hit in hits:
        onehot = onehot + hit.astype(F32)
    r = lax.broadcasted_iota(jnp.int32, (tm, tm), 0)
    c = lax.broadcasted_iota(jnp.int32, (tm, tm), 1)
    before = jnp.dot((c < r).astype(BF16), onehot.astype(BF16), preferred_element_type=F32) + run_scr[...]
    lane4 = lax.broadcasted_iota(jnp.int32, (tm, TOP_K), 1)
    rank = jnp.zeros((tm, TOP_K), F32)
    for kk, hit in enumerate(hits):
        rk = jnp.sum(jnp.where(hit, before, 0.0), axis=-1, keepdims=True)
        rank = jnp.where(lane4 == kk, rk, rank)
    rank_ref[...] = rank.astype(jnp.int32)
    run_scr[...] = run_scr[...] + jnp.sum(onehot, axis=0, keepdims=True)
    cnt_ref[...] = run_scr[...]


def _rank(idx, tm):
    t = idx.shape[0]
    return pl.pallas_call(
        _rank_kernel,
        grid=(t // tm,),
        in_specs=[pl.BlockSpec((tm, TOP_K), lambda i: (i, 0))],
        out_specs=[pl.BlockSpec((tm, TOP_K), lambda i: (i, 0)),
                   pl.BlockSpec((1, LANES), lambda i: (0, 0))],
        out_shape=[jax.ShapeDtypeStruct((t, TOP_K), jnp.int32),
                   jax.ShapeDtypeStruct((1, LANES), F32)],
        scratch_shapes=[pltpu.VMEM((1, LANES), F32)],
        compiler_params=_cparams(("arbitrary",)),
        name="rank",
    )(idx)


def _dispatch_kernel(pad_ref, dest_hbm, xn_ref, xs_hbm, dsm, zeros_scr, sem_idx, sem_rows, sem_pad):
    i = pl.program_id(0)
    tm = xn_ref.shape[0]
    n_idx = tm * TOP_K
    idx_copy = pltpu.make_async_copy(dest_hbm.at[pl.ds(i * n_idx, n_idx)], dsm, sem_idx)
    idx_copy.start()

    def pad_copy(off, size):
        return pltpu.make_async_copy(zeros_scr.at[pl.ds(0, size), :], xs_hbm.at[pl.ds(off, size), :], sem_pad)

    def for_each_pad_piece(fn):
        def per_expert(e, carry):
            off = pad_ref[2 * e]
            n = pad_ref[2 * e + 1]
            head = n & (SUBLANES - 1)
            for r in range(SUBLANES - 1):
                @pl.when(r < head)
                def _(r=r):
                    fn(pad_copy(off + r, 1))

            off = off + head
            size = ROW_BLOCK // 2
            while size >= SUBLANES:
                take = (n & size) != 0

                @pl.when(take)
                def _(off=off, size=size):
                    fn(pad_copy(pl.multiple_of(off, SUBLANES), size))

                off = off + jnp.where(take, size, 0)
                size //= 2
            return carry
        lax.fori_loop(0, N_EXPERTS, per_expert, 0)

    @pl.when(i == 0)
    def _():
        zeros_scr[...] = jnp.zeros_like(zeros_scr)
        for_each_pad_piece(lambda cp: cp.start())
        for_each_pad_piece(lambda cp: cp.wait())

    idx_copy.wait()

    def row_copy(t, kk):
        return pltpu.make_async_copy(xn_ref.at[pl.ds(t, 1), :],
                                     xs_hbm.at[pl.ds(dsm[t * TOP_K + kk], 1), :], sem_rows)

    def issue(t, carry):
        for kk in range(TOP_K):
            row_copy(t, kk).start()
        return carry

    def drain(t, carry):
        for kk in range(TOP_K):
            row_copy(t, kk).wait()
        return carry

    lax.fori_loop(0, tm, issue, 0)
    lax.fori_loop(0, tm, drain, 0)


def _dispatch(pad_info, dest_flat, xn2, n_rows, tm):
    t, d = xn2.shape
    return pl.pallas_call(
        _dispatch_kernel,
        grid_spec=pltpu.PrefetchScalarGridSpec(
            num_scalar_prefetch=1,
            grid=(t // tm,),
            in_specs=[pl.BlockSpec(memory_space=pl.ANY),
                      pl.BlockSpec((tm, d), lambda i, pad: (i, 0))],
            out_specs=pl.BlockSpec(memory_space=pl.ANY),
            scratch_shapes=[pltpu.SMEM((tm * TOP_K,), jnp.int32),
                            pltpu.VMEM((ROW_BLOCK // 2, d), F32),
                            pltpu.SemaphoreType.DMA(()),
                            pltpu.SemaphoreType.DMA(()),
                            pltpu.SemaphoreType.DMA(())]),
        out_shape=jax.ShapeDtypeStruct((n_rows, d), F32),
        compiler_params=_cparams(("arbitrary",)),
        name="dispatch",
    )(pad_info, dest_flat, xn2)


def _new_expert(be_ref, j):
    return jnp.logical_or(j == 0, be_ref[j] != be_ref[jnp.maximum(j - 1, 0)])


def _expert_gu_kernel(be_ref, nu_ref, xs_ref, wg_ref, wu_ref, bg_ref, bu_ref, act_ref, wg_scr, wu_scr):
    j = pl.program_id(1)

    @pl.when(j < nu_ref[0])
    def _():
        @pl.when(_new_expert(be_ref, j))
        def _():
            wg_scr[...] = wg_ref[...].astype(BF16)
            wu_scr[...] = wu_ref[...].astype(BF16)

        x = xs_ref[...].astype(BF16)
        g = jnp.dot(x, wg_scr[...], preferred_element_type=F32) + bg_ref[...]
        u = jnp.dot(x, wu_scr[...], preferred_element_type=F32) + bu_ref[...]
        gate = jnp.minimum(g, SWIGLU_LIMIT)
        up = jnp.clip(u, -SWIGLU_LIMIT, SWIGLU_LIMIT)
        act_ref[...] = ((up + 1.0) * gate * jax.nn.sigmoid(SWIGLU_ALPHA * gate)).astype(BF16)


def _expert_gu(blk_expert, n_used, xs, w_gu, b_gu, tn):
    n_rows, d = xs.shape
    dff = w_gu.shape[2] // 2
    nt = dff // tn
    nb = n_rows // ROW_BLOCK
    blk = lambda j, nu: jnp.minimum(j, nu[0] - 1)
    exp = lambda j, be, nu: be[blk(j, nu)]
    return pl.pallas_call(
        _expert_gu_kernel,
        grid_spec=pltpu.PrefetchScalarGridSpec(
            num_scalar_prefetch=2,
            grid=(nt, nb),
            in_specs=[pl.BlockSpec((ROW_BLOCK, d), lambda n, j, be, nu: (blk(j, nu), 0)),
                      pl.BlockSpec((None, d, tn), lambda n, j, be, nu: (exp(j, be, nu), 0, n)),
                      pl.BlockSpec((None, d, tn), lambda n, j, be, nu: (exp(j, be, nu), 0, nt + n)),
                      pl.BlockSpec((None, 1, tn), lambda n, j, be, nu: (exp(j, be, nu), 0, n)),
                      pl.BlockSpec((None, 1, tn), lambda n, j, be, nu: (exp(j, be, nu), 0, nt + n))],
            out_specs=pl.BlockSpec((ROW_BLOCK, tn), lambda n, j, be, nu: (blk(j, nu), n)),
            scratch_shapes=[pltpu.VMEM((d, tn), BF16), pltpu.VMEM((d, tn), BF16)]),
        out_shape=jax.ShapeDtypeStruct((n_rows, dff), BF16),
        compiler_params=_cparams(("arbitrary", "arbitrary")),
        name="expert_gu",
    )(blk_expert, n_used, xs, w_gu, w_gu, b_gu, b_gu)


def _expert_down_kernel(be_ref, nu_ref, act_ref, w_ref, b_ref, y_ref, w_scr):
    j = pl.program_id(1)

    @pl.when(j < nu_ref[0])
    def _():
        @pl.when(_new_expert(be_ref, j))
        def _():
            w_scr[...] = w_ref[...].astype(BF16)

        y_ref[...] = jnp.dot(act_ref[...], w_scr[...], preferred_element_type=F32) + b_ref[...]


def _expert_down(blk_expert, n_used, act, w_down, b_down, tn):
    n_rows, dff = act.shape
    d = w_down.shape[2]
    nb = n_rows // ROW_BLOCK
    blk = lambda j, nu: jnp.minimum(j, nu[0] - 1)
    exp = lambda j, be, nu: be[blk(j, nu)]
    return pl.pallas_call(
        _expert_down_kernel,
        grid_spec=pltpu.PrefetchScalarGridSpec(
            num_scalar_prefetch=2,
            grid=(d // tn, nb),
            in_specs=[pl.BlockSpec((ROW_BLOCK, dff), lambda n, j, be, nu: (blk(j, nu), 0)),
                      pl.BlockSpec((None, dff, tn), lambda n, j, be, nu: (exp(j, be, nu), 0, n)),
                      pl.BlockSpec((None, 1, tn), lambda n, j, be, nu: (exp(j, be, nu), 0, n))],
            out_specs=pl.BlockSpec((ROW_BLOCK, tn), lambda n, j, be, nu: (blk(j, nu), n)),
            scratch_shapes=[pltpu.VMEM((dff, tn), BF16)]),
        out_shape=jax.ShapeDtypeStruct((n_rows, d), F32),
        compiler_params=_cparams(("arbitrary", "arbitrary")),
        name="expert_down",
    )(blk_expert, n_used, act, w_down, b_down)


def _combine_kernel(dest_hbm, y_hbm, x1_ref, tw_ref, gt_ref, gfin_ref, out_ref, dsm, buf, sem_idx, sem_rows):
    i = pl.program_id(1) + pl.program_id(0) * pl.num_programs(1)
    tm = x1_ref.shape[0]
    n_idx = tm * TOP_K
    idx_copy = pltpu.make_async_copy(dest_hbm.at[pl.ds(i * n_idx, n_idx)], dsm, sem_idx)
    idx_copy.start()
    idx_copy.wait()

    def row_copy(t, kk):
        return pltpu.make_async_copy(y_hbm.at[pl.ds(dsm[t * TOP_K + kk], 1), :],
                                     buf.at[kk, pl.ds(t, 1), :], sem_rows)

    def issue(t, carry):
        for kk in range(TOP_K):
            row_copy(t, kk).start()
        return carry

    def drain(t, carry):
        for kk in range(TOP_K):
            row_copy(t, kk).wait()
        return carry

    lax.fori_loop(0, tm, issue, 0)
    lax.fori_loop(0, tm, drain, 0)

    tw = tw_ref[...]
    acc = buf[0] * tw[:, 0:1]
    for kk in range(1, TOP_K):
        acc = acc + buf[kk] * tw[:, kk:kk + 1]
    x2 = x1_ref[...] + gt_ref[...] * acc
    out_ref[...] = x2 * lax.rsqrt(jnp.mean(x2 * x2, axis=-1, keepdims=True) + EPS) * gfin_ref[...]


def _combine(dest_flat, y, x1, tw, gt, g_final, bsz, s, tm):
    t, d = x1.shape
    nt = s // tm
    row = lambda b, i: (b * nt + i, 0)
    return pl.pallas_call(
        _combine_kernel,
        grid=(bsz, nt),
        in_specs=[pl.BlockSpec(memory_space=pl.ANY),
                  pl.BlockSpec(memory_space=pl.ANY),
                  pl.BlockSpec((tm, d), row),
                  pl.BlockSpec((tm, TOP_K), row),
                  pl.BlockSpec((None, 1, d), lambda b, i: (b, 0, 0)),
                  pl.BlockSpec((1, d), lambda b, i: (0, 0))],
        out_specs=pl.BlockSpec((tm, d), row),
        out_shape=jax.ShapeDtypeStruct((t, d), F32),
        scratch_shapes=[pltpu.SMEM((tm * TOP_K,), jnp.int32),
                        pltpu.VMEM((TOP_K, tm, d), F32),
                        pltpu.SemaphoreType.DMA(()),
                        pltpu.SemaphoreType.DMA(())],
        compiler_params=_cparams(("arbitrary", "arbitrary")),
        name="combine",
    )(dest_flat, y, x1, tw, gt, g_final)


def _pad_lanes(a, value=0.0):
    return jnp.pad(a, ((0, 0), (0, LANES - a.shape[1])), constant_values=value)


def _routing_tables(idx, rank, counts_f, n_blocks):
    counts = counts_f[0, :N_EXPERTS].astype(jnp.int32)
    padded = (counts + ROW_BLOCK - 1) // ROW_BLOCK * ROW_BLOCK
    pend = jnp.cumsum(padded)
    pstart = pend - padded
    dest = (pstart[idx] + rank).reshape(-1)
    blk_start = jnp.arange(n_blocks, dtype=jnp.int32) * ROW_BLOCK
    blk_expert = jnp.clip(jnp.searchsorted(pend, blk_start, side="right"), 0, N_EXPERTS - 1).astype(jnp.int32)
    n_used = (pend[-1:] // ROW_BLOCK).astype(jnp.int32)
    pad_info = jnp.stack([pstart + counts, padded - counts], axis=1).reshape(-1).astype(jnp.int32)
    return dest, blk_expert, n_used, pad_info


def _layer(x, c, ctx, c_ctx, w_ada, b_ada, g_mix, w_in, b_if, conv_w, norm_g, w_out,
           g_ffn, w_router, b_router, w_gu, b_gu, w_down, b_down, g_final):
    bsz, s, d = x.shape
    s_ctx = ctx.shape[1]

    cond = jnp.zeros((8, d), F32).at[:bsz].set(c).at[bsz].set(c_ctx)
    mod = _adaln(cond, w_ada, b_ada[None, :])
    sh_m, sc_m, gt_m, sh_f, sc_f, gt_f = [m[:, None, :] for m in jnp.split(mod, N_MOD, axis=-1)]
    lat = lambda m: m[:bsz]
    ctxm = lambda m: jnp.broadcast_to(m[bsz:bsz + 1], (bsz, 1, d))

    g0 = 2 * QK_COLS + 2 * MLSTM_WIDTH
    w_main = jnp.concatenate([w_in[:, :g0], w_in[:, g0 + N_GATE_COLS:]], axis=1).astype(BF16)
    w_gate = _pad_lanes(w_in[:, g0:g0 + N_GATE_COLS]).astype(BF16)
    b_gate = _pad_lanes(b_if[None, :])
    g_mix2 = g_mix[None, :]

    proj_c, gpre_c = _inproj(ctx, g_mix2, ctxm(sh_m), ctxm(sc_m), w_main, w_gate, min(s_ctx, 512))
    gcol_c, grow_c = _gates(gpre_c, b_gate)
    zeros_state = (jnp.zeros((bsz, 2 * N_HEADS, DK, DV), F32),
                   jnp.zeros((bsz, 2 * N_HEADS, 1, DK), F32),
                   jnp.zeros((bsz, 2 * N_HEADS, 1, LANES), F32))
    _, _, c0, n0, m0 = _mlstm(proj_c, gcol_c, grow_c, bsz, s_ctx, *zeros_state)

    proj, gpre = _inproj(x, g_mix2, lat(sh_m), lat(sc_m), w_main, w_gate, 512)
    gcol, grow = _gates(gpre, b_gate)
    hf, hb, _, _, _ = _mlstm(proj, gcol, grow, bsz, s, c0, n0, m0)
    x1, xn2, idx, tw = _mixout(
        proj, hf, hb, x.reshape(bsz * s, d), conv_w, norm_g[None, :], w_out.astype(BF16), lat(gt_m),
        g_ffn[None, :], lat(sh_f), lat(sc_f), _pad_lanes(w_router).astype(BF16),
        _pad_lanes(b_router[None, :], NEG_BIG), bsz, s, 256)

    t = bsz * s
    n_blocks = -(-(t * TOP_K) // ROW_BLOCK) + N_EXPERTS
    rank, counts = _rank(idx, 512)
    dest, blk_expert, n_used, pad_info = _routing_tables(idx, rank, counts, n_blocks)
    xs = _dispatch(pad_info, dest, xn2, n_blocks * ROW_BLOCK, 512)
    act = _expert_gu(blk_expert, n_used, xs, w_gu, b_gu[:, None, :], 512)
    y = _expert_down(blk_expert, n_used, act, w_down, b_down[:, None, :], 1024)
    out = _combine(dest, y, x1, tw, lat(gt_f), g_final[None, :], bsz, s, 256)
    return out.reshape(bsz, s, d)


def kernel(x, c, ctx, c_ctx, w_ada, b_ada, g_mix, w_in, b_if, conv_w, mlstm_norm_g, w_out,
           g_ffn, w_router, b_router, w_gu, b_gu, w_down, b_down, g_final):
    return _layer(x, c, ctx, c_ctx, w_ada[0], b_ada[0], g_mix[0], w_in[0], b_if[0], conv_w[0],
                  mlstm_norm_g[0], w_out[0], g_ffn[0], w_router[0], b_router[0], w_gu[0], b_gu[0],
                  w_down[0], b_down[0], g_final)
```

```python
import functools

import jax
import jax.numpy as jnp
from jax import lax
from jax.experimental import pallas as pl
from jax.experimental.pallas import tpu as pltpu

F32 = jnp.float32
BF16 = jnp.bfloat16

N_HEADS = 4
DK = 128
DV = 256
QK_COLS = N_HEADS * DK
MLSTM_WIDTH = N_HEADS * DV
CONV_WIDTH = 1024
CONV_HALF = CONV_WIDTH // 2
N_GATE_COLS = 4 * N_HEADS
GRID_W = 64
CHUNK = 128
GATE_SOFT_CAP = 15.0
N_EXPERTS = 32
TOP_K = 4
SWIGLU_LIMIT = 7.0
SWIGLU_ALPHA = 1.702
N_MOD = 6
EPS = 1e-6
LANES = 128
SUBLANES = 8
ROW_BLOCK = 512
NEG_BIG = -1e30
VMEM_LIMIT = 56 * 1024 * 1024


def _cparams(sem):
    return pltpu.CompilerParams(dimension_semantics=sem, vmem_limit_bytes=VMEM_LIMIT)


def _adaln_kernel(c_ref, w_ref, b_ref, o_ref):
    s = c_ref[...]
    s = s * jax.nn.sigmoid(s)
    o_ref[...] = jnp.dot(s.astype(BF16), w_ref[...].astype(BF16),
                         preferred_element_type=F32) + b_ref[...]


def _adaln(cond, w, b):
    d, n = w.shape
    tn = 1024
    return pl.pallas_call(
        _adaln_kernel,
        grid=(n // tn,),
        in_specs=[pl.BlockSpec((8, d), lambda j: (0, 0)),
                  pl.BlockSpec((d, tn), lambda j: (0, j)),
                  pl.BlockSpec((1, tn), lambda j: (0, j))],
        out_specs=pl.BlockSpec((8, tn), lambda j: (0, j)),
        out_shape=jax.ShapeDtypeStruct((8, n), F32),
        compiler_params=_cparams(("arbitrary",)),
        name="adaln",
    )(cond, w, b)


def _inproj_kernel(x_ref, g_ref, sh_ref, sc_ref, w_ref, wg_ref, proj_ref, gate_ref, xn_scr):
    @pl.when(pl.program_id(2) == 0)
    def _():
        x = x_ref[...]
        y = x * lax.rsqrt(jnp.mean(x * x, axis=-1, keepdims=True) + EPS) * g_ref[...]
        xn = y * (1.0 + sc_ref[...]) + sh_ref[...]
        xn_scr[...] = xn.astype(BF16)
        gate_ref[...] = jnp.dot(xn_scr[...], wg_ref[...], preferred_element_type=F32)

    proj_ref[...] = jnp.dot(xn_scr[...], w_ref[...], preferred_element_type=F32).astype(BF16)


def _inproj(x, g, sh, sc, w, wg, tm):
    bsz, s, d = x.shape
    p = w.shape[1]
    tn = 1024
    nt = s // tm
    x2 = x.reshape(bsz * s, d)
    return pl.pallas_call(
        _inproj_kernel,
        grid=(bsz, nt, p // tn),
        in_specs=[pl.BlockSpec((tm, d), lambda b, i, j: (b * nt + i, 0)),
                  pl.BlockSpec((1, d), lambda b, i, j: (0, 0)),
                  pl.BlockSpec((None, 1, d), lambda b, i, j: (b, 0, 0)),
                  pl.BlockSpec((None, 1, d), lambda b, i, j: (b, 0, 0)),
                  pl.BlockSpec((d, tn), lambda b, i, j: (0, j)),
                  pl.BlockSpec((d, LANES), lambda b, i, j: (0, 0))],
        out_specs=[pl.BlockSpec((tm, tn), lambda b, i, j: (b * nt + i, j)),
                   pl.BlockSpec((tm, LANES), lambda b, i, j: (b * nt + i, 0))],
        out_shape=[jax.ShapeDtypeStruct((bsz * s, p), BF16),
                   jax.ShapeDtypeStruct((bsz * s, LANES), F32)],
        scratch_shapes=[pltpu.VMEM((tm, d), BF16)],
        compiler_params=_cparams(("arbitrary", "arbitrary", "arbitrary")),
        name="inproj",
    )(x2, g, sh, sc, w, wg)


def _log_sigmoid(x):
    return jnp.minimum(x, 0.0) - jnp.log1p(jnp.exp(-jnp.abs(x)))


def _gates_kernel(g_ref, b_ref, gc_ref, gr_ref):
    tm = g_ref.shape[0]
    row = lax.broadcasted_iota(jnp.int32, (tm, LANES), 0)
    lane = lax.broadcasted_iota(jnp.int32, (tm, LANES), 1)
    gp = GATE_SOFT_CAP * jnp.tanh((g_ref[...] + b_ref[...]) / GATE_SOFT_CAP)
    is_f = ((lane >> 2) & 1) == 1
    fwd_lane = lane < 2 * N_HEADS
    lf = jnp.where(is_f, _log_sigmoid(gp), 0.0)
    r2 = lax.broadcasted_iota(jnp.int32, (tm, tm), 0)
    c2 = lax.broadcasted_iota(jnp.int32, (tm, tm), 1)
    same_chunk = (r2 // CHUNK) == (c2 // CHUNK)
    lower = jnp.logical_and(same_chunk, r2 >= c2).astype(F32)
    upper = jnp.logical_and(same_chunk, r2 <= c2).astype(F32)
    cf = jnp.dot(lower, lf, precision=lax.Precision.HIGHEST, preferred_element_type=F32)
    cb = jnp.dot(upper, lf, precision=lax.Precision.HIGHEST, preferred_element_type=F32)
    cdir = jnp.where(fwd_lane, cf, cb)
    a = jnp.where(is_f, cdir, gp - pltpu.roll(cdir, LANES - N_HEADS, 1))

    pos = row % CHUNK
    x = a
    k = 1
    while k < CHUNK:
        from_before = jnp.where(pos >= k, pltpu.roll(x, k, 0), -jnp.inf)
        from_after = jnp.where(pos < CHUNK - k, pltpu.roll(x, tm - k, 0), -jnp.inf)
        x = jnp.maximum(x, jnp.where(fwd_lane, from_before, from_after))
        k *= 2
    gc_ref[...] = jnp.where(is_f, a, x)

    lane_c = lax.broadcasted_iota(jnp.int32, (CHUNK, LANES), 1)
    lane_1 = lax.broadcasted_iota(jnp.int32, (1, LANES), 1)
    for c in range(tm // CHUNK):
        lo = c * CHUNK
        xc, ac = x[lo:lo + CHUNK], a[lo:lo + CHUNK]
        end_max = jnp.where(lane_1 < 2 * N_HEADS, xc[CHUNK - 1:CHUNK], xc[0:1])
        e = jnp.exp(ac - end_max)
        rows = jnp.where(((lane_c >> 2) & 1) == 1, pltpu.roll(e, N_HEADS, 1), ac)
        gr_ref[:, lo:lo + CHUNK] = rows.T[:N_GATE_COLS, :]


def _gates(gpre, b_if, tm):
    t = gpre.shape[0]
    return pl.pallas_call(
        _gates_kernel,
        grid=(t // tm,),
        in_specs=[pl.BlockSpec((tm, LANES), lambda i: (i, 0)),
                  pl.BlockSpec((1, LANES), lambda i: (0, 0))],
        out_specs=[pl.BlockSpec((tm, LANES), lambda i: (i, 0)),
                   pl.BlockSpec((N_GATE_COLS, tm), lambda i: (0, i))],
        out_shape=[jax.ShapeDtypeStruct((t, LANES), F32),
                   jax.ShapeDtypeStruct((N_GATE_COLS, t), F32)],
        compiler_params=_cparams(("arbitrary",)),
        name="gates",
    )(gpre, b_if)


DVX = DV + LANES


def _mlstm_chunk(q, k, v_ext, rmax_col, b_col, r_row, e_row, b_last, rmax_last, mask, cx, m_st):
    scale = DK ** -0.5
    mb = jnp.maximum(m_st, jnp.broadcast_to(rmax_col, (CHUNK, CHUNK)))
    w_intra = jnp.exp(jnp.where(mask, r_row - mb, -jnp.inf))
    w_state = jnp.exp(m_st - mb)
    qk = lax.dot_general(q, k, (((1,), (1,)), ((), ())), preferred_element_type=F32)
    s = qk * (w_intra * scale)
    lhs = jnp.concatenate([s.astype(BF16), (q.astype(F32) * (w_state * scale)).astype(BF16)], axis=1)
    rhs = jnp.concatenate([v_ext, cx.astype(BF16)], axis=0)
    nx = jnp.dot(lhs, rhs, preferred_element_type=F32)
    denom = jnp.maximum(jnp.abs(nx[:, DV:]), jnp.exp(-(jnp.broadcast_to(b_col, (CHUNK, CHUNK)) + mb)))
    h = nx[:, :DV] / jnp.concatenate([denom, denom], axis=1)
    ke_t = (k.T.astype(F32) * e_row).astype(BF16)
    c_loc = jnp.dot(ke_t, v_ext, preferred_element_type=F32)
    m_loc = b_last + rmax_last
    m_new = jnp.maximum(b_last + m_st, m_loc)
    return h, jnp.exp(b_last + m_st - m_new) * cx + jnp.exp(m_loc - m_new) * c_loc, m_new


def _mlstm_kernel(qf_ref, kf_ref, vf_ref, gcf_ref, grf_ref, qb_ref, kb_ref, vb_ref, gcb_ref, grb_ref,
                  c0_ref, m0_ref, hf_ref, hb_ref, cout_ref, mout_ref, m_scr, *c_scrs):
    c = pl.program_id(1)

    @pl.when(c == 0)
    def _():
        for idx, c_scr in enumerate(c_scrs):
            c_scr[...] = c0_ref[idx]
        m_scr[...] = m0_ref[...]

    row = lax.broadcasted_iota(jnp.int32, (CHUNK, CHUNK), 0)
    col = lax.broadcasted_iota(jnp.int32, (CHUNK, CHUNK), 1)
    ones = jnp.ones((CHUNK, LANES), BF16)
    m_all = m_scr[...]
    dirs = ((qf_ref, kf_ref, vf_ref, gcf_ref, grf_ref, hf_ref, 0, CHUNK - 1, col <= row),
            (qb_ref, kb_ref, vb_ref, gcb_ref, grb_ref, hb_ref, 2 * N_HEADS, 0, col >= row))
    m_news = []
    for di, (q_ref, k_ref, v_ref, gc_ref, gr_ref, h_ref, off, last, mask) in enumerate(dirs):
        for hd in range(N_HEADS):
            idx = di * N_HEADS + hd
            lr, lb = off + hd, off + N_HEADS + hd
            v_ext = jnp.concatenate([v_ref[:, hd * DV:(hd + 1) * DV], ones], axis=1)
            h, c_new, m_new = _mlstm_chunk(
                q_ref[:, hd * DK:(hd + 1) * DK], k_ref[:, hd * DK:(hd + 1) * DK], v_ext,
                gc_ref[:, lr:lr + 1], gc_ref[:, lb:lb + 1], gr_ref[lr:lr + 1, :], gr_ref[lb:lb + 1, :],
                gc_ref[last:last + 1, lb:lb + 1], gc_ref[last:last + 1, lr:lr + 1], mask,
                c_scrs[idx][...], m_all[idx][:, 0:1])
            h_ref[:, hd * DV:(hd + 1) * DV] = h
            c_scrs[idx][...] = c_new
            m_news.append(jnp.broadcast_to(m_new, (1, LANES)))
    for idx, m_new in enumerate(m_news):
        m_scr[idx] = m_new

    @pl.when(c == pl.num_programs(1) - 1)
    def _():
        for idx, c_scr in enumerate(c_scrs):
            cout_ref[idx] = c_scr[...]
        mout_ref[...] = m_scr[...]


def _mlstm(proj, gcol, grow, bsz, s, c0, m0):
    nc = s // CHUNK
    t = bsz * s
    fwd = lambda b, c: b * nc + c
    bwd = lambda b, c: b * nc + (nc - 1 - c)

    def specs(ci):
        return [pl.BlockSpec((CHUNK, QK_COLS), lambda b, c: (ci(b, c), 0)),
                pl.BlockSpec((CHUNK, QK_COLS), lambda b, c: (ci(b, c), 1)),
                pl.BlockSpec((CHUNK, MLSTM_WIDTH), lambda b, c: (ci(b, c), 1)),
                pl.BlockSpec((CHUNK, LANES), lambda b, c: (ci(b, c), 0)),
                pl.BlockSpec((N_GATE_COLS, CHUNK), lambda b, c: (0, ci(b, c)))]

    st_specs = [pl.BlockSpec((None, 2 * N_HEADS, DK, DVX), lambda b, c: (b, 0, 0, 0)),
                pl.BlockSpec((None, 2 * N_HEADS, 1, LANES), lambda b, c: (b, 0, 0, 0))]
    return pl.pallas_call(
        _mlstm_kernel,
        grid=(bsz, nc),
        in_specs=specs(fwd) + specs(bwd) + st_specs,
        out_specs=[pl.BlockSpec((CHUNK, MLSTM_WIDTH), lambda b, c: (fwd(b, c), 0)),
                   pl.BlockSpec((CHUNK, MLSTM_WIDTH), lambda b, c: (bwd(b, c), 0))] + st_specs,
        out_shape=[jax.ShapeDtypeStruct((t, MLSTM_WIDTH), F32),
                   jax.ShapeDtypeStruct((t, MLSTM_WIDTH), F32),
                   jax.ShapeDtypeStruct(c0.shape, F32),
                   jax.ShapeDtypeStruct(m0.shape, F32)],
        scratch_shapes=[pltpu.VMEM((2 * N_HEADS, 1, LANES), F32)]
        + [pltpu.VMEM((DK, DVX), F32) for _ in range(2 * N_HEADS)],
        compiler_params=_cparams(("arbitrary", "arbitrary")),
        name="mlstm",
    )(proj, proj, proj, gcol, grow, proj, proj, proj, gcol, grow, c0, m0)


def _mixout_kernel(o_ref, cb_ref, cc_ref, cx_ref, ccp_ref, cxp_ref, ccn_ref, cxn_ref, hf_ref, hb_ref, x_ref,
                   cw_ref, ng_ref, wout_ref, gt_ref, gffn_ref, shf_ref, scf_ref, wr_ref, br_ref,
                   x1_ref, xn2_ref, idx_ref, tw_ref):
    i = pl.program_id(1)
    tm = x_ref.shape[0]
    cw = cw_ref[...]
    u = cc_ref[...].astype(F32) * cx_ref[...].astype(F32)

    uh = u[:, :CONV_HALF]
    pos = lax.broadcasted_iota(jnp.int32, (tm, CONV_HALF), 0) & (GRID_W - 1)
    left = jnp.where(pos == 0, 0.0, pltpu.roll(uh, 1, 0))
    right = jnp.where(pos == GRID_W - 1, 0.0, pltpu.roll(uh, tm - 1, 0))
    yh = cw[0:1, :CONV_HALF] * left + cw[1:2, :CONV_HALF] * uh + cw[2:3, :CONV_HALF] * right

    has_prev = jnp.where(i > 0, 1.0, 0.0)
    has_next = jnp.where(i < pl.num_programs(1) - 1, 1.0, 0.0)
    up = ccp_ref[...].astype(F32) * cxp_ref[...].astype(F32) * has_prev
    un = ccn_ref[...].astype(F32) * cxn_ref[...].astype(F32) * has_next
    ext = jnp.concatenate([up, u[:, CONV_HALF:], un], axis=0)
    yv = (cw[0:1, CONV_HALF:] * ext[0:tm] + cw[1:2, CONV_HALF:] * ext[GRID_W:GRID_W + tm]
          + cw[2:3, CONV_HALF:] * ext[2 * GRID_W:2 * GRID_W + tm])
    yc = cb_ref[...].astype(F32) * jnp.concatenate([yh, yv], axis=1)

    hs = hf_ref[...] + hb_ref[...]
    parts = []
    for hd in range(N_HEADS):
        seg = hs[:, hd * DV:(hd + 1) * DV]
        parts.append(seg * lax.rsqrt(jnp.mean(seg * seg, axis=-1, keepdims=True) + EPS))
    hm = jnp.concatenate(parts, axis=1) * ng_ref[...] * jax.nn.sigmoid(o_ref[...].astype(F32))

    z = jnp.concatenate([hm.astype(BF16), yc.astype(BF16)], axis=1)
    x1 = x_ref[...] + gt_ref[...] * jnp.dot(z, wout_ref[...], preferred_element_type=F32)
    x1_ref[...] = x1

    y = x1 * lax.rsqrt(jnp.mean(x1 * x1, axis=-1, keepdims=True) + EPS) * gffn_ref[...]
    xn2 = y * (1.0 + scf_ref[...]) + shf_ref[...]
    xn2_ref[...] = xn2

    logits = jnp.dot(xn2.astype(BF16), wr_ref[...], preferred_element_type=F32) + br_ref[...]
    lane = lax.broadcasted_iota(jnp.int32, (tm, LANES), 1)
    lane_f = lane.astype(F32)
    vals, idxs = [], []
    for _ in range(TOP_K):
        mx = jnp.max(logits, axis=-1, keepdims=True)
        ik = jnp.min(jnp.where(logits == mx, lane_f, float(LANES)), axis=-1, keepdims=True)
        vals.append(mx)
        idxs.append(ik)
        logits = jnp.where(lane_f == ik, -jnp.inf, logits)
    es = [jnp.exp(v - vals[0]) for v in vals]
    tot = es[0] + es[1] + es[2] + es[3]
    lane4 = lax.broadcasted_iota(jnp.int32, (tm, TOP_K), 1)
    idx_out = jnp.zeros((tm, TOP_K), F32)
    tw_out = jnp.zeros((tm, TOP_K), F32)
    for kk in range(TOP_K):
        idx_out = jnp.where(lane4 == kk, idxs[kk], idx_out)
        tw_out = jnp.where(lane4 == kk, es[kk] / tot, tw_out)
    idx_ref[...] = idx_out.astype(jnp.int32)
    tw_ref[...] = tw_out


def _mixout(proj, hf, hb, x2, conv_w, norm_g, w_out, gt, g_ffn, sh_f, sc_f, w_r, b_r, bsz, s, tm):
    t, d = x2.shape
    nt = s // tm
    rb = tm // GRID_W
    last_rb = t // GRID_W - 1
    row = lambda b, i: b * nt + i
    w = MLSTM_WIDTH
    vec = lambda n: pl.BlockSpec((1, n), lambda b, i: (0, 0))
    per_b = pl.BlockSpec((None, 1, d), lambda b, i: (b, 0, 0))
    halo_prev = lambda cblk: pl.BlockSpec(
        (GRID_W, CONV_HALF), lambda b, i: (jnp.maximum(row(b, i) * rb - 1, 0), cblk))
    halo_next = lambda cblk: pl.BlockSpec(
        (GRID_W, CONV_HALF), lambda b, i: (jnp.minimum((row(b, i) + 1) * rb, last_rb), cblk))
    return pl.pallas_call(
        _mixout_kernel,
        grid=(bsz, nt),
        in_specs=[pl.BlockSpec((tm, w), lambda b, i: (row(b, i), 2)),
                  pl.BlockSpec((tm, w), lambda b, i: (row(b, i), 3)),
                  pl.BlockSpec((tm, w), lambda b, i: (row(b, i), 4)),
                  pl.BlockSpec((tm, w), lambda b, i: (row(b, i), 5)),
                  halo_prev(9), halo_prev(11), halo_next(9), halo_next(11),
                  pl.BlockSpec((tm, w), lambda b, i: (row(b, i), 0)),
                  pl.BlockSpec((tm, w), lambda b, i: (row(b, i), 0)),
                  pl.BlockSpec((tm, d), lambda b, i: (row(b, i), 0)),
                  pl.BlockSpec((3, CONV_WIDTH), lambda b, i: (0, 0)),
                  vec(w),
                  pl.BlockSpec((d, d), lambda b, i: (0, 0)),
                  per_b, vec(d), per_b, per_b,
                  pl.BlockSpec((d, LANES), lambda b, i: (0, 0)),
                  vec(LANES)],
        out_specs=[pl.BlockSpec((tm, d), lambda b, i: (row(b, i), 0)),
                   pl.BlockSpec((tm, d), lambda b, i: (row(b, i), 0)),
                   pl.BlockSpec((tm, TOP_K), lambda b, i: (row(b, i), 0)),
                   pl.BlockSpec((tm, TOP_K), lambda b, i: (row(b, i), 0))],
        out_shape=[jax.ShapeDtypeStruct((t, d), F32),
                   jax.ShapeDtypeStruct((t, d), F32),
                   jax.ShapeDtypeStruct((t, TOP_K), jnp.int32),
                   jax.ShapeDtypeStruct((t, TOP_K), F32)],
        compiler_params=_cparams(("arbitrary", "arbitrary")),
        name="mixout",
    )(proj, proj, proj, proj, proj, proj, proj, proj, hf, hb, x2,
      conv_w, norm_g, w_out, gt, g_ffn, sh_f, sc_f, w_r, b_r)


def _rank_kernel(idx_ref, rank_ref, cnt_ref, run_scr):
    @pl.when(pl.program_id(0) == 0)
    def _():
        run_scr[...] = jnp.zeros_like(run_scr)

    tm = idx_ref.shape[0]
    idx = idx_ref[...]
    lane = lax.broadcasted_iota(jnp.int32, (tm, LANES), 1)
    hits = [lane == idx[:, kk:kk + 1] for kk in range(TOP_K)]
    onehot = jnp.zeros((tm, LANES), F32)
    for hit in hits:
        onehot = onehot + hit.astype(F32)
    r = lax.broadcasted_iota(jnp.int32, (tm, tm), 0)
    c = lax.broadcasted_iota(jnp.int32, (tm, tm), 1)
    before = jnp.dot((c < r).astype(BF16), onehot.astype(BF16), preferred_element_type=F32) + run_scr[...]
    lane4 = lax.broadcasted_iota(jnp.int32, (tm, TOP_K), 1)
    rank = jnp.zeros((tm, TOP_K), F32)
    for kk, hit in enumerate(hits):
        rk = jnp.sum(jnp.where(hit, before, 0.0), axis=-1, keepdims=True)
        rank = jnp.where(lane4 == kk, rk, rank)
    rank_ref[...] = rank.astype(jnp.int32)
    run_scr[...] = run_scr[...] + jnp.sum(onehot, axis=0, keepdims=True)
    cnt_ref[...] = run_scr[...]


def _rank(idx, tm):
    t = idx.shape[0]
    return pl.pallas_call(
        _rank_kernel,
        grid=(t // tm,),
        in_specs=[pl.BlockSpec((tm, TOP_K), lambda i: (i, 0))],
        out_specs=[pl.BlockSpec((tm, TOP_K), lambda i: (i, 0)),
                   pl.BlockSpec((1, LANES), lambda i: (0, 0))],
        out_shape=[jax.ShapeDtypeStruct((t, TOP_K), jnp.int32),
                   jax.ShapeDtypeStruct((1, LANES), F32)],
        scratch_shapes=[pltpu.VMEM((1, LANES), F32)],
        compiler_params=_cparams(("arbitrary",)),
        name="rank",
    )(idx)


def _dispatch_kernel(pad_ref, dest_hbm, xn_ref, xs_hbm, dsm, zeros_scr, sem_idx, sem_rows, sem_pad):
    i = pl.program_id(0)
    tm = xn_ref.shape[0]
    n_idx = tm * TOP_K
    idx_copy = pltpu.make_async_copy(dest_hbm.at[pl.ds(i * n_idx, n_idx)], dsm, sem_idx)
    idx_copy.start()

    def pad_copy(off, size):
        return pltpu.make_async_copy(zeros_scr.at[pl.ds(0, size), :], xs_hbm.at[pl.ds(off, size), :], sem_pad)

    def for_each_pad_piece(fn):
        def per_expert(e, carry):
            off = pad_ref[2 * e]
            n = pad_ref[2 * e + 1]
            head = n & (SUBLANES - 1)
            for r in range(SUBLANES - 1):
                @pl.when(r < head)
                def _(r=r):
                    fn(pad_copy(off + r, 1))

            off = off + head
            size = ROW_BLOCK // 2
            while size >= SUBLANES:
                take = (n & size) != 0

                @pl.when(take)
                def _(off=off, size=size):
                    fn(pad_copy(pl.multiple_of(off, SUBLANES), size))

                off = off + jnp.where(take, size, 0)
                size //= 2
            return carry
        lax.fori_loop(0, N_EXPERTS, per_expert, 0)

    @pl.when(i == 0)
    def _():
        zeros_scr[...] = jnp.zeros_like(zeros_scr)
        for_each_pad_piece(lambda cp: cp.start())
        for_each_pad_piece(lambda cp: cp.wait())

    idx_copy.wait()

    def row_copy(t, kk):
        return pltpu.make_async_copy(xn_ref.at[pl.ds(t, 1), :],
                                     xs_hbm.at[pl.ds(dsm[t * TOP_K + kk], 1), :], sem_rows)

    def issue(t, carry):
        for kk in range(TOP_K):
            row_copy(t, kk).start()
        return carry

    def drain(t, carry):
        for kk in range(TOP_K):
            row_copy(t, kk).wait()
        return carry

    lax.fori_loop(0, tm, issue, 0)
    lax.fori_loop(0, tm, drain, 0)


def _dispatch(pad_info, dest_flat, xn2, n_rows, tm):
    t, d = xn2.shape
    return pl.pallas_call(
        _dispatch_kernel,
        grid_spec=pltpu.PrefetchScalarGridSpec(
            num_scalar_prefetch=1,
            grid=(t // tm,),
            in_specs=[pl.BlockSpec(memory_space=pl.ANY),
                      pl.BlockSpec((tm, d), lambda i, pad: (i, 0))],
            out_specs=pl.BlockSpec(memory_space=pl.ANY),
            scratch_shapes=[pltpu.SMEM((tm * TOP_K,), jnp.int32),
                            pltpu.VMEM((ROW_BLOCK // 2, d), F32),
                            pltpu.SemaphoreType.DMA(()),
                            pltpu.SemaphoreType.DMA(()),
                            pltpu.SemaphoreType.DMA(())]),
        out_shape=jax.ShapeDtypeStruct((n_rows, d), F32),
        compiler_params=_cparams(("arbitrary",)),
        name="dispatch",
    )(pad_info, dest_flat, xn2)


def _new_expert(be_ref, j):
    return jnp.logical_or(j == 0, be_ref[j] != be_ref[jnp.maximum(j - 1, 0)])


def _expert_gu_kernel(be_ref, nu_ref, xs_ref, wg_ref, wu_ref, bg_ref, bu_ref, act_ref, wg_scr, wu_scr):
    j = pl.program_id(1)

    @pl.when(j < nu_ref[0])
    def _():
        @pl.when(_new_expert(be_ref, j))
        def _():
            wg_scr[...] = wg_ref[...].astype(BF16)
            wu_scr[...] = wu_ref[...].astype(BF16)

        x = xs_ref[...].astype(BF16)
        g = jnp.dot(x, wg_scr[...], preferred_element_type=F32) + bg_ref[...]
        u = jnp.dot(x, wu_scr[...], preferred_element_type=F32) + bu_ref[...]
        gate = jnp.minimum(g, SWIGLU_LIMIT)
        up = jnp.clip(u, -SWIGLU_LIMIT, SWIGLU_LIMIT)
        act_ref[...] = ((up + 1.0) * gate * jax.nn.sigmoid(SWIGLU_ALPHA * gate)).astype(BF16)


def _expert_gu(blk_expert, n_used, xs, w_gu, b_gu, tn):
    n_rows, d = xs.shape
    dff = w_gu.shape[2] // 2
    nt = dff // tn
    nb = n_rows // ROW_BLOCK
    blk = lambda j, nu: jnp.minimum(j, nu[0] - 1)
    exp = lambda j, be, nu: be[blk(j, nu)]
    return pl.pallas_call(
        _expert_gu_kernel,
        grid_spec=pltpu.PrefetchScalarGridSpec(
            num_scalar_prefetch=2,
            grid=(nt, nb),
            in_specs=[pl.BlockSpec((ROW_BLOCK, d), lambda n, j, be, nu: (blk(j, nu), 0)),
                      pl.BlockSpec((None, d, tn), lambda n, j, be, nu: (exp(j, be, nu), 0, n)),
                      pl.BlockSpec((None, d, tn), lambda n, j, be, nu: (exp(j, be, nu), 0, nt + n)),
                      pl.BlockSpec((None, 1, tn), lambda n, j, be, nu: (exp(j, be, nu), 0, n)),
                      pl.BlockSpec((None, 1, tn), lambda n, j, be, nu: (exp(j, be, nu), 0, nt + n))],
            out_specs=pl.BlockSpec((ROW_BLOCK, tn), lambda n, j, be, nu: (blk(j, nu), n)),
            scratch_shapes=[pltpu.VMEM((d, tn), BF16), pltpu.VMEM((d, tn), BF16)]),
        out_shape=jax.ShapeDtypeStruct((n_rows, dff), BF16),
        compiler_params=_cparams(("arbitrary", "arbitrary")),
        name="expert_gu",
    )(blk_expert, n_used, xs, w_gu, w_gu, b_gu, b_gu)


def _expert_down_kernel(be_ref, nu_ref, act_ref, w_ref, b_ref, y_ref, w_scr):
    j = pl.program_id(1)

    @pl.when(j < nu_ref[0])
    def _():
        @pl.when(_new_expert(be_ref, j))
        def _():
            w_scr[...] = w_ref[...].astype(BF16)

        y_ref[...] = jnp.dot(act_ref[...], w_scr[...], preferred_element_type=F32) + b_ref[...]


def _expert_down(blk_expert, n_used, act, w_down, b_down, tn):
    n_rows, dff = act.shape
    d = w_down.shape[2]
    nb = n_rows // ROW_BLOCK
    blk = lambda j, nu: jnp.minimum(j, nu[0] - 1)
    exp = lambda j, be, nu: be[blk(j, nu)]
    return pl.pallas_call(
        _expert_down_kernel,
        grid_spec=pltpu.PrefetchScalarGridSpec(
            num_scalar_prefetch=2,
            grid=(d // tn, nb),
            in_specs=[pl.BlockSpec((ROW_BLOCK, dff), lambda n, j, be, nu: (blk(j, nu), 0)),
                      pl.BlockSpec((None, dff, tn), lambda n, j, be, nu: (exp(j, be, nu), 0, n)),
                      pl.BlockSpec((None, 1, tn), lambda n, j, be, nu: (exp(j, be, nu), 0, n))],
            out_specs=pl.BlockSpec((ROW_BLOCK, tn), lambda n, j, be, nu: (blk(j, nu), n)),
            scratch_shapes=[pltpu.VMEM((dff, tn), BF16)]),
        out_shape=jax.ShapeDtypeStruct((n_rows, d), F32),
        compiler_params=_cparams(("arbitrary", "arbitrary")),
        name="expert_down",
    )(blk_expert, n_used, act, w_down, b_down)


def _combine_kernel(dest_hbm, y_hbm, x1_ref, tw_ref, gt_ref, gfin_ref, out_ref, dsm, buf, sem_idx, sem_rows):
    i = pl.program_id(1) + pl.program_id(0) * pl.num_programs(1)
    tm = x1_ref.shape[0]
    n_idx = tm * TOP_K
    idx_copy = pltpu.make_async_copy(dest_hbm.at[pl.ds(i * n_idx, n_idx)], dsm, sem_idx)
    idx_copy.start()
    idx_copy.wait()

    def row_copy(t, kk):
        return pltpu.make_async_copy(y_hbm.at[pl.ds(dsm[t * TOP_K + kk], 1), :],
                                     buf.at[kk, pl.ds(t, 1), :], sem_rows)

    def issue(t, carry):
        for kk in range(TOP_K):
            row_copy(t, kk).start()
        return carry

    def drain(t, carry):
        for kk in range(TOP_K):
            row_copy(t, kk).wait()
        return carry

    lax.fori_loop(0, tm, issue, 0)
    lax.fori_loop(0, tm, drain, 0)

    tw = tw_ref[...]
    acc = buf[0] * tw[:, 0:1]
    for kk in range(1, TOP_K):
        acc = acc + buf[kk] * tw[:, kk:kk + 1]
    x2 = x1_ref[...] + gt_ref[...] * acc
    out_ref[...] = x2 * lax.rsqrt(jnp.mean(x2 * x2, axis=-1, keepdims=True) + EPS) * gfin_ref[...]


def _combine(dest_flat, y, x1, tw, gt, g_final, bsz, s, tm):
    t, d = x1.shape
    nt = s // tm
    row = lambda b, i: (b * nt + i, 0)
    return pl.pallas_call(
        _combine_kernel,
        grid=(bsz, nt),
        in_specs=[pl.BlockSpec(memory_space=pl.ANY),
                  pl.BlockSpec(memory_space=pl.ANY),
                  pl.BlockSpec((tm, d), row),
                  pl.BlockSpec((tm, TOP_K), row),
                  pl.BlockSpec((None, 1, d), lambda b, i: (b, 0, 0)),
                  pl.BlockSpec((1, d), lambda b, i: (0, 0))],
        out_specs=pl.BlockSpec((tm, d), row),
        out_shape=jax.ShapeDtypeStruct((t, d), F32),
        scratch_shapes=[pltpu.SMEM((tm * TOP_K,), jnp.int32),
                        pltpu.VMEM((TOP_K, tm, d), F32),
                        pltpu.SemaphoreType.DMA(()),
                        pltpu.SemaphoreType.DMA(())],
        compiler_params=_cparams(("arbitrary", "arbitrary")),
        name="combine",
    )(dest_flat, y, x1, tw, gt, g_final)


def _pad_lanes(a, value=0.0):
    return jnp.pad(a, ((0, 0), (0, LANES - a.shape[1])), constant_values=value)


def _routing_tables(idx, rank, counts_f, n_blocks):
    counts = counts_f[0, :N_EXPERTS].astype(jnp.int32)
    padded = (counts + ROW_BLOCK - 1) // ROW_BLOCK * ROW_BLOCK
    pend = jnp.cumsum(padded)
    pstart = pend - padded
    dest = (pstart[idx] + rank).reshape(-1)
    blk_start = jnp.arange(n_blocks, dtype=jnp.int32) * ROW_BLOCK
    blk_expert = jnp.minimum(jnp.sum((pend[None, :] <= blk_start[:, None]).astype(jnp.int32), axis=1),
                             N_EXPERTS - 1)
    n_used = (pend[-1:] // ROW_BLOCK).astype(jnp.int32)
    pad_info = jnp.stack([pstart + counts, padded - counts], axis=1).reshape(-1).astype(jnp.int32)
    return dest, blk_expert, n_used, pad_info


def _layer(x, c, ctx, c_ctx, w_ada, b_ada, g_mix, w_in, b_if, conv_w, norm_g, w_out,
           g_ffn, w_router, b_router, w_gu, b_gu, w_down, b_down, g_final):
    bsz, s, d = x.shape
    s_ctx = ctx.shape[1]

    cond = jnp.zeros((8, d), F32).at[:bsz].set(c).at[bsz].set(c_ctx)
    mod = _adaln(cond, w_ada, b_ada[None, :])
    sh_m, sc_m, gt_m, sh_f, sc_f, gt_f = [m[:, None, :] for m in jnp.split(mod, N_MOD, axis=-1)]
    lat = lambda m: m[:bsz]
    ctxm = lambda m: jnp.broadcast_to(m[bsz:bsz + 1], (bsz, 1, d))

    g0 = 2 * QK_COLS + 2 * MLSTM_WIDTH
    w_main = jnp.concatenate([w_in[:, :g0], w_in[:, g0 + N_GATE_COLS:]], axis=1).astype(BF16)
    w_gate = _pad_lanes(w_in[:, g0:g0 + N_GATE_COLS]).astype(BF16)
    b_gate = _pad_lanes(b_if[None, :])
    g_mix2 = g_mix[None, :]

    proj_c, gpre_c = _inproj(ctx, g_mix2, ctxm(sh_m), ctxm(sc_m), w_main, w_gate, min(s_ctx, 512))
    gcol_c, grow_c = _gates(gpre_c, b_gate, 512)
    zeros_state = (jnp.zeros((bsz, 2 * N_HEADS, DK, DVX), F32),
                   jnp.zeros((bsz, 2 * N_HEADS, 1, LANES), F32))
    _, _, c0, m0 = _mlstm(proj_c, gcol_c, grow_c, bsz, s_ctx, *zeros_state)

    proj, gpre = _inproj(x, g_mix2, lat(sh_m), lat(sc_m), w_main, w_gate, 512)
    gcol, grow = _gates(gpre, b_gate, 512)
    hf, hb, _, _ = _mlstm(proj, gcol, grow, bsz, s, c0, m0)
    x1, xn2, idx, tw = _mixout(
        proj, hf, hb, x.reshape(bsz * s, d), conv_w, norm_g[None, :], w_out.astype(BF16), lat(gt_m),
        g_ffn[None, :], lat(sh_f), lat(sc_f), _pad_lanes(w_router).astype(BF16),
        _pad_lanes(b_router[None, :], NEG_BIG), bsz, s, 256)

    t = bsz * s
    n_blocks = -(-(t * TOP_K) // ROW_BLOCK) + N_EXPERTS
    rank, counts = _rank(idx, 512)
    dest, blk_expert, n_used, pad_info = _routing_tables(idx, rank, counts, n_blocks)
    xs = _dispatch(pad_info, dest, xn2, n_blocks * ROW_BLOCK, 512)
    act = _expert_gu(blk_expert, n_used, xs, w_gu, b_gu[:, None, :], 512)
    y = _expert_down(blk_expert, n_used, act, w_down, b_down[:, None, :], 1024)
    out = _combine(dest, y, x1, tw, lat(gt_f), g_final[None, :], bsz, s, 256)
    return out.reshape(bsz, s, d)


def kernel(x, c, ctx, c_ctx, w_ada, b_ada, g_mix, w_in, b_if, conv_w, mlstm_norm_g, w_out,
           g_ffn, w_router, b_router, w_gu, b_gu, w_down, b_down, g_final):
    return _layer(x, c, ctx, c_ctx, w_ada[0], b_ada[0], g_mix[0], w_in[0], b_if[0], conv_w[0],
                  mlstm_norm_g[0], w_out[0], g_ffn[0], w_router[0], b_router[0], w_gu[0], b_gu[0],
                  w_down[0], b_down[0], g_final)
```

```python
import functools

import jax
import jax.numpy as jnp
from jax import lax
from jax.experimental import pallas as pl
from jax.experimental.pallas import tpu as pltpu

F32 = jnp.float32
BF16 = jnp.bfloat16

N_HEADS = 4
DK = 128
DV = 256
QK_COLS = N_HEADS * DK
MLSTM_WIDTH = N_HEADS * DV
CONV_WIDTH = 1024
CONV_HALF = CONV_WIDTH // 2
N_GATE_COLS = 4 * N_HEADS
GRID_W = 64
CHUNK = 128
GATE_SOFT_CAP = 15.0
N_EXPERTS = 32
TOP_K = 4
SWIGLU_LIMIT = 7.0
SWIGLU_ALPHA = 1.702
N_MOD = 6
EPS = 1e-6
LANES = 128
SUBLANES = 8
ROW_BLOCK = 1024
NEG_BIG = -1e30
VMEM_LIMIT = 56 * 1024 * 1024


def _cparams(sem):
    return pltpu.CompilerParams(dimension_semantics=sem, vmem_limit_bytes=VMEM_LIMIT)


def _adaln_kernel(c_ref, w_ref, b_ref, o_ref):
    s = c_ref[...]
    s = s * jax.nn.sigmoid(s)
    o_ref[...] = jnp.dot(s.astype(BF16), w_ref[...].astype(BF16),
                         preferred_element_type=F32) + b_ref[...]


def _adaln(cond, w, b):
    d, n = w.shape
    tn = 1024
    return pl.pallas_call(
        _adaln_kernel,
        grid=(n // tn,),
        in_specs=[pl.BlockSpec((8, d), lambda j: (0, 0)),
                  pl.BlockSpec((d, tn), lambda j: (0, j)),
                  pl.BlockSpec((1, tn), lambda j: (0, j))],
        out_specs=pl.BlockSpec((8, tn), lambda j: (0, j)),
        out_shape=jax.ShapeDtypeStruct((8, n), F32),
        compiler_params=_cparams(("arbitrary",)),
        name="adaln",
    )(cond, w, b)


INPROJ_COLS = 1024


def _inproj_kernel(x_ref, g_ref, sh_ref, sc_ref, w_ref, wg_ref, proj_ref, gate_ref):
    x = x_ref[...]
    y = x * lax.rsqrt(jnp.mean(x * x, axis=-1, keepdims=True) + EPS) * g_ref[...]
    xn = (y * (1.0 + sc_ref[...]) + sh_ref[...]).astype(BF16)
    gate_ref[...] = jnp.dot(xn, wg_ref[...], preferred_element_type=F32)
    for j in range(w_ref.shape[1] // INPROJ_COLS):
        cols = slice(j * INPROJ_COLS, (j + 1) * INPROJ_COLS)
        proj_ref[:, cols] = jnp.dot(xn, w_ref[:, cols], preferred_element_type=F32).astype(BF16)


def _inproj(x, g, sh, sc, w, wg, tm):
    bsz, s, d = x.shape
    p = w.shape[1]
    nt = s // tm
    x2 = x.reshape(bsz * s, d)
    resident = lambda shape: pl.BlockSpec(shape, lambda b, i: (0, 0), pipeline_mode=pl.Buffered(1))
    return pl.pallas_call(
        _inproj_kernel,
        grid=(bsz, nt),
        in_specs=[pl.BlockSpec((tm, d), lambda b, i: (b * nt + i, 0)),
                  pl.BlockSpec((1, d), lambda b, i: (0, 0)),
                  pl.BlockSpec((None, 1, d), lambda b, i: (b, 0, 0)),
                  pl.BlockSpec((None, 1, d), lambda b, i: (b, 0, 0)),
                  resident((d, p)),
                  resident((d, LANES))],
        out_specs=[pl.BlockSpec((tm, p), lambda b, i: (b * nt + i, 0)),
                   pl.BlockSpec((tm, LANES), lambda b, i: (b * nt + i, 0))],
        out_shape=[jax.ShapeDtypeStruct((bsz * s, p), BF16),
                   jax.ShapeDtypeStruct((bsz * s, LANES), F32)],
        compiler_params=_cparams(("arbitrary", "arbitrary")),
        name="inproj",
    )(x2, g, sh, sc, w, wg)


def _log_sigmoid(x):
    return jnp.minimum(x, 0.0) - jnp.log1p(jnp.exp(-jnp.abs(x)))


def _gates_kernel(g_ref, b_ref, gc_ref, gr_ref):
    tm = g_ref.shape[0]
    row = lax.broadcasted_iota(jnp.int32, (tm, LANES), 0)
    lane = lax.broadcasted_iota(jnp.int32, (tm, LANES), 1)
    gp = GATE_SOFT_CAP * jnp.tanh((g_ref[...] + b_ref[...]) / GATE_SOFT_CAP)
    is_f = ((lane >> 2) & 1) == 1
    fwd_lane = lane < 2 * N_HEADS
    lf = jnp.where(is_f, _log_sigmoid(gp), 0.0)
    r2 = lax.broadcasted_iota(jnp.int32, (tm, tm), 0)
    c2 = lax.broadcasted_iota(jnp.int32, (tm, tm), 1)
    same_chunk = (r2 // CHUNK) == (c2 // CHUNK)
    lower = jnp.logical_and(same_chunk, r2 >= c2).astype(F32)
    upper = jnp.logical_and(same_chunk, r2 <= c2).astype(F32)
    cf = jnp.dot(lower, lf, precision=lax.Precision.HIGHEST, preferred_element_type=F32)
    cb = jnp.dot(upper, lf, precision=lax.Precision.HIGHEST, preferred_element_type=F32)
    cdir = jnp.where(fwd_lane, cf, cb)
    a = jnp.where(is_f, cdir, gp - pltpu.roll(cdir, LANES - N_HEADS, 1))

    pos = row % CHUNK
    x = a
    k = 1
    while k < CHUNK:
        from_before = jnp.where(pos >= k, pltpu.roll(x, k, 0), -jnp.inf)
        from_after = jnp.where(pos < CHUNK - k, pltpu.roll(x, tm - k, 0), -jnp.inf)
        x = jnp.maximum(x, jnp.where(fwd_lane, from_before, from_after))
        k *= 2
    gc_ref[...] = jnp.where(is_f, a, x)

    lane_c = lax.broadcasted_iota(jnp.int32, (CHUNK, LANES), 1)
    lane_1 = lax.broadcasted_iota(jnp.int32, (1, LANES), 1)
    for c in range(tm // CHUNK):
        lo = c * CHUNK
        xc, ac = x[lo:lo + CHUNK], a[lo:lo + CHUNK]
        end_max = jnp.where(lane_1 < 2 * N_HEADS, xc[CHUNK - 1:CHUNK], xc[0:1])
        e = jnp.exp(ac - end_max)
        rows = jnp.where(((lane_c >> 2) & 1) == 1, pltpu.roll(e, N_HEADS, 1), ac)
        gr_ref[:, lo:lo + CHUNK] = rows.T[:N_GATE_COLS, :]


def _gates(gpre, b_if, tm):
    t = gpre.shape[0]
    return pl.pallas_call(
        _gates_kernel,
        grid=(t // tm,),
        in_specs=[pl.BlockSpec((tm, LANES), lambda i: (i, 0)),
                  pl.BlockSpec((1, LANES), lambda i: (0, 0))],
        out_specs=[pl.BlockSpec((tm, LANES), lambda i: (i, 0)),
                   pl.BlockSpec((N_GATE_COLS, tm), lambda i: (0, i))],
        out_shape=[jax.ShapeDtypeStruct((t, LANES), F32),
                   jax.ShapeDtypeStruct((N_GATE_COLS, t), F32)],
        compiler_params=_cparams(("arbitrary",)),
        name="gates",
    )(gpre, b_if)


DVX = DV + LANES


def _mlstm_chunk(q, k, v_ext, rmax_col, b_col, r_row, e_row, b_last, rmax_last, mask, cx, m_st):
    scale = DK ** -0.5
    mb = jnp.maximum(m_st, jnp.broadcast_to(rmax_col, (CHUNK, CHUNK)))
    w_intra = jnp.exp(jnp.where(mask, r_row - mb, -jnp.inf))
    w_state = jnp.exp(m_st - mb)
    qk = lax.dot_general(q, k, (((1,), (1,)), ((), ())), preferred_element_type=F32)
    s = qk * (w_intra * scale)
    lhs = jnp.concatenate([s.astype(BF16), (q.astype(F32) * (w_state * scale)).astype(BF16)], axis=1)
    rhs = jnp.concatenate([v_ext, cx.astype(BF16)], axis=0)
    nx = jnp.dot(lhs, rhs, preferred_element_type=F32)
    denom = jnp.maximum(jnp.abs(nx[:, DV:]), jnp.exp(-(jnp.broadcast_to(b_col, (CHUNK, CHUNK)) + mb)))
    h = nx[:, :DV] / jnp.concatenate([denom, denom], axis=1)
    ke_t = (k.T.astype(F32) * e_row).astype(BF16)
    c_loc = jnp.dot(ke_t, v_ext, preferred_element_type=F32)
    m_loc = b_last + rmax_last
    m_new = jnp.maximum(b_last + m_st, m_loc)
    return h, jnp.exp(b_last + m_st - m_new) * cx + jnp.exp(m_loc - m_new) * c_loc, m_new


def _mlstm_kernel(qf_ref, kf_ref, vf_ref, gcf_ref, grf_ref, qb_ref, kb_ref, vb_ref, gcb_ref, grb_ref,
                  c0_ref, m0_ref, hf_ref, hb_ref, cout_ref, mout_ref, m_scr, *c_scrs):
    c = pl.program_id(1)

    @pl.when(c == 0)
    def _():
        for idx, c_scr in enumerate(c_scrs):
            c_scr[...] = c0_ref[idx]
        m_scr[...] = m0_ref[...]

    row = lax.broadcasted_iota(jnp.int32, (CHUNK, CHUNK), 0)
    col = lax.broadcasted_iota(jnp.int32, (CHUNK, CHUNK), 1)
    ones = jnp.ones((CHUNK, LANES), BF16)
    m_all = m_scr[...]
    dirs = ((qf_ref, kf_ref, vf_ref, gcf_ref, grf_ref, hf_ref, 0, CHUNK - 1, col <= row),
            (qb_ref, kb_ref, vb_ref, gcb_ref, grb_ref, hb_ref, 2 * N_HEADS, 0, col >= row))
    m_news = []
    for di, (q_ref, k_ref, v_ref, gc_ref, gr_ref, h_ref, off, last, mask) in enumerate(dirs):
        for hd in range(N_HEADS):
            idx = di * N_HEADS + hd
            lr, lb = off + hd, off + N_HEADS + hd
            v_ext = jnp.concatenate([v_ref[:, hd * DV:(hd + 1) * DV], ones], axis=1)
            h, c_new, m_new = _mlstm_chunk(
                q_ref[:, hd * DK:(hd + 1) * DK], k_ref[:, hd * DK:(hd + 1) * DK], v_ext,
                gc_ref[:, lr:lr + 1], gc_ref[:, lb:lb + 1], gr_ref[lr:lr + 1, :], gr_ref[lb:lb + 1, :],
                gc_ref[last:last + 1, lb:lb + 1], gc_ref[last:last + 1, lr:lr + 1], mask,
                c_scrs[idx][...], m_all[idx][:, 0:1])
            h_ref[:, hd * DV:(hd + 1) * DV] = h
            c_scrs[idx][...] = c_new
            m_news.append(jnp.broadcast_to(m_new, (1, LANES)))
    for idx, m_new in enumerate(m_news):
        m_scr[idx] = m_new

    @pl.when(c == pl.num_programs(1) - 1)
    def _():
        for idx, c_scr in enumerate(c_scrs):
            cout_ref[idx] = c_scr[...]
        mout_ref[...] = m_scr[...]


def _mlstm(proj, gcol, grow, bsz, s, c0, m0):
    nc = s // CHUNK
    t = bsz * s
    fwd = lambda b, c: b * nc + c
    bwd = lambda b, c: b * nc + (nc - 1 - c)

    def specs(ci):
        return [pl.BlockSpec((CHUNK, QK_COLS), lambda b, c: (ci(b, c), 0)),
                pl.BlockSpec((CHUNK, QK_COLS), lambda b, c: (ci(b, c), 1)),
                pl.BlockSpec((CHUNK, MLSTM_WIDTH), lambda b, c: (ci(b, c), 1)),
                pl.BlockSpec((CHUNK, LANES), lambda b, c: (ci(b, c), 0)),
                pl.BlockSpec((N_GATE_COLS, CHUNK), lambda b, c: (0, ci(b, c)))]

    st_specs = [pl.BlockSpec((None, 2 * N_HEADS, DK, DVX), lambda b, c: (b, 0, 0, 0)),
                pl.BlockSpec((None, 2 * N_HEADS, 1, LANES), lambda b, c: (b, 0, 0, 0))]
    return pl.pallas_call(
        _mlstm_kernel,
        grid=(bsz, nc),
        in_specs=specs(fwd) + specs(bwd) + st_specs,
        out_specs=[pl.BlockSpec((CHUNK, MLSTM_WIDTH), lambda b, c: (fwd(b, c), 0)),
                   pl.BlockSpec((CHUNK, MLSTM_WIDTH), lambda b, c: (bwd(b, c), 0))] + st_specs,
        out_shape=[jax.ShapeDtypeStruct((t, MLSTM_WIDTH), F32),
                   jax.ShapeDtypeStruct((t, MLSTM_WIDTH), F32),
                   jax.ShapeDtypeStruct(c0.shape, F32),
                   jax.ShapeDtypeStruct(m0.shape, F32)],
        scratch_shapes=[pltpu.VMEM((2 * N_HEADS, 1, LANES), F32)]
        + [pltpu.VMEM((DK, DVX), F32) for _ in range(2 * N_HEADS)],
        compiler_params=_cparams(("arbitrary", "arbitrary")),
        name="mlstm",
    )(proj, proj, proj, gcol, grow, proj, proj, proj, gcol, grow, c0, m0)


MIX_ROWS = 256


def _mixout_kernel(o_ref, cb_ref, cc_ref, cx_ref, ccp_ref, cxp_ref, ccn_ref, cxn_ref, hf_ref, hb_ref, x_ref,
                   cw_ref, ng_ref, wout_ref, gt_ref, gffn_ref, shf_ref, scf_ref, wr_ref, br_ref,
                   x1_ref, xn2_ref, idx_ref, tw_ref):
    i = pl.program_id(1)
    tm = x_ref.shape[0]
    cw = cw_ref[...]

    has_prev = jnp.where(i > 0, 1.0, 0.0)
    has_next = jnp.where(i < pl.num_programs(1) - 1, 1.0, 0.0)
    up = ccp_ref[...].astype(F32) * cxp_ref[...].astype(F32) * has_prev
    un = ccn_ref[...].astype(F32) * cxn_ref[...].astype(F32) * has_next
    uv = cc_ref[:, CONV_HALF:].astype(F32) * cx_ref[:, CONV_HALF:].astype(F32)
    ext = jnp.concatenate([up, uv, un], axis=0)

    pos = lax.broadcasted_iota(jnp.int32, (MIX_ROWS, CONV_HALF), 0) & (GRID_W - 1)
    lane_f = lax.broadcasted_iota(jnp.int32, (MIX_ROWS, LANES), 1).astype(F32)
    lane4 = lax.broadcasted_iota(jnp.int32, (MIX_ROWS, TOP_K), 1)

    for r0 in range(0, tm, MIX_ROWS):
        rows = slice(r0, r0 + MIX_ROWS)

        uh = cc_ref[rows, :CONV_HALF].astype(F32) * cx_ref[rows, :CONV_HALF].astype(F32)
        left = jnp.where(pos == 0, 0.0, pltpu.roll(uh, 1, 0))
        right = jnp.where(pos == GRID_W - 1, 0.0, pltpu.roll(uh, MIX_ROWS - 1, 0))
        yh = cw[0:1, :CONV_HALF] * left + cw[1:2, :CONV_HALF] * uh + cw[2:3, :CONV_HALF] * right
        yv = (cw[0:1, CONV_HALF:] * ext[r0:r0 + MIX_ROWS]
              + cw[1:2, CONV_HALF:] * ext[r0 + GRID_W:r0 + GRID_W + MIX_ROWS]
              + cw[2:3, CONV_HALF:] * ext[r0 + 2 * GRID_W:r0 + 2 * GRID_W + MIX_ROWS])
        yc = cb_ref[rows, :].astype(F32) * jnp.concatenate([yh, yv], axis=1)

        hs = hf_ref[rows, :] + hb_ref[rows, :]
        parts = []
        for hd in range(N_HEADS):
            seg = hs[:, hd * DV:(hd + 1) * DV]
            parts.append(seg * lax.rsqrt(jnp.mean(seg * seg, axis=-1, keepdims=True) + EPS))
        hm = jnp.concatenate(parts, axis=1) * ng_ref[...] * jax.nn.sigmoid(o_ref[rows, :].astype(F32))

        z = jnp.concatenate([hm.astype(BF16), yc.astype(BF16)], axis=1)
        x1 = x_ref[rows, :] + gt_ref[...] * jnp.dot(z, wout_ref[...], preferred_element_type=F32)
        x1_ref[rows, :] = x1

        y = x1 * lax.rsqrt(jnp.mean(x1 * x1, axis=-1, keepdims=True) + EPS) * gffn_ref[...]
        xn2 = y * (1.0 + scf_ref[...]) + shf_ref[...]
        xn2_ref[rows, :] = xn2

        logits = jnp.dot(xn2.astype(BF16), wr_ref[...], preferred_element_type=F32) + br_ref[...]
        vals, idxs = [], []
        for _ in range(TOP_K):
            mx = jnp.max(logits, axis=-1, keepdims=True)
            ik = jnp.min(jnp.where(logits == mx, lane_f, float(LANES)), axis=-1, keepdims=True)
            vals.append(mx)
            idxs.append(ik)
            logits = jnp.where(lane_f == ik, -jnp.inf, logits)
        es = [jnp.exp(v - vals[0]) for v in vals]
        tot = es[0] + es[1] + es[2] + es[3]
        idx_out = jnp.zeros((MIX_ROWS, TOP_K), F32)
        tw_out = jnp.zeros((MIX_ROWS, TOP_K), F32)
        for kk in range(TOP_K):
            idx_out = jnp.where(lane4 == kk, idxs[kk], idx_out)
            tw_out = jnp.where(lane4 == kk, es[kk] / tot, tw_out)
        idx_ref[rows, :] = idx_out.astype(jnp.int32)
        tw_ref[rows, :] = tw_out


def _mixout(proj, hf, hb, x2, conv_w, norm_g, w_out, gt, g_ffn, sh_f, sc_f, w_r, b_r, bsz, s, tm):
    t, d = x2.shape
    nt = s // tm
    rb = tm // GRID_W
    last_rb = t // GRID_W - 1
    row = lambda b, i: b * nt + i
    w = MLSTM_WIDTH
    vec = lambda n: pl.BlockSpec((1, n), lambda b, i: (0, 0))
    per_b = pl.BlockSpec((None, 1, d), lambda b, i: (b, 0, 0))
    halo_prev = lambda cblk: pl.BlockSpec(
        (GRID_W, CONV_HALF), lambda b, i: (jnp.maximum(row(b, i) * rb - 1, 0), cblk))
    halo_next = lambda cblk: pl.BlockSpec(
        (GRID_W, CONV_HALF), lambda b, i: (jnp.minimum((row(b, i) + 1) * rb, last_rb), cblk))
    return pl.pallas_call(
        _mixout_kernel,
        grid=(bsz, nt),
        in_specs=[pl.BlockSpec((tm, w), lambda b, i: (row(b, i), 2)),
                  pl.BlockSpec((tm, w), lambda b, i: (row(b, i), 3)),
                  pl.BlockSpec((tm, w), lambda b, i: (row(b, i), 4)),
                  pl.BlockSpec((tm, w), lambda b, i: (row(b, i), 5)),
                  halo_prev(9), halo_prev(11), halo_next(9), halo_next(11),
                  pl.BlockSpec((tm, w), lambda b, i: (row(b, i), 0)),
                  pl.BlockSpec((tm, w), lambda b, i: (row(b, i), 0)),
                  pl.BlockSpec((tm, d), lambda b, i: (row(b, i), 0)),
                  pl.BlockSpec((3, CONV_WIDTH), lambda b, i: (0, 0)),
                  vec(w),
                  pl.BlockSpec((d, d), lambda b, i: (0, 0)),
                  per_b, vec(d), per_b, per_b,
                  pl.BlockSpec((d, LANES), lambda b, i: (0, 0)),
                  vec(LANES)],
        out_specs=[pl.BlockSpec((tm, d), lambda b, i: (row(b, i), 0)),
                   pl.BlockSpec((tm, d), lambda b, i: (row(b, i), 0)),
                   pl.BlockSpec((tm, TOP_K), lambda b, i: (row(b, i), 0)),
                   pl.BlockSpec((tm, TOP_K), lambda b, i: (row(b, i), 0))],
        out_shape=[jax.ShapeDtypeStruct((t, d), F32),
                   jax.ShapeDtypeStruct((t, d), F32),
                   jax.ShapeDtypeStruct((t, TOP_K), jnp.int32),
                   jax.ShapeDtypeStruct((t, TOP_K), F32)],
        compiler_params=_cparams(("arbitrary", "arbitrary")),
        name="mixout",
    )(proj, proj, proj, proj, proj, proj, proj, proj, hf, hb, x2,
      conv_w, norm_g, w_out, gt, g_ffn, sh_f, sc_f, w_r, b_r)


def _rank_kernel(idx_ref, rank_ref, cnt_ref, run_scr):
    @pl.when(pl.program_id(0) == 0)
    def _():
        run_scr[...] = jnp.zeros_like(run_scr)

    tm = idx_ref.shape[0]
    idx = idx_ref[...]
    lane = lax.broadcasted_iota(jnp.int32, (tm, LANES), 1)
    hits = [lane == idx[:, kk:kk + 1] for kk in range(TOP_K)]
    onehot = jnp.zeros((tm, LANES), F32)
    for hit in hits:
        onehot = onehot + hit.astype(F32)
    r = lax.broadcasted_iota(jnp.int32, (tm, tm), 0)
    c = lax.broadcasted_iota(jnp.int32, (tm, tm), 1)
    before = jnp.dot((c < r).astype(BF16), onehot.astype(BF16), preferred_element_type=F32) + run_scr[...]
    lane4 = lax.broadcasted_iota(jnp.int32, (tm, TOP_K), 1)
    rank = jnp.zeros((tm, TOP_K), F32)
    for kk, hit in enumerate(hits):
        rk = jnp.sum(jnp.where(hit, before, 0.0), axis=-1, keepdims=True)
        rank = jnp.where(lane4 == kk, rk, rank)
    rank_ref[...] = rank.astype(jnp.int32)
    run_scr[...] = run_scr[...] + jnp.sum(onehot, axis=0, keepdims=True)
    cnt_ref[...] = run_scr[...]


def _rank(idx, tm):
    t = idx.shape[0]
    return pl.pallas_call(
        _rank_kernel,
        grid=(t // tm,),
        in_specs=[pl.BlockSpec((tm, TOP_K), lambda i: (i, 0))],
        out_specs=[pl.BlockSpec((tm, TOP_K), lambda i: (i, 0)),
                   pl.BlockSpec((1, LANES), lambda i: (0, 0))],
        out_shape=[jax.ShapeDtypeStruct((t, TOP_K), jnp.int32),
                   jax.ShapeDtypeStruct((1, LANES), F32)],
        scratch_shapes=[pltpu.VMEM((1, LANES), F32)],
        compiler_params=_cparams(("arbitrary",)),
        name="rank",
    )(idx)


def _dispatch_kernel(pad_ref, dest_hbm, xn_ref, xs_hbm, dsm, zeros_scr, sem_idx, sem_rows, sem_pad):
    i = pl.program_id(0)
    tm = xn_ref.shape[0]
    n_idx = tm * TOP_K
    idx_copy = pltpu.make_async_copy(dest_hbm.at[pl.ds(i * n_idx, n_idx)], dsm, sem_idx)
    idx_copy.start()

    def pad_copy(off, size):
        return pltpu.make_async_copy(zeros_scr.at[pl.ds(0, size), :], xs_hbm.at[pl.ds(off, size), :], sem_pad)

    def for_each_pad_piece(fn):
        def per_expert(e, carry):
            off = pad_ref[2 * e]
            n = pad_ref[2 * e + 1]
            head = n & (SUBLANES - 1)
            for r in range(SUBLANES - 1):
                @pl.when(r < head)
                def _(r=r):
                    fn(pad_copy(off + r, 1))

            off = off + head
            size = ROW_BLOCK // 2
            while size >= SUBLANES:
                take = (n & size) != 0

                @pl.when(take)
                def _(off=off, size=size):
                    fn(pad_copy(pl.multiple_of(off, SUBLANES), size))

                off = off + jnp.where(take, size, 0)
                size //= 2
            return carry
        lax.fori_loop(0, N_EXPERTS, per_expert, 0)

    @pl.when(i == 0)
    def _():
        zeros_scr[...] = jnp.zeros_like(zeros_scr)
        for_each_pad_piece(lambda cp: cp.start())
        for_each_pad_piece(lambda cp: cp.wait())

    idx_copy.wait()

    def row_copy(t, kk):
        return pltpu.make_async_copy(xn_ref.at[pl.ds(t, 1), :],
                                     xs_hbm.at[pl.ds(dsm[t * TOP_K + kk], 1), :], sem_rows)

    def issue(t, carry):
        for kk in range(TOP_K):
            row_copy(t, kk).start()
        return carry

    lax.fori_loop(0, tm, issue, 0)
    pltpu.make_async_copy(xs_hbm.at[pl.ds(0, n_idx), :], xs_hbm.at[pl.ds(0, n_idx), :], sem_rows).wait()


def _dispatch(pad_info, dest_flat, xn2, n_rows, tm):
    t, d = xn2.shape
    return pl.pallas_call(
        _dispatch_kernel,
        grid_spec=pltpu.PrefetchScalarGridSpec(
            num_scalar_prefetch=1,
            grid=(t // tm,),
            in_specs=[pl.BlockSpec(memory_space=pl.ANY),
                      pl.BlockSpec((tm, d), lambda i, pad: (i, 0))],
            out_specs=pl.BlockSpec(memory_space=pl.ANY),
            scratch_shapes=[pltpu.SMEM((tm * TOP_K,), jnp.int32),
                            pltpu.VMEM((ROW_BLOCK // 2, d), F32),
                            pltpu.SemaphoreType.DMA(()),
                            pltpu.SemaphoreType.DMA(()),
                            pltpu.SemaphoreType.DMA(())]),
        out_shape=jax.ShapeDtypeStruct((n_rows, d), F32),
        compiler_params=_cparams(("arbitrary",)),
        name="dispatch",
    )(pad_info, dest_flat, xn2)


def _new_expert(be_ref, j):
    return jnp.logical_or(j == 0, be_ref[j] != be_ref[jnp.maximum(j - 1, 0)])


def _expert_gu_kernel(be_ref, nu_ref, xs_ref, wg_ref, wu_ref, bg_ref, bu_ref, act_ref, wg_scr, wu_scr):
    j = pl.program_id(1)

    @pl.when(j < nu_ref[0])
    def _():
        @pl.when(_new_expert(be_ref, j))
        def _():
            wg_scr[...] = wg_ref[...].astype(BF16)
            wu_scr[...] = wu_ref[...].astype(BF16)

        x = xs_ref[...].astype(BF16)
        g = jnp.dot(x, wg_scr[...], preferred_element_type=F32) + bg_ref[...]
        u = jnp.dot(x, wu_scr[...], preferred_element_type=F32) + bu_ref[...]
        gate = jnp.minimum(g, SWIGLU_LIMIT)
        up = jnp.clip(u, -SWIGLU_LIMIT, SWIGLU_LIMIT)
        act_ref[...] = ((up + 1.0) * gate * jax.nn.sigmoid(SWIGLU_ALPHA * gate)).astype(BF16)


def _expert_gu(blk_expert, n_used, xs, w_gu, b_gu, tn):
    n_rows, d = xs.shape
    dff = w_gu.shape[2] // 2
    nt = dff // tn
    nb = n_rows // ROW_BLOCK
    blk = lambda j, nu: jnp.minimum(j, nu[0] - 1)
    exp = lambda j, be, nu: be[blk(j, nu)]
    return pl.pallas_call(
        _expert_gu_kernel,
        grid_spec=pltpu.PrefetchScalarGridSpec(
            num_scalar_prefetch=2,
            grid=(nt, nb),
            in_specs=[pl.BlockSpec((ROW_BLOCK, d), lambda n, j, be, nu: (blk(j, nu), 0)),
                      pl.BlockSpec((None, d, tn), lambda n, j, be, nu: (exp(j, be, nu), 0, n)),
                      pl.BlockSpec((None, d, tn), lambda n, j, be, nu: (exp(j, be, nu), 0, nt + n)),
                      pl.BlockSpec((None, 1, tn), lambda n, j, be, nu: (exp(j, be, nu), 0, n)),
                      pl.BlockSpec((None, 1, tn), lambda n, j, be, nu: (exp(j, be, nu), 0, nt + n))],
            out_specs=pl.BlockSpec((ROW_BLOCK, tn), lambda n, j, be, nu: (blk(j, nu), n)),
            scratch_shapes=[pltpu.VMEM((d, tn), BF16), pltpu.VMEM((d, tn), BF16)]),
        out_shape=jax.ShapeDtypeStruct((n_rows, dff), BF16),
        compiler_params=_cparams(("arbitrary", "arbitrary")),
        name="expert_gu",
    )(blk_expert, n_used, xs, w_gu, w_gu, b_gu, b_gu)


def _expert_down_kernel(be_ref, nu_ref, act_ref, w_ref, b_ref, y_ref, w_scr):
    j = pl.program_id(1)

    @pl.when(j < nu_ref[0])
    def _():
        @pl.when(_new_expert(be_ref, j))
        def _():
            w_scr[...] = w_ref[...].astype(BF16)

        y_ref[...] = jnp.dot(act_ref[...], w_scr[...], preferred_element_type=F32) + b_ref[...]


def _expert_down(blk_expert, n_used, act, w_down, b_down, tn):
    n_rows, dff = act.shape
    d = w_down.shape[2]
    nb = n_rows // ROW_BLOCK
    blk = lambda j, nu: jnp.minimum(j, nu[0] - 1)
    exp = lambda j, be, nu: be[blk(j, nu)]
    return pl.pallas_call(
        _expert_down_kernel,
        grid_spec=pltpu.PrefetchScalarGridSpec(
            num_scalar_prefetch=2,
            grid=(d // tn, nb),
            in_specs=[pl.BlockSpec((ROW_BLOCK, dff), lambda n, j, be, nu: (blk(j, nu), 0)),
                      pl.BlockSpec((None, dff, tn), lambda n, j, be, nu: (exp(j, be, nu), 0, n)),
                      pl.BlockSpec((None, 1, tn), lambda n, j, be, nu: (exp(j, be, nu), 0, n))],
            out_specs=pl.BlockSpec((ROW_BLOCK, tn), lambda n, j, be, nu: (blk(j, nu), n)),
            scratch_shapes=[pltpu.VMEM((dff, tn), BF16)]),
        out_shape=jax.ShapeDtypeStruct((n_rows, d), F32),
        compiler_params=_cparams(("arbitrary", "arbitrary")),
        name="expert_down",
    )(blk_expert, n_used, act, w_down, b_down)


def _combine_kernel(dest_hbm, y_hbm, x1_ref, tw_ref, gt_ref, gfin_ref, out_ref, dsm, buf, sem_idx, sem_rows):
    i = pl.program_id(1) + pl.program_id(0) * pl.num_programs(1)
    tm = x1_ref.shape[0]
    n_idx = tm * TOP_K
    idx_copy = pltpu.make_async_copy(dest_hbm.at[pl.ds(i * n_idx, n_idx)], dsm, sem_idx)
    idx_copy.start()
    idx_copy.wait()

    def row_copy(t, kk):
        return pltpu.make_async_copy(y_hbm.at[pl.ds(dsm[t * TOP_K + kk], 1), :],
                                     buf.at[kk, pl.ds(t, 1), :], sem_rows)

    def issue(t, carry):
        for kk in range(TOP_K):
            row_copy(t, kk).start()
        return carry

    lax.fori_loop(0, tm, issue, 0)
    for kk in range(TOP_K):
        pltpu.make_async_copy(y_hbm.at[pl.ds(0, tm), :], buf.at[kk], sem_rows).wait()

    tw = tw_ref[...]
    acc = buf[0] * tw[:, 0:1]
    for kk in range(1, TOP_K):
        acc = acc + buf[kk] * tw[:, kk:kk + 1]
    x2 = x1_ref[...] + gt_ref[...] * acc
    out_ref[...] = x2 * lax.rsqrt(jnp.mean(x2 * x2, axis=-1, keepdims=True) + EPS) * gfin_ref[...]


def _combine(dest_flat, y, x1, tw, gt, g_final, bsz, s, tm):
    t, d = x1.shape
    nt = s // tm
    row = lambda b, i: (b * nt + i, 0)
    return pl.pallas_call(
        _combine_kernel,
        grid=(bsz, nt),
        in_specs=[pl.BlockSpec(memory_space=pl.ANY),
                  pl.BlockSpec(memory_space=pl.ANY),
                  pl.BlockSpec((tm, d), row),
                  pl.BlockSpec((tm, TOP_K), row),
                  pl.BlockSpec((None, 1, d), lambda b, i: (b, 0, 0)),
                  pl.BlockSpec((1, d), lambda b, i: (0, 0))],
        out_specs=pl.BlockSpec((tm, d), row),
        out_shape=jax.ShapeDtypeStruct((t, d), F32),
        scratch_shapes=[pltpu.SMEM((tm * TOP_K,), jnp.int32),
                        pltpu.VMEM((TOP_K, tm, d), F32),
                        pltpu.SemaphoreType.DMA(()),
                        pltpu.SemaphoreType.DMA(())],
        compiler_params=_cparams(("arbitrary", "arbitrary")),
        name="combine",
    )(dest_flat, y, x1, tw, gt, g_final)


def _pad_lanes(a, value=0.0):
    return jnp.pad(a, ((0, 0), (0, LANES - a.shape[1])), constant_values=value)


def _routing_tables(idx, rank, counts_f, n_blocks):
    counts = counts_f[0, :N_EXPERTS].astype(jnp.int32)
    padded = (counts + ROW_BLOCK - 1) // ROW_BLOCK * ROW_BLOCK
    pend = jnp.cumsum(padded)
    pstart = pend - padded
    dest = (pstart[idx] + rank).reshape(-1)
    blk_start = jnp.arange(n_blocks, dtype=jnp.int32) * ROW_BLOCK
    blk_expert = jnp.minimum(jnp.sum((pend[None, :] <= blk_start[:, None]).astype(jnp.int32), axis=1),
                             N_EXPERTS - 1)
    n_used = (pend[-1:] // ROW_BLOCK).astype(jnp.int32)
    pad_info = jnp.stack([pstart + counts, padded - counts], axis=1).reshape(-1).astype(jnp.int32)
    return dest, blk_expert, n_used, pad_info


def _layer(x, c, ctx, c_ctx, w_ada, b_ada, g_mix, w_in, b_if, conv_w, norm_g, w_out,
           g_ffn, w_router, b_router, w_gu, b_gu, w_down, b_down, g_final):
    bsz, s, d = x.shape
    s_ctx = ctx.shape[1]

    cond = jnp.zeros((8, d), F32).at[:bsz].set(c).at[bsz].set(c_ctx)
    mod = _adaln(cond, w_ada, b_ada[None, :])
    sh_m, sc_m, gt_m, sh_f, sc_f, gt_f = [m[:, None, :] for m in jnp.split(mod, N_MOD, axis=-1)]
    lat = lambda m: m[:bsz]
    ctxm = lambda m: jnp.broadcast_to(m[bsz:bsz + 1], (bsz, 1, d))

    g0 = 2 * QK_COLS + 2 * MLSTM_WIDTH
    w_main = jnp.concatenate([w_in[:, :g0], w_in[:, g0 + N_GATE_COLS:]], axis=1).astype(BF16)
    w_gate = _pad_lanes(w_in[:, g0:g0 + N_GATE_COLS]).astype(BF16)
    b_gate = _pad_lanes(b_if[None, :])
    g_mix2 = g_mix[None, :]

    proj_c, gpre_c = _inproj(ctx, g_mix2, ctxm(sh_m), ctxm(sc_m), w_main, w_gate, min(s_ctx, 512))
    gcol_c, grow_c = _gates(gpre_c, b_gate, 512)
    zeros_state = (jnp.zeros((bsz, 2 * N_HEADS, DK, DVX), F32),
                   jnp.zeros((bsz, 2 * N_HEADS, 1, LANES), F32))
    _, _, c0, m0 = _mlstm(proj_c, gcol_c, grow_c, bsz, s_ctx, *zeros_state)

    proj, gpre = _inproj(x, g_mix2, lat(sh_m), lat(sc_m), w_main, w_gate, 512)
    gcol, grow = _gates(gpre, b_gate, 512)
    hf, hb, _, _ = _mlstm(proj, gcol, grow, bsz, s, c0, m0)
    x1, xn2, idx, tw = _mixout(
        proj, hf, hb, x.reshape(bsz * s, d), conv_w, norm_g[None, :], w_out.astype(BF16), lat(gt_m),
        g_ffn[None, :], lat(sh_f), lat(sc_f), _pad_lanes(w_router).astype(BF16),
        _pad_lanes(b_router[None, :], NEG_BIG), bsz, s, 512)

    t = bsz * s
    n_blocks = -(-(t * TOP_K) // ROW_BLOCK) + N_EXPERTS
    rank, counts = _rank(idx, 512)
    dest, blk_expert, n_used, pad_info = _routing_tables(idx, rank, counts, n_blocks)
    xs = _dispatch(pad_info, dest, xn2, n_blocks * ROW_BLOCK, 512)
    act = _expert_gu(blk_expert, n_used, xs, w_gu, b_gu[:, None, :], 512)
    y = _expert_down(blk_expert, n_used, act, w_down, b_down[:, None, :], 1024)
    out = _combine(dest, y, x1, tw, lat(gt_f), g_final[None, :], bsz, s, 256)
    return out.reshape(bsz, s, d)


def kernel(x, c, ctx, c_ctx, w_ada, b_ada, g_mix, w_in, b_if, conv_w, mlstm_norm_g, w_out,
           g_ffn, w_router, b_router, w_gu, b_gu, w_down, b_down, g_final):
    return _layer(x, c, ctx, c_ctx, w_ada[0], b_ada[0], g_mix[0], w_in[0], b_if[0], conv_w[0],
                  mlstm_norm_g[0], w_out[0], g_ffn[0], w_router[0], b_router[0], w_gu[0], b_gu[0],
                  w_down[0], b_down[0], g_final)
```

```python
import functools

import jax
import jax.numpy as jnp
from jax import lax
from jax.experimental import pallas as pl
from jax.experimental.pallas import tpu as pltpu

F32 = jnp.float32
BF16 = jnp.bfloat16

N_HEADS = 4
DK = 128
DV = 256
QK_COLS = N_HEADS * DK
MLSTM_WIDTH = N_HEADS * DV
CONV_WIDTH = 1024
CONV_HALF = CONV_WIDTH // 2
N_GATE_COLS = 4 * N_HEADS
GRID_W = 64
CHUNK = 128
GATE_SOFT_CAP = 15.0
N_EXPERTS = 32
TOP_K = 4
SWIGLU_LIMIT = 7.0
SWIGLU_ALPHA = 1.702
N_MOD = 6
EPS = 1e-6
LANES = 128
SUBLANES = 8
ROW_BLOCK = 1024
NEG_BIG = -1e30
VMEM_LIMIT = 56 * 1024 * 1024


def _cparams(sem):
    return pltpu.CompilerParams(dimension_semantics=sem, vmem_limit_bytes=VMEM_LIMIT)


def _adaln_kernel(c_ref, w_ref, b_ref, o_ref):
    s = c_ref[...]
    s = s * jax.nn.sigmoid(s)
    o_ref[...] = jnp.dot(s.astype(BF16), w_ref[...].astype(BF16),
                         preferred_element_type=F32) + b_ref[...]


def _adaln(cond, w, b):
    d, n = w.shape
    tn = 1024
    return pl.pallas_call(
        _adaln_kernel,
        grid=(n // tn,),
        in_specs=[pl.BlockSpec((8, d), lambda j: (0, 0)),
                  pl.BlockSpec((d, tn), lambda j: (0, j)),
                  pl.BlockSpec((1, tn), lambda j: (0, j))],
        out_specs=pl.BlockSpec((8, tn), lambda j: (0, j)),
        out_shape=jax.ShapeDtypeStruct((8, n), F32),
        compiler_params=_cparams(("arbitrary",)),
        name="adaln",
    )(cond, w, b)


INPROJ_COLS = 1024


def _inproj_kernel(x_ref, g_ref, sh_ref, sc_ref, w_ref, wg_ref, proj_ref, gate_ref):
    x = x_ref[...]
    y = x * lax.rsqrt(jnp.mean(x * x, axis=-1, keepdims=True) + EPS) * g_ref[...]
    xn = (y * (1.0 + sc_ref[...]) + sh_ref[...]).astype(BF16)
    gate_ref[...] = jnp.dot(xn, wg_ref[...], preferred_element_type=F32)
    for j in range(w_ref.shape[1] // INPROJ_COLS):
        cols = slice(j * INPROJ_COLS, (j + 1) * INPROJ_COLS)
        proj_ref[:, cols] = jnp.dot(xn, w_ref[:, cols], preferred_element_type=F32).astype(BF16)


def _inproj(x, g, sh, sc, w, wg, tm):
    bsz, s, d = x.shape
    p = w.shape[1]
    nt = s // tm
    x2 = x.reshape(bsz * s, d)
    resident = lambda shape: pl.BlockSpec(shape, lambda b, i: (0, 0), pipeline_mode=pl.Buffered(1))
    return pl.pallas_call(
        _inproj_kernel,
        grid=(bsz, nt),
        in_specs=[pl.BlockSpec((tm, d), lambda b, i: (b * nt + i, 0)),
                  pl.BlockSpec((1, d), lambda b, i: (0, 0)),
                  pl.BlockSpec((None, 1, d), lambda b, i: (b, 0, 0)),
                  pl.BlockSpec((None, 1, d), lambda b, i: (b, 0, 0)),
                  resident((d, p)),
                  resident((d, LANES))],
        out_specs=[pl.BlockSpec((tm, p), lambda b, i: (b * nt + i, 0)),
                   pl.BlockSpec((tm, LANES), lambda b, i: (b * nt + i, 0))],
        out_shape=[jax.ShapeDtypeStruct((bsz * s, p), BF16),
                   jax.ShapeDtypeStruct((bsz * s, LANES), F32)],
        compiler_params=_cparams(("arbitrary", "arbitrary")),
        name="inproj",
    )(x2, g, sh, sc, w, wg)


def _log_sigmoid(x):
    return jnp.minimum(x, 0.0) - jnp.log1p(jnp.exp(-jnp.abs(x)))


def _gates_kernel(g_ref, b_ref, gc_ref, gr_ref):
    tm = g_ref.shape[0]
    row = lax.broadcasted_iota(jnp.int32, (tm, LANES), 0)
    lane = lax.broadcasted_iota(jnp.int32, (tm, LANES), 1)
    gp = GATE_SOFT_CAP * jnp.tanh((g_ref[...] + b_ref[...]) / GATE_SOFT_CAP)
    is_f = ((lane >> 2) & 1) == 1
    fwd_lane = lane < 2 * N_HEADS
    lf = jnp.where(is_f, _log_sigmoid(gp), 0.0)
    r2 = lax.broadcasted_iota(jnp.int32, (tm, tm), 0)
    c2 = lax.broadcasted_iota(jnp.int32, (tm, tm), 1)
    same_chunk = (r2 // CHUNK) == (c2 // CHUNK)
    lower = jnp.logical_and(same_chunk, r2 >= c2).astype(F32)
    upper = jnp.logical_and(same_chunk, r2 <= c2).astype(F32)
    cf = jnp.dot(lower, lf, precision=lax.Precision.HIGHEST, preferred_element_type=F32)
    cb = jnp.dot(upper, lf, precision=lax.Precision.HIGHEST, preferred_element_type=F32)
    cdir = jnp.where(fwd_lane, cf, cb)
    a = jnp.where(is_f, cdir, gp - pltpu.roll(cdir, LANES - N_HEADS, 1))

    pos = row % CHUNK
    x = a
    k = 1
    while k < CHUNK:
        from_before = jnp.where(pos >= k, pltpu.roll(x, k, 0), -jnp.inf)
        from_after = jnp.where(pos < CHUNK - k, pltpu.roll(x, tm - k, 0), -jnp.inf)
        x = jnp.maximum(x, jnp.where(fwd_lane, from_before, from_after))
        k *= 2
    gc_ref[...] = jnp.where(is_f, a, x)

    lane_c = lax.broadcasted_iota(jnp.int32, (CHUNK, LANES), 1)
    lane_1 = lax.broadcasted_iota(jnp.int32, (1, LANES), 1)
    for c in range(tm // CHUNK):
        lo = c * CHUNK
        xc, ac = x[lo:lo + CHUNK], a[lo:lo + CHUNK]
        end_max = jnp.where(lane_1 < 2 * N_HEADS, xc[CHUNK - 1:CHUNK], xc[0:1])
        e = jnp.exp(ac - end_max)
        rows = jnp.where(((lane_c >> 2) & 1) == 1, pltpu.roll(e, N_HEADS, 1), ac)
        gr_ref[:, lo:lo + CHUNK] = rows.T[:N_GATE_COLS, :]


def _gates(gpre, b_if, tm):
    t = gpre.shape[0]
    return pl.pallas_call(
        _gates_kernel,
        grid=(t // tm,),
        in_specs=[pl.BlockSpec((tm, LANES), lambda i: (i, 0)),
                  pl.BlockSpec((1, LANES), lambda i: (0, 0))],
        out_specs=[pl.BlockSpec((tm, LANES), lambda i: (i, 0)),
                   pl.BlockSpec((N_GATE_COLS, tm), lambda i: (0, i))],
        out_shape=[jax.ShapeDtypeStruct((t, LANES), F32),
                   jax.ShapeDtypeStruct((N_GATE_COLS, t), F32)],
        compiler_params=_cparams(("arbitrary",)),
        name="gates",
    )(gpre, b_if)


DVX = DV + LANES


def _mlstm_chunk(q, k, v_ext, rmax_col, b_col, r_row, e_row, b_last, rmax_last, mask, cx, m_st):
    scale = DK ** -0.5
    mb = jnp.maximum(m_st, jnp.broadcast_to(rmax_col, (CHUNK, CHUNK)))
    w_intra = jnp.exp(jnp.where(mask, r_row - mb, -jnp.inf))
    w_state = jnp.exp(m_st - mb)
    qk = lax.dot_general(q, k, (((1,), (1,)), ((), ())), preferred_element_type=F32)
    s = qk * (w_intra * scale)
    lhs = jnp.concatenate([s.astype(BF16), (q.astype(F32) * (w_state * scale)).astype(BF16)], axis=1)
    rhs = jnp.concatenate([v_ext, cx.astype(BF16)], axis=0)
    nx = jnp.dot(lhs, rhs, preferred_element_type=F32)
    denom = jnp.maximum(jnp.abs(nx[:, DV:]), jnp.exp(-(jnp.broadcast_to(b_col, (CHUNK, CHUNK)) + mb)))
    h = nx[:, :DV] / jnp.concatenate([denom, denom], axis=1)
    ke_t = (k.T.astype(F32) * e_row).astype(BF16)
    c_loc = jnp.dot(ke_t, v_ext, preferred_element_type=F32)
    m_loc = b_last + rmax_last
    m_new = jnp.maximum(b_last + m_st, m_loc)
    return h, jnp.exp(b_last + m_st - m_new) * cx + jnp.exp(m_loc - m_new) * c_loc, m_new


def _mlstm_kernel(qf_ref, kf_ref, vf_ref, gcf_ref, grf_ref, qb_ref, kb_ref, vb_ref, gcb_ref, grb_ref,
                  c0_ref, m0_ref, hf_ref, hb_ref, cout_ref, mout_ref, m_scr, *c_scrs):
    c = pl.program_id(1)

    @pl.when(c == 0)
    def _():
        for idx, c_scr in enumerate(c_scrs):
            c_scr[...] = c0_ref[idx]
        m_scr[...] = m0_ref[...]

    row = lax.broadcasted_iota(jnp.int32, (CHUNK, CHUNK), 0)
    col = lax.broadcasted_iota(jnp.int32, (CHUNK, CHUNK), 1)
    ones = jnp.ones((CHUNK, LANES), BF16)
    m_all = m_scr[...]
    dirs = ((qf_ref, kf_ref, vf_ref, gcf_ref, grf_ref, hf_ref, 0, CHUNK - 1, col <= row),
            (qb_ref, kb_ref, vb_ref, gcb_ref, grb_ref, hb_ref, 2 * N_HEADS, 0, col >= row))
    m_news = []
    for di, (q_ref, k_ref, v_ref, gc_ref, gr_ref, h_ref, off, last, mask) in enumerate(dirs):
        for hd in range(N_HEADS):
            idx = di * N_HEADS + hd
            lr, lb = off + hd, off + N_HEADS + hd
            v_ext = jnp.concatenate([v_ref[:, hd * DV:(hd + 1) * DV], ones], axis=1)
            h, c_new, m_new = _mlstm_chunk(
                q_ref[:, hd * DK:(hd + 1) * DK], k_ref[:, hd * DK:(hd + 1) * DK], v_ext,
                gc_ref[:, lr:lr + 1], gc_ref[:, lb:lb + 1], gr_ref[lr:lr + 1, :], gr_ref[lb:lb + 1, :],
                gc_ref[last:last + 1, lb:lb + 1], gc_ref[last:last + 1, lr:lr + 1], mask,
                c_scrs[idx][...], m_all[idx][:, 0:1])
            h_ref[:, hd * DV:(hd + 1) * DV] = h
            c_scrs[idx][...] = c_new
            m_news.append(jnp.broadcast_to(m_new, (1, LANES)))
    for idx, m_new in enumerate(m_news):
        m_scr[idx] = m_new

    @pl.when(c == pl.num_programs(1) - 1)
    def _():
        for idx, c_scr in enumerate(c_scrs):
            cout_ref[idx] = c_scr[...]
        mout_ref[...] = m_scr[...]


def _mlstm(proj, gcol, grow, bsz, s, c0, m0):
    nc = s // CHUNK
    t = bsz * s
    fwd = lambda b, c: b * nc + c
    bwd = lambda b, c: b * nc + (nc - 1 - c)

    def specs(ci):
        return [pl.BlockSpec((CHUNK, QK_COLS), lambda b, c: (ci(b, c), 0)),
                pl.BlockSpec((CHUNK, QK_COLS), lambda b, c: (ci(b, c), 1)),
                pl.BlockSpec((CHUNK, MLSTM_WIDTH), lambda b, c: (ci(b, c), 1)),
                pl.BlockSpec((CHUNK, LANES), lambda b, c: (ci(b, c), 0)),
                pl.BlockSpec((N_GATE_COLS, CHUNK), lambda b, c: (0, ci(b, c)))]

    st_specs = [pl.BlockSpec((None, 2 * N_HEADS, DK, DVX), lambda b, c: (b, 0, 0, 0)),
                pl.BlockSpec((None, 2 * N_HEADS, 1, LANES), lambda b, c: (b, 0, 0, 0))]
    return pl.pallas_call(
        _mlstm_kernel,
        grid=(bsz, nc),
        in_specs=specs(fwd) + specs(bwd) + st_specs,
        out_specs=[pl.BlockSpec((CHUNK, MLSTM_WIDTH), lambda b, c: (fwd(b, c), 0)),
                   pl.BlockSpec((CHUNK, MLSTM_WIDTH), lambda b, c: (bwd(b, c), 0))] + st_specs,
        out_shape=[jax.ShapeDtypeStruct((t, MLSTM_WIDTH), F32),
                   jax.ShapeDtypeStruct((t, MLSTM_WIDTH), F32),
                   jax.ShapeDtypeStruct(c0.shape, F32),
                   jax.ShapeDtypeStruct(m0.shape, F32)],
        scratch_shapes=[pltpu.VMEM((2 * N_HEADS, 1, LANES), F32)]
        + [pltpu.VMEM((DK, DVX), F32) for _ in range(2 * N_HEADS)],
        compiler_params=_cparams(("arbitrary", "arbitrary")),
        name="mlstm",
    )(proj, proj, proj, gcol, grow, proj, proj, proj, gcol, grow, c0, m0)


MIX_ROWS = 256


def _mixout_kernel(o_ref, cb_ref, cc_ref, cx_ref, ccp_ref, cxp_ref, ccn_ref, cxn_ref, hf_ref, hb_ref, x_ref,
                   cw_ref, ng_ref, wout_ref, gt_ref, gffn_ref, shf_ref, scf_ref, wr_ref, br_ref,
                   x1_ref, xn2_ref, idx_ref, tw_ref):
    i = pl.program_id(1)
    tm = x_ref.shape[0]
    cw = cw_ref[...]

    has_prev = jnp.where(i > 0, 1.0, 0.0)
    has_next = jnp.where(i < pl.num_programs(1) - 1, 1.0, 0.0)
    up = ccp_ref[...].astype(F32) * cxp_ref[...].astype(F32) * has_prev
    un = ccn_ref[...].astype(F32) * cxn_ref[...].astype(F32) * has_next
    uv = cc_ref[:, CONV_HALF:].astype(F32) * cx_ref[:, CONV_HALF:].astype(F32)
    ext = jnp.concatenate([up, uv, un], axis=0)

    pos = lax.broadcasted_iota(jnp.int32, (MIX_ROWS, CONV_HALF), 0) & (GRID_W - 1)
    lane_f = lax.broadcasted_iota(jnp.int32, (MIX_ROWS, LANES), 1).astype(F32)
    lane4 = lax.broadcasted_iota(jnp.int32, (MIX_ROWS, TOP_K), 1)

    for r0 in range(0, tm, MIX_ROWS):
        rows = slice(r0, r0 + MIX_ROWS)

        uh = cc_ref[rows, :CONV_HALF].astype(F32) * cx_ref[rows, :CONV_HALF].astype(F32)
        left = jnp.where(pos == 0, 0.0, pltpu.roll(uh, 1, 0))
        right = jnp.where(pos == GRID_W - 1, 0.0, pltpu.roll(uh, MIX_ROWS - 1, 0))
        yh = cw[0:1, :CONV_HALF] * left + cw[1:2, :CONV_HALF] * uh + cw[2:3, :CONV_HALF] * right
        yv = (cw[0:1, CONV_HALF:] * ext[r0:r0 + MIX_ROWS]
              + cw[1:2, CONV_HALF:] * ext[r0 + GRID_W:r0 + GRID_W + MIX_ROWS]
              + cw[2:3, CONV_HALF:] * ext[r0 + 2 * GRID_W:r0 + 2 * GRID_W + MIX_ROWS])
        yc = cb_ref[rows, :].astype(F32) * jnp.concatenate([yh, yv], axis=1)

        hs = hf_ref[rows, :] + hb_ref[rows, :]
        parts = []
        for hd in range(N_HEADS):
            seg = hs[:, hd * DV:(hd + 1) * DV]
            parts.append(seg * lax.rsqrt(jnp.mean(seg * seg, axis=-1, keepdims=True) + EPS))
        hm = jnp.concatenate(parts, axis=1) * ng_ref[...] * jax.nn.sigmoid(o_ref[rows, :].astype(F32))

        z = jnp.concatenate([hm.astype(BF16), yc.astype(BF16)], axis=1)
        x1 = x_ref[rows, :] + gt_ref[...] * jnp.dot(z, wout_ref[...], preferred_element_type=F32)
        x1_ref[rows, :] = x1

        y = x1 * lax.rsqrt(jnp.mean(x1 * x1, axis=-1, keepdims=True) + EPS) * gffn_ref[...]
        xn2 = y * (1.0 + scf_ref[...]) + shf_ref[...]
        xn2_ref[rows, :] = xn2

        logits = jnp.dot(xn2.astype(BF16), wr_ref[...], preferred_element_type=F32) + br_ref[...]
        vals, idxs = [], []
        for _ in range(TOP_K):
            mx = jnp.max(logits, axis=-1, keepdims=True)
            ik = jnp.min(jnp.where(logits == mx, lane_f, float(LANES)), axis=-1, keepdims=True)
            vals.append(mx)
            idxs.append(ik)
            logits = jnp.where(lane_f == ik, -jnp.inf, logits)
        es = [jnp.exp(v - vals[0]) for v in vals]
        tot = es[0] + es[1] + es[2] + es[3]
        idx_out = jnp.zeros((MIX_ROWS, TOP_K), F32)
        tw_out = jnp.zeros((MIX_ROWS, TOP_K), F32)
        for kk in range(TOP_K):
            idx_out = jnp.where(lane4 == kk, idxs[kk], idx_out)
            tw_out = jnp.where(lane4 == kk, es[kk] / tot, tw_out)
        idx_ref[rows, :] = idx_out.astype(jnp.int32)
        tw_ref[rows, :] = tw_out


def _mixout(proj, hf, hb, x2, conv_w, norm_g, w_out, gt, g_ffn, sh_f, sc_f, w_r, b_r, bsz, s, tm):
    t, d = x2.shape
    nt = s // tm
    rb = tm // GRID_W
    last_rb = t // GRID_W - 1
    row = lambda b, i: b * nt + i
    w = MLSTM_WIDTH
    vec = lambda n: pl.BlockSpec((1, n), lambda b, i: (0, 0))
    per_b = pl.BlockSpec((None, 1, d), lambda b, i: (b, 0, 0))
    halo_prev = lambda cblk: pl.BlockSpec(
        (GRID_W, CONV_HALF), lambda b, i: (jnp.maximum(row(b, i) * rb - 1, 0), cblk))
    halo_next = lambda cblk: pl.BlockSpec(
        (GRID_W, CONV_HALF), lambda b, i: (jnp.minimum((row(b, i) + 1) * rb, last_rb), cblk))
    return pl.pallas_call(
        _mixout_kernel,
        grid=(bsz, nt),
        in_specs=[pl.BlockSpec((tm, w), lambda b, i: (row(b, i), 2)),
                  pl.BlockSpec((tm, w), lambda b, i: (row(b, i), 3)),
                  pl.BlockSpec((tm, w), lambda b, i: (row(b, i), 4)),
                  pl.BlockSpec((tm, w), lambda b, i: (row(b, i), 5)),
                  halo_prev(9), halo_prev(11), halo_next(9), halo_next(11),
                  pl.BlockSpec((tm, w), lambda b, i: (row(b, i), 0)),
                  pl.BlockSpec((tm, w), lambda b, i: (row(b, i), 0)),
                  pl.BlockSpec((tm, d), lambda b, i: (row(b, i), 0)),
                  pl.BlockSpec((3, CONV_WIDTH), lambda b, i: (0, 0)),
                  vec(w),
                  pl.BlockSpec((d, d), lambda b, i: (0, 0)),
                  per_b, vec(d), per_b, per_b,
                  pl.BlockSpec((d, LANES), lambda b, i: (0, 0)),
                  vec(LANES)],
        out_specs=[pl.BlockSpec((tm, d), lambda b, i: (row(b, i), 0)),
                   pl.BlockSpec((tm, d), lambda b, i: (row(b, i), 0)),
                   pl.BlockSpec((tm, TOP_K), lambda b, i: (row(b, i), 0)),
                   pl.BlockSpec((tm, TOP_K), lambda b, i: (row(b, i), 0))],
        out_shape=[jax.ShapeDtypeStruct((t, d), F32),
                   jax.ShapeDtypeStruct((t, d), F32),
                   jax.ShapeDtypeStruct((t, TOP_K), jnp.int32),
                   jax.ShapeDtypeStruct((t, TOP_K), F32)],
        compiler_params=_cparams(("arbitrary", "arbitrary")),
        name="mixout",
    )(proj, proj, proj, proj, proj, proj, proj, proj, hf, hb, x2,
      conv_w, norm_g, w_out, gt, g_ffn, sh_f, sc_f, w_r, b_r)


def _rank_kernel(idx_ref, rank_ref, cnt_ref, run_scr):
    @pl.when(pl.program_id(0) == 0)
    def _():
        run_scr[...] = jnp.zeros_like(run_scr)

    tm = idx_ref.shape[0]
    idx = idx_ref[...]
    lane = lax.broadcasted_iota(jnp.int32, (tm, LANES), 1)
    hits = [lane == idx[:, kk:kk + 1] for kk in range(TOP_K)]
    onehot = jnp.zeros((tm, LANES), F32)
    for hit in hits:
        onehot = onehot + hit.astype(F32)
    r = lax.broadcasted_iota(jnp.int32, (tm, tm), 0)
    c = lax.broadcasted_iota(jnp.int32, (tm, tm), 1)
    before = jnp.dot((c < r).astype(BF16), onehot.astype(BF16), preferred_element_type=F32) + run_scr[...]
    lane4 = lax.broadcasted_iota(jnp.int32, (tm, TOP_K), 1)
    rank = jnp.zeros((tm, TOP_K), F32)
    for kk, hit in enumerate(hits):
        rk = jnp.sum(jnp.where(hit, before, 0.0), axis=-1, keepdims=True)
        rank = jnp.where(lane4 == kk, rk, rank)
    rank_ref[...] = rank.astype(jnp.int32)
    run_scr[...] = run_scr[...] + jnp.sum(onehot, axis=0, keepdims=True)
    cnt_ref[...] = run_scr[...]


def _rank(idx, tm):
    t = idx.shape[0]
    return pl.pallas_call(
        _rank_kernel,
        grid=(t // tm,),
        in_specs=[pl.BlockSpec((tm, TOP_K), lambda i: (i, 0))],
        out_specs=[pl.BlockSpec((tm, TOP_K), lambda i: (i, 0)),
                   pl.BlockSpec((1, LANES), lambda i: (0, 0))],
        out_shape=[jax.ShapeDtypeStruct((t, TOP_K), jnp.int32),
                   jax.ShapeDtypeStruct((1, LANES), F32)],
        scratch_shapes=[pltpu.VMEM((1, LANES), F32)],
        compiler_params=_cparams(("arbitrary",)),
        name="rank",
    )(idx)


def _dispatch_kernel(pad_ref, dest_hbm, xn_ref, xs_hbm, dsm, zeros_scr, sem_idx, sem_rows, sem_pad):
    i = pl.program_id(0)
    tm = xn_ref.shape[0]
    n_idx = tm * TOP_K
    idx_copy = pltpu.make_async_copy(dest_hbm.at[pl.ds(i * n_idx, n_idx)], dsm, sem_idx)
    idx_copy.start()

    def pad_copy(off, size):
        return pltpu.make_async_copy(zeros_scr.at[pl.ds(0, size), :], xs_hbm.at[pl.ds(off, size), :], sem_pad)

    def for_each_pad_piece(fn):
        def per_expert(e, carry):
            off = pad_ref[2 * e]
            n = pad_ref[2 * e + 1]
            head = n & (SUBLANES - 1)
            for r in range(SUBLANES - 1):
                @pl.when(r < head)
                def _(r=r):
                    fn(pad_copy(off + r, 1))

            off = off + head
            size = ROW_BLOCK // 2
            while size >= SUBLANES:
                take = (n & size) != 0

                @pl.when(take)
                def _(off=off, size=size):
                    fn(pad_copy(pl.multiple_of(off, SUBLANES), size))

                off = off + jnp.where(take, size, 0)
                size //= 2
            return carry
        lax.fori_loop(0, N_EXPERTS, per_expert, 0)

    @pl.when(i == 0)
    def _():
        zeros_scr[...] = jnp.zeros_like(zeros_scr)
        for_each_pad_piece(lambda cp: cp.start())
        for_each_pad_piece(lambda cp: cp.wait())

    idx_copy.wait()

    def row_copy(t, kk):
        return pltpu.make_async_copy(xn_ref.at[pl.ds(t, 1), :],
                                     xs_hbm.at[pl.ds(dsm[t * TOP_K + kk], 1), :], sem_rows)

    def issue(t, carry):
        for kk in range(TOP_K):
            row_copy(t, kk).start()
        return carry

    lax.fori_loop(0, tm, issue, 0)
    pltpu.make_async_copy(xs_hbm.at[pl.ds(0, n_idx), :], xs_hbm.at[pl.ds(0, n_idx), :], sem_rows).wait()


def _dispatch(pad_info, dest_flat, xn2, n_rows, tm):
    t, d = xn2.shape
    return pl.pallas_call(
        _dispatch_kernel,
        grid_spec=pltpu.PrefetchScalarGridSpec(
            num_scalar_prefetch=1,
            grid=(t // tm,),
            in_specs=[pl.BlockSpec(memory_space=pl.ANY),
                      pl.BlockSpec((tm, d), lambda i, pad: (i, 0))],
            out_specs=pl.BlockSpec(memory_space=pl.ANY),
            scratch_shapes=[pltpu.SMEM((tm * TOP_K,), jnp.int32),
                            pltpu.VMEM((ROW_BLOCK // 2, d), F32),
                            pltpu.SemaphoreType.DMA(()),
                            pltpu.SemaphoreType.DMA(()),
                            pltpu.SemaphoreType.DMA(())]),
        out_shape=jax.ShapeDtypeStruct((n_rows, d), F32),
        compiler_params=_cparams(("arbitrary",)),
        name="dispatch",
    )(pad_info, dest_flat, xn2)


def _new_expert(be_ref, j):
    return jnp.logical_or(j == 0, be_ref[j] != be_ref[jnp.maximum(j - 1, 0)])


EXPERT_SUB = 256


def _for_block_rows(valid, rows, body):
    sub = min(EXPERT_SUB, rows)

    @pl.when(valid == rows)
    def _():
        body(slice(0, rows))

    @pl.when(valid < rows)
    def _():
        for r0 in range(0, rows, sub):
            @pl.when(r0 < valid)
            def _(r0=r0):
                body(slice(r0, r0 + sub))


def _expert_gu_kernel(be_ref, nu_ref, bv_ref, xs_ref, wg_ref, wu_ref, bg_ref, bu_ref, act_ref, wg_scr, wu_scr):
    j = pl.program_id(1)

    @pl.when(j < nu_ref[0])
    def _():
        @pl.when(_new_expert(be_ref, j))
        def _():
            wg_scr[...] = wg_ref[...].astype(BF16)
            wu_scr[...] = wu_ref[...].astype(BF16)

        def body(rows):
            x = xs_ref[rows, :].astype(BF16)
            g = jnp.dot(x, wg_scr[...], preferred_element_type=F32) + bg_ref[...]
            u = jnp.dot(x, wu_scr[...], preferred_element_type=F32) + bu_ref[...]
            gate = jnp.minimum(g, SWIGLU_LIMIT)
            up = jnp.clip(u, -SWIGLU_LIMIT, SWIGLU_LIMIT)
            act_ref[rows, :] = ((up + 1.0) * gate * jax.nn.sigmoid(SWIGLU_ALPHA * gate)).astype(BF16)

        _for_block_rows(bv_ref[j], xs_ref.shape[0], body)


def _expert_gu(blk_expert, n_used, blk_valid, xs, w_gu, b_gu, tn):
    n_rows, d = xs.shape
    dff = w_gu.shape[2] // 2
    nt = dff // tn
    nb = n_rows // ROW_BLOCK
    blk = lambda j, nu: jnp.minimum(j, nu[0] - 1)
    exp = lambda j, be, nu: be[blk(j, nu)]
    return pl.pallas_call(
        _expert_gu_kernel,
        grid_spec=pltpu.PrefetchScalarGridSpec(
            num_scalar_prefetch=3,
            grid=(nt, nb),
            in_specs=[pl.BlockSpec((ROW_BLOCK, d), lambda n, j, be, nu, bv: (blk(j, nu), 0)),
                      pl.BlockSpec((None, d, tn), lambda n, j, be, nu, bv: (exp(j, be, nu), 0, n)),
                      pl.BlockSpec((None, d, tn), lambda n, j, be, nu, bv: (exp(j, be, nu), 0, nt + n)),
                      pl.BlockSpec((None, 1, tn), lambda n, j, be, nu, bv: (exp(j, be, nu), 0, n)),
                      pl.BlockSpec((None, 1, tn), lambda n, j, be, nu, bv: (exp(j, be, nu), 0, nt + n))],
            out_specs=pl.BlockSpec((ROW_BLOCK, tn), lambda n, j, be, nu, bv: (blk(j, nu), n)),
            scratch_shapes=[pltpu.VMEM((d, tn), BF16), pltpu.VMEM((d, tn), BF16)]),
        out_shape=jax.ShapeDtypeStruct((n_rows, dff), BF16),
        compiler_params=_cparams(("arbitrary", "arbitrary")),
        name="expert_gu",
    )(blk_expert, n_used, blk_valid, xs, w_gu, w_gu, b_gu, b_gu)


def _expert_down_kernel(be_ref, nu_ref, bv_ref, act_ref, w_ref, b_ref, y_ref, w_scr):
    j = pl.program_id(1)

    @pl.when(j < nu_ref[0])
    def _():
        @pl.when(_new_expert(be_ref, j))
        def _():
            w_scr[...] = w_ref[...].astype(BF16)

        def body(rows):
            y_ref[rows, :] = jnp.dot(act_ref[rows, :], w_scr[...], preferred_element_type=F32) + b_ref[...]

        _for_block_rows(bv_ref[j], act_ref.shape[0], body)


def _expert_down(blk_expert, n_used, blk_valid, act, w_down, b_down, tn):
    n_rows, dff = act.shape
    d = w_down.shape[2]
    nb = n_rows // ROW_BLOCK
    blk = lambda j, nu: jnp.minimum(j, nu[0] - 1)
    exp = lambda j, be, nu: be[blk(j, nu)]
    return pl.pallas_call(
        _expert_down_kernel,
        grid_spec=pltpu.PrefetchScalarGridSpec(
            num_scalar_prefetch=3,
            grid=(d // tn, nb),
            in_specs=[pl.BlockSpec((ROW_BLOCK, dff), lambda n, j, be, nu, bv: (blk(j, nu), 0)),
                      pl.BlockSpec((None, dff, tn), lambda n, j, be, nu, bv: (exp(j, be, nu), 0, n)),
                      pl.BlockSpec((None, 1, tn), lambda n, j, be, nu, bv: (exp(j, be, nu), 0, n))],
            out_specs=pl.BlockSpec((ROW_BLOCK, tn), lambda n, j, be, nu, bv: (blk(j, nu), n)),
            scratch_shapes=[pltpu.VMEM((dff, tn), BF16)]),
        out_shape=jax.ShapeDtypeStruct((n_rows, d), F32),
        compiler_params=_cparams(("arbitrary", "arbitrary")),
        name="expert_down",
    )(blk_expert, n_used, blk_valid, act, w_down, b_down)


def _combine_kernel(dest_hbm, y_hbm, x1_ref, tw_ref, gt_ref, gfin_ref, out_ref, dsm, buf, sem_idx, sem_rows):
    i = pl.program_id(1) + pl.program_id(0) * pl.num_programs(1)
    n_steps = pl.num_programs(0) * pl.num_programs(1)
    tm = x1_ref.shape[0]
    n_idx = tm * TOP_K

    def idx_copy(tile):
        slot = tile % 2
        return pltpu.make_async_copy(dest_hbm.at[pl.ds(tile * n_idx, n_idx)], dsm.at[slot], sem_idx.at[slot])

    def issue_rows(tile):
        slot = tile % 2

        def issue(t, carry):
            for kk in range(TOP_K):
                pltpu.make_async_copy(y_hbm.at[pl.ds(dsm[slot, t * TOP_K + kk], 1), :],
                                      buf.at[slot, kk, pl.ds(t, 1), :], sem_rows.at[slot]).start()
            return carry

        lax.fori_loop(0, tm, issue, 0)

    @pl.when(i == 0)
    def _():
        idx_copy(0).start()
        idx_copy(0).wait()
        issue_rows(0)

        @pl.when(n_steps > 1)
        def _():
            idx_copy(1).start()

    @pl.when(i + 2 < n_steps)
    def _():
        idx_copy(i + 2).start()

    @pl.when(i + 1 < n_steps)
    def _():
        idx_copy(i + 1).wait()
        issue_rows(i + 1)

    slot = i % 2
    for kk in range(TOP_K):
        pltpu.make_async_copy(y_hbm.at[pl.ds(0, tm), :], buf.at[slot, kk], sem_rows.at[slot]).wait()

    tw = tw_ref[...]
    acc = buf[slot, 0] * tw[:, 0:1]
    for kk in range(1, TOP_K):
        acc = acc + buf[slot, kk] * tw[:, kk:kk + 1]
    x2 = x1_ref[...] + gt_ref[...] * acc
    out_ref[...] = x2 * lax.rsqrt(jnp.mean(x2 * x2, axis=-1, keepdims=True) + EPS) * gfin_ref[...]


def _combine(dest_flat, y, x1, tw, gt, g_final, bsz, s, tm):
    t, d = x1.shape
    nt = s // tm
    row = lambda b, i: (b * nt + i, 0)
    return pl.pallas_call(
        _combine_kernel,
        grid=(bsz, nt),
        in_specs=[pl.BlockSpec(memory_space=pl.ANY),
                  pl.BlockSpec(memory_space=pl.ANY),
                  pl.BlockSpec((tm, d), row),
                  pl.BlockSpec((tm, TOP_K), row),
                  pl.BlockSpec((None, 1, d), lambda b, i: (b, 0, 0)),
                  pl.BlockSpec((1, d), lambda b, i: (0, 0))],
        out_specs=pl.BlockSpec((tm, d), row),
        out_shape=jax.ShapeDtypeStruct((t, d), F32),
        scratch_shapes=[pltpu.SMEM((2, tm * TOP_K), jnp.int32),
                        pltpu.VMEM((2, TOP_K, tm, d), F32),
                        pltpu.SemaphoreType.DMA((2,)),
                        pltpu.SemaphoreType.DMA((2,))],
        compiler_params=_cparams(("arbitrary", "arbitrary")),
        name="combine",
    )(dest_flat, y, x1, tw, gt, g_final)


def _pad_lanes(a, value=0.0):
    return jnp.pad(a, ((0, 0), (0, LANES - a.shape[1])), constant_values=value)


def _routing_tables(idx, rank, counts_f, n_blocks):
    counts = counts_f[0, :N_EXPERTS].astype(jnp.int32)
    padded = (counts + ROW_BLOCK - 1) // ROW_BLOCK * ROW_BLOCK
    pend = jnp.cumsum(padded)
    pstart = pend - padded
    dest = (pstart[idx] + rank).reshape(-1)
    blk_start = jnp.arange(n_blocks, dtype=jnp.int32) * ROW_BLOCK
    blk_expert = jnp.minimum(jnp.sum((pend[None, :] <= blk_start[:, None]).astype(jnp.int32), axis=1),
                             N_EXPERTS - 1)
    n_used = (pend[-1:] // ROW_BLOCK).astype(jnp.int32)
    blk_valid = jnp.clip((pstart + counts)[blk_expert] - blk_start, 0, ROW_BLOCK).astype(jnp.int32)
    pad_info = jnp.stack([pstart + counts, padded - counts], axis=1).reshape(-1).astype(jnp.int32)
    return dest, blk_expert, n_used, blk_valid, pad_info


def _layer(x, c, ctx, c_ctx, w_ada, b_ada, g_mix, w_in, b_if, conv_w, norm_g, w_out,
           g_ffn, w_router, b_router, w_gu, b_gu, w_down, b_down, g_final):
    bsz, s, d = x.shape
    s_ctx = ctx.shape[1]

    cond = jnp.zeros((8, d), F32).at[:bsz].set(c).at[bsz].set(c_ctx)
    mod = _adaln(cond, w_ada, b_ada[None, :])
    sh_m, sc_m, gt_m, sh_f, sc_f, gt_f = [m[:, None, :] for m in jnp.split(mod, N_MOD, axis=-1)]
    lat = lambda m: m[:bsz]
    ctxm = lambda m: jnp.broadcast_to(m[bsz:bsz + 1], (bsz, 1, d))

    g0 = 2 * QK_COLS + 2 * MLSTM_WIDTH
    w_main = jnp.concatenate([w_in[:, :g0], w_in[:, g0 + N_GATE_COLS:]], axis=1).astype(BF16)
    w_gate = _pad_lanes(w_in[:, g0:g0 + N_GATE_COLS]).astype(BF16)
    b_gate = _pad_lanes(b_if[None, :])
    g_mix2 = g_mix[None, :]

    proj_c, gpre_c = _inproj(ctx, g_mix2, ctxm(sh_m), ctxm(sc_m), w_main, w_gate, min(s_ctx, 512))
    gcol_c, grow_c = _gates(gpre_c, b_gate, 512)
    zeros_state = (jnp.zeros((bsz, 2 * N_HEADS, DK, DVX), F32),
                   jnp.zeros((bsz, 2 * N_HEADS, 1, LANES), F32))
    _, _, c0, m0 = _mlstm(proj_c, gcol_c, grow_c, bsz, s_ctx, *zeros_state)

    proj, gpre = _inproj(x, g_mix2, lat(sh_m), lat(sc_m), w_main, w_gate, 512)
    gcol, grow = _gates(gpre, b_gate, 512)
    hf, hb, _, _ = _mlstm(proj, gcol, grow, bsz, s, c0, m0)
    x1, xn2, idx, tw = _mixout(
        proj, hf, hb, x.reshape(bsz * s, d), conv_w, norm_g[None, :], w_out.astype(BF16), lat(gt_m),
        g_ffn[None, :], lat(sh_f), lat(sc_f), _pad_lanes(w_router).astype(BF16),
        _pad_lanes(b_router[None, :], NEG_BIG), bsz, s, 512)

    t = bsz * s
    n_blocks = -(-(t * TOP_K) // ROW_BLOCK) + N_EXPERTS
    rank, counts = _rank(idx, 512)
    dest, blk_expert, n_used, blk_valid, pad_info = _routing_tables(idx, rank, counts, n_blocks)
    xs = _dispatch(pad_info, dest, xn2, n_blocks * ROW_BLOCK, 512)
    act = _expert_gu(blk_expert, n_used, blk_valid, xs, w_gu, b_gu[:, None, :], 512)
    y = _expert_down(blk_expert, n_used, blk_valid, act, w_down, b_down[:, None, :], 1024)
    out = _combine(dest, y, x1, tw, lat(gt_f), g_final[None, :], bsz, s, 256)
    return out.reshape(bsz, s, d)


def kernel(x, c, ctx, c_ctx, w_ada, b_ada, g_mix, w_in, b_if, conv_w, mlstm_norm_g, w_out,
           g_ffn, w_router, b_router, w_gu, b_gu, w_down, b_down, g_final):
    return _layer(x, c, ctx, c_ctx, w_ada[0], b_ada[0], g_mix[0], w_in[0], b_if[0], conv_w[0],
                  mlstm_norm_g[0], w_out[0], g_ffn[0], w_router[0], b_router[0], w_gu[0], b_gu[0],
                  w_down[0], b_down[0], g_final)
```

```python
import functools

import jax
import jax.numpy as jnp
from jax import lax
from jax.experimental import pallas as pl
from jax.experimental.pallas import tpu as pltpu

F32 = jnp.float32
BF16 = jnp.bfloat16

N_HEADS = 4
DK = 128
DV = 256
QK_COLS = N_HEADS * DK
MLSTM_WIDTH = N_HEADS * DV
CONV_WIDTH = 1024
CONV_HALF = CONV_WIDTH // 2
N_GATE_COLS = 4 * N_HEADS
GRID_W = 64
CHUNK = 128
GATE_SOFT_CAP = 15.0
N_EXPERTS = 32
TOP_K = 4
SWIGLU_LIMIT = 7.0
SWIGLU_ALPHA = 1.702
N_MOD = 6
EPS = 1e-6
LANES = 128
SUBLANES = 8
ROW_BLOCK = 1024
NEG_BIG = -1e30
VMEM_LIMIT = 56 * 1024 * 1024


def _cparams(sem):
    return pltpu.CompilerParams(dimension_semantics=sem, vmem_limit_bytes=VMEM_LIMIT)


def _adaln_kernel(c_ref, w_ref, b_ref, o_ref):
    s = c_ref[...]
    s = s * jax.nn.sigmoid(s)
    o_ref[...] = jnp.dot(s.astype(BF16), w_ref[...].astype(BF16),
                         preferred_element_type=F32) + b_ref[...]


def _adaln(cond, w, b):
    d, n = w.shape
    tn = 1024
    return pl.pallas_call(
        _adaln_kernel,
        grid=(n // tn,),
        in_specs=[pl.BlockSpec((8, d), lambda j: (0, 0)),
                  pl.BlockSpec((d, tn), lambda j: (0, j)),
                  pl.BlockSpec((1, tn), lambda j: (0, j))],
        out_specs=pl.BlockSpec((8, tn), lambda j: (0, j)),
        out_shape=jax.ShapeDtypeStruct((8, n), F32),
        compiler_params=_cparams(("arbitrary",)),
        name="adaln",
    )(cond, w, b)


INPROJ_COLS = 1024


def _inproj_kernel(x_ref, g_ref, sh_ref, sc_ref, w_ref, wg_ref, proj_ref, gate_ref):
    x = x_ref[...]
    y = x * lax.rsqrt(jnp.mean(x * x, axis=-1, keepdims=True) + EPS) * g_ref[...]
    xn = (y * (1.0 + sc_ref[...]) + sh_ref[...]).astype(BF16)
    gate_ref[...] = jnp.dot(xn, wg_ref[...], preferred_element_type=F32)
    for j in range(w_ref.shape[1] // INPROJ_COLS):
        cols = slice(j * INPROJ_COLS, (j + 1) * INPROJ_COLS)
        proj_ref[:, cols] = jnp.dot(xn, w_ref[:, cols], preferred_element_type=F32).astype(BF16)


def _inproj(x, g, sh, sc, w, wg, tm):
    bsz, s, d = x.shape
    p = w.shape[1]
    nt = s // tm
    x2 = x.reshape(bsz * s, d)
    resident = lambda shape: pl.BlockSpec(shape, lambda b, i: (0, 0), pipeline_mode=pl.Buffered(1))
    return pl.pallas_call(
        _inproj_kernel,
        grid=(bsz, nt),
        in_specs=[pl.BlockSpec((tm, d), lambda b, i: (b * nt + i, 0)),
                  pl.BlockSpec((1, d), lambda b, i: (0, 0)),
                  pl.BlockSpec((None, 1, d), lambda b, i: (b, 0, 0)),
                  pl.BlockSpec((None, 1, d), lambda b, i: (b, 0, 0)),
                  resident((d, p)),
                  resident((d, LANES))],
        out_specs=[pl.BlockSpec((tm, p), lambda b, i: (b * nt + i, 0)),
                   pl.BlockSpec((tm, LANES), lambda b, i: (b * nt + i, 0))],
        out_shape=[jax.ShapeDtypeStruct((bsz * s, p), BF16),
                   jax.ShapeDtypeStruct((bsz * s, LANES), F32)],
        compiler_params=_cparams(("arbitrary", "arbitrary")),
        name="inproj",
    )(x2, g, sh, sc, w, wg)


def _log_sigmoid(x):
    return jnp.minimum(x, 0.0) - jnp.log1p(jnp.exp(-jnp.abs(x)))


def _gates_kernel(g_ref, b_ref, gc_ref, gr_ref):
    tm = g_ref.shape[0]
    row = lax.broadcasted_iota(jnp.int32, (tm, LANES), 0)
    lane = lax.broadcasted_iota(jnp.int32, (tm, LANES), 1)
    gp = GATE_SOFT_CAP * jnp.tanh((g_ref[...] + b_ref[...]) / GATE_SOFT_CAP)
    is_f = ((lane >> 2) & 1) == 1
    fwd_lane = lane < 2 * N_HEADS
    lf = jnp.where(is_f, _log_sigmoid(gp), 0.0)
    r2 = lax.broadcasted_iota(jnp.int32, (tm, tm), 0)
    c2 = lax.broadcasted_iota(jnp.int32, (tm, tm), 1)
    same_chunk = (r2 // CHUNK) == (c2 // CHUNK)
    lower = jnp.logical_and(same_chunk, r2 >= c2).astype(F32)
    upper = jnp.logical_and(same_chunk, r2 <= c2).astype(F32)
    cf = jnp.dot(lower, lf, precision=lax.Precision.HIGHEST, preferred_element_type=F32)
    cb = jnp.dot(upper, lf, precision=lax.Precision.HIGHEST, preferred_element_type=F32)
    cdir = jnp.where(fwd_lane, cf, cb)
    a = jnp.where(is_f, cdir, gp - pltpu.roll(cdir, LANES - N_HEADS, 1))

    pos = row % CHUNK
    x = a
    k = 1
    while k < CHUNK:
        from_before = jnp.where(pos >= k, pltpu.roll(x, k, 0), -jnp.inf)
        from_after = jnp.where(pos < CHUNK - k, pltpu.roll(x, tm - k, 0), -jnp.inf)
        x = jnp.maximum(x, jnp.where(fwd_lane, from_before, from_after))
        k *= 2
    gc_ref[...] = jnp.where(is_f, a, x)

    lane_c = lax.broadcasted_iota(jnp.int32, (CHUNK, LANES), 1)
    lane_1 = lax.broadcasted_iota(jnp.int32, (1, LANES), 1)
    for c in range(tm // CHUNK):
        lo = c * CHUNK
        xc, ac = x[lo:lo + CHUNK], a[lo:lo + CHUNK]
        end_max = jnp.where(lane_1 < 2 * N_HEADS, xc[CHUNK - 1:CHUNK], xc[0:1])
        e = jnp.exp(ac - end_max)
        rows = jnp.where(((lane_c >> 2) & 1) == 1, pltpu.roll(e, N_HEADS, 1), ac)
        gr_ref[:, lo:lo + CHUNK] = rows.T[:N_GATE_COLS, :]


def _gates(gpre, b_if, tm):
    t = gpre.shape[0]
    return pl.pallas_call(
        _gates_kernel,
        grid=(t // tm,),
        in_specs=[pl.BlockSpec((tm, LANES), lambda i: (i, 0)),
                  pl.BlockSpec((1, LANES), lambda i: (0, 0))],
        out_specs=[pl.BlockSpec((tm, LANES), lambda i: (i, 0)),
                   pl.BlockSpec((N_GATE_COLS, tm), lambda i: (0, i))],
        out_shape=[jax.ShapeDtypeStruct((t, LANES), F32),
                   jax.ShapeDtypeStruct((N_GATE_COLS, t), F32)],
        compiler_params=_cparams(("arbitrary",)),
        name="gates",
    )(gpre, b_if)


DVX = DV + LANES


def _mlstm_chunk(q, k, v_ext, rmax_col, b_col, r_row, e_row, b_last, rmax_last, mask, cx, m_st):
    scale = DK ** -0.5
    mb = jnp.maximum(m_st, jnp.broadcast_to(rmax_col, (CHUNK, CHUNK)))
    w_intra = jnp.exp(jnp.where(mask, r_row - mb, -jnp.inf))
    w_state = jnp.exp(m_st - mb)
    qk = lax.dot_general(q, k, (((1,), (1,)), ((), ())), preferred_element_type=F32)
    s = qk * (w_intra * scale)
    lhs = jnp.concatenate([s.astype(BF16), (q.astype(F32) * (w_state * scale)).astype(BF16)], axis=1)
    rhs = jnp.concatenate([v_ext, cx.astype(BF16)], axis=0)
    nx = jnp.dot(lhs, rhs, preferred_element_type=F32)
    denom = jnp.maximum(jnp.abs(nx[:, DV:]), jnp.exp(-(jnp.broadcast_to(b_col, (CHUNK, CHUNK)) + mb)))
    h = nx[:, :DV] / jnp.concatenate([denom, denom], axis=1)
    ke_t = (k.T.astype(F32) * e_row).astype(BF16)
    c_loc = jnp.dot(ke_t, v_ext, preferred_element_type=F32)
    m_loc = b_last + rmax_last
    m_new = jnp.maximum(b_last + m_st, m_loc)
    return h, jnp.exp(b_last + m_st - m_new) * cx + jnp.exp(m_loc - m_new) * c_loc, m_new


def _mlstm_kernel(qf_ref, kf_ref, vf_ref, gcf_ref, grf_ref, qb_ref, kb_ref, vb_ref, gcb_ref, grb_ref,
                  c0_ref, m0_ref, hf_ref, hb_ref, cout_ref, mout_ref, m_scr, *c_scrs):
    c = pl.program_id(1)

    @pl.when(c == 0)
    def _():
        for idx, c_scr in enumerate(c_scrs):
            c_scr[...] = c0_ref[idx]
        m_scr[...] = m0_ref[...]

    row = lax.broadcasted_iota(jnp.int32, (CHUNK, CHUNK), 0)
    col = lax.broadcasted_iota(jnp.int32, (CHUNK, CHUNK), 1)
    ones = jnp.ones((CHUNK, LANES), BF16)
    m_all = m_scr[...]
    dirs = ((qf_ref, kf_ref, vf_ref, gcf_ref, grf_ref, hf_ref, 0, CHUNK - 1, col <= row),
            (qb_ref, kb_ref, vb_ref, gcb_ref, grb_ref, hb_ref, 2 * N_HEADS, 0, col >= row))
    m_news = []
    for di, (q_ref, k_ref, v_ref, gc_ref, gr_ref, h_ref, off, last, mask) in enumerate(dirs):
        for hd in range(N_HEADS):
            idx = di * N_HEADS + hd
            lr, lb = off + hd, off + N_HEADS + hd
            v_ext = jnp.concatenate([v_ref[:, hd * DV:(hd + 1) * DV], ones], axis=1)
            h, c_new, m_new = _mlstm_chunk(
                q_ref[:, hd * DK:(hd + 1) * DK], k_ref[:, hd * DK:(hd + 1) * DK], v_ext,
                gc_ref[:, lr:lr + 1], gc_ref[:, lb:lb + 1], gr_ref[lr:lr + 1, :], gr_ref[lb:lb + 1, :],
                gc_ref[last:last + 1, lb:lb + 1], gc_ref[last:last + 1, lr:lr + 1], mask,
                c_scrs[idx][...], m_all[idx][:, 0:1])
            h_ref[:, hd * DV:(hd + 1) * DV] = h
            c_scrs[idx][...] = c_new
            m_news.append(jnp.broadcast_to(m_new, (1, LANES)))
    for idx, m_new in enumerate(m_news):
        m_scr[idx] = m_new

    @pl.when(c == pl.num_programs(1) - 1)
    def _():
        for idx, c_scr in enumerate(c_scrs):
            cout_ref[idx] = c_scr[...]
        mout_ref[...] = m_scr[...]


def _mlstm(proj, gcol, grow, bsz, s, c0, m0):
    nc = s // CHUNK
    t = bsz * s
    fwd = lambda b, c: b * nc + c
    bwd = lambda b, c: b * nc + (nc - 1 - c)

    def specs(ci):
        return [pl.BlockSpec((CHUNK, QK_COLS), lambda b, c: (ci(b, c), 0)),
                pl.BlockSpec((CHUNK, QK_COLS), lambda b, c: (ci(b, c), 1)),
                pl.BlockSpec((CHUNK, MLSTM_WIDTH), lambda b, c: (ci(b, c), 1)),
                pl.BlockSpec((CHUNK, LANES), lambda b, c: (ci(b, c), 0)),
                pl.BlockSpec((N_GATE_COLS, CHUNK), lambda b, c: (0, ci(b, c)))]

    st_specs = [pl.BlockSpec((None, 2 * N_HEADS, DK, DVX), lambda b, c: (b, 0, 0, 0)),
                pl.BlockSpec((None, 2 * N_HEADS, 1, LANES), lambda b, c: (b, 0, 0, 0))]
    return pl.pallas_call(
        _mlstm_kernel,
        grid=(bsz, nc),
        in_specs=specs(fwd) + specs(bwd) + st_specs,
        out_specs=[pl.BlockSpec((CHUNK, MLSTM_WIDTH), lambda b, c: (fwd(b, c), 0)),
                   pl.BlockSpec((CHUNK, MLSTM_WIDTH), lambda b, c: (bwd(b, c), 0))] + st_specs,
        out_shape=[jax.ShapeDtypeStruct((t, MLSTM_WIDTH), F32),
                   jax.ShapeDtypeStruct((t, MLSTM_WIDTH), F32),
                   jax.ShapeDtypeStruct(c0.shape, F32),
                   jax.ShapeDtypeStruct(m0.shape, F32)],
        scratch_shapes=[pltpu.VMEM((2 * N_HEADS, 1, LANES), F32)]
        + [pltpu.VMEM((DK, DVX), F32) for _ in range(2 * N_HEADS)],
        compiler_params=_cparams(("arbitrary", "arbitrary")),
        name="mlstm",
    )(proj, proj, proj, gcol, grow, proj, proj, proj, gcol, grow, c0, m0)


MIX_ROWS = 256


def _mixout_kernel(o_ref, cb_ref, cc_ref, cx_ref, ccp_ref, cxp_ref, ccn_ref, cxn_ref, hf_ref, hb_ref, x_ref,
                   cw_ref, ng_ref, wout_ref, gt_ref, gffn_ref, shf_ref, scf_ref, wr_ref, br_ref,
                   x1_ref, xn2_ref, idx_ref, tw_ref):
    i = pl.program_id(1)
    tm = x_ref.shape[0]
    cw = cw_ref[...]

    has_prev = jnp.where(i > 0, 1.0, 0.0)
    has_next = jnp.where(i < pl.num_programs(1) - 1, 1.0, 0.0)
    up = ccp_ref[...].astype(F32) * cxp_ref[...].astype(F32) * has_prev
    un = ccn_ref[...].astype(F32) * cxn_ref[...].astype(F32) * has_next
    uv = cc_ref[:, CONV_HALF:].astype(F32) * cx_ref[:, CONV_HALF:].astype(F32)
    ext = jnp.concatenate([up, uv, un], axis=0)

    pos = lax.broadcasted_iota(jnp.int32, (MIX_ROWS, CONV_HALF), 0) & (GRID_W - 1)
    lane_f = lax.broadcasted_iota(jnp.int32, (MIX_ROWS, LANES), 1).astype(F32)
    lane4 = lax.broadcasted_iota(jnp.int32, (MIX_ROWS, TOP_K), 1)

    for r0 in range(0, tm, MIX_ROWS):
        rows = slice(r0, r0 + MIX_ROWS)

        uh = cc_ref[rows, :CONV_HALF].astype(F32) * cx_ref[rows, :CONV_HALF].astype(F32)
        left = jnp.where(pos == 0, 0.0, pltpu.roll(uh, 1, 0))
        right = jnp.where(pos == GRID_W - 1, 0.0, pltpu.roll(uh, MIX_ROWS - 1, 0))
        yh = cw[0:1, :CONV_HALF] * left + cw[1:2, :CONV_HALF] * uh + cw[2:3, :CONV_HALF] * right
        yv = (cw[0:1, CONV_HALF:] * ext[r0:r0 + MIX_ROWS]
              + cw[1:2, CONV_HALF:] * ext[r0 + GRID_W:r0 + GRID_W + MIX_ROWS]
              + cw[2:3, CONV_HALF:] * ext[r0 + 2 * GRID_W:r0 + 2 * GRID_W + MIX_ROWS])
        yc = cb_ref[rows, :].astype(F32) * jnp.concatenate([yh, yv], axis=1)

        hs = hf_ref[rows, :] + hb_ref[rows, :]
        parts = []
        for hd in range(N_HEADS):
            seg = hs[:, hd * DV:(hd + 1) * DV]
            parts.append(seg * lax.rsqrt(jnp.mean(seg * seg, axis=-1, keepdims=True) + EPS))
        hm = jnp.concatenate(parts, axis=1) * ng_ref[...] * jax.nn.sigmoid(o_ref[rows, :].astype(F32))

        z = jnp.concatenate([hm.astype(BF16), yc.astype(BF16)], axis=1)
        x1 = x_ref[rows, :] + gt_ref[...] * jnp.dot(z, wout_ref[...], preferred_element_type=F32)
        x1_ref[rows, :] = x1

        y = x1 * lax.rsqrt(jnp.mean(x1 * x1, axis=-1, keepdims=True) + EPS) * gffn_ref[...]
        xn2 = y * (1.0 + scf_ref[...]) + shf_ref[...]
        xn2_ref[rows, :] = xn2

        logits = jnp.dot(xn2.astype(BF16), wr_ref[...], preferred_element_type=F32) + br_ref[...]
        vals, idxs = [], []
        for _ in range(TOP_K):
            mx = jnp.max(logits, axis=-1, keepdims=True)
            ik = jnp.min(jnp.where(logits == mx, lane_f, float(LANES)), axis=-1, keepdims=True)
            vals.append(mx)
            idxs.append(ik)
            logits = jnp.where(lane_f == ik, -jnp.inf, logits)
        es = [jnp.exp(v - vals[0]) for v in vals]
        tot = es[0] + es[1] + es[2] + es[3]
        idx_out = jnp.zeros((MIX_ROWS, TOP_K), F32)
        tw_out = jnp.zeros((MIX_ROWS, TOP_K), F32)
        for kk in range(TOP_K):
            idx_out = jnp.where(lane4 == kk, idxs[kk], idx_out)
            tw_out = jnp.where(lane4 == kk, es[kk] / tot, tw_out)
        idx_ref[rows, :] = idx_out.astype(jnp.int32)
        tw_ref[rows, :] = tw_out


def _mixout(proj, hf, hb, x2, conv_w, norm_g, w_out, gt, g_ffn, sh_f, sc_f, w_r, b_r, bsz, s, tm):
    t, d = x2.shape
    nt = s // tm
    rb = tm // GRID_W
    last_rb = t // GRID_W - 1
    row = lambda b, i: b * nt + i
    w = MLSTM_WIDTH
    vec = lambda n: pl.BlockSpec((1, n), lambda b, i: (0, 0))
    per_b = pl.BlockSpec((None, 1, d), lambda b, i: (b, 0, 0))
    halo_prev = lambda cblk: pl.BlockSpec(
        (GRID_W, CONV_HALF), lambda b, i: (jnp.maximum(row(b, i) * rb - 1, 0), cblk))
    halo_next = lambda cblk: pl.BlockSpec(
        (GRID_W, CONV_HALF), lambda b, i: (jnp.minimum((row(b, i) + 1) * rb, last_rb), cblk))
    return pl.pallas_call(
        _mixout_kernel,
        grid=(bsz, nt),
        in_specs=[pl.BlockSpec((tm, w), lambda b, i: (row(b, i), 2)),
                  pl.BlockSpec((tm, w), lambda b, i: (row(b, i), 3)),
                  pl.BlockSpec((tm, w), lambda b, i: (row(b, i), 4)),
                  pl.BlockSpec((tm, w), lambda b, i: (row(b, i), 5)),
                  halo_prev(9), halo_prev(11), halo_next(9), halo_next(11),
                  pl.BlockSpec((tm, w), lambda b, i: (row(b, i), 0)),
                  pl.BlockSpec((tm, w), lambda b, i: (row(b, i), 0)),
                  pl.BlockSpec((tm, d), lambda b, i: (row(b, i), 0)),
                  pl.BlockSpec((3, CONV_WIDTH), lambda b, i: (0, 0)),
                  vec(w),
                  pl.BlockSpec((d, d), lambda b, i: (0, 0)),
                  per_b, vec(d), per_b, per_b,
                  pl.BlockSpec((d, LANES), lambda b, i: (0, 0)),
                  vec(LANES)],
        out_specs=[pl.BlockSpec((tm, d), lambda b, i: (row(b, i), 0)),
                   pl.BlockSpec((tm, d), lambda b, i: (row(b, i), 0)),
                   pl.BlockSpec((tm, TOP_K), lambda b, i: (row(b, i), 0)),
                   pl.BlockSpec((tm, TOP_K), lambda b, i: (row(b, i), 0))],
        out_shape=[jax.ShapeDtypeStruct((t, d), F32),
                   jax.ShapeDtypeStruct((t, d), F32),
                   jax.ShapeDtypeStruct((t, TOP_K), jnp.int32),
                   jax.ShapeDtypeStruct((t, TOP_K), F32)],
        compiler_params=_cparams(("arbitrary", "arbitrary")),
        name="mixout",
    )(proj, proj, proj, proj, proj, proj, proj, proj, hf, hb, x2,
      conv_w, norm_g, w_out, gt, g_ffn, sh_f, sc_f, w_r, b_r)


def _rank_kernel(idx_ref, rank_ref, cnt_ref, run_scr):
    @pl.when(pl.program_id(0) == 0)
    def _():
        run_scr[...] = jnp.zeros_like(run_scr)

    tm = idx_ref.shape[0]
    idx = idx_ref[...]
    lane = lax.broadcasted_iota(jnp.int32, (tm, LANES), 1)
    hits = [lane == idx[:, kk:kk + 1] for kk in range(TOP_K)]
    onehot = jnp.zeros((tm, LANES), F32)
    for hit in hits:
        onehot = onehot + hit.astype(F32)
    r = lax.broadcasted_iota(jnp.int32, (tm, tm), 0)
    c = lax.broadcasted_iota(jnp.int32, (tm, tm), 1)
    before = jnp.dot((c < r).astype(BF16), onehot.astype(BF16), preferred_element_type=F32) + run_scr[...]
    lane4 = lax.broadcasted_iota(jnp.int32, (tm, TOP_K), 1)
    rank = jnp.zeros((tm, TOP_K), F32)
    for kk, hit in enumerate(hits):
        rk = jnp.sum(jnp.where(hit, before, 0.0), axis=-1, keepdims=True)
        rank = jnp.where(lane4 == kk, rk, rank)
    rank_ref[...] = rank.astype(jnp.int32)
    run_scr[...] = run_scr[...] + jnp.sum(onehot, axis=0, keepdims=True)
    cnt_ref[...] = run_scr[...]


def _rank(idx, tm):
    t = idx.shape[0]
    return pl.pallas_call(
        _rank_kernel,
        grid=(t // tm,),
        in_specs=[pl.BlockSpec((tm, TOP_K), lambda i: (i, 0))],
        out_specs=[pl.BlockSpec((tm, TOP_K), lambda i: (i, 0)),
                   pl.BlockSpec((1, LANES), lambda i: (0, 0))],
        out_shape=[jax.ShapeDtypeStruct((t, TOP_K), jnp.int32),
                   jax.ShapeDtypeStruct((1, LANES), F32)],
        scratch_shapes=[pltpu.VMEM((1, LANES), F32)],
        compiler_params=_cparams(("arbitrary",)),
        name="rank",
    )(idx)


def _dispatch_kernel(pad_ref, dest_hbm, xn_ref, xs_hbm, dsm, zeros_scr, sem_idx, sem_rows, sem_pad):
    i = pl.program_id(0)
    tm = xn_ref.shape[0]
    n_idx = tm * TOP_K
    idx_copy = pltpu.make_async_copy(dest_hbm.at[pl.ds(i * n_idx, n_idx)], dsm, sem_idx)
    idx_copy.start()

    def pad_copy(off, size):
        return pltpu.make_async_copy(zeros_scr.at[pl.ds(0, size), :], xs_hbm.at[pl.ds(off, size), :], sem_pad)

    def for_each_pad_piece(fn):
        def per_expert(e, carry):
            off = pad_ref[2 * e]
            n = pad_ref[2 * e + 1]
            head = n & (SUBLANES - 1)
            for r in range(SUBLANES - 1):
                @pl.when(r < head)
                def _(r=r):
                    fn(pad_copy(off + r, 1))

            off = off + head
            size = ROW_BLOCK // 2
            while size >= SUBLANES:
                take = (n & size) != 0

                @pl.when(take)
                def _(off=off, size=size):
                    fn(pad_copy(pl.multiple_of(off, SUBLANES), size))

                off = off + jnp.where(take, size, 0)
                size //= 2
            return carry
        lax.fori_loop(0, N_EXPERTS, per_expert, 0)

    @pl.when(i == 0)
    def _():
        zeros_scr[...] = jnp.zeros_like(zeros_scr)
        for_each_pad_piece(lambda cp: cp.start())
        for_each_pad_piece(lambda cp: cp.wait())

    idx_copy.wait()

    def row_copy(t, kk):
        return pltpu.make_async_copy(xn_ref.at[pl.ds(t, 1), :],
                                     xs_hbm.at[pl.ds(dsm[t * TOP_K + kk], 1), :], sem_rows)

    def issue(t, carry):
        for kk in range(TOP_K):
            row_copy(t, kk).start(priority=kk % 2)
        return carry

    lax.fori_loop(0, tm, issue, 0)
    pltpu.make_async_copy(xs_hbm.at[pl.ds(0, n_idx), :], xs_hbm.at[pl.ds(0, n_idx), :], sem_rows).wait()


def _dispatch(pad_info, dest_flat, xn2, n_rows, tm):
    t, d = xn2.shape
    return pl.pallas_call(
        _dispatch_kernel,
        grid_spec=pltpu.PrefetchScalarGridSpec(
            num_scalar_prefetch=1,
            grid=(t // tm,),
            in_specs=[pl.BlockSpec(memory_space=pl.ANY),
                      pl.BlockSpec((tm, d), lambda i, pad: (i, 0))],
            out_specs=pl.BlockSpec(memory_space=pl.ANY),
            scratch_shapes=[pltpu.SMEM((tm * TOP_K,), jnp.int32),
                            pltpu.VMEM((ROW_BLOCK // 2, d), F32),
                            pltpu.SemaphoreType.DMA(()),
                            pltpu.SemaphoreType.DMA(()),
                            pltpu.SemaphoreType.DMA(())]),
        out_shape=jax.ShapeDtypeStruct((n_rows, d), F32),
        compiler_params=_cparams(("arbitrary",)),
        name="dispatch",
    )(pad_info, dest_flat, xn2)


def _new_expert(be_ref, j):
    return jnp.logical_or(j == 0, be_ref[j] != be_ref[jnp.maximum(j - 1, 0)])


EXPERT_SUB = 512


def _for_block_rows(valid, rows, body):
    sub = min(EXPERT_SUB, rows)

    @pl.when(valid == rows)
    def _():
        body(slice(0, rows))

    @pl.when(valid < rows)
    def _():
        for r0 in range(0, rows, sub):
            @pl.when(r0 < valid)
            def _(r0=r0):
                body(slice(r0, r0 + sub))


def _expert_gu_kernel(be_ref, nu_ref, bv_ref, xs_ref, wg_ref, wu_ref, bg_ref, bu_ref, act_ref, wg_scr, wu_scr):
    j = pl.program_id(1)

    @pl.when(j < nu_ref[0])
    def _():
        @pl.when(_new_expert(be_ref, j))
        def _():
            wg_scr[...] = wg_ref[...].astype(BF16)
            wu_scr[...] = wu_ref[...].astype(BF16)

        def body(rows):
            x = xs_ref[rows, :].astype(BF16)
            g = jnp.dot(x, wg_scr[...], preferred_element_type=F32) + bg_ref[...]
            u = jnp.dot(x, wu_scr[...], preferred_element_type=F32) + bu_ref[...]
            gate = jnp.minimum(g, SWIGLU_LIMIT)
            up = jnp.clip(u, -SWIGLU_LIMIT, SWIGLU_LIMIT)
            act_ref[rows, :] = ((up + 1.0) * gate * jax.nn.sigmoid(SWIGLU_ALPHA * gate)).astype(BF16)

        _for_block_rows(bv_ref[j], xs_ref.shape[0], body)


def _expert_gu(blk_expert, n_used, blk_valid, xs, w_gu, b_gu, tn):
    n_rows, d = xs.shape
    dff = w_gu.shape[2] // 2
    nt = dff // tn
    nb = n_rows // ROW_BLOCK
    blk = lambda j, nu: jnp.minimum(j, nu[0] - 1)
    exp = lambda j, be, nu: be[blk(j, nu)]
    return pl.pallas_call(
        _expert_gu_kernel,
        grid_spec=pltpu.PrefetchScalarGridSpec(
            num_scalar_prefetch=3,
            grid=(nt, nb),
            in_specs=[pl.BlockSpec((ROW_BLOCK, d), lambda n, j, be, nu, bv: (blk(j, nu), 0)),
                      pl.BlockSpec((None, d, tn), lambda n, j, be, nu, bv: (exp(j, be, nu), 0, n)),
                      pl.BlockSpec((None, d, tn), lambda n, j, be, nu, bv: (exp(j, be, nu), 0, nt + n)),
                      pl.BlockSpec((None, 1, tn), lambda n, j, be, nu, bv: (exp(j, be, nu), 0, n)),
                      pl.BlockSpec((None, 1, tn), lambda n, j, be, nu, bv: (exp(j, be, nu), 0, nt + n))],
            out_specs=pl.BlockSpec((ROW_BLOCK, tn), lambda n, j, be, nu, bv: (blk(j, nu), n)),
            scratch_shapes=[pltpu.VMEM((d, tn), BF16), pltpu.VMEM((d, tn), BF16)]),
        out_shape=jax.ShapeDtypeStruct((n_rows, dff), BF16),
        compiler_params=_cparams(("arbitrary", "arbitrary")),
        name="expert_gu",
    )(blk_expert, n_used, blk_valid, xs, w_gu, w_gu, b_gu, b_gu)


def _expert_down_kernel(be_ref, nu_ref, bv_ref, act_ref, w_ref, b_ref, y_ref, w_scr):
    j = pl.program_id(1)

    @pl.when(j < nu_ref[0])
    def _():
        @pl.when(_new_expert(be_ref, j))
        def _():
            w_scr[...] = w_ref[...].astype(BF16)

        def body(rows):
            y_ref[rows, :] = jnp.dot(act_ref[rows, :], w_scr[...], preferred_element_type=F32) + b_ref[...]

        _for_block_rows(bv_ref[j], act_ref.shape[0], body)


def _expert_down(blk_expert, n_used, blk_valid, act, w_down, b_down, tn):
    n_rows, dff = act.shape
    d = w_down.shape[2]
    nb = n_rows // ROW_BLOCK
    blk = lambda j, nu: jnp.minimum(j, nu[0] - 1)
    exp = lambda j, be, nu: be[blk(j, nu)]
    return pl.pallas_call(
        _expert_down_kernel,
        grid_spec=pltpu.PrefetchScalarGridSpec(
            num_scalar_prefetch=3,
            grid=(d // tn, nb),
            in_specs=[pl.BlockSpec((ROW_BLOCK, dff), lambda n, j, be, nu, bv: (blk(j, nu), 0)),
                      pl.BlockSpec((None, dff, tn), lambda n, j, be, nu, bv: (exp(j, be, nu), 0, n)),
                      pl.BlockSpec((None, 1, tn), lambda n, j, be, nu, bv: (exp(j, be, nu), 0, n))],
            out_specs=pl.BlockSpec((ROW_BLOCK, tn), lambda n, j, be, nu, bv: (blk(j, nu), n)),
            scratch_shapes=[pltpu.VMEM((dff, tn), BF16)]),
        out_shape=jax.ShapeDtypeStruct((n_rows, d), F32),
        compiler_params=_cparams(("arbitrary", "arbitrary")),
        name="expert_down",
    )(blk_expert, n_used, blk_valid, act, w_down, b_down)


def _combine_kernel(dest_hbm, y_hbm, x1_ref, tw_ref, gt_ref, gfin_ref, out_ref, dsm, buf, sem_idx, sem_rows):
    i = pl.program_id(1) + pl.program_id(0) * pl.num_programs(1)
    n_steps = pl.num_programs(0) * pl.num_programs(1)
    tm = x1_ref.shape[0]
    n_idx = tm * TOP_K

    def idx_copy(tile):
        slot = tile % 2
        return pltpu.make_async_copy(dest_hbm.at[pl.ds(tile * n_idx, n_idx)], dsm.at[slot], sem_idx.at[slot])

    def issue_rows(tile):
        slot = tile % 2

        def issue(t, carry):
            for kk in range(TOP_K):
                pltpu.make_async_copy(y_hbm.at[pl.ds(dsm[slot, t * TOP_K + kk], 1), :],
                                      buf.at[slot, kk, pl.ds(t, 1), :], sem_rows.at[slot]
                                      ).start(priority=kk % 2)
            return carry

        lax.fori_loop(0, tm, issue, 0)

    @pl.when(i == 0)
    def _():
        idx_copy(0).start()
        idx_copy(0).wait()
        issue_rows(0)

        @pl.when(n_steps > 1)
        def _():
            idx_copy(1).start()

    @pl.when(i + 2 < n_steps)
    def _():
        idx_copy(i + 2).start()

    @pl.when(i + 1 < n_steps)
    def _():
        idx_copy(i + 1).wait()
        issue_rows(i + 1)

    slot = i % 2
    for kk in range(TOP_K):
        pltpu.make_async_copy(y_hbm.at[pl.ds(0, tm), :], buf.at[slot, kk], sem_rows.at[slot]).wait()

    tw = tw_ref[...]
    acc = buf[slot, 0] * tw[:, 0:1]
    for kk in range(1, TOP_K):
        acc = acc + buf[slot, kk] * tw[:, kk:kk + 1]
    x2 = x1_ref[...] + gt_ref[...] * acc
    out_ref[...] = x2 * lax.rsqrt(jnp.mean(x2 * x2, axis=-1, keepdims=True) + EPS) * gfin_ref[...]


def _combine(dest_flat, y, x1, tw, gt, g_final, bsz, s, tm):
    t, d = x1.shape
    nt = s // tm
    row = lambda b, i: (b * nt + i, 0)
    return pl.pallas_call(
        _combine_kernel,
        grid=(bsz, nt),
        in_specs=[pl.BlockSpec(memory_space=pl.ANY),
                  pl.BlockSpec(memory_space=pl.ANY),
                  pl.BlockSpec((tm, d), row),
                  pl.BlockSpec((tm, TOP_K), row),
                  pl.BlockSpec((None, 1, d), lambda b, i: (b, 0, 0)),
                  pl.BlockSpec((1, d), lambda b, i: (0, 0))],
        out_specs=pl.BlockSpec((tm, d), row),
        out_shape=jax.ShapeDtypeStruct((t, d), F32),
        scratch_shapes=[pltpu.SMEM((2, tm * TOP_K), jnp.int32),
                        pltpu.VMEM((2, TOP_K, tm, d), F32),
                        pltpu.SemaphoreType.DMA((2,)),
                        pltpu.SemaphoreType.DMA((2,))],
        compiler_params=_cparams(("arbitrary", "arbitrary")),
        name="combine",
    )(dest_flat, y, x1, tw, gt, g_final)


def _pad_lanes(a, value=0.0):
    return jnp.pad(a, ((0, 0), (0, LANES - a.shape[1])), constant_values=value)


def _routing_tables(idx, rank, counts_f, n_blocks):
    counts = counts_f[0, :N_EXPERTS].astype(jnp.int32)
    padded = (counts + ROW_BLOCK - 1) // ROW_BLOCK * ROW_BLOCK
    pend = jnp.cumsum(padded)
    pstart = pend - padded
    dest = (pstart[idx] + rank).reshape(-1)
    blk_start = jnp.arange(n_blocks, dtype=jnp.int32) * ROW_BLOCK
    blk_expert = jnp.minimum(jnp.sum((pend[None, :] <= blk_start[:, None]).astype(jnp.int32), axis=1),
                             N_EXPERTS - 1)
    n_used = (pend[-1:] // ROW_BLOCK).astype(jnp.int32)
    blk_valid = jnp.clip((pstart + counts)[blk_expert] - blk_start, 0, ROW_BLOCK).astype(jnp.int32)
    pad_info = jnp.stack([pstart + counts, padded - counts], axis=1).reshape(-1).astype(jnp.int32)
    return dest, blk_expert, n_used, blk_valid, pad_info


def _layer(x, c, ctx, c_ctx, w_ada, b_ada, g_mix, w_in, b_if, conv_w, norm_g, w_out,
           g_ffn, w_router, b_router, w_gu, b_gu, w_down, b_down, g_final):
    bsz, s, d = x.shape
    s_ctx = ctx.shape[1]

    cond = jnp.zeros((8, d), F32).at[:bsz].set(c).at[bsz].set(c_ctx)
    mod = _adaln(cond, w_ada, b_ada[None, :])
    sh_m, sc_m, gt_m, sh_f, sc_f, gt_f = [m[:, None, :] for m in jnp.split(mod, N_MOD, axis=-1)]
    lat = lambda m: m[:bsz]
    ctxm = lambda m: jnp.broadcast_to(m[bsz:bsz + 1], (bsz, 1, d))

    g0 = 2 * QK_COLS + 2 * MLSTM_WIDTH
    w_main = jnp.concatenate([w_in[:, :g0], w_in[:, g0 + N_GATE_COLS:]], axis=1).astype(BF16)
    w_gate = _pad_lanes(w_in[:, g0:g0 + N_GATE_COLS]).astype(BF16)
    b_gate = _pad_lanes(b_if[None, :])
    g_mix2 = g_mix[None, :]

    proj_c, gpre_c = _inproj(ctx, g_mix2, ctxm(sh_m), ctxm(sc_m), w_main, w_gate, min(s_ctx, 512))
    gcol_c, grow_c = _gates(gpre_c, b_gate, 512)
    zeros_state = (jnp.zeros((bsz, 2 * N_HEADS, DK, DVX), F32),
                   jnp.zeros((bsz, 2 * N_HEADS, 1, LANES), F32))
    _, _, c0, m0 = _mlstm(proj_c, gcol_c, grow_c, bsz, s_ctx, *zeros_state)

    proj, gpre = _inproj(x, g_mix2, lat(sh_m), lat(sc_m), w_main, w_gate, 512)
    gcol, grow = _gates(gpre, b_gate, 512)
    hf, hb, _, _ = _mlstm(proj, gcol, grow, bsz, s, c0, m0)
    x1, xn2, idx, tw = _mixout(
        proj, hf, hb, x.reshape(bsz * s, d), conv_w, norm_g[None, :], w_out.astype(BF16), lat(gt_m),
        g_ffn[None, :], lat(sh_f), lat(sc_f), _pad_lanes(w_router).astype(BF16),
        _pad_lanes(b_router[None, :], NEG_BIG), bsz, s, 512)

    t = bsz * s
    n_blocks = -(-(t * TOP_K) // ROW_BLOCK) + N_EXPERTS
    rank, counts = _rank(idx, 512)
    dest, blk_expert, n_used, blk_valid, pad_info = _routing_tables(idx, rank, counts, n_blocks)
    xs = _dispatch(pad_info, dest, xn2, n_blocks * ROW_BLOCK, 512)
    act = _expert_gu(blk_expert, n_used, blk_valid, xs, w_gu, b_gu[:, None, :], 512)
    y = _expert_down(blk_expert, n_used, blk_valid, act, w_down, b_down[:, None, :], 1024)
    out = _combine(dest, y, x1, tw, lat(gt_f), g_final[None, :], bsz, s, 256)
    return out.reshape(bsz, s, d)


def kernel(x, c, ctx, c_ctx, w_ada, b_ada, g_mix, w_in, b_if, conv_w, mlstm_norm_g, w_out,
           g_ffn, w_router, b_router, w_gu, b_gu, w_down, b_down, g_final):
    return _layer(x, c, ctx, c_ctx, w_ada[0], b_ada[0], g_mix[0], w_in[0], b_if[0], conv_w[0],
                  mlstm_norm_g[0], w_out[0], g_ffn[0], w_router[0], b_router[0], w_gu[0], b_gu[0],
                  w_down[0], b_down[0], g_final)
```

```python
import functools

import jax
import jax.numpy as jnp
from jax import lax
from jax.experimental import pallas as pl
from jax.experimental.pallas import tpu as pltpu

F32 = jnp.float32
BF16 = jnp.bfloat16

N_HEADS = 4
DK = 128
DV = 256
QK_COLS = N_HEADS * DK
MLSTM_WIDTH = N_HEADS * DV
CONV_WIDTH = 1024
CONV_HALF = CONV_WIDTH // 2
N_GATE_COLS = 4 * N_HEADS
GRID_W = 64
CHUNK = 128
GATE_SOFT_CAP = 15.0
N_EXPERTS = 32
TOP_K = 4
SWIGLU_LIMIT = 7.0
SWIGLU_ALPHA = 1.702
N_MOD = 6
EPS = 1e-6
LANES = 128
SUBLANES = 8
ROW_BLOCK = 1024
DOWN_COLS = 1024
NEG_BIG = -1e30
VMEM_LIMIT = 56 * 1024 * 1024


def _cparams(sem):
    return pltpu.CompilerParams(dimension_semantics=sem, vmem_limit_bytes=VMEM_LIMIT)


def _pack_pairs(x):
    bits = lax.bitcast_convert_type(x.astype(BF16).astype(F32), jnp.uint32)
    g = x.shape[1] // 2
    return bits[:, :g] | (bits[:, g:] >> 16)


def _unpack_pairs(p, group):
    hi = lax.bitcast_convert_type(p & jnp.uint32(0xFFFF0000), F32)
    lo = lax.bitcast_convert_type(p << 16, F32)
    parts = []
    for g0 in range(0, p.shape[1], group):
        parts += [hi[:, g0:g0 + group], lo[:, g0:g0 + group]]
    return jnp.concatenate(parts, axis=1)


def _adaln_kernel(c_ref, w_ref, b_ref, o_ref):
    s = c_ref[...]
    s = s * jax.nn.sigmoid(s)
    o_ref[...] = jnp.dot(s.astype(BF16), w_ref[...].astype(BF16),
                         preferred_element_type=F32) + b_ref[...]


def _adaln(cond, w, b):
    d, n = w.shape
    tn = 1024
    return pl.pallas_call(
        _adaln_kernel,
        grid=(n // tn,),
        in_specs=[pl.BlockSpec((8, d), lambda j: (0, 0)),
                  pl.BlockSpec((d, tn), lambda j: (0, j)),
                  pl.BlockSpec((1, tn), lambda j: (0, j))],
        out_specs=pl.BlockSpec((8, tn), lambda j: (0, j)),
        out_shape=jax.ShapeDtypeStruct((8, n), F32),
        compiler_params=_cparams(("arbitrary",)),
        name="adaln",
    )(cond, w, b)


INPROJ_COLS = 1024


def _inproj_kernel(x_ref, g_ref, sh_ref, sc_ref, w_ref, wg_ref, proj_ref, gate_ref):
    x = x_ref[...]
    y = x * lax.rsqrt(jnp.mean(x * x, axis=-1, keepdims=True) + EPS) * g_ref[...]
    xn = (y * (1.0 + sc_ref[...]) + sh_ref[...]).astype(BF16)
    gate_ref[...] = jnp.dot(xn, wg_ref[...], preferred_element_type=F32)
    for j in range(w_ref.shape[1] // INPROJ_COLS):
        cols = slice(j * INPROJ_COLS, (j + 1) * INPROJ_COLS)
        proj_ref[:, cols] = jnp.dot(xn, w_ref[:, cols], preferred_element_type=F32).astype(BF16)


def _inproj(x, g, sh, sc, w, wg, tm):
    bsz, s, d = x.shape
    p = w.shape[1]
    nt = s // tm
    x2 = x.reshape(bsz * s, d)
    resident = lambda shape: pl.BlockSpec(shape, lambda b, i: (0, 0), pipeline_mode=pl.Buffered(1))
    return pl.pallas_call(
        _inproj_kernel,
        grid=(bsz, nt),
        in_specs=[pl.BlockSpec((tm, d), lambda b, i: (b * nt + i, 0)),
                  pl.BlockSpec((1, d), lambda b, i: (0, 0)),
                  pl.BlockSpec((None, 1, d), lambda b, i: (b, 0, 0)),
                  pl.BlockSpec((None, 1, d), lambda b, i: (b, 0, 0)),
                  resident((d, p)),
                  resident((d, LANES))],
        out_specs=[pl.BlockSpec((tm, p), lambda b, i: (b * nt + i, 0)),
                   pl.BlockSpec((tm, LANES), lambda b, i: (b * nt + i, 0))],
        out_shape=[jax.ShapeDtypeStruct((bsz * s, p), BF16),
                   jax.ShapeDtypeStruct((bsz * s, LANES), F32)],
        compiler_params=_cparams(("arbitrary", "arbitrary")),
        name="inproj",
    )(x2, g, sh, sc, w, wg)


def _log_sigmoid(x):
    return jnp.minimum(x, 0.0) - jnp.log1p(jnp.exp(-jnp.abs(x)))


def _gates_kernel(g_ref, b_ref, gc_ref, gr_ref):
    tm = g_ref.shape[0]
    row = lax.broadcasted_iota(jnp.int32, (tm, LANES), 0)
    lane = lax.broadcasted_iota(jnp.int32, (tm, LANES), 1)
    gp = GATE_SOFT_CAP * jnp.tanh((g_ref[...] + b_ref[...]) / GATE_SOFT_CAP)
    is_f = ((lane >> 2) & 1) == 1
    fwd_lane = lane < 2 * N_HEADS
    lf = jnp.where(is_f, _log_sigmoid(gp), 0.0)
    r2 = lax.broadcasted_iota(jnp.int32, (tm, tm), 0)
    c2 = lax.broadcasted_iota(jnp.int32, (tm, tm), 1)
    same_chunk = (r2 // CHUNK) == (c2 // CHUNK)
    lower = jnp.logical_and(same_chunk, r2 >= c2).astype(F32)
    upper = jnp.logical_and(same_chunk, r2 <= c2).astype(F32)
    cf = jnp.dot(lower, lf, precision=lax.Precision.HIGHEST, preferred_element_type=F32)
    cb = jnp.dot(upper, lf, precision=lax.Precision.HIGHEST, preferred_element_type=F32)
    cdir = jnp.where(fwd_lane, cf, cb)
    a = jnp.where(is_f, cdir, gp - pltpu.roll(cdir, LANES - N_HEADS, 1))

    pos = row % CHUNK
    x = a
    k = 1
    while k < CHUNK:
        from_before = jnp.where(pos >= k, pltpu.roll(x, k, 0), -jnp.inf)
        from_after = jnp.where(pos < CHUNK - k, pltpu.roll(x, tm - k, 0), -jnp.inf)
        x = jnp.maximum(x, jnp.where(fwd_lane, from_before, from_after))
        k *= 2
    gc_ref[...] = jnp.where(is_f, a, x)

    lane_c = lax.broadcasted_iota(jnp.int32, (CHUNK, LANES), 1)
    lane_1 = lax.broadcasted_iota(jnp.int32, (1, LANES), 1)
    for c in range(tm // CHUNK):
        lo = c * CHUNK
        xc, ac = x[lo:lo + CHUNK], a[lo:lo + CHUNK]
        end_max = jnp.where(lane_1 < 2 * N_HEADS, xc[CHUNK - 1:CHUNK], xc[0:1])
        e = jnp.exp(ac - end_max)
        rows = jnp.where(((lane_c >> 2) & 1) == 1, pltpu.roll(e, N_HEADS, 1), ac)
        gr_ref[:, lo:lo + CHUNK] = rows.T[:N_GATE_COLS, :]


def _gates(gpre, b_if, tm):
    t = gpre.shape[0]
    return pl.pallas_call(
        _gates_kernel,
        grid=(t // tm,),
        in_specs=[pl.BlockSpec((tm, LANES), lambda i: (i, 0)),
                  pl.BlockSpec((1, LANES), lambda i: (0, 0))],
        out_specs=[pl.BlockSpec((tm, LANES), lambda i: (i, 0)),
                   pl.BlockSpec((N_GATE_COLS, tm), lambda i: (0, i))],
        out_shape=[jax.ShapeDtypeStruct((t, LANES), F32),
                   jax.ShapeDtypeStruct((N_GATE_COLS, t), F32)],
        compiler_params=_cparams(("arbitrary",)),
        name="gates",
    )(gpre, b_if)


DVX = DV + LANES


def _mlstm_chunk(q, k, v_ext, rmax_col, b_col, r_row, e_row, b_last, rmax_last, mask, cx, m_st):
    scale = DK ** -0.5
    mb = jnp.maximum(m_st, jnp.broadcast_to(rmax_col, (CHUNK, CHUNK)))
    w_intra = jnp.exp(jnp.where(mask, r_row - mb, -jnp.inf))
    w_state = jnp.exp(m_st - mb)
    qk = lax.dot_general(q, k, (((1,), (1,)), ((), ())), preferred_element_type=F32)
    s = qk * (w_intra * scale)
    lhs = jnp.concatenate([s.astype(BF16), (q.astype(F32) * (w_state * scale)).astype(BF16)], axis=1)
    rhs = jnp.concatenate([v_ext, cx.astype(BF16)], axis=0)
    nx = jnp.dot(lhs, rhs, preferred_element_type=F32)
    denom = jnp.maximum(jnp.abs(nx[:, DV:]), jnp.exp(-(jnp.broadcast_to(b_col, (CHUNK, CHUNK)) + mb)))
    h = nx[:, :DV] / jnp.concatenate([denom, denom], axis=1)
    ke_t = (k.T.astype(F32) * e_row).astype(BF16)
    c_loc = jnp.dot(ke_t, v_ext, preferred_element_type=F32)
    m_loc = b_last + rmax_last
    m_new = jnp.maximum(b_last + m_st, m_loc)
    return h, jnp.exp(b_last + m_st - m_new) * cx + jnp.exp(m_loc - m_new) * c_loc, m_new


def _mlstm_kernel(qf_ref, kf_ref, vf_ref, gcf_ref, grf_ref, qb_ref, kb_ref, vb_ref, gcb_ref, grb_ref,
                  c0_ref, m0_ref, hf_ref, hb_ref, cout_ref, mout_ref, m_scr, *c_scrs):
    c = pl.program_id(1)

    @pl.when(c == 0)
    def _():
        for idx, c_scr in enumerate(c_scrs):
            c_scr[...] = c0_ref[idx]
        m_scr[...] = m0_ref[...]

    row = lax.broadcasted_iota(jnp.int32, (CHUNK, CHUNK), 0)
    col = lax.broadcasted_iota(jnp.int32, (CHUNK, CHUNK), 1)
    ones = jnp.ones((CHUNK, LANES), BF16)
    m_all = m_scr[...]
    dirs = ((qf_ref, kf_ref, vf_ref, gcf_ref, grf_ref, hf_ref, 0, CHUNK - 1, col <= row),
            (qb_ref, kb_ref, vb_ref, gcb_ref, grb_ref, hb_ref, 2 * N_HEADS, 0, col >= row))
    m_news = []
    for di, (q_ref, k_ref, v_ref, gc_ref, gr_ref, h_ref, off, last, mask) in enumerate(dirs):
        for hd in range(N_HEADS):
            idx = di * N_HEADS + hd
            lr, lb = off + hd, off + N_HEADS + hd
            v_ext = jnp.concatenate([v_ref[:, hd * DV:(hd + 1) * DV], ones], axis=1)
            h, c_new, m_new = _mlstm_chunk(
                q_ref[:, hd * DK:(hd + 1) * DK], k_ref[:, hd * DK:(hd + 1) * DK], v_ext,
                gc_ref[:, lr:lr + 1], gc_ref[:, lb:lb + 1], gr_ref[lr:lr + 1, :], gr_ref[lb:lb + 1, :],
                gc_ref[last:last + 1, lb:lb + 1], gc_ref[last:last + 1, lr:lr + 1], mask,
                c_scrs[idx][...], m_all[idx][:, 0:1])
            h_ref[:, hd * DV:(hd + 1) * DV] = h
            c_scrs[idx][...] = c_new
            m_news.append(jnp.broadcast_to(m_new, (1, LANES)))
    for idx, m_new in enumerate(m_news):
        m_scr[idx] = m_new

    @pl.when(c == pl.num_programs(1) - 1)
    def _():
        for idx, c_scr in enumerate(c_scrs):
            cout_ref[idx] = c_scr[...]
        mout_ref[...] = m_scr[...]


def _mlstm(proj, gcol, grow, bsz, s, c0, m0):
    nc = s // CHUNK
    t = bsz * s
    fwd = lambda b, c: b * nc + c
    bwd = lambda b, c: b * nc + (nc - 1 - c)

    def specs(ci):
        return [pl.BlockSpec((CHUNK, QK_COLS), lambda b, c: (ci(b, c), 0)),
                pl.BlockSpec((CHUNK, QK_COLS), lambda b, c: (ci(b, c), 1)),
                pl.BlockSpec((CHUNK, MLSTM_WIDTH), lambda b, c: (ci(b, c), 1)),
                pl.BlockSpec((CHUNK, LANES), lambda b, c: (ci(b, c), 0)),
                pl.BlockSpec((N_GATE_COLS, CHUNK), lambda b, c: (0, ci(b, c)))]

    st_specs = [pl.BlockSpec((None, 2 * N_HEADS, DK, DVX), lambda b, c: (b, 0, 0, 0)),
                pl.BlockSpec((None, 2 * N_HEADS, 1, LANES), lambda b, c: (b, 0, 0, 0))]
    return pl.pallas_call(
        _mlstm_kernel,
        grid=(bsz, nc),
        in_specs=specs(fwd) + specs(bwd) + st_specs,
        out_specs=[pl.BlockSpec((CHUNK, MLSTM_WIDTH), lambda b, c: (fwd(b, c), 0)),
                   pl.BlockSpec((CHUNK, MLSTM_WIDTH), lambda b, c: (bwd(b, c), 0))] + st_specs,
        out_shape=[jax.ShapeDtypeStruct((t, MLSTM_WIDTH), F32),
                   jax.ShapeDtypeStruct((t, MLSTM_WIDTH), F32),
                   jax.ShapeDtypeStruct(c0.shape, F32),
                   jax.ShapeDtypeStruct(m0.shape, F32)],
        scratch_shapes=[pltpu.VMEM((2 * N_HEADS, 1, LANES), F32)]
        + [pltpu.VMEM((DK, DVX), F32) for _ in range(2 * N_HEADS)],
        compiler_params=_cparams(("arbitrary", "arbitrary")),
        name="mlstm",
    )(proj, proj, proj, gcol, grow, proj, proj, proj, gcol, grow, c0, m0)


MIX_ROWS = 256


def _mixout_kernel(o_ref, cb_ref, cc_ref, cx_ref, ccp_ref, cxp_ref, ccn_ref, cxn_ref, hf_ref, hb_ref, x_ref,
                   cw_ref, ng_ref, wout_ref, gt_ref, gffn_ref, shf_ref, scf_ref, wr_ref, br_ref,
                   x1_ref, xn2_ref, idx_ref, tw_ref):
    i = pl.program_id(1)
    tm = x_ref.shape[0]
    cw = cw_ref[...]

    has_prev = jnp.where(i > 0, 1.0, 0.0)
    has_next = jnp.where(i < pl.num_programs(1) - 1, 1.0, 0.0)
    up = ccp_ref[...].astype(F32) * cxp_ref[...].astype(F32) * has_prev
    un = ccn_ref[...].astype(F32) * cxn_ref[...].astype(F32) * has_next
    uv = cc_ref[:, CONV_HALF:].astype(F32) * cx_ref[:, CONV_HALF:].astype(F32)
    ext = jnp.concatenate([up, uv, un], axis=0)

    pos = lax.broadcasted_iota(jnp.int32, (MIX_ROWS, CONV_HALF), 0) & (GRID_W - 1)
    lane_f = lax.broadcasted_iota(jnp.int32, (MIX_ROWS, LANES), 1).astype(F32)
    lane4 = lax.broadcasted_iota(jnp.int32, (MIX_ROWS, TOP_K), 1)

    for r0 in range(0, tm, MIX_ROWS):
        rows = slice(r0, r0 + MIX_ROWS)

        uh = cc_ref[rows, :CONV_HALF].astype(F32) * cx_ref[rows, :CONV_HALF].astype(F32)
        left = jnp.where(pos == 0, 0.0, pltpu.roll(uh, 1, 0))
        right = jnp.where(pos == GRID_W - 1, 0.0, pltpu.roll(uh, MIX_ROWS - 1, 0))
        yh = cw[0:1, :CONV_HALF] * left + cw[1:2, :CONV_HALF] * uh + cw[2:3, :CONV_HALF] * right
        yv = (cw[0:1, CONV_HALF:] * ext[r0:r0 + MIX_ROWS]
              + cw[1:2, CONV_HALF:] * ext[r0 + GRID_W:r0 + GRID_W + MIX_ROWS]
              + cw[2:3, CONV_HALF:] * ext[r0 + 2 * GRID_W:r0 + 2 * GRID_W + MIX_ROWS])
        yc = cb_ref[rows, :].astype(F32) * jnp.concatenate([yh, yv], axis=1)

        hs = hf_ref[rows, :] + hb_ref[rows, :]
        parts = []
        for hd in range(N_HEADS):
            seg = hs[:, hd * DV:(hd + 1) * DV]
            parts.append(seg * lax.rsqrt(jnp.mean(seg * seg, axis=-1, keepdims=True) + EPS))
        hm = jnp.concatenate(parts, axis=1) * ng_ref[...] * jax.nn.sigmoid(o_ref[rows, :].astype(F32))

        z = jnp.concatenate([hm.astype(BF16), yc.astype(BF16)], axis=1)
        x1 = x_ref[rows, :] + gt_ref[...] * jnp.dot(z, wout_ref[...], preferred_element_type=F32)
        x1_ref[rows, :] = x1

        y = x1 * lax.rsqrt(jnp.mean(x1 * x1, axis=-1, keepdims=True) + EPS) * gffn_ref[...]
        xn2 = y * (1.0 + scf_ref[...]) + shf_ref[...]
        xn2_ref[rows, :] = _pack_pairs(xn2)

        logits = jnp.dot(xn2.astype(BF16), wr_ref[...], preferred_element_type=F32) + br_ref[...]
        vals, idxs = [], []
        for _ in range(TOP_K):
            mx = jnp.max(logits, axis=-1, keepdims=True)
            ik = jnp.min(jnp.where(logits == mx, lane_f, float(LANES)), axis=-1, keepdims=True)
            vals.append(mx)
            idxs.append(ik)
            logits = jnp.where(lane_f == ik, -jnp.inf, logits)
        es = [jnp.exp(v - vals[0]) for v in vals]
        tot = es[0] + es[1] + es[2] + es[3]
        idx_out = jnp.zeros((MIX_ROWS, TOP_K), F32)
        tw_out = jnp.zeros((MIX_ROWS, TOP_K), F32)
        for kk in range(TOP_K):
            idx_out = jnp.where(lane4 == kk, idxs[kk], idx_out)
            tw_out = jnp.where(lane4 == kk, es[kk] / tot, tw_out)
        idx_ref[rows, :] = idx_out.astype(jnp.int32)
        tw_ref[rows, :] = tw_out


def _mixout(proj, hf, hb, x2, conv_w, norm_g, w_out, gt, g_ffn, sh_f, sc_f, w_r, b_r, bsz, s, tm):
    t, d = x2.shape
    nt = s // tm
    rb = tm // GRID_W
    last_rb = t // GRID_W - 1
    row = lambda b, i: b * nt + i
    w = MLSTM_WIDTH
    vec = lambda n: pl.BlockSpec((1, n), lambda b, i: (0, 0))
    per_b = pl.BlockSpec((None, 1, d), lambda b, i: (b, 0, 0))
    halo_prev = lambda cblk: pl.BlockSpec(
        (GRID_W, CONV_HALF), lambda b, i: (jnp.maximum(row(b, i) * rb - 1, 0), cblk))
    halo_next = lambda cblk: pl.BlockSpec(
        (GRID_W, CONV_HALF), lambda b, i: (jnp.minimum((row(b, i) + 1) * rb, last_rb), cblk))
    return pl.pallas_call(
        _mixout_kernel,
        grid=(bsz, nt),
        in_specs=[pl.BlockSpec((tm, w), lambda b, i: (row(b, i), 2)),
                  pl.BlockSpec((tm, w), lambda b, i: (row(b, i), 3)),
                  pl.BlockSpec((tm, w), lambda b, i: (row(b, i), 4)),
                  pl.BlockSpec((tm, w), lambda b, i: (row(b, i), 5)),
                  halo_prev(9), halo_prev(11), halo_next(9), halo_next(11),
                  pl.BlockSpec((tm, w), lambda b, i: (row(b, i), 0)),
                  pl.BlockSpec((tm, w), lambda b, i: (row(b, i), 0)),
                  pl.BlockSpec((tm, d), lambda b, i: (row(b, i), 0)),
                  pl.BlockSpec((3, CONV_WIDTH), lambda b, i: (0, 0)),
                  vec(w),
                  pl.BlockSpec((d, d), lambda b, i: (0, 0)),
                  per_b, vec(d), per_b, per_b,
                  pl.BlockSpec((d, LANES), lambda b, i: (0, 0)),
                  vec(LANES)],
        out_specs=[pl.BlockSpec((tm, d), lambda b, i: (row(b, i), 0)),
                   pl.BlockSpec((tm, d // 2), lambda b, i: (row(b, i), 0)),
                   pl.BlockSpec((tm, TOP_K), lambda b, i: (row(b, i), 0)),
                   pl.BlockSpec((tm, TOP_K), lambda b, i: (row(b, i), 0))],
        out_shape=[jax.ShapeDtypeStruct((t, d), F32),
                   jax.ShapeDtypeStruct((t, d // 2), jnp.uint32),
                   jax.ShapeDtypeStruct((t, TOP_K), jnp.int32),
                   jax.ShapeDtypeStruct((t, TOP_K), F32)],
        compiler_params=_cparams(("arbitrary", "arbitrary")),
        name="mixout",
    )(proj, proj, proj, proj, proj, proj, proj, proj, hf, hb, x2,
      conv_w, norm_g, w_out, gt, g_ffn, sh_f, sc_f, w_r, b_r)


def _rank_kernel(idx_ref, rank_ref, cnt_ref, run_scr):
    @pl.when(pl.program_id(0) == 0)
    def _():
        run_scr[...] = jnp.zeros_like(run_scr)

    tm = idx_ref.shape[0]
    idx = idx_ref[...]
    lane = lax.broadcasted_iota(jnp.int32, (tm, LANES), 1)
    hits = [lane == idx[:, kk:kk + 1] for kk in range(TOP_K)]
    onehot = jnp.zeros((tm, LANES), F32)
    for hit in hits:
        onehot = onehot + hit.astype(F32)
    r = lax.broadcasted_iota(jnp.int32, (tm, tm), 0)
    c = lax.broadcasted_iota(jnp.int32, (tm, tm), 1)
    before = jnp.dot((c < r).astype(BF16), onehot.astype(BF16), preferred_element_type=F32) + run_scr[...]
    lane4 = lax.broadcasted_iota(jnp.int32, (tm, TOP_K), 1)
    rank = jnp.zeros((tm, TOP_K), F32)
    for kk, hit in enumerate(hits):
        rk = jnp.sum(jnp.where(hit, before, 0.0), axis=-1, keepdims=True)
        rank = jnp.where(lane4 == kk, rk, rank)
    rank_ref[...] = rank.astype(jnp.int32)
    run_scr[...] = run_scr[...] + jnp.sum(onehot, axis=0, keepdims=True)
    cnt_ref[...] = run_scr[...]


def _rank(idx, tm):
    t = idx.shape[0]
    return pl.pallas_call(
        _rank_kernel,
        grid=(t // tm,),
        in_specs=[pl.BlockSpec((tm, TOP_K), lambda i: (i, 0))],
        out_specs=[pl.BlockSpec((tm, TOP_K), lambda i: (i, 0)),
                   pl.BlockSpec((1, LANES), lambda i: (0, 0))],
        out_shape=[jax.ShapeDtypeStruct((t, TOP_K), jnp.int32),
                   jax.ShapeDtypeStruct((1, LANES), F32)],
        scratch_shapes=[pltpu.VMEM((1, LANES), F32)],
        compiler_params=_cparams(("arbitrary",)),
        name="rank",
    )(idx)


def _dispatch_kernel(pad_ref, dest_hbm, xn_ref, xs_hbm, dsm, zeros_scr, sem_idx, sem_rows, sem_pad):
    i = pl.program_id(0)
    tm = xn_ref.shape[0]
    n_idx = tm * TOP_K
    idx_copy = pltpu.make_async_copy(dest_hbm.at[pl.ds(i * n_idx, n_idx)], dsm, sem_idx)
    idx_copy.start()

    def pad_copy(off, size):
        return pltpu.make_async_copy(zeros_scr.at[pl.ds(0, size), :], xs_hbm.at[pl.ds(off, size), :], sem_pad)

    def for_each_pad_piece(fn):
        def per_expert(e, carry):
            off = pad_ref[2 * e]
            n = pad_ref[2 * e + 1]
            head = n & (SUBLANES - 1)
            for r in range(SUBLANES - 1):
                @pl.when(r < head)
                def _(r=r):
                    fn(pad_copy(off + r, 1))

            off = off + head
            size = ROW_BLOCK // 2
            while size >= SUBLANES:
                take = (n & size) != 0

                @pl.when(take)
                def _(off=off, size=size):
                    fn(pad_copy(pl.multiple_of(off, SUBLANES), size))

                off = off + jnp.where(take, size, 0)
                size //= 2
            return carry
        lax.fori_loop(0, N_EXPERTS, per_expert, 0)

    @pl.when(i == 0)
    def _():
        zeros_scr[...] = jnp.zeros_like(zeros_scr)
        for_each_pad_piece(lambda cp: cp.start())
        for_each_pad_piece(lambda cp: cp.wait())

    idx_copy.wait()

    def row_copy(t, kk):
        return pltpu.make_async_copy(xn_ref.at[pl.ds(t, 1), :],
                                     xs_hbm.at[pl.ds(dsm[t * TOP_K + kk], 1), :], sem_rows)

    def issue(t, carry):
        for kk in range(TOP_K):
            row_copy(t, kk).start()
        return carry

    lax.fori_loop(0, tm, issue, 0)
    pltpu.make_async_copy(xs_hbm.at[pl.ds(0, n_idx), :], xs_hbm.at[pl.ds(0, n_idx), :], sem_rows).wait()


def _dispatch(pad_info, dest_flat, xn2, n_rows, tm):
    t, d = xn2.shape
    return pl.pallas_call(
        _dispatch_kernel,
        grid_spec=pltpu.PrefetchScalarGridSpec(
            num_scalar_prefetch=1,
            grid=(t // tm,),
            in_specs=[pl.BlockSpec(memory_space=pl.ANY),
                      pl.BlockSpec((tm, d), lambda i, pad: (i, 0))],
            out_specs=pl.BlockSpec(memory_space=pl.ANY),
            scratch_shapes=[pltpu.SMEM((tm * TOP_K,), jnp.int32),
                            pltpu.VMEM((ROW_BLOCK // 2, d), xn2.dtype),
                            pltpu.SemaphoreType.DMA(()),
                            pltpu.SemaphoreType.DMA(()),
                            pltpu.SemaphoreType.DMA(())]),
        out_shape=jax.ShapeDtypeStruct((n_rows, d), xn2.dtype),
        compiler_params=_cparams(("arbitrary",)),
        name="dispatch",
    )(pad_info, dest_flat, xn2)


def _new_expert(be_ref, j):
    return jnp.logical_or(j == 0, be_ref[j] != be_ref[jnp.maximum(j - 1, 0)])


def _expert_gu_kernel(be_ref, nu_ref, xs_ref, wg_ref, wu_ref, bg_ref, bu_ref, act_ref, wg_scr, wu_scr):
    j = pl.program_id(1)

    @pl.when(j < nu_ref[0])
    def _():
        @pl.when(_new_expert(be_ref, j))
        def _():
            wg_scr[...] = wg_ref[...].astype(BF16)
            wu_scr[...] = wu_ref[...].astype(BF16)

        x = _unpack_pairs(xs_ref[...], xs_ref.shape[1]).astype(BF16)
        g = jnp.dot(x, wg_scr[...], preferred_element_type=F32) + bg_ref[...]
        u = jnp.dot(x, wu_scr[...], preferred_element_type=F32) + bu_ref[...]
        gate = jnp.minimum(g, SWIGLU_LIMIT)
        up = jnp.clip(u, -SWIGLU_LIMIT, SWIGLU_LIMIT)
        act_ref[...] = ((up + 1.0) * gate * jax.nn.sigmoid(SWIGLU_ALPHA * gate)).astype(BF16)


def _expert_gu(blk_expert, n_used, xs, w_gu, b_gu, tn):
    n_rows, dp = xs.shape
    d = w_gu.shape[1]
    dff = w_gu.shape[2] // 2
    nt = dff // tn
    nb = n_rows // ROW_BLOCK
    blk = lambda j, nu: jnp.minimum(j, nu[0] - 1)
    exp = lambda j, be, nu: be[blk(j, nu)]
    return pl.pallas_call(
        _expert_gu_kernel,
        grid_spec=pltpu.PrefetchScalarGridSpec(
            num_scalar_prefetch=2,
            grid=(nt, nb),
            in_specs=[pl.BlockSpec((ROW_BLOCK, dp), lambda n, j, be, nu: (blk(j, nu), 0)),
                      pl.BlockSpec((None, d, tn), lambda n, j, be, nu: (exp(j, be, nu), 0, n)),
                      pl.BlockSpec((None, d, tn), lambda n, j, be, nu: (exp(j, be, nu), 0, nt + n)),
                      pl.BlockSpec((None, 1, tn), lambda n, j, be, nu: (exp(j, be, nu), 0, n)),
                      pl.BlockSpec((None, 1, tn), lambda n, j, be, nu: (exp(j, be, nu), 0, nt + n))],
            out_specs=pl.BlockSpec((ROW_BLOCK, tn), lambda n, j, be, nu: (blk(j, nu), n)),
            scratch_shapes=[pltpu.VMEM((d, tn), BF16), pltpu.VMEM((d, tn), BF16)]),
        out_shape=jax.ShapeDtypeStruct((n_rows, dff), BF16),
        compiler_params=_cparams(("arbitrary", "arbitrary")),
        name="expert_gu",
    )(blk_expert, n_used, xs, w_gu, w_gu, b_gu, b_gu)


def _expert_down_kernel(be_ref, nu_ref, act_ref, w_ref, b_ref, y_ref, w_scr):
    j = pl.program_id(1)

    @pl.when(j < nu_ref[0])
    def _():
        @pl.when(_new_expert(be_ref, j))
        def _():
            w_scr[...] = w_ref[...].astype(BF16)

        y_ref[...] = _pack_pairs(jnp.dot(act_ref[...], w_scr[...], preferred_element_type=F32) + b_ref[...])


def _expert_down(blk_expert, n_used, act, w_down, b_down, tn):
    n_rows, dff = act.shape
    d = w_down.shape[2]
    nb = n_rows // ROW_BLOCK
    blk = lambda j, nu: jnp.minimum(j, nu[0] - 1)
    exp = lambda j, be, nu: be[blk(j, nu)]
    return pl.pallas_call(
        _expert_down_kernel,
        grid_spec=pltpu.PrefetchScalarGridSpec(
            num_scalar_prefetch=2,
            grid=(d // tn, nb),
            in_specs=[pl.BlockSpec((ROW_BLOCK, dff), lambda n, j, be, nu: (blk(j, nu), 0)),
                      pl.BlockSpec((None, dff, tn), lambda n, j, be, nu: (exp(j, be, nu), 0, n)),
                      pl.BlockSpec((None, 1, tn), lambda n, j, be, nu: (exp(j, be, nu), 0, n))],
            out_specs=pl.BlockSpec((ROW_BLOCK, tn // 2), lambda n, j, be, nu: (blk(j, nu), n)),
            scratch_shapes=[pltpu.VMEM((dff, tn), BF16)]),
        out_shape=jax.ShapeDtypeStruct((n_rows, d // 2), jnp.uint32),
        compiler_params=_cparams(("arbitrary", "arbitrary")),
        name="expert_down",
    )(blk_expert, n_used, act, w_down, b_down)


def _combine_kernel(dest_hbm, y_hbm, x1_ref, tw_ref, gt_ref, gfin_ref, out_ref, dsm, buf, sem_idx, sem_rows):
    i = pl.program_id(1) + pl.program_id(0) * pl.num_programs(1)
    tm = x1_ref.shape[0]
    n_idx = tm * TOP_K
    idx_copy = pltpu.make_async_copy(dest_hbm.at[pl.ds(i * n_idx, n_idx)], dsm, sem_idx)
    idx_copy.start()
    idx_copy.wait()

    def row_copy(t, kk):
        return pltpu.make_async_copy(y_hbm.at[pl.ds(dsm[t * TOP_K + kk], 1), :],
                                     buf.at[kk, pl.ds(t, 1), :], sem_rows)

    def issue(t, carry):
        for kk in range(TOP_K):
            row_copy(t, kk).start()
        return carry

    lax.fori_loop(0, tm, issue, 0)
    for kk in range(TOP_K):
        pltpu.make_async_copy(y_hbm.at[pl.ds(0, tm), :], buf.at[kk], sem_rows).wait()

    tw = tw_ref[...]
    acc = _unpack_pairs(buf[0], DOWN_COLS // 2) * tw[:, 0:1]
    for kk in range(1, TOP_K):
        acc = acc + _unpack_pairs(buf[kk], DOWN_COLS // 2) * tw[:, kk:kk + 1]
    x2 = x1_ref[...] + gt_ref[...] * acc
    out_ref[...] = x2 * lax.rsqrt(jnp.mean(x2 * x2, axis=-1, keepdims=True) + EPS) * gfin_ref[...]


def _combine(dest_flat, y, x1, tw, gt, g_final, bsz, s, tm):
    t, d = x1.shape
    nt = s // tm
    row = lambda b, i: (b * nt + i, 0)
    return pl.pallas_call(
        _combine_kernel,
        grid=(bsz, nt),
        in_specs=[pl.BlockSpec(memory_space=pl.ANY),
                  pl.BlockSpec(memory_space=pl.ANY),
                  pl.BlockSpec((tm, d), row),
                  pl.BlockSpec((tm, TOP_K), row),
                  pl.BlockSpec((None, 1, d), lambda b, i: (b, 0, 0)),
                  pl.BlockSpec((1, d), lambda b, i: (0, 0))],
        out_specs=pl.BlockSpec((tm, d), row),
        out_shape=jax.ShapeDtypeStruct((t, d), F32),
        scratch_shapes=[pltpu.SMEM((tm * TOP_K,), jnp.int32),
                        pltpu.VMEM((TOP_K, tm, y.shape[1]), y.dtype),
                        pltpu.SemaphoreType.DMA(()),
                        pltpu.SemaphoreType.DMA(())],
        compiler_params=_cparams(("arbitrary", "arbitrary")),
        name="combine",
    )(dest_flat, y, x1, tw, gt, g_final)


def _pad_lanes(a, value=0.0):
    return jnp.pad(a, ((0, 0), (0, LANES - a.shape[1])), constant_values=value)


def _routing_tables(idx, rank, counts_f, n_blocks):
    counts = counts_f[0, :N_EXPERTS].astype(jnp.int32)
    padded = (counts + ROW_BLOCK - 1) // ROW_BLOCK * ROW_BLOCK
    pend = jnp.cumsum(padded)
    pstart = pend - padded
    dest = (pstart[idx] + rank).reshape(-1)
    blk_start = jnp.arange(n_blocks, dtype=jnp.int32) * ROW_BLOCK
    blk_expert = jnp.minimum(jnp.sum((pend[None, :] <= blk_start[:, None]).astype(jnp.int32), axis=1),
                             N_EXPERTS - 1)
    n_used = (pend[-1:] // ROW_BLOCK).astype(jnp.int32)
    pad_info = jnp.stack([pstart + counts, padded - counts], axis=1).reshape(-1).astype(jnp.int32)
    return dest, blk_expert, n_used, pad_info


def _layer(x, c, ctx, c_ctx, w_ada, b_ada, g_mix, w_in, b_if, conv_w, norm_g, w_out,
           g_ffn, w_router, b_router, w_gu, b_gu, w_down, b_down, g_final):
    bsz, s, d = x.shape
    s_ctx = ctx.shape[1]

    cond = jnp.zeros((8, d), F32).at[:bsz].set(c).at[bsz].set(c_ctx)
    mod = _adaln(cond, w_ada, b_ada[None, :])
    sh_m, sc_m, gt_m, sh_f, sc_f, gt_f = [m[:, None, :] for m in jnp.split(mod, N_MOD, axis=-1)]
    lat = lambda m: m[:bsz]
    ctxm = lambda m: jnp.broadcast_to(m[bsz:bsz + 1], (bsz, 1, d))

    g0 = 2 * QK_COLS + 2 * MLSTM_WIDTH
    w_main = jnp.concatenate([w_in[:, :g0], w_in[:, g0 + N_GATE_COLS:]], axis=1).astype(BF16)
    w_gate = _pad_lanes(w_in[:, g0:g0 + N_GATE_COLS]).astype(BF16)
    b_gate = _pad_lanes(b_if[None, :])
    g_mix2 = g_mix[None, :]

    proj_c, gpre_c = _inproj(ctx, g_mix2, ctxm(sh_m), ctxm(sc_m), w_main, w_gate, min(s_ctx, 512))
    gcol_c, grow_c = _gates(gpre_c, b_gate, 512)
    zeros_state = (jnp.zeros((bsz, 2 * N_HEADS, DK, DVX), F32),
                   jnp.zeros((bsz, 2 * N_HEADS, 1, LANES), F32))
    _, _, c0, m0 = _mlstm(proj_c, gcol_c, grow_c, bsz, s_ctx, *zeros_state)

    proj, gpre = _inproj(x, g_mix2, lat(sh_m), lat(sc_m), w_main, w_gate, 512)
    gcol, grow = _gates(gpre, b_gate, 512)
    hf, hb, _, _ = _mlstm(proj, gcol, grow, bsz, s, c0, m0)
    x1, xn2, idx, tw = _mixout(
        proj, hf, hb, x.reshape(bsz * s, d), conv_w, norm_g[None, :], w_out.astype(BF16), lat(gt_m),
        g_ffn[None, :], lat(sh_f), lat(sc_f), _pad_lanes(w_router).astype(BF16),
        _pad_lanes(b_router[None, :], NEG_BIG), bsz, s, 512)

    t = bsz * s
    n_blocks = -(-(t * TOP_K) // ROW_BLOCK) + N_EXPERTS
    rank, counts = _rank(idx, 512)
    dest, blk_expert, n_used, pad_info = _routing_tables(idx, rank, counts, n_blocks)
    xs = _dispatch(pad_info, dest, xn2, n_blocks * ROW_BLOCK, 512)
    act = _expert_gu(blk_expert, n_used, xs, w_gu, b_gu[:, None, :], 512)
    y = _expert_down(blk_expert, n_used, act, w_down, b_down[:, None, :], DOWN_COLS)
    out = _combine(dest, y, x1, tw, lat(gt_f), g_final[None, :], bsz, s, 256)
    return out.reshape(bsz, s, d)


def kernel(x, c, ctx, c_ctx, w_ada, b_ada, g_mix, w_in, b_if, conv_w, mlstm_norm_g, w_out,
           g_ffn, w_router, b_router, w_gu, b_gu, w_down, b_down, g_final):
    return _layer(x, c, ctx, c_ctx, w_ada[0], b_ada[0], g_mix[0], w_in[0], b_if[0], conv_w[0],
                  mlstm_norm_g[0], w_out[0], g_ffn[0], w_router[0], b_router[0], w_gu[0], b_gu[0],
                  w_down[0], b_down[0], g_final)
```

```python
import functools

import jax
import jax.numpy as jnp
from jax import lax
from jax.experimental import pallas as pl
from jax.experimental.pallas import tpu as pltpu

F32 = jnp.float32
BF16 = jnp.bfloat16

N_HEADS = 4
DK = 128
DV = 256
QK_COLS = N_HEADS * DK
MLSTM_WIDTH = N_HEADS * DV
CONV_WIDTH = 1024
CONV_HALF = CONV_WIDTH // 2
N_GATE_COLS = 4 * N_HEADS
GRID_W = 64
CHUNK = 128
GATE_SOFT_CAP = 15.0
N_EXPERTS = 32
TOP_K = 4
SWIGLU_LIMIT = 7.0
SWIGLU_ALPHA = 1.702
N_MOD = 6
EPS = 1e-6
LANES = 128
SUBLANES = 8
ROW_BLOCK = 1024
DOWN_COLS = 1024
ISSUE_UNROLL = 8
NEG_BIG = -1e30
VMEM_LIMIT = 56 * 1024 * 1024


def _cparams(sem):
    return pltpu.CompilerParams(dimension_semantics=sem, vmem_limit_bytes=VMEM_LIMIT)


def _pack_pairs(x):
    bits = lax.bitcast_convert_type(x.astype(BF16).astype(F32), jnp.uint32)
    g = x.shape[1] // 2
    return bits[:, :g] | (bits[:, g:] >> 16)


def _unpack_pairs(p, group):
    hi = lax.bitcast_convert_type(p & jnp.uint32(0xFFFF0000), F32)
    lo = lax.bitcast_convert_type(p << 16, F32)
    parts = []
    for g0 in range(0, p.shape[1], group):
        parts += [hi[:, g0:g0 + group], lo[:, g0:g0 + group]]
    return jnp.concatenate(parts, axis=1)


def _adaln_kernel(c_ref, w_ref, b_ref, o_ref):
    s = c_ref[...]
    s = s * jax.nn.sigmoid(s)
    o_ref[...] = jnp.dot(s.astype(BF16), w_ref[...].astype(BF16),
                         preferred_element_type=F32) + b_ref[...]


def _adaln(cond, w, b):
    d, n = w.shape
    tn = 1024
    return pl.pallas_call(
        _adaln_kernel,
        grid=(n // tn,),
        in_specs=[pl.BlockSpec((8, d), lambda j: (0, 0)),
                  pl.BlockSpec((d, tn), lambda j: (0, j)),
                  pl.BlockSpec((1, tn), lambda j: (0, j))],
        out_specs=pl.BlockSpec((8, tn), lambda j: (0, j)),
        out_shape=jax.ShapeDtypeStruct((8, n), F32),
        compiler_params=_cparams(("arbitrary",)),
        name="adaln",
    )(cond, w, b)


INPROJ_COLS = 1024


def _inproj_kernel(x_ref, g_ref, sh_ref, sc_ref, w_ref, wg_ref, proj_ref, gate_ref):
    x = x_ref[...]
    y = x * lax.rsqrt(jnp.mean(x * x, axis=-1, keepdims=True) + EPS) * g_ref[...]
    xn = (y * (1.0 + sc_ref[...]) + sh_ref[...]).astype(BF16)
    gate_ref[...] = jnp.dot(xn, wg_ref[...], preferred_element_type=F32)
    for j in range(w_ref.shape[1] // INPROJ_COLS):
        cols = slice(j * INPROJ_COLS, (j + 1) * INPROJ_COLS)
        proj_ref[:, cols] = jnp.dot(xn, w_ref[:, cols], preferred_element_type=F32).astype(BF16)


def _inproj(x, g, sh, sc, w, wg, tm):
    bsz, s, d = x.shape
    p = w.shape[1]
    nt = s // tm
    x2 = x.reshape(bsz * s, d)
    resident = lambda shape: pl.BlockSpec(shape, lambda b, i: (0, 0), pipeline_mode=pl.Buffered(1))
    return pl.pallas_call(
        _inproj_kernel,
        grid=(bsz, nt),
        in_specs=[pl.BlockSpec((tm, d), lambda b, i: (b * nt + i, 0)),
                  pl.BlockSpec((1, d), lambda b, i: (0, 0)),
                  pl.BlockSpec((None, 1, d), lambda b, i: (b, 0, 0)),
                  pl.BlockSpec((None, 1, d), lambda b, i: (b, 0, 0)),
                  resident((d, p)),
                  resident((d, LANES))],
        out_specs=[pl.BlockSpec((tm, p), lambda b, i: (b * nt + i, 0)),
                   pl.BlockSpec((tm, LANES), lambda b, i: (b * nt + i, 0))],
        out_shape=[jax.ShapeDtypeStruct((bsz * s, p), BF16),
                   jax.ShapeDtypeStruct((bsz * s, LANES), F32)],
        compiler_params=_cparams(("arbitrary", "arbitrary")),
        name="inproj",
    )(x2, g, sh, sc, w, wg)


def _log_sigmoid(x):
    return jnp.minimum(x, 0.0) - jnp.log1p(jnp.exp(-jnp.abs(x)))


def _gates_kernel(g_ref, b_ref, gc_ref, gr_ref):
    tm = g_ref.shape[0]
    row = lax.broadcasted_iota(jnp.int32, (tm, LANES), 0)
    lane = lax.broadcasted_iota(jnp.int32, (tm, LANES), 1)
    gp = GATE_SOFT_CAP * jnp.tanh((g_ref[...] + b_ref[...]) / GATE_SOFT_CAP)
    is_f = ((lane >> 2) & 1) == 1
    fwd_lane = lane < 2 * N_HEADS
    lf = jnp.where(is_f, _log_sigmoid(gp), 0.0)
    r2 = lax.broadcasted_iota(jnp.int32, (CHUNK, CHUNK), 0)
    c2 = lax.broadcasted_iota(jnp.int32, (CHUNK, CHUNK), 1)
    lower = (r2 >= c2).astype(F32)
    upper = (r2 <= c2).astype(F32)
    lane_c = lax.broadcasted_iota(jnp.int32, (CHUNK, LANES), 1)
    cums = []
    for c in range(tm // CHUNK):
        lf_c = lf[c * CHUNK:(c + 1) * CHUNK]
        cf = jnp.dot(lower, lf_c, precision=lax.Precision.HIGHEST, preferred_element_type=F32)
        cb = jnp.dot(upper, lf_c, precision=lax.Precision.HIGHEST, preferred_element_type=F32)
        cums.append(jnp.where(lane_c < 2 * N_HEADS, cf, cb))
    cdir = jnp.concatenate(cums, axis=0)
    a = jnp.where(is_f, cdir, gp - pltpu.roll(cdir, LANES - N_HEADS, 1))

    pos = row % CHUNK
    x = a
    k = 1
    while k < CHUNK:
        from_before = jnp.where(pos >= k, pltpu.roll(x, k, 0), -jnp.inf)
        from_after = jnp.where(pos < CHUNK - k, pltpu.roll(x, tm - k, 0), -jnp.inf)
        x = jnp.maximum(x, jnp.where(fwd_lane, from_before, from_after))
        k *= 2
    gc_ref[...] = jnp.where(is_f, a, x)

    lane_1 =lax.broadcasted_iota(jnp.int32, (1, LANES), 1)
    for c in range(tm // CHUNK):
        lo = c * CHUNK
        xc, ac = x[lo:lo + CHUNK], a[lo:lo + CHUNK]
        end_max = jnp.where(lane_1 < 2 * N_HEADS, xc[CHUNK - 1:CHUNK], xc[0:1])
        e = jnp.exp(ac - end_max)
        rows = jnp.where(((lane_c >> 2) & 1) == 1, pltpu.roll(e, N_HEADS, 1), ac)
        gr_ref[:, lo:lo + CHUNK] = rows.T[:N_GATE_COLS, :]


def _gates(gpre, b_if, tm):
    t = gpre.shape[0]
    return pl.pallas_call(
        _gates_kernel,
        grid=(t // tm,),
        in_specs=[pl.BlockSpec((tm, LANES), lambda i: (i, 0)),
                  pl.BlockSpec((1, LANES), lambda i: (0, 0))],
        out_specs=[pl.BlockSpec((tm, LANES), lambda i: (i, 0)),
                   pl.BlockSpec((N_GATE_COLS, tm), lambda i: (0, i))],
        out_shape=[jax.ShapeDtypeStruct((t, LANES), F32),
                   jax.ShapeDtypeStruct((N_GATE_COLS, t), F32)],
        compiler_params=_cparams(("arbitrary",)),
        name="gates",
    )(gpre, b_if)


DVX = DV + LANES
MLSTM_CHUNKS_PER_STEP = 2


def _mlstm_chunk(q, k, v_ext, rmax_col, b_col, r_row, e_row, b_last, rmax_last, mask, cx, m_st):
    scale = DK ** -0.5
    mb = jnp.maximum(m_st, jnp.broadcast_to(rmax_col, (CHUNK, CHUNK)))
    w_intra = jnp.exp(jnp.where(mask, r_row - mb, -jnp.inf))
    w_state = jnp.exp(m_st - mb)
    qk = lax.dot_general(q, k, (((1,), (1,)), ((), ())), preferred_element_type=F32)
    s = qk * (w_intra * scale)
    lhs = jnp.concatenate([s.astype(BF16), (q.astype(F32) * (w_state * scale)).astype(BF16)], axis=1)
    rhs = jnp.concatenate([v_ext, cx.astype(BF16)], axis=0)
    nx = jnp.dot(lhs, rhs, preferred_element_type=F32)
    denom = jnp.maximum(jnp.abs(nx[:, DV:]), jnp.exp(-(jnp.broadcast_to(b_col, (CHUNK, CHUNK)) + mb)))
    h = nx[:, :DV] / jnp.concatenate([denom, denom], axis=1)
    ke_t = (k.T.astype(F32) * e_row).astype(BF16)
    c_loc = jnp.dot(ke_t, v_ext, preferred_element_type=F32)
    m_loc = b_last + rmax_last
    m_new = jnp.maximum(b_last + m_st, m_loc)
    return h, jnp.exp(b_last + m_st - m_new) * cx + jnp.exp(m_loc - m_new) * c_loc, m_new


def _mlstm_kernel(qf_ref, kf_ref, vf_ref, gcf_ref, grf_ref, qb_ref, kb_ref, vb_ref, gcb_ref, grb_ref,
                  c0_ref, m0_ref, hf_ref, hb_ref, cout_ref, mout_ref, m_scr, *c_scrs):
    c = pl.program_id(1)

    @pl.when(c == 0)
    def _():
        for idx, c_scr in enumerate(c_scrs):
            c_scr[...] = c0_ref[idx]
        m_scr[...] = m0_ref[...]

    row = lax.broadcasted_iota(jnp.int32, (CHUNK, CHUNK), 0)
    col = lax.broadcasted_iota(jnp.int32, (CHUNK, CHUNK), 1)
    ones = jnp.ones((CHUNK, LANES), BF16)
    m_all = m_scr[...]
    dirs = ((qf_ref, kf_ref, vf_ref, gcf_ref, grf_ref, hf_ref, 0, CHUNK - 1, col <= row),
            (qb_ref, kb_ref, vb_ref, gcb_ref, grb_ref, hb_ref, 2 * N_HEADS, 0, col >= row))
    n_sub = qf_ref.shape[0] // CHUNK
    m_news = []
    for di, (q_ref, k_ref, v_ref, gc_ref, gr_ref, h_ref, off, last, mask) in enumerate(dirs):
        order = range(n_sub) if di == 0 else range(n_sub - 1, -1, -1)
        for hd in range(N_HEADS):
            idx = di * N_HEADS + hd
            lr, lb = off + hd, off + N_HEADS + hd
            cx, m_st = c_scrs[idx][...], m_all[idx][:, 0:1]
            for sub in order:
                r0 = sub * CHUNK
                rows = slice(r0, r0 + CHUNK)
                v_ext = jnp.concatenate([v_ref[rows, hd * DV:(hd + 1) * DV], ones], axis=1)
                h, cx, m_st = _mlstm_chunk(
                    q_ref[rows, hd * DK:(hd + 1) * DK], k_ref[rows, hd * DK:(hd + 1) * DK], v_ext,
                    gc_ref[rows, lr:lr + 1], gc_ref[rows, lb:lb + 1],
                    gr_ref[lr:lr + 1, rows], gr_ref[lb:lb + 1, rows],
                    gc_ref[r0 + last:r0 + last + 1, lb:lb + 1], gc_ref[r0 + last:r0 + last + 1, lr:lr + 1],
                    mask, cx, m_st)
                h_ref[rows, hd * DV:(hd + 1) * DV] = h
            c_scrs[idx][...] = cx
            m_news.append(jnp.broadcast_to(m_st, (1, LANES)))
    for idx, m_new in enumerate(m_news):
        m_scr[idx] = m_new

    @pl.when(c == pl.num_programs(1) - 1)
    def _():
        for idx, c_scr in enumerate(c_scrs):
            cout_ref[idx] = c_scr[...]
        mout_ref[...] = m_scr[...]


def _mlstm(proj, gcol, grow, bsz, s, c0, m0):
    rows = MLSTM_CHUNKS_PER_STEP * CHUNK
    nc = s // rows
    t = bsz * s
    fwd = lambda b, c: b * nc + c
    bwd = lambda b, c: b * nc + (nc - 1 - c)

    def specs(ci):
        return [pl.BlockSpec((rows, QK_COLS), lambda b, c: (ci(b, c), 0)),
                pl.BlockSpec((rows, QK_COLS), lambda b, c: (ci(b, c), 1)),
                pl.BlockSpec((rows, MLSTM_WIDTH), lambda b, c: (ci(b, c), 1)),
                pl.BlockSpec((rows, LANES), lambda b, c: (ci(b, c), 0)),
                pl.BlockSpec((N_GATE_COLS, rows), lambda b, c: (0, ci(b, c)))]

    st_specs = [pl.BlockSpec((None, 2 * N_HEADS, DK, DVX), lambda b, c: (b, 0, 0, 0)),
                pl.BlockSpec((None, 2 * N_HEADS, 1, LANES), lambda b, c: (b, 0, 0, 0))]
    return pl.pallas_call(
        _mlstm_kernel,
        grid=(bsz, nc),
        in_specs=specs(fwd) + specs(bwd) + st_specs,
        out_specs=[pl.BlockSpec((rows, MLSTM_WIDTH), lambda b, c: (fwd(b, c), 0)),
                   pl.BlockSpec((rows, MLSTM_WIDTH), lambda b, c: (bwd(b, c), 0))] + st_specs,
        out_shape=[jax.ShapeDtypeStruct((t, MLSTM_WIDTH), F32),
                   jax.ShapeDtypeStruct((t, MLSTM_WIDTH), F32),
                   jax.ShapeDtypeStruct(c0.shape, F32),
                   jax.ShapeDtypeStruct(m0.shape, F32)],
        scratch_shapes=[pltpu.VMEM((2 * N_HEADS, 1, LANES), F32)]
        + [pltpu.VMEM((DK, DVX), F32) for _ in range(2 * N_HEADS)],
        compiler_params=_cparams(("arbitrary", "arbitrary")),
        name="mlstm",
    )(proj, proj, proj, gcol, grow, proj, proj, proj, gcol, grow, c0, m0)


MIX_ROWS = 256


def _mixout_kernel(o_ref, cb_ref, cc_ref, cx_ref, ccp_ref, cxp_ref, ccn_ref, cxn_ref, hf_ref, hb_ref, x_ref,
                   cw_ref, ng_ref, wout_ref, gt_ref, gffn_ref, shf_ref, scf_ref, wr_ref, br_ref,
                   x1_ref, xn2_ref, idx_ref, tw_ref):
    i = pl.program_id(1)
    tm = x_ref.shape[0]
    cw = cw_ref[...]

    has_prev = jnp.where(i > 0, 1.0, 0.0)
    has_next = jnp.where(i < pl.num_programs(1) - 1, 1.0, 0.0)
    up = ccp_ref[...].astype(F32) * cxp_ref[...].astype(F32) * has_prev
    un = ccn_ref[...].astype(F32) * cxn_ref[...].astype(F32) * has_next
    uv = cc_ref[:, CONV_HALF:].astype(F32) * cx_ref[:, CONV_HALF:].astype(F32)
    ext = jnp.concatenate([up, uv, un], axis=0)

    pos = lax.broadcasted_iota(jnp.int32, (MIX_ROWS, CONV_HALF), 0) & (GRID_W - 1)
    lane_f = lax.broadcasted_iota(jnp.int32, (MIX_ROWS, LANES), 1).astype(F32)
    lane4 = lax.broadcasted_iota(jnp.int32, (MIX_ROWS, TOP_K), 1)

    for r0 in range(0, tm, MIX_ROWS):
        rows = slice(r0, r0 + MIX_ROWS)

        uh = cc_ref[rows, :CONV_HALF].astype(F32) * cx_ref[rows, :CONV_HALF].astype(F32)
        left = jnp.where(pos == 0, 0.0, pltpu.roll(uh, 1, 0))
        right = jnp.where(pos == GRID_W - 1, 0.0, pltpu.roll(uh, MIX_ROWS - 1, 0))
        yh = cw[0:1, :CONV_HALF] * left + cw[1:2, :CONV_HALF] * uh + cw[2:3, :CONV_HALF] * right
        yv = (cw[0:1, CONV_HALF:] * ext[r0:r0 + MIX_ROWS]
              + cw[1:2, CONV_HALF:] * ext[r0 + GRID_W:r0 + GRID_W + MIX_ROWS]
              + cw[2:3, CONV_HALF:] * ext[r0 + 2 * GRID_W:r0 + 2 * GRID_W + MIX_ROWS])
        yc = cb_ref[rows, :].astype(F32) * jnp.concatenate([yh, yv], axis=1)

        hs = hf_ref[rows, :] + hb_ref[rows, :]
        parts = []
        for hd in range(N_HEADS):
            seg = hs[:, hd * DV:(hd + 1) * DV]
            parts.append(seg * lax.rsqrt(jnp.mean(seg * seg, axis=-1, keepdims=True) + EPS))
        hm = jnp.concatenate(parts, axis=1) * ng_ref[...] * jax.nn.sigmoid(o_ref[rows, :].astype(F32))

        z = jnp.concatenate([hm.astype(BF16), yc.astype(BF16)], axis=1)
        x1 = x_ref[rows, :] + gt_ref[...] * jnp.dot(z, wout_ref[...], preferred_element_type=F32)
        x1_ref[rows, :] = x1

        y = x1 * lax.rsqrt(jnp.mean(x1 * x1, axis=-1, keepdims=True) + EPS) * gffn_ref[...]
        xn2 = y * (1.0 + scf_ref[...]) + shf_ref[...]
        xn2_ref[rows, :] = _pack_pairs(xn2)

        logits = jnp.dot(xn2.astype(BF16), wr_ref[...], preferred_element_type=F32) + br_ref[...]
        vals, idxs = [], []
        for _ in range(TOP_K):
            mx = jnp.max(logits, axis=-1, keepdims=True)
            ik = jnp.min(jnp.where(logits == mx, lane_f, float(LANES)), axis=-1, keepdims=True)
            vals.append(mx)
            idxs.append(ik)
            logits = jnp.where(lane_f == ik, -jnp.inf, logits)
        es = [jnp.exp(v - vals[0]) for v in vals]
        tot = es[0] + es[1] + es[2] + es[3]
        idx_out = jnp.zeros((MIX_ROWS, TOP_K), F32)
        tw_out = jnp.zeros((MIX_ROWS, TOP_K), F32)
        for kk in range(TOP_K):
            idx_out = jnp.where(lane4 == kk, idxs[kk], idx_out)
            tw_out = jnp.where(lane4 == kk, es[kk] / tot, tw_out)
        idx_ref[rows, :] = idx_out.astype(jnp.int32)
        tw_ref[rows, :] = tw_out


def _mixout(proj, hf, hb, x2, conv_w, norm_g, w_out, gt, g_ffn, sh_f, sc_f, w_r, b_r, bsz, s, tm):
    t, d = x2.shape
    nt = s // tm
    rb = tm // GRID_W
    last_rb = t // GRID_W - 1
    row = lambda b, i: b * nt + i
    w = MLSTM_WIDTH
    vec = lambda n: pl.BlockSpec((1, n), lambda b, i: (0, 0))
    per_b = pl.BlockSpec((None, 1, d), lambda b, i: (b, 0, 0))
    halo_prev = lambda cblk: pl.BlockSpec(
        (GRID_W, CONV_HALF), lambda b, i: (jnp.maximum(row(b, i) * rb - 1, 0), cblk))
    halo_next = lambda cblk: pl.BlockSpec(
        (GRID_W, CONV_HALF), lambda b, i: (jnp.minimum((row(b, i) + 1) * rb, last_rb), cblk))
    return pl.pallas_call(
        _mixout_kernel,
        grid=(bsz, nt),
        in_specs=[pl.BlockSpec((tm, w), lambda b, i: (row(b, i), 2)),
                  pl.BlockSpec((tm, w), lambda b, i: (row(b, i), 3)),
                  pl.BlockSpec((tm, w), lambda b, i: (row(b, i), 4)),
                  pl.BlockSpec((tm, w), lambda b, i: (row(b, i), 5)),
                  halo_prev(9), halo_prev(11), halo_next(9), halo_next(11),
                  pl.BlockSpec((tm, w), lambda b, i: (row(b, i), 0)),
                  pl.BlockSpec((tm, w), lambda b, i: (row(b, i), 0)),
                  pl.BlockSpec((tm, d), lambda b, i: (row(b, i), 0)),
                  pl.BlockSpec((3, CONV_WIDTH), lambda b, i: (0, 0)),
                  vec(w),
                  pl.BlockSpec((d, d), lambda b, i: (0, 0)),
                  per_b, vec(d), per_b, per_b,
                  pl.BlockSpec((d, LANES), lambda b, i: (0, 0)),
                  vec(LANES)],
        out_specs=[pl.BlockSpec((tm, d), lambda b, i: (row(b, i), 0)),
                   pl.BlockSpec((tm, d // 2), lambda b, i: (row(b, i), 0)),
                   pl.BlockSpec((tm, TOP_K), lambda b, i: (row(b, i), 0)),
                   pl.BlockSpec((tm, TOP_K), lambda b, i: (row(b, i), 0))],
        out_shape=[jax.ShapeDtypeStruct((t, d), F32),
                   jax.ShapeDtypeStruct((t, d // 2), jnp.uint32),
                   jax.ShapeDtypeStruct((t, TOP_K), jnp.int32),
                   jax.ShapeDtypeStruct((t, TOP_K), F32)],
        compiler_params=_cparams(("arbitrary", "arbitrary")),
        name="mixout",
    )(proj, proj, proj, proj, proj, proj, proj, proj, hf, hb, x2,
      conv_w, norm_g, w_out, gt, g_ffn, sh_f, sc_f, w_r, b_r)


def _rank_kernel(idx_ref, rank_ref, cnt_ref, run_scr):
    @pl.when(pl.program_id(0) == 0)
    def _():
        run_scr[...] = jnp.zeros_like(run_scr)

    tm = idx_ref.shape[0]
    idx = idx_ref[...]
    lane = lax.broadcasted_iota(jnp.int32, (tm, LANES), 1)
    hits = [lane == idx[:, kk:kk + 1] for kk in range(TOP_K)]
    onehot = jnp.zeros((tm, LANES), F32)
    for hit in hits:
        onehot = onehot + hit.astype(F32)
    r = lax.broadcasted_iota(jnp.int32, (tm, tm), 0)
    c = lax.broadcasted_iota(jnp.int32, (tm, tm), 1)
    before = jnp.dot((c < r).astype(BF16), onehot.astype(BF16), preferred_element_type=F32) + run_scr[...]
    lane4 = lax.broadcasted_iota(jnp.int32, (tm, TOP_K), 1)
    rank = jnp.zeros((tm, TOP_K), F32)
    for kk, hit in enumerate(hits):
        rk = jnp.sum(jnp.where(hit, before, 0.0), axis=-1, keepdims=True)
        rank = jnp.where(lane4 == kk, rk, rank)
    rank_ref[...] = rank.astype(jnp.int32)
    run_scr[...] = run_scr[...] + jnp.sum(onehot, axis=0, keepdims=True)
    cnt_ref[...] = run_scr[...]


def _rank(idx, tm):
    t = idx.shape[0]
    return pl.pallas_call(
        _rank_kernel,
        grid=(t // tm,),
        in_specs=[pl.BlockSpec((tm, TOP_K), lambda i: (i, 0))],
        out_specs=[pl.BlockSpec((tm, TOP_K), lambda i: (i, 0)),
                   pl.BlockSpec((1, LANES), lambda i: (0, 0))],
        out_shape=[jax.ShapeDtypeStruct((t, TOP_K), jnp.int32),
                   jax.ShapeDtypeStruct((1, LANES), F32)],
        scratch_shapes=[pltpu.VMEM((1, LANES), F32)],
        compiler_params=_cparams(("arbitrary",)),
        name="rank",
    )(idx)


def _dispatch_kernel(pad_ref, dest_hbm, xn_ref, xs_hbm, dsm, zeros_scr, sem_idx, sem_rows, sem_pad):
    i = pl.program_id(0)
    tm = xn_ref.shape[0]
    n_idx = tm * TOP_K
    idx_copy = pltpu.make_async_copy(dest_hbm.at[pl.ds(i * n_idx, n_idx)], dsm, sem_idx)
    idx_copy.start()

    def pad_copy(off, size):
        return pltpu.make_async_copy(zeros_scr.at[pl.ds(0, size), :], xs_hbm.at[pl.ds(off, size), :], sem_pad)

    def for_each_pad_piece(fn):
        def per_expert(e, carry):
            off = pad_ref[2 * e]
            n = pad_ref[2 * e + 1]
            head = n & (SUBLANES - 1)
            for r in range(SUBLANES - 1):
                @pl.when(r < head)
                def _(r=r):
                    fn(pad_copy(off + r, 1))

            off = off + head
            size = ROW_BLOCK // 2
            while size >= SUBLANES:
                take = (n & size) != 0

                @pl.when(take)
                def _(off=off, size=size):
                    fn(pad_copy(pl.multiple_of(off, SUBLANES), size))

                off = off + jnp.where(take, size, 0)
                size //= 2
            return carry
        lax.fori_loop(0, N_EXPERTS, per_expert, 0)

    @pl.when(i == 0)
    def _():
        zeros_scr[...] = jnp.zeros_like(zeros_scr)
        for_each_pad_piece(lambda cp: cp.start())
        for_each_pad_piece(lambda cp: cp.wait())

    idx_copy.wait()

    def row_copy(t, kk):
        return pltpu.make_async_copy(xn_ref.at[pl.ds(t, 1), :],
                                     xs_hbm.at[pl.ds(dsm[t * TOP_K + kk], 1), :], sem_rows)

    def issue(t, carry):
        for kk in range(TOP_K):
            row_copy(t, kk).start()
        return carry

    lax.fori_loop(0, tm, issue, 0, unroll=ISSUE_UNROLL)
    pltpu.make_async_copy(xs_hbm.at[pl.ds(0, n_idx), :], xs_hbm.at[pl.ds(0, n_idx), :], sem_rows).wait()


def _dispatch(pad_info, dest_flat, xn2, n_rows, tm):
    t, d = xn2.shape
    return pl.pallas_call(
        _dispatch_kernel,
        grid_spec=pltpu.PrefetchScalarGridSpec(
            num_scalar_prefetch=1,
            grid=(t // tm,),
            in_specs=[pl.BlockSpec(memory_space=pl.ANY),
                      pl.BlockSpec((tm, d), lambda i, pad: (i, 0))],
            out_specs=pl.BlockSpec(memory_space=pl.ANY),
            scratch_shapes=[pltpu.SMEM((tm * TOP_K,), jnp.int32),
                            pltpu.VMEM((ROW_BLOCK // 2, d), xn2.dtype),
                            pltpu.SemaphoreType.DMA(()),
                            pltpu.SemaphoreType.DMA(()),
                            pltpu.SemaphoreType.DMA(())]),
        out_shape=jax.ShapeDtypeStruct((n_rows, d), xn2.dtype),
        compiler_params=_cparams(("arbitrary",)),
        name="dispatch",
    )(pad_info, dest_flat, xn2)


def _new_expert(be_ref, j):
    return jnp.logical_or(j == 0, be_ref[j] != be_ref[jnp.maximum(j - 1, 0)])


def _expert_gu_kernel(be_ref, nu_ref, xs_ref, wg_ref, wu_ref, bg_ref, bu_ref, act_ref, wg_scr, wu_scr):
    j = pl.program_id(1)

    @pl.when(j < nu_ref[0])
    def _():
        @pl.when(_new_expert(be_ref, j))
        def _():
            wg_scr[...] = wg_ref[...].astype(BF16)
            wu_scr[...] = wu_ref[...].astype(BF16)

        x = _unpack_pairs(xs_ref[...], xs_ref.shape[1]).astype(BF16)
        g = jnp.dot(x, wg_scr[...], preferred_element_type=F32) + bg_ref[...]
        u = jnp.dot(x, wu_scr[...], preferred_element_type=F32) + bu_ref[...]
        gate = jnp.minimum(g, SWIGLU_LIMIT)
        up = jnp.clip(u, -SWIGLU_LIMIT, SWIGLU_LIMIT)
        act_ref[...] = ((up + 1.0) * gate * jax.nn.sigmoid(SWIGLU_ALPHA * gate)).astype(BF16)


def _expert_gu(blk_expert, n_used, xs, w_gu, b_gu, tn):
    n_rows, dp = xs.shape
    d = w_gu.shape[1]
    dff = w_gu.shape[2] // 2
    nt = dff // tn
    nb = n_rows // ROW_BLOCK
    blk = lambda j, nu: jnp.minimum(j, nu[0] - 1)
    exp = lambda j, be, nu: be[blk(j, nu)]
    return pl.pallas_call(
        _expert_gu_kernel,
        grid_spec=pltpu.PrefetchScalarGridSpec(
            num_scalar_prefetch=2,
            grid=(nt, nb),
            in_specs=[pl.BlockSpec((ROW_BLOCK, dp), lambda n, j, be, nu: (blk(j, nu), 0)),
                      pl.BlockSpec((None, d, tn), lambda n, j, be, nu: (exp(j, be, nu), 0, n)),
                      pl.BlockSpec((None, d, tn), lambda n, j, be, nu: (exp(j, be, nu), 0, nt + n)),
                      pl.BlockSpec((None, 1, tn), lambda n, j, be, nu: (exp(j, be, nu), 0, n)),
                      pl.BlockSpec((None, 1, tn), lambda n, j, be, nu: (exp(j, be, nu), 0, nt + n))],
            out_specs=pl.BlockSpec((ROW_BLOCK, tn), lambda n, j, be, nu: (blk(j, nu), n)),
            scratch_shapes=[pltpu.VMEM((d, tn), BF16), pltpu.VMEM((d, tn), BF16)]),
        out_shape=jax.ShapeDtypeStruct((n_rows, dff), BF16),
        compiler_params=_cparams(("arbitrary", "arbitrary")),
        name="expert_gu",
    )(blk_expert, n_used, xs, w_gu, w_gu, b_gu, b_gu)


def _expert_down_kernel(be_ref, nu_ref, act_ref, w_ref, b_ref, y_ref, w_scr):
    j = pl.program_id(1)

    @pl.when(j < nu_ref[0])
    def _():
        @pl.when(_new_expert(be_ref, j))
        def _():
            w_scr[...] = w_ref[...].astype(BF16)

        y_ref[...] = _pack_pairs(jnp.dot(act_ref[...], w_scr[...], preferred_element_type=F32) + b_ref[...])


def _expert_down(blk_expert, n_used, act, w_down, b_down, tn):
    n_rows, dff = act.shape
    d = w_down.shape[2]
    nb = n_rows // ROW_BLOCK
    blk = lambda j, nu: jnp.minimum(j, nu[0] - 1)
    exp = lambda j, be, nu: be[blk(j, nu)]
    return pl.pallas_call(
        _expert_down_kernel,
        grid_spec=pltpu.PrefetchScalarGridSpec(
            num_scalar_prefetch=2,
            grid=(d // tn, nb),
            in_specs=[pl.BlockSpec((ROW_BLOCK, dff), lambda n, j, be, nu: (blk(j, nu), 0)),
                      pl.BlockSpec((None, dff, tn), lambda n, j, be, nu: (exp(j, be, nu), 0, n)),
                      pl.BlockSpec((None, 1, tn), lambda n, j, be, nu: (exp(j, be, nu), 0, n))],
            out_specs=pl.BlockSpec((ROW_BLOCK, tn // 2), lambda n, j, be, nu: (blk(j, nu), n)),
            scratch_shapes=[pltpu.VMEM((dff, tn), BF16)]),
        out_shape=jax.ShapeDtypeStruct((n_rows, d // 2), jnp.uint32),
        compiler_params=_cparams(("arbitrary", "arbitrary")),
        name="expert_down",
    )(blk_expert, n_used, act, w_down, b_down)


def _combine_kernel(dest_hbm, y_hbm, x1_ref, tw_ref, gt_ref, gfin_ref, out_ref, dsm, buf, sem_idx, sem_rows):
    i = pl.program_id(1) + pl.program_id(0) * pl.num_programs(1)
    tm = x1_ref.shape[0]
    n_idx = tm * TOP_K
    idx_copy = pltpu.make_async_copy(dest_hbm.at[pl.ds(i * n_idx, n_idx)], dsm, sem_idx)
    idx_copy.start()
    idx_copy.wait()

    def row_copy(t, kk):
        return pltpu.make_async_copy(y_hbm.at[pl.ds(dsm[t * TOP_K + kk], 1), :],
                                     buf.at[kk, pl.ds(t, 1), :], sem_rows)

    def issue(t, carry):
        for kk in range(TOP_K):
            row_copy(t, kk).start()
        return carry

    lax.fori_loop(0, tm, issue, 0, unroll=ISSUE_UNROLL)
    for kk in range(TOP_K):
        pltpu.make_async_copy(y_hbm.at[pl.ds(0, tm), :], buf.at[kk], sem_rows).wait()

    tw = tw_ref[...]
    acc = _unpack_pairs(buf[0], DOWN_COLS // 2) * tw[:, 0:1]
    for kk in range(1, TOP_K):
        acc = acc + _unpack_pairs(buf[kk], DOWN_COLS // 2) * tw[:, kk:kk + 1]
    x2 = x1_ref[...] + gt_ref[...] * acc
    out_ref[...] = x2 * lax.rsqrt(jnp.mean(x2 * x2, axis=-1, keepdims=True) + EPS) * gfin_ref[...]


def _combine(dest_flat, y, x1, tw, gt, g_final, bsz, s, tm):
    t, d = x1.shape
    nt = s // tm
    row = lambda b, i: (b * nt + i, 0)
    return pl.pallas_call(
        _combine_kernel,
        grid=(bsz, nt),
        in_specs=[pl.BlockSpec(memory_space=pl.ANY),
                  pl.BlockSpec(memory_space=pl.ANY),
                  pl.BlockSpec((tm, d), row),
                  pl.BlockSpec((tm, TOP_K), row),
                  pl.BlockSpec((None, 1, d), lambda b, i: (b, 0, 0)),
                  pl.BlockSpec((1, d), lambda b, i: (0, 0))],
        out_specs=pl.BlockSpec((tm, d), row),
        out_shape=jax.ShapeDtypeStruct((t, d), F32),
        scratch_shapes=[pltpu.SMEM((tm * TOP_K,), jnp.int32),
                        pltpu.VMEM((TOP_K, tm, y.shape[1]), y.dtype),
                        pltpu.SemaphoreType.DMA(()),
                        pltpu.SemaphoreType.DMA(())],
        compiler_params=_cparams(("arbitrary", "arbitrary")),
        name="combine",
    )(dest_flat, y, x1, tw, gt, g_final)


def _pad_lanes(a, value=0.0):
    return jnp.pad(a, ((0, 0), (0, LANES - a.shape[1])), constant_values=value)


def _routing_tables(idx, rank, counts_f, n_blocks):
    counts = counts_f[0, :N_EXPERTS].astype(jnp.int32)
    padded = (counts + ROW_BLOCK - 1) // ROW_BLOCK * ROW_BLOCK
    pend = jnp.cumsum(padded)
    pstart = pend - padded
    dest = (pstart[idx] + rank).reshape(-1)
    blk_start = jnp.arange(n_blocks, dtype=jnp.int32) * ROW_BLOCK
    blk_expert = jnp.minimum(jnp.sum((pend[None, :] <= blk_start[:, None]).astype(jnp.int32), axis=1),
                             N_EXPERTS - 1)
    n_used = (pend[-1:] // ROW_BLOCK).astype(jnp.int32)
    pad_info = jnp.stack([pstart + counts, padded - counts], axis=1).reshape(-1).astype(jnp.int32)
    return dest, blk_expert, n_used, pad_info


def _layer(x, c, ctx, c_ctx, w_ada, b_ada, g_mix, w_in, b_if, conv_w, norm_g, w_out,
           g_ffn, w_router, b_router, w_gu, b_gu, w_down, b_down, g_final):
    bsz, s, d = x.shape
    s_ctx = ctx.shape[1]

    cond = jnp.zeros((8, d), F32).at[:bsz].set(c).at[bsz].set(c_ctx)
    mod = _adaln(cond, w_ada, b_ada[None, :])
    sh_m, sc_m, gt_m, sh_f, sc_f, gt_f = [m[:, None, :] for m in jnp.split(mod, N_MOD, axis=-1)]
    lat = lambda m: m[:bsz]
    ctxm = lambda m: jnp.broadcast_to(m[bsz:bsz + 1], (bsz, 1, d))

    g0 = 2 * QK_COLS + 2 * MLSTM_WIDTH
    w_main = jnp.concatenate([w_in[:, :g0], w_in[:, g0 + N_GATE_COLS:]], axis=1).astype(BF16)
    w_gate = _pad_lanes(w_in[:, g0:g0 + N_GATE_COLS]).astype(BF16)
    b_gate = _pad_lanes(b_if[None, :])
    g_mix2 = g_mix[None, :]

    proj_c, gpre_c = _inproj(ctx, g_mix2, ctxm(sh_m), ctxm(sc_m), w_main, w_gate, min(s_ctx, 512))
    gcol_c, grow_c = _gates(gpre_c, b_gate, 512)
    zeros_state = (jnp.zeros((bsz, 2 * N_HEADS, DK, DVX), F32),
                   jnp.zeros((bsz, 2 * N_HEADS, 1, LANES), F32))
    _, _, c0, m0 = _mlstm(proj_c, gcol_c, grow_c, bsz, s_ctx, *zeros_state)

    proj, gpre = _inproj(x, g_mix2, lat(sh_m), lat(sc_m), w_main, w_gate, 512)
    gcol, grow = _gates(gpre, b_gate, 512)
    hf, hb, _, _ = _mlstm(proj, gcol, grow, bsz, s, c0, m0)
    x1, xn2, idx, tw = _mixout(
        proj, hf, hb, x.reshape(bsz * s, d), conv_w, norm_g[None, :], w_out.astype(BF16), lat(gt_m),
        g_ffn[None, :], lat(sh_f), lat(sc_f), _pad_lanes(w_router).astype(BF16),
        _pad_lanes(b_router[None, :], NEG_BIG), bsz, s, 512)

    t = bsz * s
    n_blocks = -(-(t * TOP_K) // ROW_BLOCK) + N_EXPERTS
    rank, counts = _rank(idx, 512)
    dest, blk_expert, n_used, pad_info = _routing_tables(idx, rank, counts, n_blocks)
    xs = _dispatch(pad_info, dest, xn2, n_blocks * ROW_BLOCK, 512)
    act = _expert_gu(blk_expert, n_used, xs, w_gu, b_gu[:, None, :], 512)
    y = _expert_down(blk_expert, n_used, act, w_down, b_down[:, None, :], DOWN_COLS)
    out = _combine(dest, y, x1, tw, lat(gt_f), g_final[None, :], bsz, s, 256)
    return out.reshape(bsz, s, d)


def kernel(x, c, ctx, c_ctx, w_ada, b_ada, g_mix, w_in, b_if, conv_w, mlstm_norm_g, w_out,
           g_ffn, w_router, b_router, w_gu, b_gu, w_down, b_down, g_final):
    return _layer(x, c, ctx, c_ctx, w_ada[0], b_ada[0], g_mix[0], w_in[0], b_if[0], conv_w[0],
                  mlstm_norm_g[0], w_out[0], g_ffn[0], w_router[0], b_router[0], w_gu[0], b_gu[0],
                  w_down[0], b_down[0], g_final)
```

```python
import functools

import jax
import jax.numpy as jnp
from jax import lax
from jax.experimental import pallas as pl
from jax.experimental.pallas import tpu as pltpu

F32 = jnp.float32
BF16 = jnp.bfloat16

N_HEADS = 4
DK = 128
DV = 256
QK_COLS = N_HEADS * DK
MLSTM_WIDTH = N_HEADS * DV
CONV_WIDTH = 1024
CONV_HALF = CONV_WIDTH // 2
N_GATE_COLS = 4 * N_HEADS
GRID_W = 64
CHUNK = 128
GATE_SOFT_CAP = 15.0
N_EXPERTS = 32
TOP_K = 4
SWIGLU_LIMIT = 7.0
SWIGLU_ALPHA = 1.702
N_MOD = 6
EPS = 1e-6
LANES = 128
SUBLANES = 8
ROW_BLOCK = 1024
DOWN_COLS = 1024
ISSUE_UNROLL = 8
NEG_BIG = -1e30
VMEM_LIMIT = 56 * 1024 * 1024


def _cparams(sem):
    return pltpu.CompilerParams(dimension_semantics=sem, vmem_limit_bytes=VMEM_LIMIT)


def _pack_pairs(x):
    bits = lax.bitcast_convert_type(x.astype(BF16).astype(F32), jnp.uint32)
    g = x.shape[1] // 2
    return bits[:, :g] | (bits[:, g:] >> 16)


def _unpack_pairs(p, group):
    hi = lax.bitcast_convert_type(p & jnp.uint32(0xFFFF0000), F32)
    lo = lax.bitcast_convert_type(p << 16, F32)
    parts = []
    for g0 in range(0, p.shape[1], group):
        parts += [hi[:, g0:g0 + group], lo[:, g0:g0 + group]]
    return jnp.concatenate(parts, axis=1)


def _adaln_kernel(c_ref, w_ref, b_ref, o_ref):
    s = c_ref[...]
    s = s * jax.nn.sigmoid(s)
    o_ref[...] = jnp.dot(s.astype(BF16), w_ref[...].astype(BF16),
                         preferred_element_type=F32) + b_ref[...]


def _adaln(cond, w, b):
    d, n = w.shape
    tn = 1024
    return pl.pallas_call(
        _adaln_kernel,
        grid=(n // tn,),
        in_specs=[pl.BlockSpec((8, d), lambda j: (0, 0)),
                  pl.BlockSpec((d, tn), lambda j: (0, j)),
                  pl.BlockSpec((1, tn), lambda j: (0, j))],
        out_specs=pl.BlockSpec((8, tn), lambda j: (0, j)),
        out_shape=jax.ShapeDtypeStruct((8, n), F32),
        compiler_params=_cparams(("arbitrary",)),
        name="adaln",
    )(cond, w, b)


INPROJ_COLS = 1024


def _inproj_kernel(x_ref, g_ref, sh_ref, sc_ref, w_ref, wg_ref, proj_ref, gate_ref):
    x = x_ref[...]
    y = x * lax.rsqrt(jnp.mean(x * x, axis=-1, keepdims=True) + EPS) * g_ref[...]
    xn = (y * (1.0 + sc_ref[...]) + sh_ref[...]).astype(BF16)
    gate_ref[...] = jnp.dot(xn, wg_ref[...], preferred_element_type=F32)
    for j in range(w_ref.shape[1] // INPROJ_COLS):
        cols = slice(j * INPROJ_COLS, (j + 1) * INPROJ_COLS)
        proj_ref[:, cols] = jnp.dot(xn, w_ref[:, cols], preferred_element_type=F32).astype(BF16)


def _inproj(x, g, sh, sc, w, wg, tm):
    bsz, s, d = x.shape
    p = w.shape[1]
    nt = s // tm
    x2 = x.reshape(bsz * s, d)
    resident = lambda shape: pl.BlockSpec(shape, lambda b, i: (0, 0), pipeline_mode=pl.Buffered(1))
    return pl.pallas_call(
        _inproj_kernel,
        grid=(bsz, nt),
        in_specs=[pl.BlockSpec((tm, d), lambda b, i: (b * nt + i, 0)),
                  pl.BlockSpec((1, d), lambda b, i: (0, 0)),
                  pl.BlockSpec((None, 1, d), lambda b, i: (b, 0, 0)),
                  pl.BlockSpec((None, 1, d), lambda b, i: (b, 0, 0)),
                  resident((d, p)),
                  resident((d, LANES))],
        out_specs=[pl.BlockSpec((tm, p), lambda b, i: (b * nt + i, 0)),
                   pl.BlockSpec((tm, LANES), lambda b, i: (b * nt + i, 0))],
        out_shape=[jax.ShapeDtypeStruct((bsz * s, p), BF16),
                   jax.ShapeDtypeStruct((bsz * s, LANES), F32)],
        compiler_params=_cparams(("arbitrary", "arbitrary")),
        name="inproj",
    )(x2, g, sh, sc, w, wg)


def _log_sigmoid(x):
    return jnp.minimum(x, 0.0) - jnp.log1p(jnp.exp(-jnp.abs(x)))


def _gates_kernel(g_ref, b_ref, gc_ref, gr_ref):
    tm = g_ref.shape[0]
    row = lax.broadcasted_iota(jnp.int32, (tm, LANES), 0)
    lane = lax.broadcasted_iota(jnp.int32, (tm, LANES), 1)
    gp = GATE_SOFT_CAP * jnp.tanh((g_ref[...] + b_ref[...]) / GATE_SOFT_CAP)
    is_f = ((lane >> 2) & 1) == 1
    fwd_lane = lane < 2 * N_HEADS
    lf = jnp.where(is_f, _log_sigmoid(gp), 0.0)
    r2 = lax.broadcasted_iota(jnp.int32, (CHUNK, CHUNK), 0)
    c2 = lax.broadcasted_iota(jnp.int32, (CHUNK, CHUNK), 1)
    lower = (r2 >= c2).astype(F32)
    upper = (r2 <= c2).astype(F32)
    lane_c = lax.broadcasted_iota(jnp.int32, (CHUNK, LANES), 1)
    cums = []
    for c in range(tm // CHUNK):
        lf_c = lf[c * CHUNK:(c + 1) * CHUNK]
        cf = jnp.dot(lower, lf_c, precision=lax.Precision.HIGHEST, preferred_element_type=F32)
        cb = jnp.dot(upper, lf_c, precision=lax.Precision.HIGHEST, preferred_element_type=F32)
        cums.append(jnp.where(lane_c < 2 * N_HEADS, cf, cb))
    cdir = jnp.concatenate(cums, axis=0)
    a = jnp.where(is_f, cdir, gp - pltpu.roll(cdir, LANES - N_HEADS, 1))

    pos = row % CHUNK
    x = a
    k = 1
    while k < CHUNK:
        from_before = jnp.where(pos >= k, pltpu.roll(x, k, 0), -jnp.inf)
        from_after = jnp.where(pos < CHUNK - k, pltpu.roll(x, tm - k, 0), -jnp.inf)
        x = jnp.maximum(x, jnp.where(fwd_lane, from_before, from_after))
        k *= 2
    gc_ref[...] = jnp.where(is_f, a, x)

    lane_1 =lax.broadcasted_iota(jnp.int32, (1, LANES), 1)
    for c in range(tm // CHUNK):
        lo = c * CHUNK
        xc, ac = x[lo:lo + CHUNK], a[lo:lo + CHUNK]
        end_max = jnp.where(lane_1 < 2 * N_HEADS, xc[CHUNK - 1:CHUNK], xc[0:1])
        e = jnp.exp(ac - end_max)
        rows = jnp.where(((lane_c >> 2) & 1) == 1, pltpu.roll(e, N_HEADS, 1), ac)
        gr_ref[:, lo:lo + CHUNK] = rows.T[:N_GATE_COLS, :]


def _gates(gpre, b_if, tm):
    t = gpre.shape[0]
    return pl.pallas_call(
        _gates_kernel,
        grid=(t // tm,),
        in_specs=[pl.BlockSpec((tm, LANES), lambda i: (i, 0)),
                  pl.BlockSpec((1, LANES), lambda i: (0, 0))],
        out_specs=[pl.BlockSpec((tm, LANES), lambda i: (i, 0)),
                   pl.BlockSpec((N_GATE_COLS, tm), lambda i: (0, i))],
        out_shape=[jax.ShapeDtypeStruct((t, LANES), F32),
                   jax.ShapeDtypeStruct((N_GATE_COLS, t), F32)],
        compiler_params=_cparams(("arbitrary",)),
        name="gates",
    )(gpre, b_if)


DVX = DV + LANES
MLSTM_CHUNKS_PER_STEP = 2


def _mlstm_chunk(q, k, v_ext, rmax_col, b_col, r_row, e_row, b_last, rmax_last, mask, cx, m_st):
    scale = DK ** -0.5
    mb = jnp.maximum(m_st, jnp.broadcast_to(rmax_col, (CHUNK, CHUNK)))
    w_intra = jnp.exp(jnp.where(mask, r_row - mb, -jnp.inf))
    w_state = jnp.exp(m_st - mb)
    qk = lax.dot_general(q, k, (((1,), (1,)), ((), ())), preferred_element_type=F32)
    s = qk * (w_intra * scale)
    lhs = jnp.concatenate([s.astype(BF16), (q.astype(F32) * (w_state * scale)).astype(BF16)], axis=1)
    rhs = jnp.concatenate([v_ext, cx.astype(BF16)], axis=0)
    nx = jnp.dot(lhs, rhs, preferred_element_type=F32)
    denom = jnp.maximum(jnp.abs(nx[:, DV:]), jnp.exp(-(jnp.broadcast_to(b_col, (CHUNK, CHUNK)) + mb)))
    h = nx[:, :DV] / jnp.concatenate([denom, denom], axis=1)
    ke_t = (k.T.astype(F32) * e_row).astype(BF16)
    c_loc = jnp.dot(ke_t, v_ext, preferred_element_type=F32)
    m_loc = b_last + rmax_last
    m_new = jnp.maximum(b_last + m_st, m_loc)
    return h, jnp.exp(b_last + m_st - m_new) * cx + jnp.exp(m_loc - m_new) * c_loc, m_new


def _mlstm_kernel(qf_ref, kf_ref, vf_ref, gcf_ref, grf_ref, qb_ref, kb_ref, vb_ref, gcb_ref, grb_ref,
                  c0_ref, m0_ref, hf_ref, hb_ref, cout_ref, mout_ref, m_scr, *c_scrs):
    c = pl.program_id(1)

    @pl.when(c == 0)
    def _():
        for idx, c_scr in enumerate(c_scrs):
            c_scr[...] = c0_ref[idx]
        m_scr[...] = m0_ref[...]

    row = lax.broadcasted_iota(jnp.int32, (CHUNK, CHUNK), 0)
    col = lax.broadcasted_iota(jnp.int32, (CHUNK, CHUNK), 1)
    ones = jnp.ones((CHUNK, LANES), BF16)
    m_all = m_scr[...]
    dirs = ((qf_ref, kf_ref, vf_ref, gcf_ref, grf_ref, hf_ref, 0, CHUNK - 1, col <= row),
            (qb_ref, kb_ref, vb_ref, gcb_ref, grb_ref, hb_ref, 2 * N_HEADS, 0, col >= row))
    n_sub = qf_ref.shape[0] // CHUNK
    m_news = []
    for di, (q_ref, k_ref, v_ref, gc_ref, gr_ref, h_ref, off, last, mask) in enumerate(dirs):
        order = range(n_sub) if di == 0 else range(n_sub - 1, -1, -1)
        for hd in range(N_HEADS):
            idx = di * N_HEADS + hd
            lr, lb = off + hd, off + N_HEADS + hd
            cx, m_st = c_scrs[idx][...], m_all[idx][:, 0:1]
            for sub in order:
                r0 = sub * CHUNK
                rows = slice(r0, r0 + CHUNK)
                v_ext = jnp.concatenate([v_ref[rows, hd * DV:(hd + 1) * DV], ones], axis=1)
                h, cx, m_st = _mlstm_chunk(
                    q_ref[rows, hd * DK:(hd + 1) * DK], k_ref[rows, hd * DK:(hd + 1) * DK], v_ext,
                    gc_ref[rows, lr:lr + 1], gc_ref[rows, lb:lb + 1],
                    gr_ref[lr:lr + 1, rows], gr_ref[lb:lb + 1, rows],
                    gc_ref[r0 + last:r0 + last + 1, lb:lb + 1], gc_ref[r0 + last:r0 + last + 1, lr:lr + 1],
                    mask, cx, m_st)
                h_ref[rows, hd * DV:(hd + 1) * DV] = h
            c_scrs[idx][...] = cx
            m_news.append(jnp.broadcast_to(m_st, (1, LANES)))
    for idx, m_new in enumerate(m_news):
        m_scr[idx] = m_new

    @pl.when(c == pl.num_programs(1) - 1)
    def _():
        for idx, c_scr in enumerate(c_scrs):
            cout_ref[idx] = c_scr[...]
        mout_ref[...] = m_scr[...]


def _mlstm(proj, gcol, grow, bsz, s, c0, m0):
    rows = MLSTM_CHUNKS_PER_STEP * CHUNK
    nc = s // rows
    t = bsz * s
    fwd = lambda b, c: b * nc + c
    bwd = lambda b, c: b * nc + (nc - 1 - c)

    def specs(ci):
        return [pl.BlockSpec((rows, QK_COLS), lambda b, c: (ci(b, c), 0)),
                pl.BlockSpec((rows, QK_COLS), lambda b, c: (ci(b, c), 1)),
                pl.BlockSpec((rows, MLSTM_WIDTH), lambda b, c: (ci(b, c), 1)),
                pl.BlockSpec((rows, LANES), lambda b, c: (ci(b, c), 0)),
                pl.BlockSpec((N_GATE_COLS, rows), lambda b, c: (0, ci(b, c)))]

    st_specs = [pl.BlockSpec((None, 2 * N_HEADS, DK, DVX), lambda b, c: (b, 0, 0, 0)),
                pl.BlockSpec((None, 2 * N_HEADS, 1, LANES), lambda b, c: (b, 0, 0, 0))]
    return pl.pallas_call(
        _mlstm_kernel,
        grid=(bsz, nc),
        in_specs=specs(fwd) + specs(bwd) + st_specs,
        out_specs=[pl.BlockSpec((rows, MLSTM_WIDTH), lambda b, c: (fwd(b, c), 0)),
                   pl.BlockSpec((rows, MLSTM_WIDTH), lambda b, c: (bwd(b, c), 0))] + st_specs,
        out_shape=[jax.ShapeDtypeStruct((t, MLSTM_WIDTH), F32),
                   jax.ShapeDtypeStruct((t, MLSTM_WIDTH), F32),
                   jax.ShapeDtypeStruct(c0.shape, F32),
                   jax.ShapeDtypeStruct(m0.shape, F32)],
        scratch_shapes=[pltpu.VMEM((2 * N_HEADS, 1, LANES), F32)]
        + [pltpu.VMEM((DK, DVX), F32) for _ in range(2 * N_HEADS)],
        compiler_params=_cparams(("arbitrary", "arbitrary")),
        name="mlstm",
    )(proj, proj, proj, gcol, grow, proj, proj, proj, gcol, grow, c0, m0)


MIX_ROWS = 256


def _mixout_kernel(o_ref, cb_ref, cc_ref, cx_ref, ccp_ref, cxp_ref, ccn_ref, cxn_ref, hf_ref, hb_ref, x_ref,
                   cw_ref, ng_ref, wout_ref, gt_ref, gffn_ref, shf_ref, scf_ref, wr_ref, br_ref,
                   x1_ref, xn2_ref, idx_ref, tw_ref):
    i = pl.program_id(1)
    tm = x_ref.shape[0]
    cw = cw_ref[...]

    has_prev = jnp.where(i > 0, 1.0, 0.0)
    has_next = jnp.where(i < pl.num_programs(1) - 1, 1.0, 0.0)
    up = ccp_ref[...].astype(F32) * cxp_ref[...].astype(F32) * has_prev
    un = ccn_ref[...].astype(F32) * cxn_ref[...].astype(F32) * has_next
    uv = cc_ref[:, CONV_HALF:].astype(F32) * cx_ref[:, CONV_HALF:].astype(F32)
    ext = jnp.concatenate([up, uv, un], axis=0)

    pos = lax.broadcasted_iota(jnp.int32, (MIX_ROWS, CONV_HALF), 0) & (GRID_W - 1)
    lane_f = lax.broadcasted_iota(jnp.int32, (MIX_ROWS, LANES), 1).astype(F32)
    lane4 = lax.broadcasted_iota(jnp.int32, (MIX_ROWS, TOP_K), 1)

    for r0 in range(0, tm, MIX_ROWS):
        rows = slice(r0, r0 + MIX_ROWS)

        uh = cc_ref[rows, :CONV_HALF].astype(F32) * cx_ref[rows, :CONV_HALF].astype(F32)
        left = jnp.where(pos == 0, 0.0, pltpu.roll(uh, 1, 0))
        right = jnp.where(pos == GRID_W - 1, 0.0, pltpu.roll(uh, MIX_ROWS - 1, 0))
        yh = cw[0:1, :CONV_HALF] * left + cw[1:2, :CONV_HALF] * uh + cw[2:3, :CONV_HALF] * right
        yv = (cw[0:1, CONV_HALF:] * ext[r0:r0 + MIX_ROWS]
              + cw[1:2, CONV_HALF:] * ext[r0 + GRID_W:r0 + GRID_W + MIX_ROWS]
              + cw[2:3, CONV_HALF:] * ext[r0 + 2 * GRID_W:r0 + 2 * GRID_W + MIX_ROWS])
        yc = cb_ref[rows, :].astype(F32) * jnp.concatenate([yh, yv], axis=1)

        hs = hf_ref[rows, :] + hb_ref[rows, :]
        parts = []
        for hd in range(N_HEADS):
            seg = hs[:, hd * DV:(hd + 1) * DV]
            parts.append(seg * lax.rsqrt(jnp.mean(seg * seg, axis=-1, keepdims=True) + EPS))
        hm = jnp.concatenate(parts, axis=1) * ng_ref[...] * jax.nn.sigmoid(o_ref[rows, :].astype(F32))

        z = jnp.concatenate([hm.astype(BF16), yc.astype(BF16)], axis=1)
        x1 = x_ref[rows, :] + gt_ref[...] * jnp.dot(z, wout_ref[...], preferred_element_type=F32)
        x1_ref[rows, :] = x1

        y = x1 * lax.rsqrt(jnp.mean(x1 * x1, axis=-1, keepdims=True) + EPS) * gffn_ref[...]
        xn2 = y * (1.0 + scf_ref[...]) + shf_ref[...]
        xn2_ref[rows, :] = _pack_pairs(xn2)

        logits = jnp.dot(xn2.astype(BF16), wr_ref[...], preferred_element_type=F32) + br_ref[...]
        vals, idxs = [], []
        for _ in range(TOP_K):
            mx = jnp.max(logits, axis=-1, keepdims=True)
            ik = jnp.min(jnp.where(logits == mx, lane_f, float(LANES)), axis=-1, keepdims=True)
            vals.append(mx)
            idxs.append(ik)
            logits = jnp.where(lane_f == ik, -jnp.inf, logits)
        es = [jnp.exp(v - vals[0]) for v in vals]
        tot = es[0] + es[1] + es[2] + es[3]
        idx_out = jnp.zeros((MIX_ROWS, TOP_K), F32)
        tw_out = jnp.zeros((MIX_ROWS, TOP_K), F32)
        for kk in range(TOP_K):
            idx_out = jnp.where(lane4 == kk, idxs[kk], idx_out)
            tw_out = jnp.where(lane4 == kk, es[kk] / tot, tw_out)
        idx_ref[rows, :] = idx_out.astype(jnp.int32)
        tw_ref[rows, :] = tw_out


def _mixout(proj, hf, hb, x2, conv_w, norm_g, w_out, gt, g_ffn, sh_f, sc_f, w_r, b_r, bsz, s, tm):
    t, d = x2.shape
    nt = s // tm
    rb = tm // GRID_W
    last_rb = t // GRID_W - 1
    row = lambda b, i: b * nt + i
    w = MLSTM_WIDTH
    vec = lambda n: pl.BlockSpec((1, n), lambda b, i: (0, 0))
    per_b = pl.BlockSpec((None, 1, d), lambda b, i: (b, 0, 0))
    halo_prev = lambda cblk: pl.BlockSpec(
        (GRID_W, CONV_HALF), lambda b, i: (jnp.maximum(row(b, i) * rb - 1, 0), cblk))
    halo_next = lambda cblk: pl.BlockSpec(
        (GRID_W, CONV_HALF), lambda b, i: (jnp.minimum((row(b, i) + 1) * rb, last_rb), cblk))
    return pl.pallas_call(
        _mixout_kernel,
        grid=(bsz, nt),
        in_specs=[pl.BlockSpec((tm, w), lambda b, i: (row(b, i), 2)),
                  pl.BlockSpec((tm, w), lambda b, i: (row(b, i), 3)),
                  pl.BlockSpec((tm, w), lambda b, i: (row(b, i), 4)),
                  pl.BlockSpec((tm, w), lambda b, i: (row(b, i), 5)),
                  halo_prev(9), halo_prev(11), halo_next(9), halo_next(11),
                  pl.BlockSpec((tm, w), lambda b, i: (row(b, i), 0)),
                  pl.BlockSpec((tm, w), lambda b, i: (row(b, i), 0)),
                  pl.BlockSpec((tm, d), lambda b, i: (row(b, i), 0)),
                  pl.BlockSpec((3, CONV_WIDTH), lambda b, i: (0, 0)),
                  vec(w),
                  pl.BlockSpec((d, d), lambda b, i: (0, 0)),
                  per_b, vec(d), per_b, per_b,
                  pl.BlockSpec((d, LANES), lambda b, i: (0, 0)),
                  vec(LANES)],
        out_specs=[pl.BlockSpec((tm, d), lambda b, i: (row(b, i), 0)),
                   pl.BlockSpec((tm, d // 2), lambda b, i: (row(b, i), 0)),
                   pl.BlockSpec((tm, TOP_K), lambda b, i: (row(b, i), 0)),
                   pl.BlockSpec((tm, TOP_K), lambda b, i: (row(b, i), 0))],
        out_shape=[jax.ShapeDtypeStruct((t, d), F32),
                   jax.ShapeDtypeStruct((t, d // 2), jnp.uint32),
                   jax.ShapeDtypeStruct((t, TOP_K), jnp.int32),
                   jax.ShapeDtypeStruct((t, TOP_K), F32)],
        compiler_params=_cparams(("arbitrary", "arbitrary")),
        name="mixout",
    )(proj, proj, proj, proj, proj, proj, proj, proj, hf, hb, x2,
      conv_w, norm_g, w_out, gt, g_ffn, sh_f, sc_f, w_r, b_r)


def _rank_kernel(idx_ref, rank_ref, cnt_ref, run_scr):
    @pl.when(pl.program_id(0) == 0)
    def _():
        run_scr[...] = jnp.zeros_like(run_scr)

    tm = idx_ref.shape[0]
    idx = idx_ref[...]
    lane = lax.broadcasted_iota(jnp.int32, (tm, LANES), 1)
    hits = [lane == idx[:, kk:kk + 1] for kk in range(TOP_K)]
    onehot = jnp.zeros((tm, LANES), F32)
    for hit in hits:
        onehot = onehot + hit.astype(F32)
    r = lax.broadcasted_iota(jnp.int32, (tm, tm), 0)
    c = lax.broadcasted_iota(jnp.int32, (tm, tm), 1)
    before = jnp.dot((c < r).astype(BF16), onehot.astype(BF16), preferred_element_type=F32) + run_scr[...]
    lane4 = lax.broadcasted_iota(jnp.int32, (tm, TOP_K), 1)
    rank = jnp.zeros((tm, TOP_K), F32)
    for kk, hit in enumerate(hits):
        rk = jnp.sum(jnp.where(hit, before, 0.0), axis=-1, keepdims=True)
        rank = jnp.where(lane4 == kk, rk, rank)
    rank_ref[...] = rank.astype(jnp.int32)
    run_scr[...] = run_scr[...] + jnp.sum(onehot, axis=0, keepdims=True)
    cnt_ref[...] = run_scr[...]


def _rank(idx, tm):
    t = idx.shape[0]
    return pl.pallas_call(
        _rank_kernel,
        grid=(t // tm,),
        in_specs=[pl.BlockSpec((tm, TOP_K), lambda i: (i, 0))],
        out_specs=[pl.BlockSpec((tm, TOP_K), lambda i: (i, 0)),
                   pl.BlockSpec((1, LANES), lambda i: (0, 0))],
        out_shape=[jax.ShapeDtypeStruct((t, TOP_K), jnp.int32),
                   jax.ShapeDtypeStruct((1, LANES), F32)],
        scratch_shapes=[pltpu.VMEM((1, LANES), F32)],
        compiler_params=_cparams(("arbitrary",)),
        name="rank",
    )(idx)


def _dispatch_kernel(pad_ref, dest_hbm, xn_ref, xs_hbm, dsm, zeros_scr, sem_idx, sem_rows, sem_pad):
    i = pl.program_id(0)
    tm = xn_ref.shape[0]
    n_idx = tm * TOP_K
    idx_copy = pltpu.make_async_copy(dest_hbm.at[pl.ds(i * n_idx, n_idx)], dsm, sem_idx)
    idx_copy.start()

    def pad_copy(off, size):
        return pltpu.make_async_copy(zeros_scr.at[pl.ds(0, size), :], xs_hbm.at[pl.ds(off, size), :], sem_pad)

    def for_each_pad_piece(fn):
        def per_expert(e, carry):
            off = pad_ref[2 * e]
            n = pad_ref[2 * e + 1]
            head = n & (SUBLANES - 1)
            for r in range(SUBLANES - 1):
                @pl.when(r < head)
                def _(r=r):
                    fn(pad_copy(off + r, 1))

            off = off + head
            size = ROW_BLOCK // 2
            while size >= SUBLANES:
                take = (n & size) != 0

                @pl.when(take)
                def _(off=off, size=size):
                    fn(pad_copy(pl.multiple_of(off, SUBLANES), size))

                off = off + jnp.where(take, size, 0)
                size //= 2
            return carry
        lax.fori_loop(0, N_EXPERTS, per_expert, 0)

    @pl.when(i == 0)
    def _():
        zeros_scr[...] = jnp.zeros_like(zeros_scr)
        for_each_pad_piece(lambda cp: cp.start())
        for_each_pad_piece(lambda cp: cp.wait())

    idx_copy.wait()

    def row_copy(t, kk):
        return pltpu.make_async_copy(xn_ref.at[pl.ds(t, 1), :],
                                     xs_hbm.at[pl.ds(dsm[t * TOP_K + kk], 1), :], sem_rows)

    def issue(t, carry):
        for kk in range(TOP_K):
            row_copy(t, kk).start()
        return carry

    lax.fori_loop(0, tm, issue, 0, unroll=ISSUE_UNROLL)
    pltpu.make_async_copy(xs_hbm.at[pl.ds(0, n_idx), :], xs_hbm.at[pl.ds(0, n_idx), :], sem_rows).wait()


def _dispatch(pad_info, dest_flat, xn2, n_rows, tm):
    t, d = xn2.shape
    return pl.pallas_call(
        _dispatch_kernel,
        grid_spec=pltpu.PrefetchScalarGridSpec(
            num_scalar_prefetch=1,
            grid=(t // tm,),
            in_specs=[pl.BlockSpec(memory_space=pl.ANY),
                      pl.BlockSpec((tm, d), lambda i, pad: (i, 0))],
            out_specs=pl.BlockSpec(memory_space=pl.ANY),
            scratch_shapes=[pltpu.SMEM((tm * TOP_K,), jnp.int32),
                            pltpu.VMEM((ROW_BLOCK // 2, d), xn2.dtype),
                            pltpu.SemaphoreType.DMA(()),
                            pltpu.SemaphoreType.DMA(()),
                            pltpu.SemaphoreType.DMA(())]),
        out_shape=jax.ShapeDtypeStruct((n_rows, d), xn2.dtype),
        compiler_params=_cparams(("arbitrary",)),
        name="dispatch",
    )(pad_info, dest_flat, xn2)


def _new_expert(be_ref, j):
    return jnp.logical_or(j == 0, be_ref[j] != be_ref[jnp.maximum(j - 1, 0)])


GU_COLS = 512


def _expert_gu_kernel(be_ref, nu_ref, nxt_ref, xs_ref, w_hbm, bg_ref, bu_ref, act_ref,
                      stage_g, stage_u, wg_scr, wu_scr, sem):
    n = pl.program_id(0)
    j = pl.program_id(1)
    nt = pl.num_programs(0)
    tn = wg_scr.shape[1]

    def weight_copies(e, nn):
        col_g = pl.multiple_of(nn * tn, tn)
        col_u = pl.multiple_of((nt + nn) * tn, tn)
        return (pltpu.make_async_copy(w_hbm.at[e, :, pl.ds(col_g, tn)], stage_g, sem.at[0]),
                pltpu.make_async_copy(w_hbm.at[e, :, pl.ds(col_u, tn)], stage_u, sem.at[1]))

    @pl.when(j < nu_ref[0])
    def _():
        e = be_ref[j]

        @pl.when(_new_expert(be_ref, j))
        def _():
            @pl.when(jnp.logical_and(n == 0, j == 0))
            def _():
                for cp in weight_copies(e, n):
                    cp.start()

            for cp in weight_copies(e, n):
                cp.wait()
            wg_scr[...] = stage_g[...].astype(BF16)
            wu_scr[...] = stage_u[...].astype(BF16)

            e_next = nxt_ref[e]
            in_pass = e_next >= 0

            @pl.when(jnp.logical_or(in_pass, n + 1 < nt))
            def _():
                for cp in weight_copies(jnp.where(in_pass, e_next, be_ref[0]), jnp.where(in_pass, n, n + 1)):
                    cp.start()

        x = _unpack_pairs(xs_ref[...], xs_ref.shape[1]).astype(BF16)
        for c0 in range(0, tn, GU_COLS):
            cols = slice(c0, c0 + GU_COLS)
            g = jnp.dot(x, wg_scr[:, cols], preferred_element_type=F32) + bg_ref[:, cols]
            u = jnp.dot(x, wu_scr[:, cols], preferred_element_type=F32) + bu_ref[:, cols]
            gate = jnp.minimum(g, SWIGLU_LIMIT)
            up = jnp.clip(u, -SWIGLU_LIMIT, SWIGLU_LIMIT)
            act_ref[:, cols] = ((up + 1.0) * gate * jax.nn.sigmoid(SWIGLU_ALPHA * gate)).astype(BF16)


def _expert_gu(blk_expert, n_used, nxt_expert, xs, w_gu, b_gu, tn):
    n_rows, dp = xs.shape
    d = w_gu.shape[1]
    dff = w_gu.shape[2] // 2
    nt = dff // tn
    nb = n_rows // ROW_BLOCK
    blk = lambda j, nu: jnp.minimum(j, nu[0] - 1)
    exp = lambda j, be, nu: be[blk(j, nu)]
    return pl.pallas_call(
        _expert_gu_kernel,
        grid_spec=pltpu.PrefetchScalarGridSpec(
            num_scalar_prefetch=3,
            grid=(nt, nb),
            in_specs=[pl.BlockSpec((ROW_BLOCK, dp), lambda n, j, be, nu, nx: (blk(j, nu), 0)),
                      pl.BlockSpec(memory_space=pl.ANY),
                      pl.BlockSpec((None, 1, tn), lambda n, j, be, nu, nx: (exp(j, be, nu), 0, n)),
                      pl.BlockSpec((None, 1, tn), lambda n, j, be, nu, nx: (exp(j, be, nu), 0, nt + n))],
            out_specs=pl.BlockSpec((ROW_BLOCK, tn), lambda n, j, be, nu, nx: (blk(j, nu), n)),
            scratch_shapes=[pltpu.VMEM((d, tn), F32), pltpu.VMEM((d, tn), F32),
                            pltpu.VMEM((d, tn), BF16), pltpu.VMEM((d, tn), BF16),
                            pltpu.SemaphoreType.DMA((2,))]),
        out_shape=jax.ShapeDtypeStruct((n_rows, dff), BF16),
        compiler_params=_cparams(("arbitrary", "arbitrary")),
        name="expert_gu",
    )(blk_expert, n_used, nxt_expert, xs, w_gu, b_gu, b_gu)


def _expert_down_kernel(be_ref, nu_ref, nxt_ref, act_ref, w_hbm, b_ref, y_ref, stage, w_scr, sem):
    j = pl.program_id(0)

    def weight_copy(e):
        return pltpu.make_async_copy(w_hbm.at[e], stage, sem)

    @pl.when(j < nu_ref[0])
    def _():
        e = be_ref[j]

        @pl.when(_new_expert(be_ref, j))
        def _():
            @pl.when(j == 0)
            def _():
                weight_copy(e).start()

            weight_copy(e).wait()
            w_scr[...] = stage[...].astype(BF16)
            e_next = nxt_ref[e]

            @pl.when(e_next >= 0)
            def _():
                weight_copy(e_next).start()

        act = act_ref[...]
        for c0 in range(0, w_scr.shape[1], DOWN_COLS):
            cols = slice(c0, c0 + DOWN_COLS)
            y = jnp.dot(act, w_scr[:, cols], preferred_element_type=F32) + b_ref[:, cols]
            y_ref[:, c0 // 2:(c0 + DOWN_COLS) // 2] = _pack_pairs(y)


def _expert_down(blk_expert, n_used, nxt_expert, act, w_down, b_down):
    n_rows, dff = act.shape
    d = w_down.shape[2]
    nb = n_rows // ROW_BLOCK
    blk = lambda j, nu: jnp.minimum(j, nu[0] - 1)
    exp = lambda j, be, nu: be[blk(j, nu)]
    return pl.pallas_call(
        _expert_down_kernel,
        grid_spec=pltpu.PrefetchScalarGridSpec(
            num_scalar_prefetch=3,
            grid=(nb,),
            in_specs=[pl.BlockSpec((ROW_BLOCK, dff), lambda j, be, nu, nx: (blk(j, nu), 0)),
                      pl.BlockSpec(memory_space=pl.ANY),
                      pl.BlockSpec((None, 1, d), lambda j, be, nu, nx: (exp(j, be, nu), 0, 0))],
            out_specs=pl.BlockSpec((ROW_BLOCK, d // 2), lambda j, be, nu, nx: (blk(j, nu), 0)),
            scratch_shapes=[pltpu.VMEM((dff, d), F32), pltpu.VMEM((dff, d), BF16),
                            pltpu.SemaphoreType.DMA(())]),
        out_shape=jax.ShapeDtypeStruct((n_rows, d // 2), jnp.uint32),
        compiler_params=_cparams(("arbitrary",)),
        name="expert_down",
    )(blk_expert, n_used, nxt_expert, act, w_down, b_down)


def _combine_kernel(dest_hbm, y_hbm, x1_ref, tw_ref, gt_ref, gfin_ref, out_ref, dsm, buf, sem_idx, sem_rows):
    i = pl.program_id(1) + pl.program_id(0) * pl.num_programs(1)
    tm = x1_ref.shape[0]
    n_idx = tm * TOP_K
    idx_copy = pltpu.make_async_copy(dest_hbm.at[pl.ds(i * n_idx, n_idx)], dsm, sem_idx)
    idx_copy.start()
    idx_copy.wait()

    def row_copy(t, kk):
        return pltpu.make_async_copy(y_hbm.at[pl.ds(dsm[t * TOP_K + kk], 1), :],
                                     buf.at[kk, pl.ds(t, 1), :], sem_rows)

    def issue(t, carry):
        for kk in range(TOP_K):
            row_copy(t, kk).start()
        return carry

    lax.fori_loop(0, tm, issue, 0, unroll=ISSUE_UNROLL)
    for kk in range(TOP_K):
        pltpu.make_async_copy(y_hbm.at[pl.ds(0, tm), :], buf.at[kk], sem_rows).wait()

    tw = tw_ref[...]
    acc = _unpack_pairs(buf[0], DOWN_COLS // 2) * tw[:, 0:1]
    for kk in range(1, TOP_K):
        acc = acc + _unpack_pairs(buf[kk], DOWN_COLS // 2) * tw[:, kk:kk + 1]
    x2 = x1_ref[...] + gt_ref[...] * acc
    out_ref[...] = x2 * lax.rsqrt(jnp.mean(x2 * x2, axis=-1, keepdims=True) + EPS) * gfin_ref[...]


def _combine(dest_flat, y, x1, tw, gt, g_final, bsz, s, tm):
    t, d = x1.shape
    nt = s // tm
    row = lambda b, i: (b * nt + i, 0)
    return pl.pallas_call(
        _combine_kernel,
        grid=(bsz, nt),
        in_specs=[pl.BlockSpec(memory_space=pl.ANY),
                  pl.BlockSpec(memory_space=pl.ANY),
                  pl.BlockSpec((tm, d), row),
                  pl.BlockSpec((tm, TOP_K), row),
                  pl.BlockSpec((None, 1, d), lambda b, i: (b, 0, 0)),
                  pl.BlockSpec((1, d), lambda b, i: (0, 0))],
        out_specs=pl.BlockSpec((tm, d), row),
        out_shape=jax.ShapeDtypeStruct((t, d), F32),
        scratch_shapes=[pltpu.SMEM((tm * TOP_K,), jnp.int32),
                        pltpu.VMEM((TOP_K, tm, y.shape[1]), y.dtype),
                        pltpu.SemaphoreType.DMA(()),
                        pltpu.SemaphoreType.DMA(())],
        compiler_params=_cparams(("arbitrary", "arbitrary")),
        name="combine",
    )(dest_flat, y, x1, tw, gt, g_final)


def _pad_lanes(a, value=0.0):
    return jnp.pad(a, ((0, 0), (0, LANES - a.shape[1])), constant_values=value)


def _routing_tables(idx, rank, counts_f, n_blocks):
    counts = counts_f[0, :N_EXPERTS].astype(jnp.int32)
    padded = (counts + ROW_BLOCK - 1) // ROW_BLOCK * ROW_BLOCK
    pend = jnp.cumsum(padded)
    pstart = pend - padded
    dest = (pstart[idx] + rank).reshape(-1)
    blk_start = jnp.arange(n_blocks, dtype=jnp.int32) * ROW_BLOCK
    blk_expert = jnp.minimum(jnp.sum((pend[None, :] <= blk_start[:, None]).astype(jnp.int32), axis=1),
                             N_EXPERTS - 1)
    n_used = (pend[-1:] // ROW_BLOCK).astype(jnp.int32)
    pad_info = jnp.stack([pstart + counts, padded - counts], axis=1).reshape(-1).astype(jnp.int32)
    ids = jnp.arange(N_EXPERTS, dtype=jnp.int32)
    later = jnp.where((ids[None, :] > ids[:, None]) & (counts[None, :] > 0), ids[None, :], N_EXPERTS)
    nxt = jnp.min(later, axis=1)
    nxt_expert = jnp.where(nxt == N_EXPERTS, -1, nxt).astype(jnp.int32)
    return dest, blk_expert, n_used, nxt_expert, pad_info


def _layer(x, c, ctx, c_ctx, w_ada, b_ada, g_mix, w_in, b_if, conv_w, norm_g, w_out,
           g_ffn, w_router, b_router, w_gu, b_gu, w_down, b_down, g_final):
    bsz, s, d = x.shape
    s_ctx = ctx.shape[1]

    cond = jnp.zeros((8, d), F32).at[:bsz].set(c).at[bsz].set(c_ctx)
    mod = _adaln(cond, w_ada, b_ada[None, :])
    sh_m, sc_m, gt_m, sh_f, sc_f, gt_f = [m[:, None, :] for m in jnp.split(mod, N_MOD, axis=-1)]
    lat = lambda m: m[:bsz]
    ctxm = lambda m: jnp.broadcast_to(m[bsz:bsz + 1], (bsz, 1, d))

    g0 = 2 * QK_COLS + 2 * MLSTM_WIDTH
    w_main = jnp.concatenate([w_in[:, :g0], w_in[:, g0 + N_GATE_COLS:]], axis=1).astype(BF16)
    w_gate = _pad_lanes(w_in[:, g0:g0 + N_GATE_COLS]).astype(BF16)
    b_gate = _pad_lanes(b_if[None, :])
    g_mix2 = g_mix[None, :]

    proj_c, gpre_c = _inproj(ctx, g_mix2, ctxm(sh_m), ctxm(sc_m), w_main, w_gate, min(s_ctx, 512))
    gcol_c, grow_c = _gates(gpre_c, b_gate, 512)
    zeros_state = (jnp.zeros((bsz, 2 * N_HEADS, DK, DVX), F32),
                   jnp.zeros((bsz, 2 * N_HEADS, 1, LANES), F32))
    _, _, c0, m0 = _mlstm(proj_c, gcol_c, grow_c, bsz, s_ctx, *zeros_state)

    proj, gpre = _inproj(x, g_mix2, lat(sh_m), lat(sc_m), w_main, w_gate, 512)
    gcol, grow = _gates(gpre, b_gate, 512)
    hf, hb, _, _ = _mlstm(proj, gcol, grow, bsz, s, c0, m0)
    x1, xn2, idx, tw = _mixout(
        proj, hf, hb, x.reshape(bsz * s, d), conv_w, norm_g[None, :], w_out.astype(BF16), lat(gt_m),
        g_ffn[None, :], lat(sh_f), lat(sc_f), _pad_lanes(w_router).astype(BF16),
        _pad_lanes(b_router[None, :], NEG_BIG), bsz, s, 512)

    t = bsz * s
    n_blocks = -(-(t * TOP_K) // ROW_BLOCK) + N_EXPERTS
    rank, counts = _rank(idx, 512)
    dest, blk_expert, n_used, nxt_expert, pad_info = _routing_tables(idx, rank, counts, n_blocks)
    xs = _dispatch(pad_info, dest, xn2, n_blocks * ROW_BLOCK, 512)
    act = _expert_gu(blk_expert, n_used, nxt_expert, xs, w_gu, b_gu[:, None, :], 1024)
    y = _expert_down(blk_expert, n_used, nxt_expert, act, w_down, b_down[:, None, :])
    out = _combine(dest, y, x1, tw, lat(gt_f), g_final[None, :], bsz, s, 256)
    return out.reshape(bsz, s, d)


def kernel(x, c, ctx, c_ctx, w_ada, b_ada, g_mix, w_in, b_if, conv_w, mlstm_norm_g, w_out,
           g_ffn, w_router, b_router, w_gu, b_gu, w_down, b_down, g_final):
    return _layer(x, c, ctx, c_ctx, w_ada[0], b_ada[0], g_mix[0], w_in[0], b_if[0], conv_w[0],
                  mlstm_norm_g[0], w_out[0], g_ffn[0], w_router[0], b_router[0], w_gu[0], b_gu[0],
                  w_down[0], b_down[0], g_final)
```

```python
import functools

import jax
import jax.numpy as jnp
from jax import lax
from jax.experimental import pallas as pl
from jax.experimental.pallas import tpu as pltpu

F32 = jnp.float32
BF16 = jnp.bfloat16

N_HEADS = 4
DK = 128
DV = 256
QK_COLS = N_HEADS * DK
MLSTM_WIDTH = N_HEADS * DV
CONV_WIDTH = 1024
CONV_HALF = CONV_WIDTH // 2
N_GATE_COLS = 4 * N_HEADS
GRID_W = 64
CHUNK = 128
GATE_SOFT_CAP = 15.0
N_EXPERTS = 32
TOP_K = 4
SWIGLU_LIMIT = 7.0
SWIGLU_ALPHA = 1.702
N_MOD = 6
EPS = 1e-6
LANES = 128
SUBLANES = 8
ROW_BLOCK = 1088
DOWN_COLS = 1024
ISSUE_UNROLL = 8
NEG_BIG = -1e30
VMEM_LIMIT = 56 * 1024 * 1024


def _cparams(sem):
    return pltpu.CompilerParams(dimension_semantics=sem, vmem_limit_bytes=VMEM_LIMIT)


def _pack_pairs(x):
    bits = lax.bitcast_convert_type(x.astype(BF16).astype(F32), jnp.uint32)
    g = x.shape[1] // 2
    return bits[:, :g] | (bits[:, g:] >> 16)


def _unpack_pairs(p, group):
    hi = lax.bitcast_convert_type(p & jnp.uint32(0xFFFF0000), F32)
    lo = lax.bitcast_convert_type(p << 16, F32)
    parts = []
    for g0 in range(0, p.shape[1], group):
        parts += [hi[:, g0:g0 + group], lo[:, g0:g0 + group]]
    return jnp.concatenate(parts, axis=1)


def _adaln_kernel(c_ref, w_ref, b_ref, o_ref):
    s = c_ref[...]
    s = s * jax.nn.sigmoid(s)
    o_ref[...] = jnp.dot(s.astype(BF16), w_ref[...].astype(BF16),
                         preferred_element_type=F32) + b_ref[...]


def _adaln(cond, w, b):
    d, n = w.shape
    tn = 1024
    return pl.pallas_call(
        _adaln_kernel,
        grid=(n // tn,),
        in_specs=[pl.BlockSpec((8, d), lambda j: (0, 0)),
                  pl.BlockSpec((d, tn), lambda j: (0, j)),
                  pl.BlockSpec((1, tn), lambda j: (0, j))],
        out_specs=pl.BlockSpec((8, tn), lambda j: (0, j)),
        out_shape=jax.ShapeDtypeStruct((8, n), F32),
        compiler_params=_cparams(("arbitrary",)),
        name="adaln",
    )(cond, w, b)


INPROJ_COLS = 1024


def _inproj_kernel(x_ref, g_ref, sh_ref, sc_ref, w_ref, wg_ref, proj_ref, gate_ref):
    x = x_ref[...]
    y = x * lax.rsqrt(jnp.mean(x * x, axis=-1, keepdims=True) + EPS) * g_ref[...]
    xn = (y * (1.0 + sc_ref[...]) + sh_ref[...]).astype(BF16)
    gate_ref[...] = jnp.dot(xn, wg_ref[...], preferred_element_type=F32)
    for j in range(w_ref.shape[1] // INPROJ_COLS):
        cols = slice(j * INPROJ_COLS, (j + 1) * INPROJ_COLS)
        proj_ref[:, cols] = jnp.dot(xn, w_ref[:, cols], preferred_element_type=F32).astype(BF16)


def _inproj(x, g, sh, sc, w, wg, tm):
    bsz, s, d = x.shape
    p = w.shape[1]
    nt = s // tm
    x2 = x.reshape(bsz * s, d)
    resident = lambda shape: pl.BlockSpec(shape, lambda b, i: (0, 0), pipeline_mode=pl.Buffered(1))
    return pl.pallas_call(
        _inproj_kernel,
        grid=(bsz, nt),
        in_specs=[pl.BlockSpec((tm, d), lambda b, i: (b * nt + i, 0)),
                  pl.BlockSpec((1, d), lambda b, i: (0, 0)),
                  pl.BlockSpec((None, 1, d), lambda b, i: (b, 0, 0)),
                  pl.BlockSpec((None, 1, d), lambda b, i: (b, 0, 0)),
                  resident((d, p)),
                  resident((d, LANES))],
        out_specs=[pl.BlockSpec((tm, p), lambda b, i: (b * nt + i, 0)),
                   pl.BlockSpec((tm, LANES), lambda b, i: (b * nt + i, 0))],
        out_shape=[jax.ShapeDtypeStruct((bsz * s, p), BF16),
                   jax.ShapeDtypeStruct((bsz * s, LANES), F32)],
        compiler_params=_cparams(("arbitrary", "arbitrary")),
        name="inproj",
    )(x2, g, sh, sc, w, wg)


def _log_sigmoid(x):
    return jnp.minimum(x, 0.0) - jnp.log1p(jnp.exp(-jnp.abs(x)))


def _gates_kernel(g_ref, b_ref, gc_ref, gr_ref):
    tm = g_ref.shape[0]
    row = lax.broadcasted_iota(jnp.int32, (tm, LANES), 0)
    lane = lax.broadcasted_iota(jnp.int32, (tm, LANES), 1)
    gp = GATE_SOFT_CAP * jnp.tanh((g_ref[...] + b_ref[...]) / GATE_SOFT_CAP)
    is_f = ((lane >> 2) & 1) == 1
    fwd_lane = lane < 2 * N_HEADS
    lf = jnp.where(is_f, _log_sigmoid(gp), 0.0)
    r2 = lax.broadcasted_iota(jnp.int32, (CHUNK, CHUNK), 0)
    c2 = lax.broadcasted_iota(jnp.int32, (CHUNK, CHUNK), 1)
    lower = (r2 >= c2).astype(F32)
    upper = (r2 <= c2).astype(F32)
    lane_c = lax.broadcasted_iota(jnp.int32, (CHUNK, LANES), 1)
    cums = []
    for c in range(tm // CHUNK):
        lf_c = lf[c * CHUNK:(c + 1) * CHUNK]
        cf = jnp.dot(lower, lf_c, precision=lax.Precision.HIGHEST, preferred_element_type=F32)
        cb = jnp.dot(upper, lf_c, precision=lax.Precision.HIGHEST, preferred_element_type=F32)
        cums.append(jnp.where(lane_c < 2 * N_HEADS, cf, cb))
    cdir = jnp.concatenate(cums, axis=0)
    a = jnp.where(is_f, cdir, gp - pltpu.roll(cdir, LANES - N_HEADS, 1))

    pos = row % CHUNK
    x = a
    k = 1
    while k < CHUNK:
        from_before = jnp.where(pos >= k, pltpu.roll(x, k, 0), -jnp.inf)
        from_after = jnp.where(pos < CHUNK - k, pltpu.roll(x, tm - k, 0), -jnp.inf)
        x = jnp.maximum(x, jnp.where(fwd_lane, from_before, from_after))
        k *= 2
    gc_ref[...] = jnp.where(is_f, a, x)

    lane_1 =lax.broadcasted_iota(jnp.int32, (1, LANES), 1)
    for c in range(tm // CHUNK):
        lo = c * CHUNK
        xc, ac = x[lo:lo + CHUNK], a[lo:lo + CHUNK]
        end_max = jnp.where(lane_1 < 2 * N_HEADS, xc[CHUNK - 1:CHUNK], xc[0:1])
        e = jnp.exp(ac - end_max)
        rows = jnp.where(((lane_c >> 2) & 1) == 1, pltpu.roll(e, N_HEADS, 1), ac)
        gr_ref[:, lo:lo + CHUNK] = rows.T[:N_GATE_COLS, :]


def _gates(gpre, b_if, tm):
    t = gpre.shape[0]
    return pl.pallas_call(
        _gates_kernel,
        grid=(t // tm,),
        in_specs=[pl.BlockSpec((tm, LANES), lambda i: (i, 0)),
                  pl.BlockSpec((1, LANES), lambda i: (0, 0))],
        out_specs=[pl.BlockSpec((tm, LANES), lambda i: (i, 0)),
                   pl.BlockSpec((N_GATE_COLS, tm), lambda i: (0, i))],
        out_shape=[jax.ShapeDtypeStruct((t, LANES), F32),
                   jax.ShapeDtypeStruct((N_GATE_COLS, t), F32)],
        compiler_params=_cparams(("arbitrary",)),
        name="gates",
    )(gpre, b_if)


DVX = DV + LANES
MLSTM_CHUNKS_PER_STEP = 2


def _mlstm_chunk(q, k, v_ext, rmax_col, b_col, r_row, e_row, b_last, rmax_last, mask, cx, m_st):
    scale = DK ** -0.5
    mb = jnp.maximum(m_st, jnp.broadcast_to(rmax_col, (CHUNK, CHUNK)))
    w_intra = jnp.exp(jnp.where(mask, r_row - mb, -jnp.inf))
    w_state = jnp.exp(m_st - mb)
    qk = lax.dot_general(q, k, (((1,), (1,)), ((), ())), preferred_element_type=F32)
    s = qk * (w_intra * scale)
    lhs = jnp.concatenate([s.astype(BF16), (q.astype(F32) * (w_state * scale)).astype(BF16)], axis=1)
    rhs = jnp.concatenate([v_ext, cx.astype(BF16)], axis=0)
    nx = jnp.dot(lhs, rhs, preferred_element_type=F32)
    denom = jnp.maximum(jnp.abs(nx[:, DV:]), jnp.exp(-(jnp.broadcast_to(b_col, (CHUNK, CHUNK)) + mb)))
    h = nx[:, :DV] / jnp.concatenate([denom, denom], axis=1)
    ke_t = (k.T.astype(F32) * e_row).astype(BF16)
    c_loc = jnp.dot(ke_t, v_ext, preferred_element_type=F32)
    m_loc = b_last + rmax_last
    m_new = jnp.maximum(b_last + m_st, m_loc)
    return h, jnp.exp(b_last + m_st - m_new) * cx + jnp.exp(m_loc - m_new) * c_loc, m_new


def _mlstm_kernel(qf_ref, kf_ref, vf_ref, gcf_ref, grf_ref, qb_ref, kb_ref, vb_ref, gcb_ref, grb_ref,
                  c0_ref, m0_ref, hf_ref, hb_ref, cout_ref, mout_ref, m_scr, *c_scrs):
    c = pl.program_id(1)

    @pl.when(c == 0)
    def _():
        for idx, c_scr in enumerate(c_scrs):
            c_scr[...] = c0_ref[idx]
        m_scr[...] = m0_ref[...]

    row = lax.broadcasted_iota(jnp.int32, (CHUNK, CHUNK), 0)
    col = lax.broadcasted_iota(jnp.int32, (CHUNK, CHUNK), 1)
    ones = jnp.ones((CHUNK, LANES), BF16)
    m_all = m_scr[...]
    dirs = ((qf_ref, kf_ref, vf_ref, gcf_ref, grf_ref, hf_ref, 0, CHUNK - 1, col <= row),
            (qb_ref, kb_ref, vb_ref, gcb_ref, grb_ref, hb_ref, 2 * N_HEADS, 0, col >= row))
    n_sub = qf_ref.shape[0] // CHUNK
    m_news = []
    for di, (q_ref, k_ref, v_ref, gc_ref, gr_ref, h_ref, off, last, mask) in enumerate(dirs):
        order = range(n_sub) if di == 0 else range(n_sub - 1, -1, -1)
        for hd in range(N_HEADS):
            idx = di * N_HEADS + hd
            lr, lb = off + hd, off + N_HEADS + hd
            cx, m_st = c_scrs[idx][...], m_all[idx][:, 0:1]
            for sub in order:
                r0 = sub * CHUNK
                rows = slice(r0, r0 + CHUNK)
                v_ext = jnp.concatenate([v_ref[rows, hd * DV:(hd + 1) * DV], ones], axis=1)
                h, cx, m_st = _mlstm_chunk(
                    q_ref[rows, hd * DK:(hd + 1) * DK], k_ref[rows, hd * DK:(hd + 1) * DK], v_ext,
                    gc_ref[rows, lr:lr + 1], gc_ref[rows, lb:lb + 1],
                    gr_ref[lr:lr + 1, rows], gr_ref[lb:lb + 1, rows],
                    gc_ref[r0 + last:r0 + last + 1, lb:lb + 1], gc_ref[r0 + last:r0 + last + 1, lr:lr + 1],
                    mask, cx, m_st)
                h_ref[rows, hd * DV:(hd + 1) * DV] = h
            c_scrs[idx][...] = cx
            m_news.append(jnp.broadcast_to(m_st, (1, LANES)))
    for idx, m_new in enumerate(m_news):
        m_scr[idx] = m_new

    @pl.when(c == pl.num_programs(1) - 1)
    def _():
        for idx, c_scr in enumerate(c_scrs):
            cout_ref[idx] = c_scr[...]
        mout_ref[...] = m_scr[...]


def _mlstm(proj, gcol, grow, bsz, s, c0, m0):
    rows = MLSTM_CHUNKS_PER_STEP * CHUNK
    nc = s // rows
    t = bsz * s
    fwd = lambda b, c: b * nc + c
    bwd = lambda b, c: b * nc + (nc - 1 - c)

    def specs(ci):
        return [pl.BlockSpec((rows, QK_COLS), lambda b, c: (ci(b, c), 0)),
                pl.BlockSpec((rows, QK_COLS), lambda b, c: (ci(b, c), 1)),
                pl.BlockSpec((rows, MLSTM_WIDTH), lambda b, c: (ci(b, c), 1)),
                pl.BlockSpec((rows, LANES), lambda b, c: (ci(b, c), 0)),
                pl.BlockSpec((N_GATE_COLS, rows), lambda b, c: (0, ci(b, c)))]

    st_specs = [pl.BlockSpec((None, 2 * N_HEADS, DK, DVX), lambda b, c: (b, 0, 0, 0)),
                pl.BlockSpec((None, 2 * N_HEADS, 1, LANES), lambda b, c: (b, 0, 0, 0))]
    return pl.pallas_call(
        _mlstm_kernel,
        grid=(bsz, nc),
        in_specs=specs(fwd) + specs(bwd) + st_specs,
        out_specs=[pl.BlockSpec((rows, MLSTM_WIDTH), lambda b, c: (fwd(b, c), 0)),
                   pl.BlockSpec((rows, MLSTM_WIDTH), lambda b, c: (bwd(b, c), 0))] + st_specs,
        out_shape=[jax.ShapeDtypeStruct((t, MLSTM_WIDTH), F32),
                   jax.ShapeDtypeStruct((t, MLSTM_WIDTH), F32),
                   jax.ShapeDtypeStruct(c0.shape, F32),
                   jax.ShapeDtypeStruct(m0.shape, F32)],
        scratch_shapes=[pltpu.VMEM((2 * N_HEADS, 1, LANES), F32)]
        + [pltpu.VMEM((DK, DVX), F32) for _ in range(2 * N_HEADS)],
        compiler_params=_cparams(("arbitrary", "arbitrary")),
        name="mlstm",
    )(proj, proj, proj, gcol, grow, proj, proj, proj, gcol, grow, c0, m0)


MIX_ROWS = 256


def _mixout_kernel(o_ref, cb_ref, cc_ref, cx_ref, ccp_ref, cxp_ref, ccn_ref, cxn_ref, hf_ref, hb_ref, x_ref,
                   cw_ref, ng_ref, wout_ref, gt_ref, gffn_ref, shf_ref, scf_ref, wr_ref, br_ref,
                   x1_ref, xn2_ref, idx_ref, tw_ref):
    i = pl.program_id(1)
    tm = x_ref.shape[0]
    cw = cw_ref[...]

    has_prev = jnp.where(i > 0, 1.0, 0.0)
    has_next = jnp.where(i < pl.num_programs(1) - 1, 1.0, 0.0)
    up = ccp_ref[...].astype(F32) * cxp_ref[...].astype(F32) * has_prev
    un = ccn_ref[...].astype(F32) * cxn_ref[...].astype(F32) * has_next
    uv = cc_ref[:, CONV_HALF:].astype(F32) * cx_ref[:, CONV_HALF:].astype(F32)
    ext = jnp.concatenate([up, uv, un], axis=0)

    pos = lax.broadcasted_iota(jnp.int32, (MIX_ROWS, CONV_HALF), 0) & (GRID_W - 1)
    lane_f = lax.broadcasted_iota(jnp.int32, (MIX_ROWS, LANES), 1).astype(F32)
    lane4 = lax.broadcasted_iota(jnp.int32, (MIX_ROWS, TOP_K), 1)

    for r0 in range(0, tm, MIX_ROWS):
        rows = slice(r0, r0 + MIX_ROWS)

        uh = cc_ref[rows, :CONV_HALF].astype(F32) * cx_ref[rows, :CONV_HALF].astype(F32)
        left = jnp.where(pos == 0, 0.0, pltpu.roll(uh, 1, 0))
        right = jnp.where(pos == GRID_W - 1, 0.0, pltpu.roll(uh, MIX_ROWS - 1, 0))
        yh = cw[0:1, :CONV_HALF] * left + cw[1:2, :CONV_HALF] * uh + cw[2:3, :CONV_HALF] * right
        yv = (cw[0:1, CONV_HALF:] * ext[r0:r0 + MIX_ROWS]
              + cw[1:2, CONV_HALF:] * ext[r0 + GRID_W:r0 + GRID_W + MIX_ROWS]
              + cw[2:3, CONV_HALF:] * ext[r0 + 2 * GRID_W:r0 + 2 * GRID_W + MIX_ROWS])
        yc = cb_ref[rows, :].astype(F32) * jnp.concatenate([yh, yv], axis=1)

        hs = hf_ref[rows, :] + hb_ref[rows, :]
        parts = []
        for hd in range(N_HEADS):
            seg = hs[:, hd * DV:(hd + 1) * DV]
            parts.append(seg * lax.rsqrt(jnp.mean(seg * seg, axis=-1, keepdims=True) + EPS))
        hm = jnp.concatenate(parts, axis=1) * ng_ref[...] * jax.nn.sigmoid(o_ref[rows, :].astype(F32))

        z = jnp.concatenate([hm.astype(BF16), yc.astype(BF16)], axis=1)
        x1 = x_ref[rows, :] + gt_ref[...] * jnp.dot(z, wout_ref[...], preferred_element_type=F32)
        x1_ref[rows, :] = x1

        y = x1 * lax.rsqrt(jnp.mean(x1 * x1, axis=-1, keepdims=True) + EPS) * gffn_ref[...]
        xn2 = y * (1.0 + scf_ref[...]) + shf_ref[...]
        xn2_ref[rows, :] = _pack_pairs(xn2)

        logits = jnp.dot(xn2.astype(BF16), wr_ref[...], preferred_element_type=F32) + br_ref[...]
        vals, idxs = [], []
        for _ in range(TOP_K):
            mx = jnp.max(logits, axis=-1, keepdims=True)
            ik = jnp.min(jnp.where(logits == mx, lane_f, float(LANES)), axis=-1, keepdims=True)
            vals.append(mx)
            idxs.append(ik)
            logits = jnp.where(lane_f == ik, -jnp.inf, logits)
        es = [jnp.exp(v - vals[0]) for v in vals]
        tot = es[0] + es[1] + es[2] + es[3]
        idx_out = jnp.zeros((MIX_ROWS, TOP_K), F32)
        tw_out = jnp.zeros((MIX_ROWS, TOP_K), F32)
        for kk in range(TOP_K):
            idx_out = jnp.where(lane4 == kk, idxs[kk], idx_out)
            tw_out = jnp.where(lane4 == kk, es[kk] / tot, tw_out)
        idx_ref[rows, :] = idx_out.astype(jnp.int32)
        tw_ref[rows, :] = tw_out


def _mixout(proj, hf, hb, x2, conv_w, norm_g, w_out, gt, g_ffn, sh_f, sc_f, w_r, b_r, bsz, s, tm):
    t, d = x2.shape
    nt = s // tm
    rb = tm // GRID_W
    last_rb = t // GRID_W - 1
    row = lambda b, i: b * nt + i
    w = MLSTM_WIDTH
    vec = lambda n: pl.BlockSpec((1, n), lambda b, i: (0, 0))
    per_b = pl.BlockSpec((None, 1, d), lambda b, i: (b, 0, 0))
    halo_prev = lambda cblk: pl.BlockSpec(
        (GRID_W, CONV_HALF), lambda b, i: (jnp.maximum(row(b, i) * rb - 1, 0), cblk))
    halo_next = lambda cblk: pl.BlockSpec(
        (GRID_W, CONV_HALF), lambda b, i: (jnp.minimum((row(b, i) + 1) * rb, last_rb), cblk))
    return pl.pallas_call(
        _mixout_kernel,
        grid=(bsz, nt),
        in_specs=[pl.BlockSpec((tm, w), lambda b, i: (row(b, i), 2)),
                  pl.BlockSpec((tm, w), lambda b, i: (row(b, i), 3)),
                  pl.BlockSpec((tm, w), lambda b, i: (row(b, i), 4)),
                  pl.BlockSpec((tm, w), lambda b, i: (row(b, i), 5)),
                  halo_prev(9), halo_prev(11), halo_next(9), halo_next(11),
                  pl.BlockSpec((tm, w), lambda b, i: (row(b, i), 0)),
                  pl.BlockSpec((tm, w), lambda b, i: (row(b, i), 0)),
                  pl.BlockSpec((tm, d), lambda b, i: (row(b, i), 0)),
                  pl.BlockSpec((3, CONV_WIDTH), lambda b, i: (0, 0)),
                  vec(w),
                  pl.BlockSpec((d, d), lambda b, i: (0, 0)),
                  per_b, vec(d), per_b, per_b,
                  pl.BlockSpec((d, LANES), lambda b, i: (0, 0)),
                  vec(LANES)],
        out_specs=[pl.BlockSpec((tm, d), lambda b, i: (row(b, i), 0)),
                   pl.BlockSpec((tm, d // 2), lambda b, i: (row(b, i), 0)),
                   pl.BlockSpec((tm, TOP_K), lambda b, i: (row(b, i), 0)),
                   pl.BlockSpec((tm, TOP_K), lambda b, i: (row(b, i), 0))],
        out_shape=[jax.ShapeDtypeStruct((t, d), F32),
                   jax.ShapeDtypeStruct((t, d // 2), jnp.uint32),
                   jax.ShapeDtypeStruct((t, TOP_K), jnp.int32),
                   jax.ShapeDtypeStruct((t, TOP_K), F32)],
        compiler_params=_cparams(("arbitrary", "arbitrary")),
        name="mixout",
    )(proj, proj, proj, proj, proj, proj, proj, proj, hf, hb, x2,
      conv_w, norm_g, w_out, gt, g_ffn, sh_f, sc_f, w_r, b_r)


def _rank_kernel(idx_ref, rank_ref, cnt_ref, run_scr):
    @pl.when(pl.program_id(0) == 0)
    def _():
        run_scr[...] = jnp.zeros_like(run_scr)

    tm = idx_ref.shape[0]
    idx = idx_ref[...]
    lane = lax.broadcasted_iota(jnp.int32, (tm, LANES), 1)
    hits = [lane == idx[:, kk:kk + 1] for kk in range(TOP_K)]
    onehot = jnp.zeros((tm, LANES), F32)
    for hit in hits:
        onehot = onehot + hit.astype(F32)
    r = lax.broadcasted_iota(jnp.int32, (tm, tm), 0)
    c = lax.broadcasted_iota(jnp.int32, (tm, tm), 1)
    before = jnp.dot((c < r).astype(BF16), onehot.astype(BF16), preferred_element_type=F32) + run_scr[...]
    lane4 = lax.broadcasted_iota(jnp.int32, (tm, TOP_K), 1)
    rank = jnp.zeros((tm, TOP_K), F32)
    for kk, hit in enumerate(hits):
        rk = jnp.sum(jnp.where(hit, before, 0.0), axis=-1, keepdims=True)
        rank = jnp.where(lane4 == kk, rk, rank)
    rank_ref[...] = rank.astype(jnp.int32)
    run_scr[...] = run_scr[...] + jnp.sum(onehot, axis=0, keepdims=True)
    cnt_ref[...] = run_scr[...]


def _rank(idx, tm):
    t = idx.shape[0]
    return pl.pallas_call(
        _rank_kernel,
        grid=(t // tm,),
        in_specs=[pl.BlockSpec((tm, TOP_K), lambda i: (i, 0))],
        out_specs=[pl.BlockSpec((tm, TOP_K), lambda i: (i, 0)),
                   pl.BlockSpec((1, LANES), lambda i: (0, 0))],
        out_shape=[jax.ShapeDtypeStruct((t, TOP_K), jnp.int32),
                   jax.ShapeDtypeStruct((1, LANES), F32)],
        scratch_shapes=[pltpu.VMEM((1, LANES), F32)],
        compiler_params=_cparams(("arbitrary",)),
        name="rank",
    )(idx)


def _largest_pad_piece():
    return 1 << ((ROW_BLOCK - 1).bit_length() - 1)


def _dispatch_kernel(pad_ref, dest_hbm, xn_ref, xs_hbm, dsm, zeros_scr, sem_idx, sem_rows, sem_pad):
    i = pl.program_id(0)
    tm = xn_ref.shape[0]
    n_idx = tm * TOP_K
    idx_copy = pltpu.make_async_copy(dest_hbm.at[pl.ds(i * n_idx, n_idx)], dsm, sem_idx)
    idx_copy.start()

    def pad_copy(off, size):
        return pltpu.make_async_copy(zeros_scr.at[pl.ds(0, size), :], xs_hbm.at[pl.ds(off, size), :], sem_pad)

    def for_each_pad_piece(fn):
        def per_expert(e, carry):
            off = pad_ref[2 * e]
            n = pad_ref[2 * e + 1]
            head = n & (SUBLANES - 1)
            for r in range(SUBLANES - 1):
                @pl.when(r < head)
                def _(r=r):
                    fn(pad_copy(off + r, 1))

            off = off + head
            size = _largest_pad_piece()
            while size >= SUBLANES:
                take = (n & size) != 0

                @pl.when(take)
                def _(off=off, size=size):
                    fn(pad_copy(pl.multiple_of(off, SUBLANES), size))

                off = off + jnp.where(take, size, 0)
                size //= 2
            return carry
        lax.fori_loop(0, N_EXPERTS, per_expert, 0)

    @pl.when(i == 0)
    def _():
        zeros_scr[...] = jnp.zeros_like(zeros_scr)
        for_each_pad_piece(lambda cp: cp.start())
        for_each_pad_piece(lambda cp: cp.wait())

    idx_copy.wait()

    def row_copy(t, kk):
        return pltpu.make_async_copy(xn_ref.at[pl.ds(t, 1), :],
                                     xs_hbm.at[pl.ds(dsm[t * TOP_K + kk], 1), :], sem_rows)

    def issue(t, carry):
        for kk in range(TOP_K):
            row_copy(t, kk).start()
        return carry

    lax.fori_loop(0, tm, issue, 0, unroll=ISSUE_UNROLL)
    pltpu.make_async_copy(xs_hbm.at[pl.ds(0, n_idx), :], xs_hbm.at[pl.ds(0, n_idx), :], sem_rows).wait()


def _dispatch(pad_info, dest_flat, xn2, n_rows, tm):
    t, d = xn2.shape
    return pl.pallas_call(
        _dispatch_kernel,
        grid_spec=pltpu.PrefetchScalarGridSpec(
            num_scalar_prefetch=1,
            grid=(t // tm,),
            in_specs=[pl.BlockSpec(memory_space=pl.ANY),
                      pl.BlockSpec((tm, d), lambda i, pad: (i, 0))],
            out_specs=pl.BlockSpec(memory_space=pl.ANY),
            scratch_shapes=[pltpu.SMEM((tm * TOP_K,), jnp.int32),
                            pltpu.VMEM((_largest_pad_piece(), d), xn2.dtype),
                            pltpu.SemaphoreType.DMA(()),
                            pltpu.SemaphoreType.DMA(()),
                            pltpu.SemaphoreType.DMA(())]),
        out_shape=jax.ShapeDtypeStruct((n_rows, d), xn2.dtype),
        compiler_params=_cparams(("arbitrary",)),
        name="dispatch",
    )(pad_info, dest_flat, xn2)


def _new_expert(be_ref, j):
    return jnp.logical_or(j == 0, be_ref[j] != be_ref[jnp.maximum(j - 1, 0)])


GU_COLS = 512


def _expert_gu_kernel(be_ref, nu_ref, nxt_ref, xs_ref, w_hbm, bg_ref, bu_ref, act_ref,
                      stage_g, stage_u, wg_scr, wu_scr, sem):
    n = pl.program_id(0)
    j = pl.program_id(1)
    nt = pl.num_programs(0)
    tn = wg_scr.shape[1]

    def weight_copies(e, nn):
        col_g = pl.multiple_of(nn * tn, tn)
        col_u = pl.multiple_of((nt + nn) * tn, tn)
        return (pltpu.make_async_copy(w_hbm.at[e, :, pl.ds(col_g, tn)], stage_g, sem.at[0]),
                pltpu.make_async_copy(w_hbm.at[e, :, pl.ds(col_u, tn)], stage_u, sem.at[1]))

    @pl.when(j < nu_ref[0])
    def _():
        e = be_ref[j]

        @pl.when(_new_expert(be_ref, j))
        def _():
            @pl.when(jnp.logical_and(n == 0, j == 0))
            def _():
                for cp in weight_copies(e, n):
                    cp.start()

            for cp in weight_copies(e, n):
                cp.wait()
            wg_scr[...] = stage_g[...].astype(BF16)
            wu_scr[...] = stage_u[...].astype(BF16)

            e_next = nxt_ref[e]
            in_pass = e_next >= 0

            @pl.when(jnp.logical_or(in_pass, n + 1 < nt))
            def _():
                for cp in weight_copies(jnp.where(in_pass, e_next, be_ref[0]), jnp.where(in_pass, n, n + 1)):
                    cp.start()

        x = _unpack_pairs(xs_ref[...], xs_ref.shape[1]).astype(BF16)
        for c0 in range(0, tn, GU_COLS):
            cols = slice(c0, c0 + GU_COLS)
            g = jnp.dot(x, wg_scr[:, cols], preferred_element_type=F32) + bg_ref[:, cols]
            u = jnp.dot(x, wu_scr[:, cols], preferred_element_type=F32) + bu_ref[:, cols]
            gate = jnp.minimum(g, SWIGLU_LIMIT)
            up = jnp.clip(u, -SWIGLU_LIMIT, SWIGLU_LIMIT)
            act_ref[:, cols] = ((up + 1.0) * gate * jax.nn.sigmoid(SWIGLU_ALPHA * gate)).astype(BF16)


def _expert_gu(blk_expert, n_used, nxt_expert, xs, w_gu, b_gu, tn):
    n_rows, dp = xs.shape
    d = w_gu.shape[1]
    dff = w_gu.shape[2] // 2
    nt = dff // tn
    nb = n_rows // ROW_BLOCK
    blk = lambda j, nu: jnp.minimum(j, nu[0] - 1)
    exp = lambda j, be, nu: be[blk(j, nu)]
    return pl.pallas_call(
        _expert_gu_kernel,
        grid_spec=pltpu.PrefetchScalarGridSpec(
            num_scalar_prefetch=3,
            grid=(nt, nb),
            in_specs=[pl.BlockSpec((ROW_BLOCK, dp), lambda n, j, be, nu, nx: (blk(j, nu), 0)),
                      pl.BlockSpec(memory_space=pl.ANY),
                      pl.BlockSpec((None, 1, tn), lambda n, j, be, nu, nx: (exp(j, be, nu), 0, n)),
                      pl.BlockSpec((None, 1, tn), lambda n, j, be, nu, nx: (exp(j, be, nu), 0, nt + n))],
            out_specs=pl.BlockSpec((ROW_BLOCK, tn), lambda n, j, be, nu, nx: (blk(j, nu), n)),
            scratch_shapes=[pltpu.VMEM((d, tn), F32), pltpu.VMEM((d, tn), F32),
                            pltpu.VMEM((d, tn), BF16), pltpu.VMEM((d, tn), BF16),
                            pltpu.SemaphoreType.DMA((2,))]),
        out_shape=jax.ShapeDtypeStruct((n_rows, dff), BF16),
        compiler_params=_cparams(("arbitrary", "arbitrary")),
        name="expert_gu",
    )(blk_expert, n_used, nxt_expert, xs, w_gu, b_gu, b_gu)


def _expert_down_kernel(be_ref, nu_ref, nxt_ref, act_ref, w_hbm, b_ref, y_ref, stage, w_scr, sem):
    j = pl.program_id(0)

    def weight_copy(e):
        return pltpu.make_async_copy(w_hbm.at[e], stage, sem)

    @pl.when(j < nu_ref[0])
    def _():
        e = be_ref[j]

        @pl.when(_new_expert(be_ref, j))
        def _():
            @pl.when(j == 0)
            def _():
                weight_copy(e).start()

            weight_copy(e).wait()
            w_scr[...] = stage[...].astype(BF16)
            e_next = nxt_ref[e]

            @pl.when(e_next >= 0)
            def _():
                weight_copy(e_next).start()

        act = act_ref[...]
        for c0 in range(0, w_scr.shape[1], DOWN_COLS):
            cols = slice(c0, c0 + DOWN_COLS)
            y = jnp.dot(act, w_scr[:, cols], preferred_element_type=F32) + b_ref[:, cols]
            y_ref[:, c0 // 2:(c0 + DOWN_COLS) // 2] = _pack_pairs(y)


def _expert_down(blk_expert, n_used, nxt_expert, act, w_down, b_down):
    n_rows, dff = act.shape
    d = w_down.shape[2]
    nb = n_rows // ROW_BLOCK
    blk = lambda j, nu: jnp.minimum(j, nu[0] - 1)
    exp = lambda j, be, nu: be[blk(j, nu)]
    return pl.pallas_call(
        _expert_down_kernel,
        grid_spec=pltpu.PrefetchScalarGridSpec(
            num_scalar_prefetch=3,
            grid=(nb,),
            in_specs=[pl.BlockSpec((ROW_BLOCK, dff), lambda j, be, nu, nx: (blk(j, nu), 0)),
                      pl.BlockSpec(memory_space=pl.ANY),
                      pl.BlockSpec((None, 1, d), lambda j, be, nu, nx: (exp(j, be, nu), 0, 0))],
            out_specs=pl.BlockSpec((ROW_BLOCK, d // 2), lambda j, be, nu, nx: (blk(j, nu), 0)),
            scratch_shapes=[pltpu.VMEM((dff, d), F32), pltpu.VMEM((dff, d), BF16),
                            pltpu.SemaphoreType.DMA(())]),
        out_shape=jax.ShapeDtypeStruct((n_rows, d // 2), jnp.uint32),
        compiler_params=_cparams(("arbitrary",)),
        name="expert_down",
    )(blk_expert, n_used, nxt_expert, act, w_down, b_down)


def _combine_kernel(dest_hbm, y_hbm, x1_ref, tw_ref, gt_ref, gfin_ref, out_ref, dsm, buf, sem_idx, sem_rows):
    i = pl.program_id(1) + pl.program_id(0) * pl.num_programs(1)
    tm = x1_ref.shape[0]
    n_idx = tm * TOP_K
    idx_copy = pltpu.make_async_copy(dest_hbm.at[pl.ds(i * n_idx, n_idx)], dsm, sem_idx)
    idx_copy.start()
    idx_copy.wait()

    def row_copy(t, kk):
        return pltpu.make_async_copy(y_hbm.at[pl.ds(dsm[t * TOP_K + kk], 1), :],
                                     buf.at[kk, pl.ds(t, 1), :], sem_rows)

    def issue(t, carry):
        for kk in range(TOP_K):
            row_copy(t, kk).start()
        return carry

    lax.fori_loop(0, tm, issue, 0, unroll=ISSUE_UNROLL)
    for kk in range(TOP_K):
        pltpu.make_async_copy(y_hbm.at[pl.ds(0, tm), :], buf.at[kk], sem_rows).wait()

    tw = tw_ref[...]
    acc = _unpack_pairs(buf[0], DOWN_COLS // 2) * tw[:, 0:1]
    for kk in range(1, TOP_K):
        acc = acc + _unpack_pairs(buf[kk], DOWN_COLS // 2) * tw[:, kk:kk + 1]
    x2 = x1_ref[...] + gt_ref[...] * acc
    out_ref[...] = x2 * lax.rsqrt(jnp.mean(x2 * x2, axis=-1, keepdims=True) + EPS) * gfin_ref[...]


def _combine(dest_flat, y, x1, tw, gt, g_final, bsz, s, tm):
    t, d = x1.shape
    nt = s // tm
    row = lambda b, i: (b * nt + i, 0)
    return pl.pallas_call(
        _combine_kernel,
        grid=(bsz, nt),
        in_specs=[pl.BlockSpec(memory_space=pl.ANY),
                  pl.BlockSpec(memory_space=pl.ANY),
                  pl.BlockSpec((tm, d), row),
                  pl.BlockSpec((tm, TOP_K), row),
                  pl.BlockSpec((None, 1, d), lambda b, i: (b, 0, 0)),
                  pl.BlockSpec((1, d), lambda b, i: (0, 0))],
        out_specs=pl.BlockSpec((tm, d), row),
        out_shape=jax.ShapeDtypeStruct((t, d), F32),
        scratch_shapes=[pltpu.SMEM((tm * TOP_K,), jnp.int32),
                        pltpu.VMEM((TOP_K, tm, y.shape[1]), y.dtype),
                        pltpu.SemaphoreType.DMA(()),
                        pltpu.SemaphoreType.DMA(())],
        compiler_params=_cparams(("arbitrary", "arbitrary")),
        name="combine",
    )(dest_flat, y, x1, tw, gt, g_final)


def _pad_lanes(a, value=0.0):
    return jnp.pad(a, ((0, 0), (0, LANES - a.shape[1])), constant_values=value)


def _routing_tables(idx, rank, counts_f, n_blocks):
    counts = counts_f[0, :N_EXPERTS].astype(jnp.int32)
    padded = (counts + ROW_BLOCK - 1) // ROW_BLOCK * ROW_BLOCK
    pend = jnp.cumsum(padded)
    pstart = pend - padded
    dest = (pstart[idx] + rank).reshape(-1)
    blk_start = jnp.arange(n_blocks, dtype=jnp.int32) * ROW_BLOCK
    blk_expert = jnp.minimum(jnp.sum((pend[None, :] <= blk_start[:, None]).astype(jnp.int32), axis=1),
                             N_EXPERTS - 1)
    n_used = (pend[-1:] // ROW_BLOCK).astype(jnp.int32)
    pad_info = jnp.stack([pstart + counts, padded - counts], axis=1).reshape(-1).astype(jnp.int32)
    ids = jnp.arange(N_EXPERTS, dtype=jnp.int32)
    later = jnp.where((ids[None, :] > ids[:, None]) & (counts[None, :] > 0), ids[None, :], N_EXPERTS)
    nxt = jnp.min(later, axis=1)
    nxt_expert = jnp.where(nxt == N_EXPERTS, -1, nxt).astype(jnp.int32)
    return dest, blk_expert, n_used, nxt_expert, pad_info


def _layer(x, c, ctx, c_ctx, w_ada, b_ada, g_mix, w_in, b_if, conv_w, norm_g, w_out,
           g_ffn, w_router, b_router, w_gu, b_gu, w_down, b_down, g_final):
    bsz, s, d = x.shape
    s_ctx = ctx.shape[1]

    cond = jnp.zeros((8, d), F32).at[:bsz].set(c).at[bsz].set(c_ctx)
    mod = _adaln(cond, w_ada, b_ada[None, :])
    sh_m, sc_m, gt_m, sh_f, sc_f, gt_f = [m[:, None, :] for m in jnp.split(mod, N_MOD, axis=-1)]
    lat = lambda m: m[:bsz]
    ctxm = lambda m: jnp.broadcast_to(m[bsz:bsz + 1], (bsz, 1, d))

    g0 = 2 * QK_COLS + 2 * MLSTM_WIDTH
    w_main = jnp.concatenate([w_in[:, :g0], w_in[:, g0 + N_GATE_COLS:]], axis=1).astype(BF16)
    w_gate = _pad_lanes(w_in[:, g0:g0 + N_GATE_COLS]).astype(BF16)
    b_gate = _pad_lanes(b_if[None, :])
    g_mix2 = g_mix[None, :]

    proj_c, gpre_c = _inproj(ctx, g_mix2, ctxm(sh_m), ctxm(sc_m), w_main, w_gate, min(s_ctx, 512))
    gcol_c, grow_c = _gates(gpre_c, b_gate, 512)
    zeros_state = (jnp.zeros((bsz, 2 * N_HEADS, DK, DVX), F32),
                   jnp.zeros((bsz, 2 * N_HEADS, 1, LANES), F32))
    _, _, c0, m0 = _mlstm(proj_c, gcol_c, grow_c, bsz, s_ctx, *zeros_state)

    proj, gpre = _inproj(x, g_mix2, lat(sh_m), lat(sc_m), w_main, w_gate, 512)
    gcol, grow = _gates(gpre, b_gate, 512)
    hf, hb, _, _ = _mlstm(proj, gcol, grow, bsz, s, c0, m0)
    x1, xn2, idx, tw = _mixout(
        proj, hf, hb, x.reshape(bsz * s, d), conv_w, norm_g[None, :], w_out.astype(BF16), lat(gt_m),
        g_ffn[None, :], lat(sh_f), lat(sc_f), _pad_lanes(w_router).astype(BF16),
        _pad_lanes(b_router[None, :], NEG_BIG), bsz, s, 512)

    t = bsz * s
    n_blocks = -(-(t * TOP_K) // ROW_BLOCK) + N_EXPERTS
    rank, counts = _rank(idx, 512)
    dest, blk_expert, n_used, nxt_expert, pad_info = _routing_tables(idx, rank, counts, n_blocks)
    xs = _dispatch(pad_info, dest, xn2, n_blocks * ROW_BLOCK, 512)
    act = _expert_gu(blk_expert, n_used, nxt_expert, xs, w_gu, b_gu[:, None, :], 1024)
    y = _expert_down(blk_expert, n_used, nxt_expert, act, w_down, b_down[:, None, :])
    out = _combine(dest, y, x1, tw, lat(gt_f), g_final[None, :], bsz, s, 256)
    return out.reshape(bsz, s, d)


def kernel(x, c, ctx, c_ctx, w_ada, b_ada, g_mix, w_in, b_if, conv_w, mlstm_norm_g, w_out,
           g_ffn, w_router, b_router, w_gu, b_gu, w_down, b_down, g_final):
    return _layer(x, c, ctx, c_ctx, w_ada[0], b_ada[0], g_mix[0], w_in[0], b_if[0], conv_w[0],
                  mlstm_norm_g[0], w_out[0], g_ffn[0], w_router[0], b_router[0], w_gu[0], b_gu[0],
                  w_down[0], b_down[0], g_final)
```

```python
import functools

import jax
import jax.numpy as jnp
from jax import lax
from jax.experimental import pallas as pl
from jax.experimental.pallas import tpu as pltpu

F32 = jnp.float32
BF16 = jnp.bfloat16

N_HEADS = 4
DK = 128
DV = 256
QK_COLS = N_HEADS * DK
MLSTM_WIDTH = N_HEADS * DV
CONV_WIDTH = 1024
CONV_HALF = CONV_WIDTH // 2
N_GATE_COLS = 4 * N_HEADS
GRID_W = 64
CHUNK = 128
GATE_SOFT_CAP = 15.0
N_EXPERTS = 32
TOP_K = 4
SWIGLU_LIMIT = 7.0
SWIGLU_ALPHA = 1.702
N_MOD = 6
EPS = 1e-6
LANES = 128
SUBLANES = 8
ROW_BLOCK = 1024
DOWN_COLS = 1024
ISSUE_UNROLL = 8
NEG_BIG = -1e30
VMEM_LIMIT = 56 * 1024 * 1024


def _cparams(sem):
    return pltpu.CompilerParams(dimension_semantics=sem, vmem_limit_bytes=VMEM_LIMIT)


def _pack_pairs(x):
    bits = lax.bitcast_convert_type(x.astype(BF16).astype(F32), jnp.uint32)
    g = x.shape[1] // 2
    return bits[:, :g] | (bits[:, g:] >> 16)


def _unpack_pairs(p, group):
    hi = lax.bitcast_convert_type(p & jnp.uint32(0xFFFF0000), F32)
    lo = lax.bitcast_convert_type(p << 16, F32)
    parts = []
    for g0 in range(0, p.shape[1], group):
        parts += [hi[:, g0:g0 + group], lo[:, g0:g0 + group]]
    return jnp.concatenate(parts, axis=1)


SLAB = 8


def _store_slab_rows(ref, r0, packed):
    rows = packed.shape[0]
    for c in range(SLAB):
        ref[pl.ds(r0 * SLAB + c, rows, stride=SLAB), :] = packed[:, c * LANES:(c + 1) * LANES]


def _load_slab_rows(ref, rows):
    return jnp.concatenate([ref[pl.ds(c, rows, stride=SLAB), :] for c in range(SLAB)], axis=1)


def _adaln_kernel(c_ref, w_ref, b_ref, o_ref):
    s = c_ref[...]
    s = s * jax.nn.sigmoid(s)
    o_ref[...] = jnp.dot(s.astype(BF16), w_ref[...].astype(BF16),
                         preferred_element_type=F32) + b_ref[...]


def _adaln(cond, w, b):
    d, n = w.shape
    tn = 1024
    return pl.pallas_call(
        _adaln_kernel,
        grid=(n // tn,),
        in_specs=[pl.BlockSpec((8, d), lambda j: (0, 0)),
                  pl.BlockSpec((d, tn), lambda j: (0, j)),
                  pl.BlockSpec((1, tn), lambda j: (0, j))],
        out_specs=pl.BlockSpec((8, tn), lambda j: (0, j)),
        out_shape=jax.ShapeDtypeStruct((8, n), F32),
        compiler_params=_cparams(("arbitrary",)),
        name="adaln",
    )(cond, w, b)


INPROJ_COLS = 1024


def _inproj_kernel(x_ref, g_ref, sh_ref, sc_ref, w_ref, wg_ref, proj_ref, gate_ref):
    x = x_ref[...]
    y = x * lax.rsqrt(jnp.mean(x * x, axis=-1, keepdims=True) + EPS) * g_ref[...]
    xn = (y * (1.0 + sc_ref[...]) + sh_ref[...]).astype(BF16)
    gate_ref[...] = jnp.dot(xn, wg_ref[...], preferred_element_type=F32)
    for j in range(w_ref.shape[1] // INPROJ_COLS):
        cols = slice(j * INPROJ_COLS, (j + 1) * INPROJ_COLS)
        proj_ref[:, cols] = jnp.dot(xn, w_ref[:, cols], preferred_element_type=F32).astype(BF16)


def _inproj(x, g, sh, sc, w, wg, tm):
    bsz, s, d = x.shape
    p = w.shape[1]
    nt = s // tm
    x2 = x.reshape(bsz * s, d)
    resident = lambda shape: pl.BlockSpec(shape, lambda b, i: (0, 0), pipeline_mode=pl.Buffered(1))
    return pl.pallas_call(
        _inproj_kernel,
        grid=(bsz, nt),
        in_specs=[pl.BlockSpec((tm, d), lambda b, i: (b * nt + i, 0)),
                  pl.BlockSpec((1, d), lambda b, i: (0, 0)),
                  pl.BlockSpec((None, 1, d), lambda b, i: (b, 0, 0)),
                  pl.BlockSpec((None, 1, d), lambda b, i: (b, 0, 0)),
                  resident((d, p)),
                  resident((d, LANES))],
        out_specs=[pl.BlockSpec((tm, p), lambda b, i: (b * nt + i, 0)),
                   pl.BlockSpec((tm, LANES), lambda b, i: (b * nt + i, 0))],
        out_shape=[jax.ShapeDtypeStruct((bsz * s, p), BF16),
                   jax.ShapeDtypeStruct((bsz * s, LANES), F32)],
        compiler_params=_cparams(("arbitrary", "arbitrary")),
        name="inproj",
    )(x2, g, sh, sc, w, wg)


def _log_sigmoid(x):
    return jnp.minimum(x, 0.0) - jnp.log1p(jnp.exp(-jnp.abs(x)))


def _gates_kernel(g_ref, b_ref, gc_ref, gr_ref):
    tm = g_ref.shape[0]
    row = lax.broadcasted_iota(jnp.int32, (tm, LANES), 0)
    lane = lax.broadcasted_iota(jnp.int32, (tm, LANES), 1)
    gp = GATE_SOFT_CAP * jnp.tanh((g_ref[...] + b_ref[...]) / GATE_SOFT_CAP)
    is_f = ((lane >> 2) & 1) == 1
    fwd_lane = lane < 2 * N_HEADS
    lf = jnp.where(is_f, _log_sigmoid(gp), 0.0)
    r2 = lax.broadcasted_iota(jnp.int32, (CHUNK, CHUNK), 0)
    c2 = lax.broadcasted_iota(jnp.int32, (CHUNK, CHUNK), 1)
    lower = (r2 >= c2).astype(F32)
    upper = (r2 <= c2).astype(F32)
    lane_c = lax.broadcasted_iota(jnp.int32, (CHUNK, LANES), 1)
    cums = []
    for c in range(tm // CHUNK):
        lf_c = lf[c * CHUNK:(c + 1) * CHUNK]
        cf = jnp.dot(lower, lf_c, precision=lax.Precision.HIGHEST, preferred_element_type=F32)
        cb = jnp.dot(upper, lf_c, precision=lax.Precision.HIGHEST, preferred_element_type=F32)
        cums.append(jnp.where(lane_c < 2 * N_HEADS, cf, cb))
    cdir = jnp.concatenate(cums, axis=0)
    a = jnp.where(is_f, cdir, gp - pltpu.roll(cdir, LANES - N_HEADS, 1))

    pos = row % CHUNK
    x = a
    k = 1
    while k < CHUNK:
        from_before = jnp.where(pos >= k, pltpu.roll(x, k, 0), -jnp.inf)
        from_after = jnp.where(pos < CHUNK - k, pltpu.roll(x, tm - k, 0), -jnp.inf)
        x = jnp.maximum(x, jnp.where(fwd_lane, from_before, from_after))
        k *= 2
    gc_ref[...] = jnp.where(is_f, a, x)

    lane_1 =lax.broadcasted_iota(jnp.int32, (1, LANES), 1)
    for c in range(tm // CHUNK):
        lo = c * CHUNK
        xc, ac = x[lo:lo + CHUNK], a[lo:lo + CHUNK]
        end_max = jnp.where(lane_1 < 2 * N_HEADS, xc[CHUNK - 1:CHUNK], xc[0:1])
        e = jnp.exp(ac - end_max)
        rows = jnp.where(((lane_c >> 2) & 1) == 1, pltpu.roll(e, N_HEADS, 1), ac)
        gr_ref[:, lo:lo + CHUNK] = rows.T[:N_GATE_COLS, :]


def _gates(gpre, b_if, tm):
    t = gpre.shape[0]
    return pl.pallas_call(
        _gates_kernel,
        grid=(t // tm,),
        in_specs=[pl.BlockSpec((tm, LANES), lambda i: (i, 0)),
                  pl.BlockSpec((1, LANES), lambda i: (0, 0))],
        out_specs=[pl.BlockSpec((tm, LANES), lambda i: (i, 0)),
                   pl.BlockSpec((N_GATE_COLS, tm), lambda i: (0, i))],
        out_shape=[jax.ShapeDtypeStruct((t, LANES), F32),
                   jax.ShapeDtypeStruct((N_GATE_COLS, t), F32)],
        compiler_params=_cparams(("arbitrary",)),
        name="gates",
    )(gpre, b_if)


DVX = DV + LANES
MLSTM_CHUNKS_PER_STEP = 2


def _mlstm_chunk(q, k, v_ext, rmax_col, b_col, r_row, e_row, b_last, rmax_last, mask, cx, m_st):
    scale = DK ** -0.5
    mb = jnp.maximum(m_st, jnp.broadcast_to(rmax_col, (CHUNK, CHUNK)))
    w_intra = jnp.exp(jnp.where(mask, r_row - mb, -jnp.inf))
    w_state = jnp.exp(m_st - mb)
    qk = lax.dot_general(q, k, (((1,), (1,)), ((), ())), preferred_element_type=F32)
    s = qk * (w_intra * scale)
    lhs = jnp.concatenate([s.astype(BF16), (q.astype(F32) * (w_state * scale)).astype(BF16)], axis=1)
    rhs = jnp.concatenate([v_ext, cx.astype(BF16)], axis=0)
    nx = jnp.dot(lhs, rhs, preferred_element_type=F32)
    denom = jnp.maximum(jnp.abs(nx[:, DV:]), jnp.exp(-(jnp.broadcast_to(b_col, (CHUNK, CHUNK)) + mb)))
    h = nx[:, :DV] / jnp.concatenate([denom, denom], axis=1)
    ke_t = (k.T.astype(F32) * e_row).astype(BF16)
    c_loc = jnp.dot(ke_t, v_ext, preferred_element_type=F32)
    m_loc = b_last + rmax_last
    m_new = jnp.maximum(b_last + m_st, m_loc)
    return h, jnp.exp(b_last + m_st - m_new) * cx + jnp.exp(m_loc - m_new) * c_loc, m_new


def _mlstm_kernel(qf_ref, kf_ref, vf_ref, gcf_ref, grf_ref, qb_ref, kb_ref, vb_ref, gcb_ref, grb_ref,
                  c0_ref, m0_ref, hf_ref, hb_ref, cout_ref, mout_ref, m_scr, *c_scrs):
    c = pl.program_id(1)

    @pl.when(c == 0)
    def _():
        for idx, c_scr in enumerate(c_scrs):
            c_scr[...] = c0_ref[idx]
        m_scr[...] = m0_ref[...]

    row = lax.broadcasted_iota(jnp.int32, (CHUNK, CHUNK), 0)
    col = lax.broadcasted_iota(jnp.int32, (CHUNK, CHUNK), 1)
    ones = jnp.ones((CHUNK, LANES), BF16)
    m_all = m_scr[...]
    dirs = ((qf_ref, kf_ref, vf_ref, gcf_ref, grf_ref, hf_ref, 0, CHUNK - 1, col <= row),
            (qb_ref, kb_ref, vb_ref, gcb_ref, grb_ref, hb_ref, 2 * N_HEADS, 0, col >= row))
    n_sub = qf_ref.shape[0] // CHUNK
    m_news = []
    for di, (q_ref, k_ref, v_ref, gc_ref, gr_ref, h_ref, off, last, mask) in enumerate(dirs):
        order = range(n_sub) if di == 0 else range(n_sub - 1, -1, -1)
        for hd in range(N_HEADS):
            idx = di * N_HEADS + hd
            lr, lb = off + hd, off + N_HEADS + hd
            cx, m_st = c_scrs[idx][...], m_all[idx][:, 0:1]
            for sub in order:
                r0 = sub * CHUNK
                rows = slice(r0, r0 + CHUNK)
                v_ext = jnp.concatenate([v_ref[rows, hd * DV:(hd + 1) * DV], ones], axis=1)
                h, cx, m_st = _mlstm_chunk(
                    q_ref[rows, hd * DK:(hd + 1) * DK], k_ref[rows, hd * DK:(hd + 1) * DK], v_ext,
                    gc_ref[rows, lr:lr + 1], gc_ref[rows, lb:lb + 1],
                    gr_ref[lr:lr + 1, rows], gr_ref[lb:lb + 1, rows],
                    gc_ref[r0 + last:r0 + last + 1, lb:lb + 1], gc_ref[r0 + last:r0 + last + 1, lr:lr + 1],
                    mask, cx, m_st)
                h_ref[rows, hd * DV:(hd + 1) * DV] = h
            c_scrs[idx][...] = cx
            m_news.append(jnp.broadcast_to(m_st, (1, LANES)))
    for idx, m_new in enumerate(m_news):
        m_scr[idx] = m_new

    @pl.when(c == pl.num_programs(1) - 1)
    def _():
        for idx, c_scr in enumerate(c_scrs):
            cout_ref[idx] = c_scr[...]
        mout_ref[...] = m_scr[...]


def _mlstm(proj, gcol, grow, bsz, s, c0, m0):
    rows = MLSTM_CHUNKS_PER_STEP * CHUNK
    nc = s // rows
    t = bsz * s
    fwd = lambda b, c: b * nc + c
    bwd = lambda b, c: b * nc + (nc - 1 - c)

    def specs(ci):
        return [pl.BlockSpec((rows, QK_COLS), lambda b, c: (ci(b, c), 0)),
                pl.BlockSpec((rows, QK_COLS), lambda b, c: (ci(b, c), 1)),
                pl.BlockSpec((rows, MLSTM_WIDTH), lambda b, c: (ci(b, c), 1)),
                pl.BlockSpec((rows, LANES), lambda b, c: (ci(b, c), 0)),
                pl.BlockSpec((N_GATE_COLS, rows), lambda b, c: (0, ci(b, c)))]

    st_specs = [pl.BlockSpec((None, 2 * N_HEADS, DK, DVX), lambda b, c: (b, 0, 0, 0)),
                pl.BlockSpec((None, 2 * N_HEADS, 1, LANES), lambda b, c: (b, 0, 0, 0))]
    return pl.pallas_call(
        _mlstm_kernel,
        grid=(bsz, nc),
        in_specs=specs(fwd) + specs(bwd) + st_specs,
        out_specs=[pl.BlockSpec((rows, MLSTM_WIDTH), lambda b, c: (fwd(b, c), 0)),
                   pl.BlockSpec((rows, MLSTM_WIDTH), lambda b, c: (bwd(b, c), 0))] + st_specs,
        out_shape=[jax.ShapeDtypeStruct((t, MLSTM_WIDTH), F32),
                   jax.ShapeDtypeStruct((t, MLSTM_WIDTH), F32),
                   jax.ShapeDtypeStruct(c0.shape, F32),
                   jax.ShapeDtypeStruct(m0.shape, F32)],
        scratch_shapes=[pltpu.VMEM((2 * N_HEADS, 1, LANES), F32)]
        + [pltpu.VMEM((DK, DVX), F32) for _ in range(2 * N_HEADS)],
        compiler_params=_cparams(("arbitrary", "arbitrary")),
        name="mlstm",
    )(proj, proj, proj, gcol, grow, proj, proj, proj, gcol, grow, c0, m0)


MIX_ROWS = 256


def _mixout_kernel(o_ref, cb_ref, cc_ref, cx_ref, ccp_ref, cxp_ref, ccn_ref, cxn_ref, hf_ref, hb_ref, x_ref,
                   cw_ref, ng_ref, wout_ref, gt_ref, gffn_ref, shf_ref, scf_ref, wr_ref, br_ref,
                   x1_ref, xn2_ref, idx_ref, tw_ref):
    i = pl.program_id(1)
    tm = x_ref.shape[0]
    cw = cw_ref[...]

    has_prev = jnp.where(i > 0, 1.0, 0.0)
    has_next = jnp.where(i < pl.num_programs(1) - 1, 1.0, 0.0)
    up = ccp_ref[...].astype(F32) * cxp_ref[...].astype(F32) * has_prev
    un = ccn_ref[...].astype(F32) * cxn_ref[...].astype(F32) * has_next
    uv = cc_ref[:, CONV_HALF:].astype(F32) * cx_ref[:, CONV_HALF:].astype(F32)
    ext = jnp.concatenate([up, uv, un], axis=0)

    pos = lax.broadcasted_iota(jnp.int32, (MIX_ROWS, CONV_HALF), 0) & (GRID_W - 1)
    lane_f = lax.broadcasted_iota(jnp.int32, (MIX_ROWS, LANES), 1).astype(F32)
    lane4 = lax.broadcasted_iota(jnp.int32, (MIX_ROWS, TOP_K), 1)

    for r0 in range(0, tm, MIX_ROWS):
        rows = slice(r0, r0 + MIX_ROWS)

        uh = cc_ref[rows, :CONV_HALF].astype(F32) * cx_ref[rows, :CONV_HALF].astype(F32)
        left = jnp.where(pos == 0, 0.0, pltpu.roll(uh, 1, 0))
        right = jnp.where(pos == GRID_W - 1, 0.0, pltpu.roll(uh, MIX_ROWS - 1, 0))
        yh = cw[0:1, :CONV_HALF] * left + cw[1:2, :CONV_HALF] * uh + cw[2:3, :CONV_HALF] * right
        yv = (cw[0:1, CONV_HALF:] * ext[r0:r0 + MIX_ROWS]
              + cw[1:2, CONV_HALF:] * ext[r0 + GRID_W:r0 + GRID_W + MIX_ROWS]
              + cw[2:3, CONV_HALF:] * ext[r0 + 2 * GRID_W:r0 + 2 * GRID_W + MIX_ROWS])
        yc = cb_ref[rows, :].astype(F32) * jnp.concatenate([yh, yv], axis=1)

        hs = hf_ref[rows, :] + hb_ref[rows, :]
        parts = []
        for hd in range(N_HEADS):
            seg = hs[:, hd * DV:(hd + 1) * DV]
            parts.append(seg * lax.rsqrt(jnp.mean(seg * seg, axis=-1, keepdims=True) + EPS))
        hm = jnp.concatenate(parts, axis=1) * ng_ref[...] * jax.nn.sigmoid(o_ref[rows, :].astype(F32))

        z = jnp.concatenate([hm.astype(BF16), yc.astype(BF16)], axis=1)
        x1 = x_ref[rows, :] + gt_ref[...] * jnp.dot(z, wout_ref[...], preferred_element_type=F32)
        x1_ref[rows, :] = x1

        y = x1 * lax.rsqrt(jnp.mean(x1 * x1, axis=-1, keepdims=True) + EPS) * gffn_ref[...]
        xn2 = y * (1.0 + scf_ref[...]) + shf_ref[...]
        _store_slab_rows(xn2_ref, r0, _pack_pairs(xn2))

        logits = jnp.dot(xn2.astype(BF16), wr_ref[...], preferred_element_type=F32) + br_ref[...]
        vals, idxs = [], []
        for _ in range(TOP_K):
            mx = jnp.max(logits, axis=-1, keepdims=True)
            ik = jnp.min(jnp.where(logits == mx, lane_f, float(LANES)), axis=-1, keepdims=True)
            vals.append(mx)
            idxs.append(ik)
            logits = jnp.where(lane_f == ik, -jnp.inf, logits)
        es = [jnp.exp(v - vals[0]) for v in vals]
        tot = es[0] + es[1] + es[2] + es[3]
        idx_out = jnp.zeros((MIX_ROWS, TOP_K), F32)
        tw_out = jnp.zeros((MIX_ROWS, TOP_K), F32)
        for kk in range(TOP_K):
            idx_out = jnp.where(lane4 == kk, idxs[kk], idx_out)
            tw_out = jnp.where(lane4 == kk, es[kk] / tot, tw_out)
        idx_ref[rows, :] = idx_out.astype(jnp.int32)
        tw_ref[rows, :] = tw_out


def _mixout(proj, hf, hb, x2, conv_w, norm_g, w_out, gt, g_ffn, sh_f, sc_f, w_r, b_r, bsz, s, tm):
    t, d = x2.shape
    nt = s // tm
    rb = tm // GRID_W
    last_rb = t // GRID_W - 1
    row = lambda b, i: b * nt + i
    w = MLSTM_WIDTH
    vec = lambda n: pl.BlockSpec((1, n), lambda b, i: (0, 0))
    per_b = pl.BlockSpec((None, 1, d), lambda b, i: (b, 0, 0))
    halo_prev = lambda cblk: pl.BlockSpec(
        (GRID_W, CONV_HALF), lambda b, i: (jnp.maximum(row(b, i) * rb - 1, 0), cblk))
    halo_next = lambda cblk: pl.BlockSpec(
        (GRID_W, CONV_HALF), lambda b, i: (jnp.minimum((row(b, i) + 1) * rb, last_rb), cblk))
    return pl.pallas_call(
        _mixout_kernel,
        grid=(bsz, nt),
        in_specs=[pl.BlockSpec((tm, w), lambda b, i: (row(b, i), 2)),
                  pl.BlockSpec((tm, w), lambda b, i: (row(b, i), 3)),
                  pl.BlockSpec((tm, w), lambda b, i: (row(b, i), 4)),
                  pl.BlockSpec((tm, w), lambda b, i: (row(b, i), 5)),
                  halo_prev(9), halo_prev(11), halo_next(9), halo_next(11),
                  pl.BlockSpec((tm, w), lambda b, i: (row(b, i), 0)),
                  pl.BlockSpec((tm, w), lambda b, i: (row(b, i), 0)),
                  pl.BlockSpec((tm, d), lambda b, i: (row(b, i), 0)),
                  pl.BlockSpec((3, CONV_WIDTH), lambda b, i: (0, 0)),
                  vec(w),
                  pl.BlockSpec((d, d), lambda b, i: (0, 0)),
                  per_b, vec(d), per_b, per_b,
                  pl.BlockSpec((d, LANES), lambda b, i: (0, 0)),
                  vec(LANES)],
        out_specs=[pl.BlockSpec((tm, d), lambda b, i: (row(b, i), 0)),
                   pl.BlockSpec((tm * SLAB, LANES), lambda b, i: (row(b, i), 0)),
                   pl.BlockSpec((tm, TOP_K), lambda b, i: (row(b, i), 0)),
                   pl.BlockSpec((tm, TOP_K), lambda b, i: (row(b, i), 0))],
        out_shape=[jax.ShapeDtypeStruct((t, d), F32),
                   jax.ShapeDtypeStruct((t * SLAB, LANES), jnp.uint32),
                   jax.ShapeDtypeStruct((t, TOP_K), jnp.int32),
                   jax.ShapeDtypeStruct((t, TOP_K), F32)],
        compiler_params=_cparams(("arbitrary", "arbitrary")),
        name="mixout",
    )(proj, proj, proj, proj, proj, proj, proj, proj, hf, hb, x2,
      conv_w, norm_g, w_out, gt, g_ffn, sh_f, sc_f, w_r, b_r)


def _rank_kernel(idx_ref, rank_ref, cnt_ref, run_scr):
    @pl.when(pl.program_id(0) == 0)
    def _():
        run_scr[...] = jnp.zeros_like(run_scr)

    tm = idx_ref.shape[0]
    idx = idx_ref[...]
    lane = lax.broadcasted_iota(jnp.int32, (tm, LANES), 1)
    hits = [lane == idx[:, kk:kk + 1] for kk in range(TOP_K)]
    onehot = jnp.zeros((tm, LANES), F32)
    for hit in hits:
        onehot = onehot + hit.astype(F32)
    r = lax.broadcasted_iota(jnp.int32, (tm, tm), 0)
    c = lax.broadcasted_iota(jnp.int32, (tm, tm), 1)
    before = jnp.dot((c < r).astype(BF16), onehot.astype(BF16), preferred_element_type=F32) + run_scr[...]
    lane4 = lax.broadcasted_iota(jnp.int32, (tm, TOP_K), 1)
    rank = jnp.zeros((tm, TOP_K), F32)
    for kk, hit in enumerate(hits):
        rk = jnp.sum(jnp.where(hit, before, 0.0), axis=-1, keepdims=True)
        rank = jnp.where(lane4 == kk, rk, rank)
    rank_ref[...] = rank.astype(jnp.int32)
    run_scr[...] = run_scr[...] + jnp.sum(onehot, axis=0, keepdims=True)
    cnt_ref[...] = run_scr[...]


def _rank(idx, tm):
    t = idx.shape[0]
    return pl.pallas_call(
        _rank_kernel,
        grid=(t // tm,),
        in_specs=[pl.BlockSpec((tm, TOP_K), lambda i: (i, 0))],
        out_specs=[pl.BlockSpec((tm, TOP_K), lambda i: (i, 0)),
                   pl.BlockSpec((1, LANES), lambda i: (0, 0))],
        out_shape=[jax.ShapeDtypeStruct((t, TOP_K), jnp.int32),
                   jax.ShapeDtypeStruct((1, LANES), F32)],
        scratch_shapes=[pltpu.VMEM((1, LANES), F32)],
        compiler_params=_cparams(("arbitrary",)),
        name="rank",
    )(idx)


def _largest_pad_piece():
    return 1 << ((ROW_BLOCK - 1).bit_length() - 1)


def _dispatch_kernel(pad_ref, dest_hbm, xn_ref, xs_hbm, dsm, zeros_scr, sem_idx, sem_rows, sem_pad):
    i = pl.program_id(0)
    tm = xn_ref.shape[0] // SLAB
    n_idx = tm * TOP_K
    idx_copy = pltpu.make_async_copy(dest_hbm.at[pl.ds(i * n_idx, n_idx)], dsm, sem_idx)
    idx_copy.start()

    def slab(ref, row, n_rows=1):
        return ref.at[pl.ds(pl.multiple_of(row * SLAB, SLAB), n_rows * SLAB), :]

    def for_each_pad_piece(fn):
        def per_expert(e, carry):
            off = pad_ref[2 * e]
            n = pad_ref[2 * e + 1]
            size = _largest_pad_piece()
            while size >= 1:
                take = (n & size) != 0

                @pl.when(take)
                def _(off=off, size=size):
                    fn(pltpu.make_async_copy(slab(zeros_scr, 0, size), slab(xs_hbm, off, size), sem_pad))

                off = off + jnp.where(take, size, 0)
                size //= 2
            return carry
        lax.fori_loop(0, N_EXPERTS, per_expert, 0)

    @pl.when(i == 0)
    def _():
        zeros_scr[...] = jnp.zeros_like(zeros_scr)
        for_each_pad_piece(lambda cp: cp.start())
        for_each_pad_piece(lambda cp: cp.wait())

    idx_copy.wait()

    def row_copy(t, kk):
        return pltpu.make_async_copy(slab(xn_ref, t), slab(xs_hbm, dsm[t * TOP_K + kk]), sem_rows)

    def issue(t, carry):
        for kk in range(TOP_K):
            row_copy(t, kk).start()
        return carry

    lax.fori_loop(0, tm, issue, 0, unroll=ISSUE_UNROLL)
    pltpu.make_async_copy(slab(xs_hbm, 0, n_idx), slab(xs_hbm, 0, n_idx), sem_rows).wait()


def _dispatch(pad_info, dest_flat, xn2, n_rows, tm):
    t = xn2.shape[0] // SLAB
    return pl.pallas_call(
        _dispatch_kernel,
        grid_spec=pltpu.PrefetchScalarGridSpec(
            num_scalar_prefetch=1,
            grid=(t // tm,),
            in_specs=[pl.BlockSpec(memory_space=pl.ANY),
                      pl.BlockSpec((tm * SLAB, LANES), lambda i, pad: (i, 0))],
            out_specs=pl.BlockSpec(memory_space=pl.ANY),
            scratch_shapes=[pltpu.SMEM((tm * TOP_K,), jnp.int32),
                            pltpu.VMEM((_largest_pad_piece() * SLAB, LANES), xn2.dtype),
                            pltpu.SemaphoreType.DMA(()),
                            pltpu.SemaphoreType.DMA(()),
                            pltpu.SemaphoreType.DMA(())]),
        out_shape=jax.ShapeDtypeStruct((n_rows * SLAB, LANES), xn2.dtype),
        compiler_params=_cparams(("arbitrary",)),
        name="dispatch",
    )(pad_info, dest_flat, xn2)


def _new_expert(be_ref, j):
    return jnp.logical_or(j == 0, be_ref[j] != be_ref[jnp.maximum(j - 1, 0)])


GU_COLS = 512


def _expert_gu_kernel(be_ref, nu_ref, nxt_ref, xs_ref, w_hbm, bg_ref, bu_ref, act_ref,
                      stage_g, stage_u, wg_scr, wu_scr, sem):
    n = pl.program_id(0)
    j = pl.program_id(1)
    nt = pl.num_programs(0)
    tn = wg_scr.shape[1]

    def weight_copies(e, nn):
        col_g = pl.multiple_of(nn * tn, tn)
        col_u = pl.multiple_of((nt + nn) * tn, tn)
        return (pltpu.make_async_copy(w_hbm.at[e, :, pl.ds(col_g, tn)], stage_g, sem.at[0]),
                pltpu.make_async_copy(w_hbm.at[e, :, pl.ds(col_u, tn)], stage_u, sem.at[1]))

    @pl.when(j < nu_ref[0])
    def _():
        e = be_ref[j]

        @pl.when(_new_expert(be_ref, j))
        def _():
            @pl.when(jnp.logical_and(n == 0, j == 0))
            def _():
                for cp in weight_copies(e, n):
                    cp.start()

            for cp in weight_copies(e, n):
                cp.wait()
            wg_scr[...] = stage_g[...].astype(BF16)
            wu_scr[...] = stage_u[...].astype(BF16)

            e_next = nxt_ref[e]
            in_pass = e_next >= 0

            @pl.when(jnp.logical_or(in_pass, n + 1 < nt))
            def _():
                for cp in weight_copies(jnp.where(in_pass, e_next, be_ref[0]), jnp.where(in_pass, n, n + 1)):
                    cp.start()

        x = _unpack_pairs(_load_slab_rows(xs_ref, xs_ref.shape[0] // SLAB), SLAB * LANES).astype(BF16)
        for c0 in range(0, tn, GU_COLS):
            cols = slice(c0, c0 + GU_COLS)
            g = jnp.dot(x, wg_scr[:, cols], preferred_element_type=F32) + bg_ref[:, cols]
            u = jnp.dot(x, wu_scr[:, cols], preferred_element_type=F32) + bu_ref[:, cols]
            gate = jnp.minimum(g, SWIGLU_LIMIT)
            up = jnp.clip(u, -SWIGLU_LIMIT, SWIGLU_LIMIT)
            act_ref[:, cols] = ((up + 1.0) * gate * jax.nn.sigmoid(SWIGLU_ALPHA * gate)).astype(BF16)


def _expert_gu(blk_expert, n_used, nxt_expert, xs, w_gu, b_gu, tn):
    n_rows = xs.shape[0] // SLAB
    d = w_gu.shape[1]
    dff = w_gu.shape[2] // 2
    nt = dff // tn
    nb = n_rows // ROW_BLOCK
    blk = lambda j, nu: jnp.minimum(j, nu[0] - 1)
    exp = lambda j, be, nu: be[blk(j, nu)]
    return pl.pallas_call(
        _expert_gu_kernel,
        grid_spec=pltpu.PrefetchScalarGridSpec(
            num_scalar_prefetch=3,
            grid=(nt, nb),
            in_specs=[pl.BlockSpec((ROW_BLOCK * SLAB, LANES), lambda n, j, be, nu, nx: (blk(j, nu), 0)),
                      pl.BlockSpec(memory_space=pl.ANY),
                      pl.BlockSpec((None, 1, tn), lambda n, j, be, nu, nx: (exp(j, be, nu), 0, n)),
                      pl.BlockSpec((None, 1, tn), lambda n, j, be, nu, nx: (exp(j, be, nu), 0, nt + n))],
            out_specs=pl.BlockSpec((ROW_BLOCK, tn), lambda n, j, be, nu, nx: (blk(j, nu), n)),
            scratch_shapes=[pltpu.VMEM((d, tn), F32), pltpu.VMEM((d, tn), F32),
                            pltpu.VMEM((d, tn), BF16), pltpu.VMEM((d, tn), BF16),
                            pltpu.SemaphoreType.DMA((2,))]),
        out_shape=jax.ShapeDtypeStruct((n_rows, dff), BF16),
        compiler_params=_cparams(("arbitrary", "arbitrary")),
        name="expert_gu",
    )(blk_expert, n_used, nxt_expert, xs, w_gu, b_gu, b_gu)


def _expert_down_kernel(be_ref, nu_ref, nxt_ref, act_ref, w_hbm, b_ref, y_ref, stage, w_scr, sem):
    j = pl.program_id(0)

    def weight_copy(e):
        return pltpu.make_async_copy(w_hbm.at[e], stage, sem)

    @pl.when(j < nu_ref[0])
    def _():
        e = be_ref[j]

        @pl.when(_new_expert(be_ref, j))
        def _():
            @pl.when(j == 0)
            def _():
                weight_copy(e).start()

            weight_copy(e).wait()
            w_scr[...] = stage[...].astype(BF16)
            e_next = nxt_ref[e]

            @pl.when(e_next >= 0)
            def _():
                weight_copy(e_next).start()

        act = act_ref[...]
        for c0 in range(0, w_scr.shape[1], DOWN_COLS):
            cols = slice(c0, c0 + DOWN_COLS)
            y = _pack_pairs(jnp.dot(act, w_scr[:, cols], preferred_element_type=F32) + b_ref[:, cols])
            for q in range(y.shape[1] // LANES):
                chunk = c0 // 2 // LANES + q
                y_ref[pl.ds(chunk, y.shape[0], stride=SLAB), :] = y[:, q * LANES:(q + 1) * LANES]


def _expert_down(blk_expert, n_used, nxt_expert, act, w_down, b_down):
    n_rows, dff = act.shape
    d = w_down.shape[2]
    nb = n_rows // ROW_BLOCK
    blk = lambda j, nu: jnp.minimum(j, nu[0] - 1)
    exp = lambda j, be, nu: be[blk(j, nu)]
    return pl.pallas_call(
        _expert_down_kernel,
        grid_spec=pltpu.PrefetchScalarGridSpec(
            num_scalar_prefetch=3,
            grid=(nb,),
            in_specs=[pl.BlockSpec((ROW_BLOCK, dff), lambda j, be, nu, nx: (blk(j, nu), 0)),
                      pl.BlockSpec(memory_space=pl.ANY),
                      pl.BlockSpec((None, 1, d), lambda j, be, nu, nx: (exp(j, be, nu), 0, 0))],
            out_specs=pl.BlockSpec((ROW_BLOCK * SLAB, LANES), lambda j, be, nu, nx: (blk(j, nu), 0)),
            scratch_shapes=[pltpu.VMEM((dff, d), F32), pltpu.VMEM((dff, d), BF16),
                            pltpu.SemaphoreType.DMA(())]),
        out_shape=jax.ShapeDtypeStruct((n_rows * SLAB, LANES), jnp.uint32),
        compiler_params=_cparams(("arbitrary",)),
        name="expert_down",
    )(blk_expert, n_used, nxt_expert, act, w_down, b_down)


def _combine_kernel(dest_hbm, y_hbm, x1_ref, tw_ref, gt_ref, gfin_ref, out_ref, dsm, buf, sem_idx, sem_rows):
    i = pl.program_id(1) + pl.program_id(0) * pl.num_programs(1)
    tm = x1_ref.shape[0]
    n_idx = tm * TOP_K
    idx_copy = pltpu.make_async_copy(dest_hbm.at[pl.ds(i * n_idx, n_idx)], dsm, sem_idx)
    idx_copy.start()
    idx_copy.wait()

    def slab(ref, row, n_rows=1):
        return ref.at[pl.ds(pl.multiple_of(row * SLAB, SLAB), n_rows * SLAB), :]

    def row_copy(t, kk):
        return pltpu.make_async_copy(slab(y_hbm, dsm[t * TOP_K + kk]), slab(buf.at[kk], t), sem_rows)

    def issue(t, carry):
        for kk in range(TOP_K):
            row_copy(t, kk).start()
        return carry

    lax.fori_loop(0, tm, issue, 0, unroll=ISSUE_UNROLL)
    for kk in range(TOP_K):
        pltpu.make_async_copy(slab(y_hbm, 0, tm), buf.at[kk], sem_rows).wait()

    tw = tw_ref[...]
    rows = lambda kk: _unpack_pairs(_load_slab_rows(buf.at[kk], tm), DOWN_COLS // 2)
    acc = rows(0) * tw[:, 0:1]
    for kk in range(1, TOP_K):
        acc = acc + rows(kk) * tw[:, kk:kk + 1]
    x2 = x1_ref[...] + gt_ref[...] * acc
    out_ref[...] = x2 * lax.rsqrt(jnp.mean(x2 * x2, axis=-1, keepdims=True) + EPS) * gfin_ref[...]


def _combine(dest_flat, y, x1, tw, gt, g_final, bsz, s, tm):
    t, d = x1.shape
    nt = s // tm
    row = lambda b, i: (b * nt + i, 0)
    return pl.pallas_call(
        _combine_kernel,
        grid=(bsz, nt),
        in_specs=[pl.BlockSpec(memory_space=pl.ANY),
                  pl.BlockSpec(memory_space=pl.ANY),
                  pl.BlockSpec((tm, d), row),
                  pl.BlockSpec((tm, TOP_K), row),
                  pl.BlockSpec((None, 1, d), lambda b, i: (b, 0, 0)),
                  pl.BlockSpec((1, d), lambda b, i: (0, 0))],
        out_specs=pl.BlockSpec((tm, d), row),
        out_shape=jax.ShapeDtypeStruct((t, d), F32),
        scratch_shapes=[pltpu.SMEM((tm * TOP_K,), jnp.int32),
                        pltpu.VMEM((TOP_K, tm * SLAB, LANES), y.dtype),
                        pltpu.SemaphoreType.DMA(()),
                        pltpu.SemaphoreType.DMA(())],
        compiler_params=_cparams(("arbitrary", "arbitrary")),
        name="combine",
    )(dest_flat, y, x1, tw, gt, g_final)


def _pad_lanes(a, value=0.0):
    return jnp.pad(a, ((0, 0), (0, LANES - a.shape[1])), constant_values=value)


def _routing_tables(idx, rank, counts_f, n_blocks):
    counts = counts_f[0, :N_EXPERTS].astype(jnp.int32)
    padded = (counts + ROW_BLOCK - 1) // ROW_BLOCK * ROW_BLOCK
    pend = jnp.cumsum(padded)
    pstart = pend - padded
    dest = (pstart[idx] + rank).reshape(-1)
    blk_start = jnp.arange(n_blocks, dtype=jnp.int32) * ROW_BLOCK
    blk_expert = jnp.minimum(jnp.sum((pend[None, :] <= blk_start[:, None]).astype(jnp.int32), axis=1),
                             N_EXPERTS - 1)
    n_used = (pend[-1:] // ROW_BLOCK).astype(jnp.int32)
    pad_info = jnp.stack([pstart + counts, padded - counts], axis=1).reshape(-1).astype(jnp.int32)
    ids = jnp.arange(N_EXPERTS, dtype=jnp.int32)
    later = jnp.where((ids[None, :] > ids[:, None]) & (counts[None, :] > 0), ids[None, :], N_EXPERTS)
    nxt = jnp.min(later, axis=1)
    nxt_expert = jnp.where(nxt == N_EXPERTS, -1, nxt).astype(jnp.int32)
    return dest, blk_expert, n_used, nxt_expert, pad_info


def _layer(x, c, ctx, c_ctx, w_ada, b_ada, g_mix, w_in, b_if, conv_w, norm_g, w_out,
           g_ffn, w_router, b_router, w_gu, b_gu, w_down, b_down, g_final):
    bsz, s, d = x.shape
    s_ctx = ctx.shape[1]

    cond = jnp.zeros((8, d), F32).at[:bsz].set(c).at[bsz].set(c_ctx)
    mod = _adaln(cond, w_ada, b_ada[None, :])
    sh_m, sc_m, gt_m, sh_f, sc_f, gt_f = [m[:, None, :] for m in jnp.split(mod, N_MOD, axis=-1)]
    lat = lambda m: m[:bsz]
    ctxm = lambda m: jnp.broadcast_to(m[bsz:bsz + 1], (bsz, 1, d))

    g0 = 2 * QK_COLS + 2 * MLSTM_WIDTH
    w_main = jnp.concatenate([w_in[:, :g0], w_in[:, g0 + N_GATE_COLS:]], axis=1).astype(BF16)
    w_gate = _pad_lanes(w_in[:, g0:g0 + N_GATE_COLS]).astype(BF16)
    b_gate = _pad_lanes(b_if[None, :])
    g_mix2 = g_mix[None, :]

    proj_c, gpre_c = _inproj(ctx, g_mix2, ctxm(sh_m), ctxm(sc_m), w_main, w_gate, min(s_ctx, 512))
    gcol_c, grow_c = _gates(gpre_c, b_gate, 512)
    zeros_state = (jnp.zeros((bsz, 2 * N_HEADS, DK, DVX), F32),
                   jnp.zeros((bsz, 2 * N_HEADS, 1, LANES), F32))
    _, _, c0, m0 = _mlstm(proj_c, gcol_c, grow_c, bsz, s_ctx, *zeros_state)

    proj, gpre = _inproj(x, g_mix2, lat(sh_m), lat(sc_m), w_main, w_gate, 512)
    gcol, grow = _gates(gpre, b_gate, 512)
    hf, hb, _, _ = _mlstm(proj, gcol, grow, bsz, s, c0, m0)
    x1, xn2, idx, tw = _mixout(
        proj, hf, hb, x.reshape(bsz * s, d), conv_w, norm_g[None, :], w_out.astype(BF16), lat(gt_m),
        g_ffn[None, :], lat(sh_f), lat(sc_f), _pad_lanes(w_router).astype(BF16),
        _pad_lanes(b_router[None, :], NEG_BIG), bsz, s, 512)

    t = bsz * s
    n_blocks = -(-(t * TOP_K) // ROW_BLOCK) + N_EXPERTS
    rank, counts = _rank(idx, 512)
    dest, blk_expert, n_used, nxt_expert, pad_info = _routing_tables(idx, rank, counts, n_blocks)
    xs = _dispatch(pad_info, dest, xn2, n_blocks * ROW_BLOCK, 512)
    act = _expert_gu(blk_expert, n_used, nxt_expert, xs, w_gu, b_gu[:, None, :], 1024)
    y = _expert_down(blk_expert, n_used, nxt_expert, act, w_down, b_down[:, None, :])
    out = _combine(dest, y, x1, tw, lat(gt_f), g_final[None, :], bsz, s, 256)
    return out.reshape(bsz, s, d)


def kernel(x, c, ctx, c_ctx, w_ada, b_ada, g_mix, w_in, b_if, conv_w, mlstm_norm_g, w_out,
           g_ffn, w_router, b_router, w_gu, b_gu, w_down, b_down, g_final):
    return _layer(x, c, ctx, c_ctx, w_ada[0], b_ada[0], g_mix[0], w_in[0], b_if[0], conv_w[0],
                  mlstm_norm_g[0], w_out[0], g_ffn[0], w_router[0], b_router[0], w_gu[0], b_gu[0],
                  w_down[0], b_down[0], g_final)
```

```python
import functools

import jax
import jax.numpy as jnp
from jax import lax
from jax.experimental import pallas as pl
from jax.experimental.pallas import tpu as pltpu

F32 = jnp.float32
BF16 = jnp.bfloat16

N_HEADS = 4
DK = 128
DV = 256
QK_COLS = N_HEADS * DK
MLSTM_WIDTH = N_HEADS * DV
CONV_WIDTH = 1024
CONV_HALF = CONV_WIDTH // 2
N_GATE_COLS = 4 * N_HEADS
GRID_W = 64
CHUNK = 128
GATE_SOFT_CAP = 15.0
N_EXPERTS = 32
TOP_K = 4
SWIGLU_LIMIT = 7.0
SWIGLU_ALPHA = 1.702
N_MOD = 6
EPS = 1e-6
LANES = 128
SUBLANES = 8
ROW_BLOCK = 1024
DOWN_COLS = 1024
ISSUE_UNROLL = 8
NEG_BIG = -1e30
VMEM_LIMIT = 56 * 1024 * 1024


def _cparams(sem):
    return pltpu.CompilerParams(dimension_semantics=sem, vmem_limit_bytes=VMEM_LIMIT)


def _pack_pairs(x):
    bits = lax.bitcast_convert_type(x.astype(BF16).astype(F32), jnp.uint32)
    g = x.shape[1] // 2
    return bits[:, :g] | (bits[:, g:] >> 16)


def _unpack_pairs(p, group):
    hi = lax.bitcast_convert_type(p & jnp.uint32(0xFFFF0000), F32)
    lo = lax.bitcast_convert_type(p << 16, F32)
    parts = []
    for g0 in range(0, p.shape[1], group):
        parts += [hi[:, g0:g0 + group], lo[:, g0:g0 + group]]
    return jnp.concatenate(parts, axis=1)


SLAB = 8


def _store_slab_rows(ref, r0, packed):
    rows = packed.shape[0]
    for c in range(SLAB):
        ref[pl.ds(r0 * SLAB + c, rows, stride=SLAB), :] = packed[:, c * LANES:(c + 1) * LANES]


def _load_slab_rows(ref, rows):
    return jnp.concatenate([ref[pl.ds(c, rows, stride=SLAB), :] for c in range(SLAB)], axis=1)


def _adaln_kernel(c_ref, w_ref, b_ref, o_ref):
    s = c_ref[...]
    s = s * jax.nn.sigmoid(s)
    o_ref[...] = jnp.dot(s.astype(BF16), w_ref[...].astype(BF16),
                         preferred_element_type=F32) + b_ref[...]


def _adaln(cond, w, b):
    d, n = w.shape
    tn = 1024
    return pl.pallas_call(
        _adaln_kernel,
        grid=(n // tn,),
        in_specs=[pl.BlockSpec((8, d), lambda j: (0, 0)),
                  pl.BlockSpec((d, tn), lambda j: (0, j)),
                  pl.BlockSpec((1, tn), lambda j: (0, j))],
        out_specs=pl.BlockSpec((8, tn), lambda j: (0, j)),
        out_shape=jax.ShapeDtypeStruct((8, n), F32),
        compiler_params=_cparams(("arbitrary",)),
        name="adaln",
    )(cond, w, b)


INPROJ_COLS = 1024


def _inproj_kernel(x_ref, g_ref, sh_ref, sc_ref, w_ref, wg_ref, proj_ref, gate_ref):
    x = x_ref[...]
    y = x * lax.rsqrt(jnp.mean(x * x, axis=-1, keepdims=True) + EPS) * g_ref[...]
    xn = (y * (1.0 + sc_ref[...]) + sh_ref[...]).astype(BF16)
    gate_ref[...] = jnp.dot(xn, wg_ref[...], preferred_element_type=F32)
    for j in range(w_ref.shape[1] // INPROJ_COLS):
        cols = slice(j * INPROJ_COLS, (j + 1) * INPROJ_COLS)
        proj_ref[:, cols] = jnp.dot(xn, w_ref[:, cols], preferred_element_type=F32).astype(BF16)


def _inproj(x, g, sh, sc, w, wg, tm):
    bsz, s, d = x.shape
    p = w.shape[1]
    nt = s // tm
    x2 = x.reshape(bsz * s, d)
    resident = lambda shape: pl.BlockSpec(shape, lambda b, i: (0, 0), pipeline_mode=pl.Buffered(1))
    return pl.pallas_call(
        _inproj_kernel,
        grid=(bsz, nt),
        in_specs=[pl.BlockSpec((tm, d), lambda b, i: (b * nt + i, 0)),
                  pl.BlockSpec((1, d), lambda b, i: (0, 0)),
                  pl.BlockSpec((None, 1, d), lambda b, i: (b, 0, 0)),
                  pl.BlockSpec((None, 1, d), lambda b, i: (b, 0, 0)),
                  resident((d, p)),
                  resident((d, LANES))],
        out_specs=[pl.BlockSpec((tm, p), lambda b, i: (b * nt + i, 0)),
                   pl.BlockSpec((tm, LANES), lambda b, i: (b * nt + i, 0))],
        out_shape=[jax.ShapeDtypeStruct((bsz * s, p), BF16),
                   jax.ShapeDtypeStruct((bsz * s, LANES), F32)],
        compiler_params=_cparams(("arbitrary", "arbitrary")),
        name="inproj",
    )(x2, g, sh, sc, w, wg)


def _log_sigmoid(x):
    return jnp.minimum(x, 0.0) - jnp.log1p(jnp.exp(-jnp.abs(x)))


def _gates_kernel(g_ref, b_ref, gc_ref, gr_ref):
    tm = g_ref.shape[0]
    row = lax.broadcasted_iota(jnp.int32, (tm, LANES), 0)
    lane = lax.broadcasted_iota(jnp.int32, (tm, LANES), 1)
    gp = GATE_SOFT_CAP * jnp.tanh((g_ref[...] + b_ref[...]) / GATE_SOFT_CAP)
    is_f = ((lane >> 2) & 1) == 1
    fwd_lane = lane < 2 * N_HEADS
    lf = jnp.where(is_f, _log_sigmoid(gp), 0.0)
    r2 = lax.broadcasted_iota(jnp.int32, (CHUNK, CHUNK), 0)
    c2 = lax.broadcasted_iota(jnp.int32, (CHUNK, CHUNK), 1)
    lower = (r2 >= c2).astype(F32)
    upper = (r2 <= c2).astype(F32)
    lane_c = lax.broadcasted_iota(jnp.int32, (CHUNK, LANES), 1)
    cums = []
    for c in range(tm // CHUNK):
        lf_c = lf[c * CHUNK:(c + 1) * CHUNK]
        cf = jnp.dot(lower, lf_c, precision=lax.Precision.HIGHEST, preferred_element_type=F32)
        cb = jnp.dot(upper, lf_c, precision=lax.Precision.HIGHEST, preferred_element_type=F32)
        cums.append(jnp.where(lane_c < 2 * N_HEADS, cf, cb))
    cdir = jnp.concatenate(cums, axis=0)
    a = jnp.where(is_f, cdir, gp - pltpu.roll(cdir, LANES - N_HEADS, 1))

    pos = row % CHUNK
    x = a
    k = 1
    while k < CHUNK:
        from_before = jnp.where(pos >= k, pltpu.roll(x, k, 0), -jnp.inf)
        from_after = jnp.where(pos < CHUNK - k, pltpu.roll(x, tm - k, 0), -jnp.inf)
        x = jnp.maximum(x, jnp.where(fwd_lane, from_before, from_after))
        k *= 2
    gc_ref[...] = jnp.where(is_f, a, x)

    lane_1 =lax.broadcasted_iota(jnp.int32, (1, LANES), 1)
    for c in range(tm // CHUNK):
        lo = c * CHUNK
        xc, ac = x[lo:lo + CHUNK], a[lo:lo + CHUNK]
        end_max = jnp.where(lane_1 < 2 * N_HEADS, xc[CHUNK - 1:CHUNK], xc[0:1])
        e = jnp.exp(ac - end_max)
        rows = jnp.where(((lane_c >> 2) & 1) == 1, pltpu.roll(e, N_HEADS, 1), ac)
        gr_ref[:, lo:lo + CHUNK] = rows.T[:N_GATE_COLS, :]


def _gates(gpre, b_if, tm):
    t = gpre.shape[0]
    return pl.pallas_call(
        _gates_kernel,
        grid=(t // tm,),
        in_specs=[pl.BlockSpec((tm, LANES), lambda i: (i, 0)),
                  pl.BlockSpec((1, LANES), lambda i: (0, 0))],
        out_specs=[pl.BlockSpec((tm, LANES), lambda i: (i, 0)),
                   pl.BlockSpec((N_GATE_COLS, tm), lambda i: (0, i))],
        out_shape=[jax.ShapeDtypeStruct((t, LANES), F32),
                   jax.ShapeDtypeStruct((N_GATE_COLS, t), F32)],
        compiler_params=_cparams(("arbitrary",)),
        name="gates",
    )(gpre, b_if)


DVX = DV + LANES
MLSTM_CHUNKS_PER_STEP = 2


def _mlstm_chunk(q, k, v_ext, rmax_col, b_col, r_row, e_row, b_last, rmax_last, mask, cx, m_st):
    scale = DK ** -0.5
    mb = jnp.maximum(m_st, jnp.broadcast_to(rmax_col, (CHUNK, CHUNK)))
    w_intra = jnp.exp(jnp.where(mask, r_row - mb, -jnp.inf))
    w_state = jnp.exp(m_st - mb)
    qk = lax.dot_general(q, k, (((1,), (1,)), ((), ())), preferred_element_type=F32)
    s = qk * (w_intra * scale)
    lhs = jnp.concatenate([s.astype(BF16), (q.astype(F32) * (w_state * scale)).astype(BF16)], axis=1)
    rhs = jnp.concatenate([v_ext, cx.astype(BF16)], axis=0)
    nx = jnp.dot(lhs, rhs, preferred_element_type=F32)
    denom = jnp.maximum(jnp.abs(nx[:, DV:]), jnp.exp(-(jnp.broadcast_to(b_col, (CHUNK, CHUNK)) + mb)))
    h = nx[:, :DV] / jnp.concatenate([denom, denom], axis=1)
    ke_t = (k.T.astype(F32) * e_row).astype(BF16)
    c_loc = jnp.dot(ke_t, v_ext, preferred_element_type=F32)
    m_loc = b_last + rmax_last
    m_new = jnp.maximum(b_last + m_st, m_loc)
    return h, jnp.exp(b_last + m_st - m_new) * cx + jnp.exp(m_loc - m_new) * c_loc, m_new


def _mlstm_kernel(qf_ref, kf_ref, vf_ref, gcf_ref, grf_ref, qb_ref, kb_ref, vb_ref, gcb_ref, grb_ref,
                  c0_ref, m0_ref, hf_ref, hb_ref, cout_ref, mout_ref, m_scr, *c_scrs):
    c = pl.program_id(1)

    @pl.when(c == 0)
    def _():
        for idx, c_scr in enumerate(c_scrs):
            c_scr[...] = c0_ref[idx]
        m_scr[...] = m0_ref[...]

    row = lax.broadcasted_iota(jnp.int32, (CHUNK, CHUNK), 0)
    col = lax.broadcasted_iota(jnp.int32, (CHUNK, CHUNK), 1)
    ones = jnp.ones((CHUNK, LANES), BF16)
    m_all = m_scr[...]
    dirs = ((qf_ref, kf_ref, vf_ref, gcf_ref, grf_ref, hf_ref, 0, CHUNK - 1, col <= row),
            (qb_ref, kb_ref, vb_ref, gcb_ref, grb_ref, hb_ref, 2 * N_HEADS, 0, col >= row))
    n_sub = qf_ref.shape[0] // CHUNK
    m_news = []
    for di, (q_ref, k_ref, v_ref, gc_ref, gr_ref, h_ref, off, last, mask) in enumerate(dirs):
        order = range(n_sub) if di == 0 else range(n_sub - 1, -1, -1)
        for hd in range(N_HEADS):
            idx = di * N_HEADS + hd
            lr, lb = off + hd, off + N_HEADS + hd
            cx, m_st = c_scrs[idx][...], m_all[idx][:, 0:1]
            for sub in order:
                r0 = sub * CHUNK
                rows = slice(r0, r0 + CHUNK)
                v_ext = jnp.concatenate([v_ref[rows, hd * DV:(hd + 1) * DV], ones], axis=1)
                h, cx, m_st = _mlstm_chunk(
                    q_ref[rows, hd * DK:(hd + 1) * DK], k_ref[rows, hd * DK:(hd + 1) * DK], v_ext,
                    gc_ref[rows, lr:lr + 1], gc_ref[rows, lb:lb + 1],
                    gr_ref[lr:lr + 1, rows], gr_ref[lb:lb + 1, rows],
                    gc_ref[r0 + last:r0 + last + 1, lb:lb + 1], gc_ref[r0 + last:r0 + last + 1, lr:lr + 1],
                    mask, cx, m_st)
                h_ref[rows, hd * DV:(hd + 1) * DV] = h
            c_scrs[idx][...] = cx
            m_news.append(jnp.broadcast_to(m_st, (1, LANES)))
    for idx, m_new in enumerate(m_news):
        m_scr[idx] = m_new

    @pl.when(c == pl.num_programs(1) - 1)
    def _():
        for idx, c_scr in enumerate(c_scrs):
            cout_ref[idx] = c_scr[...]
        mout_ref[...] = m_scr[...]


def _mlstm(proj, gcol, grow, bsz, s, c0, m0):
    rows = MLSTM_CHUNKS_PER_STEP * CHUNK
    nc = s // rows
    t = bsz * s
    fwd = lambda b, c: b * nc + c
    bwd = lambda b, c: b * nc + (nc - 1 - c)

    def specs(ci):
        return [pl.BlockSpec((rows, QK_COLS), lambda b, c: (ci(b, c), 0)),
                pl.BlockSpec((rows, QK_COLS), lambda b, c: (ci(b, c), 1)),
                pl.BlockSpec((rows, MLSTM_WIDTH), lambda b, c: (ci(b, c), 1)),
                pl.BlockSpec((rows, LANES), lambda b, c: (ci(b, c), 0)),
                pl.BlockSpec((N_GATE_COLS, rows), lambda b, c: (0, ci(b, c)))]

    st_specs = [pl.BlockSpec((None, 2 * N_HEADS, DK, DVX), lambda b, c: (b, 0, 0, 0)),
                pl.BlockSpec((None, 2 * N_HEADS, 1, LANES), lambda b, c: (b, 0, 0, 0))]
    return pl.pallas_call(
        _mlstm_kernel,
        grid=(bsz, nc),
        in_specs=specs(fwd) + specs(bwd) + st_specs,
        out_specs=[pl.BlockSpec((rows, MLSTM_WIDTH), lambda b, c: (fwd(b, c), 0)),
                   pl.BlockSpec((rows, MLSTM_WIDTH), lambda b, c: (bwd(b, c), 0))] + st_specs,
        out_shape=[jax.ShapeDtypeStruct((t, MLSTM_WIDTH), F32),
                   jax.ShapeDtypeStruct((t, MLSTM_WIDTH), F32),
                   jax.ShapeDtypeStruct(c0.shape, F32),
                   jax.ShapeDtypeStruct(m0.shape, F32)],
        scratch_shapes=[pltpu.VMEM((2 * N_HEADS, 1, LANES), F32)]
        + [pltpu.VMEM((DK, DVX), F32) for _ in range(2 * N_HEADS)],
        compiler_params=_cparams(("arbitrary", "arbitrary")),
        name="mlstm",
    )(proj, proj, proj, gcol, grow, proj, proj, proj, gcol, grow, c0, m0)


MIX_ROWS = 256


def _mixout_kernel(o_ref, cb_ref, cc_ref, cx_ref, ccp_ref, cxp_ref, ccn_ref, cxn_ref, hf_ref, hb_ref, x_ref,
                   cw_ref, ng_ref, wout_ref, gt_ref, gffn_ref, shf_ref, scf_ref, wr_ref, br_ref,
                   x1_ref, xn2_ref, idx_ref, tw_ref):
    i = pl.program_id(1)
    tm = x_ref.shape[0]
    cw = cw_ref[...]

    has_prev = jnp.where(i > 0, 1.0, 0.0)
    has_next = jnp.where(i < pl.num_programs(1) - 1, 1.0, 0.0)
    up = ccp_ref[...].astype(F32) * cxp_ref[...].astype(F32) * has_prev
    un = ccn_ref[...].astype(F32) * cxn_ref[...].astype(F32) * has_next
    uv = cc_ref[:, CONV_HALF:].astype(F32) * cx_ref[:, CONV_HALF:].astype(F32)
    ext = jnp.concatenate([up, uv, un], axis=0)

    pos = lax.broadcasted_iota(jnp.int32, (MIX_ROWS, CONV_HALF), 0) & (GRID_W - 1)
    lane_f = lax.broadcasted_iota(jnp.int32, (MIX_ROWS, LANES), 1).astype(F32)
    lane4 = lax.broadcasted_iota(jnp.int32, (MIX_ROWS, TOP_K), 1)

    for r0 in range(0, tm, MIX_ROWS):
        rows = slice(r0, r0 + MIX_ROWS)

        uh = cc_ref[rows, :CONV_HALF].astype(F32) * cx_ref[rows, :CONV_HALF].astype(F32)
        left = jnp.where(pos == 0, 0.0, pltpu.roll(uh, 1, 0))
        right = jnp.where(pos == GRID_W - 1, 0.0, pltpu.roll(uh, MIX_ROWS - 1, 0))
        yh = cw[0:1, :CONV_HALF] * left + cw[1:2, :CONV_HALF] * uh + cw[2:3, :CONV_HALF] * right
        yv = (cw[0:1, CONV_HALF:] * ext[r0:r0 + MIX_ROWS]
              + cw[1:2, CONV_HALF:] * ext[r0 + GRID_W:r0 + GRID_W + MIX_ROWS]
              + cw[2:3, CONV_HALF:] * ext[r0 + 2 * GRID_W:r0 + 2 * GRID_W + MIX_ROWS])
        yc = cb_ref[rows, :].astype(F32) * jnp.concatenate([yh, yv], axis=1)

        hs = hf_ref[rows, :] + hb_ref[rows, :]
        parts = []
        for hd in range(N_HEADS):
            seg = hs[:, hd * DV:(hd + 1) * DV]
            parts.append(seg * lax.rsqrt(jnp.mean(seg * seg, axis=-1, keepdims=True) + EPS))
        hm = jnp.concatenate(parts, axis=1) * ng_ref[...] * jax.nn.sigmoid(o_ref[rows, :].astype(F32))

        z = jnp.concatenate([hm.astype(BF16), yc.astype(BF16)], axis=1)
        x1 = x_ref[rows, :] + gt_ref[...] * jnp.dot(z, wout_ref[...], preferred_element_type=F32)
        x1_ref[rows, :] = x1

        y = x1 * lax.rsqrt(jnp.mean(x1 * x1, axis=-1, keepdims=True) + EPS) * gffn_ref[...]
        xn2 = y * (1.0 + scf_ref[...]) + shf_ref[...]
        _store_slab_rows(xn2_ref, r0, _pack_pairs(xn2))

        logits = jnp.dot(xn2.astype(BF16), wr_ref[...], preferred_element_type=F32) + br_ref[...]
        vals, idxs = [], []
        for _ in range(TOP_K):
            mx = jnp.max(logits, axis=-1, keepdims=True)
            ik = jnp.min(jnp.where(logits == mx, lane_f, float(LANES)), axis=-1, keepdims=True)
            vals.append(mx)
            idxs.append(ik)
            logits = jnp.where(lane_f == ik, -jnp.inf, logits)
        es = [jnp.exp(v - vals[0]) for v in vals]
        tot = es[0] + es[1] + es[2] + es[3]
        idx_out = jnp.zeros((MIX_ROWS, TOP_K), F32)
        tw_out = jnp.zeros((MIX_ROWS, TOP_K), F32)
        for kk in range(TOP_K):
            idx_out = jnp.where(lane4 == kk, idxs[kk], idx_out)
            tw_out = jnp.where(lane4 == kk, es[kk] / tot, tw_out)
        idx_ref[rows, :] = idx_out.astype(jnp.int32)
        tw_ref[rows, :] = tw_out


def _mixout(proj, hf, hb, x2, conv_w, norm_g, w_out, gt, g_ffn, sh_f, sc_f, w_r, b_r, bsz, s, tm):
    t, d = x2.shape
    nt = s // tm
    rb = tm // GRID_W
    last_rb = t // GRID_W - 1
    row = lambda b, i: b * nt + i
    w = MLSTM_WIDTH
    vec = lambda n: pl.BlockSpec((1, n), lambda b, i: (0, 0))
    per_b = pl.BlockSpec((None, 1, d), lambda b, i: (b, 0, 0))
    halo_prev = lambda cblk: pl.BlockSpec(
        (GRID_W, CONV_HALF), lambda b, i: (jnp.maximum(row(b, i) * rb - 1, 0), cblk))
    halo_next = lambda cblk: pl.BlockSpec(
        (GRID_W, CONV_HALF), lambda b, i: (jnp.minimum((row(b, i) + 1) * rb, last_rb), cblk))
    return pl.pallas_call(
        _mixout_kernel,
        grid=(bsz, nt),
        in_specs=[pl.BlockSpec((tm, w), lambda b, i: (row(b, i), 2)),
                  pl.BlockSpec((tm, w), lambda b, i: (row(b, i), 3)),
                  pl.BlockSpec((tm, w), lambda b, i: (row(b, i), 4)),
                  pl.BlockSpec((tm, w), lambda b, i: (row(b, i), 5)),
                  halo_prev(9), halo_prev(11), halo_next(9), halo_next(11),
                  pl.BlockSpec((tm, w), lambda b, i: (row(b, i), 0)),
                  pl.BlockSpec((tm, w), lambda b, i: (row(b, i), 0)),
                  pl.BlockSpec((tm, d), lambda b, i: (row(b, i), 0)),
                  pl.BlockSpec((3, CONV_WIDTH), lambda b, i: (0, 0)),
                  vec(w),
                  pl.BlockSpec((d, d), lambda b, i: (0, 0)),
                  per_b, vec(d), per_b, per_b,
                  pl.BlockSpec((d, LANES), lambda b, i: (0, 0)),
                  vec(LANES)],
        out_specs=[pl.BlockSpec((tm, d), lambda b, i: (row(b, i), 0)),
                   pl.BlockSpec((tm * SLAB, LANES), lambda b, i: (row(b, i), 0)),
                   pl.BlockSpec((tm, TOP_K), lambda b, i: (row(b, i), 0)),
                   pl.BlockSpec((tm, TOP_K), lambda b, i: (row(b, i), 0))],
        out_shape=[jax.ShapeDtypeStruct((t, d), F32),
                   jax.ShapeDtypeStruct((t * SLAB, LANES), jnp.uint32),
                   jax.ShapeDtypeStruct((t, TOP_K), jnp.int32),
                   jax.ShapeDtypeStruct((t, TOP_K), F32)],
        compiler_params=_cparams(("arbitrary", "arbitrary")),
        name="mixout",
    )(proj, proj, proj, proj, proj, proj, proj, proj, hf, hb, x2,
      conv_w, norm_g, w_out, gt, g_ffn, sh_f, sc_f, w_r, b_r)


def _rank_kernel(idx_ref, rank_ref, cnt_ref, run_scr):
    @pl.when(pl.program_id(0) == 0)
    def _():
        run_scr[...] = jnp.zeros_like(run_scr)

    tm = idx_ref.shape[0]
    idx = idx_ref[...]
    lane = lax.broadcasted_iota(jnp.int32, (tm, LANES), 1)
    hits = [lane == idx[:, kk:kk + 1] for kk in range(TOP_K)]
    onehot = jnp.zeros((tm, LANES), F32)
    for hit in hits:
        onehot = onehot + hit.astype(F32)
    r = lax.broadcasted_iota(jnp.int32, (tm, tm), 0)
    c = lax.broadcasted_iota(jnp.int32, (tm, tm), 1)
    before = jnp.dot((c < r).astype(BF16), onehot.astype(BF16), preferred_element_type=F32) + run_scr[...]
    lane4 = lax.broadcasted_iota(jnp.int32, (tm, TOP_K), 1)
    rank = jnp.zeros((tm, TOP_K), F32)
    for kk, hit in enumerate(hits):
        rk = jnp.sum(jnp.where(hit, before, 0.0), axis=-1, keepdims=True)
        rank = jnp.where(lane4 == kk, rk, rank)
    rank_ref[...] = rank.astype(jnp.int32)
    run_scr[...] = run_scr[...] + jnp.sum(onehot, axis=0, keepdims=True)
    cnt_ref[...] = run_scr[...]


def _rank(idx, tm):
    t = idx.shape[0]
    return pl.pallas_call(
        _rank_kernel,
        grid=(t // tm,),
        in_specs=[pl.BlockSpec((tm, TOP_K), lambda i: (i, 0))],
        out_specs=[pl.BlockSpec((tm, TOP_K), lambda i: (i, 0)),
                   pl.BlockSpec((1, LANES), lambda i: (0, 0))],
        out_shape=[jax.ShapeDtypeStruct((t, TOP_K), jnp.int32),
                   jax.ShapeDtypeStruct((1, LANES), F32)],
        scratch_shapes=[pltpu.VMEM((1, LANES), F32)],
        compiler_params=_cparams(("arbitrary",)),
        name="rank",
    )(idx)


def _largest_pad_piece():
    return 1 << ((ROW_BLOCK - 1).bit_length() - 1)


def _dispatch_kernel(pad_ref, dest_hbm, xn_ref, xs_hbm, dsm, zeros_scr, sem_idx, sem_rows, sem_pad):
    i = pl.program_id(0)
    tm = xn_ref.shape[0] // SLAB
    n_idx = tm * TOP_K
    idx_copy = pltpu.make_async_copy(dest_hbm.at[pl.ds(i * n_idx, n_idx)], dsm, sem_idx)
    idx_copy.start()

    def slab(ref, row, n_rows=1):
        return ref.at[pl.ds(pl.multiple_of(row * SLAB, SLAB), n_rows * SLAB), :]

    def for_each_pad_piece(fn):
        def per_expert(e, carry):
            off = pad_ref[2 * e]
            n = pad_ref[2 * e + 1]
            size = _largest_pad_piece()
            while size >= 1:
                take = (n & size) != 0

                @pl.when(take)
                def _(off=off, size=size):
                    fn(pltpu.make_async_copy(slab(zeros_scr, 0, size), slab(xs_hbm, off, size), sem_pad))

                off = off + jnp.where(take, size, 0)
                size //= 2
            return carry
        lax.fori_loop(0, N_EXPERTS, per_expert, 0)

    @pl.when(i == 0)
    def _():
        zeros_scr[...] = jnp.zeros_like(zeros_scr)
        for_each_pad_piece(lambda cp: cp.start())
        for_each_pad_piece(lambda cp: cp.wait())

    idx_copy.wait()

    def row_copy(t, kk):
        return pltpu.make_async_copy(slab(xn_ref, t), slab(xs_hbm, dsm[t * TOP_K + kk]), sem_rows)

    def issue(t, carry):
        for kk in range(TOP_K):
            row_copy(t, kk).start()
        return carry

    lax.fori_loop(0, tm, issue, 0, unroll=ISSUE_UNROLL)
    pltpu.make_async_copy(slab(xs_hbm, 0, n_idx), slab(xs_hbm, 0, n_idx), sem_rows).wait()


def _dispatch(pad_info, dest_flat, xn2, n_rows, tm):
    t = xn2.shape[0] // SLAB
    return pl.pallas_call(
        _dispatch_kernel,
        grid_spec=pltpu.PrefetchScalarGridSpec(
            num_scalar_prefetch=1,
            grid=(t // tm,),
            in_specs=[pl.BlockSpec(memory_space=pl.ANY),
                      pl.BlockSpec((tm * SLAB, LANES), lambda i, pad: (i, 0))],
            out_specs=pl.BlockSpec(memory_space=pl.ANY),
            scratch_shapes=[pltpu.SMEM((tm * TOP_K,), jnp.int32),
                            pltpu.VMEM((_largest_pad_piece() * SLAB, LANES), xn2.dtype),
                            pltpu.SemaphoreType.DMA(()),
                            pltpu.SemaphoreType.DMA(()),
                            pltpu.SemaphoreType.DMA(())]),
        out_shape=jax.ShapeDtypeStruct((n_rows * SLAB, LANES), xn2.dtype),
        compiler_params=_cparams(("arbitrary",)),
        name="dispatch",
    )(pad_info, dest_flat, xn2)


def _new_expert(be_ref, j):
    return jnp.logical_or(j == 0, be_ref[j] != be_ref[jnp.maximum(j - 1, 0)])


GU_COLS = 512


def _full_or_half_block(valid, rows, compute):
    @pl.when(valid > rows // 2)
    def _():
        compute(rows)

    @pl.when(valid <= rows // 2)
    def _():
        compute(rows // 2)


def _expert_gu_kernel(be_ref, nu_ref, nxt_ref, bv_ref, xs_ref, w_hbm, bg_ref, bu_ref, act_ref,
                      stage_g, stage_u, wg_scr, wu_scr, sem):
    n = pl.program_id(0)
    j = pl.program_id(1)
    nt = pl.num_programs(0)
    tn = wg_scr.shape[1]

    def weight_copies(e, nn):
        col_g = pl.multiple_of(nn * tn, tn)
        col_u = pl.multiple_of((nt + nn) * tn, tn)
        return (pltpu.make_async_copy(w_hbm.at[e, :, pl.ds(col_g, tn)], stage_g, sem.at[0]),
                pltpu.make_async_copy(w_hbm.at[e, :, pl.ds(col_u, tn)], stage_u, sem.at[1]))

    @pl.when(j < nu_ref[0])
    def _():
        e = be_ref[j]

        @pl.when(_new_expert(be_ref, j))
        def _():
            @pl.when(jnp.logical_and(n == 0, j == 0))
            def _():
                for cp in weight_copies(e, n):
                    cp.start()

            for cp in weight_copies(e, n):
                cp.wait()
            wg_scr[...] = stage_g[...].astype(BF16)
            wu_scr[...] = stage_u[...].astype(BF16)

            e_next = nxt_ref[e]
            in_pass = e_next >= 0

            @pl.when(jnp.logical_or(in_pass, n + 1 < nt))
            def _():
                for cp in weight_copies(jnp.where(in_pass, e_next, be_ref[0]), jnp.where(in_pass, n, n + 1)):
                    cp.start()

        def compute(rows):
            x = _unpack_pairs(_load_slab_rows(xs_ref, rows), SLAB * LANES).astype(BF16)
            for c0 in range(0, tn, GU_COLS):
                cols = slice(c0, c0 + GU_COLS)
                g = jnp.dot(x, wg_scr[:, cols], preferred_element_type=F32) + bg_ref[:, cols]
                u = jnp.dot(x, wu_scr[:, cols], preferred_element_type=F32) + bu_ref[:, cols]
                gate = jnp.minimum(g, SWIGLU_LIMIT)
                up = jnp.clip(u, -SWIGLU_LIMIT, SWIGLU_LIMIT)
                act_ref[0:rows, cols] = ((up + 1.0) * gate * jax.nn.sigmoid(SWIGLU_ALPHA * gate)).astype(BF16)

        _full_or_half_block(bv_ref[j], xs_ref.shape[0] // SLAB, compute)


def _expert_gu(blk_expert, n_used, nxt_expert, blk_valid, xs, w_gu, b_gu, tn):
    n_rows = xs.shape[0] // SLAB
    d = w_gu.shape[1]
    dff = w_gu.shape[2] // 2
    nt = dff // tn
    nb = n_rows // ROW_BLOCK
    blk = lambda j, nu: jnp.minimum(j, nu[0] - 1)
    exp = lambda j, be, nu: be[blk(j, nu)]
    return pl.pallas_call(
        _expert_gu_kernel,
        grid_spec=pltpu.PrefetchScalarGridSpec(
            num_scalar_prefetch=4,
            grid=(nt, nb),
            in_specs=[pl.BlockSpec((ROW_BLOCK * SLAB, LANES), lambda n, j, be, nu, nx, bv: (blk(j, nu), 0)),
                      pl.BlockSpec(memory_space=pl.ANY),
                      pl.BlockSpec((None, 1, tn), lambda n, j, be, nu, nx, bv: (exp(j, be, nu), 0, n)),
                      pl.BlockSpec((None, 1, tn), lambda n, j, be, nu, nx, bv: (exp(j, be, nu), 0, nt + n))],
            out_specs=pl.BlockSpec((ROW_BLOCK, tn), lambda n, j, be, nu, nx, bv: (blk(j, nu), n)),
            scratch_shapes=[pltpu.VMEM((d, tn), F32), pltpu.VMEM((d, tn), F32),
                            pltpu.VMEM((d, tn), BF16), pltpu.VMEM((d, tn), BF16),
                            pltpu.SemaphoreType.DMA((2,))]),
        out_shape=jax.ShapeDtypeStruct((n_rows, dff), BF16),
        compiler_params=_cparams(("arbitrary", "arbitrary")),
        name="expert_gu",
    )(blk_expert, n_used, nxt_expert, blk_valid, xs, w_gu, b_gu, b_gu)


def _expert_down_kernel(be_ref, nu_ref, nxt_ref, bv_ref, act_ref, w_hbm, b_ref, y_ref, stage, w_scr, sem):
    j = pl.program_id(0)

    def weight_copy(e):
        return pltpu.make_async_copy(w_hbm.at[e], stage, sem)

    @pl.when(j < nu_ref[0])
    def _():
        e = be_ref[j]

        @pl.when(_new_expert(be_ref, j))
        def _():
            @pl.when(j == 0)
            def _():
                weight_copy(e).start()

            weight_copy(e).wait()
            w_scr[...] = stage[...].astype(BF16)
            e_next = nxt_ref[e]

            @pl.when(e_next >= 0)
            def _():
                weight_copy(e_next).start()

        def compute(rows):
            act = act_ref[0:rows, :]
            for c0 in range(0, w_scr.shape[1], DOWN_COLS):
                cols = slice(c0, c0 + DOWN_COLS)
                y = _pack_pairs(jnp.dot(act, w_scr[:, cols], preferred_element_type=F32) + b_ref[:, cols])
                for q in range(y.shape[1] // LANES):
                    chunk = c0 // 2 // LANES + q
                    y_ref[pl.ds(chunk, rows, stride=SLAB), :] = y[:, q * LANES:(q + 1) * LANES]

        _full_or_half_block(bv_ref[j], act_ref.shape[0], compute)


def _expert_down(blk_expert, n_used, nxt_expert, blk_valid, act, w_down, b_down):
    n_rows, dff = act.shape
    d = w_down.shape[2]
    nb = n_rows // ROW_BLOCK
    blk = lambda j, nu: jnp.minimum(j, nu[0] - 1)
    exp = lambda j, be, nu: be[blk(j, nu)]
    return pl.pallas_call(
        _expert_down_kernel,
        grid_spec=pltpu.PrefetchScalarGridSpec(
            num_scalar_prefetch=4,
            grid=(nb,),
            in_specs=[pl.BlockSpec((ROW_BLOCK, dff), lambda j, be, nu, nx, bv: (blk(j, nu), 0)),
                      pl.BlockSpec(memory_space=pl.ANY),
                      pl.BlockSpec((None, 1, d), lambda j, be, nu, nx, bv: (exp(j, be, nu), 0, 0))],
            out_specs=pl.BlockSpec((ROW_BLOCK * SLAB, LANES), lambda j, be, nu, nx, bv: (blk(j, nu), 0)),
            scratch_shapes=[pltpu.VMEM((dff, d), F32), pltpu.VMEM((dff, d), BF16),
                            pltpu.SemaphoreType.DMA(())]),
        out_shape=jax.ShapeDtypeStruct((n_rows * SLAB, LANES), jnp.uint32),
        compiler_params=_cparams(("arbitrary",)),
        name="expert_down",
    )(blk_expert, n_used, nxt_expert, blk_valid, act, w_down, b_down)


def _combine_kernel(dest_hbm, y_hbm, x1_ref, tw_ref, gt_ref, gfin_ref, out_ref, dsm, buf, sem_idx, sem_rows):
    i = pl.program_id(1) + pl.program_id(0) * pl.num_programs(1)
    tm = x1_ref.shape[0]
    n_idx = tm * TOP_K
    idx_copy = pltpu.make_async_copy(dest_hbm.at[pl.ds(i * n_idx, n_idx)], dsm, sem_idx)
    idx_copy.start()
    idx_copy.wait()

    def slab(ref, row, n_rows=1):
        return ref.at[pl.ds(pl.multiple_of(row * SLAB, SLAB), n_rows * SLAB), :]

    def row_copy(t, kk):
        return pltpu.make_async_copy(slab(y_hbm, dsm[t * TOP_K + kk]), slab(buf.at[kk], t), sem_rows)

    def issue(t, carry):
        for kk in range(TOP_K):
            row_copy(t, kk).start()
        return carry

    lax.fori_loop(0, tm, issue, 0, unroll=ISSUE_UNROLL)
    for kk in range(TOP_K):
        pltpu.make_async_copy(slab(y_hbm, 0, tm), buf.at[kk], sem_rows).wait()

    tw = tw_ref[...]
    rows = lambda kk: _unpack_pairs(_load_slab_rows(buf.at[kk], tm), DOWN_COLS // 2)
    acc = rows(0) * tw[:, 0:1]
    for kk in range(1, TOP_K):
        acc = acc + rows(kk) * tw[:, kk:kk + 1]
    x2 = x1_ref[...] + gt_ref[...] * acc
    out_ref[...] = x2 * lax.rsqrt(jnp.mean(x2 * x2, axis=-1, keepdims=True) + EPS) * gfin_ref[...]


def _combine(dest_flat, y, x1, tw, gt, g_final, bsz, s, tm):
    t, d = x1.shape
    nt = s // tm
    row = lambda b, i: (b * nt + i, 0)
    return pl.pallas_call(
        _combine_kernel,
        grid=(bsz, nt),
        in_specs=[pl.BlockSpec(memory_space=pl.ANY),
                  pl.BlockSpec(memory_space=pl.ANY),
                  pl.BlockSpec((tm, d), row),
                  pl.BlockSpec((tm, TOP_K), row),
                  pl.BlockSpec((None, 1, d), lambda b, i: (b, 0, 0)),
                  pl.BlockSpec((1, d), lambda b, i: (0, 0))],
        out_specs=pl.BlockSpec((tm, d), row),
        out_shape=jax.ShapeDtypeStruct((t, d), F32),
        scratch_shapes=[pltpu.SMEM((tm * TOP_K,), jnp.int32),
                        pltpu.VMEM((TOP_K, tm * SLAB, LANES), y.dtype),
                        pltpu.SemaphoreType.DMA(()),
                        pltpu.SemaphoreType.DMA(())],
        compiler_params=_cparams(("arbitrary", "arbitrary")),
        name="combine",
    )(dest_flat, y, x1, tw, gt, g_final)


def _pad_lanes(a, value=0.0):
    return jnp.pad(a, ((0, 0), (0, LANES - a.shape[1])), constant_values=value)


def _routing_tables(idx, rank, counts_f, n_blocks):
    counts = counts_f[0, :N_EXPERTS].astype(jnp.int32)
    padded = (counts + ROW_BLOCK - 1) // ROW_BLOCK * ROW_BLOCK
    pend = jnp.cumsum(padded)
    pstart = pend - padded
    dest = (pstart[idx] + rank).reshape(-1)
    blk_start = jnp.arange(n_blocks, dtype=jnp.int32) * ROW_BLOCK
    blk_expert = jnp.minimum(jnp.sum((pend[None, :] <= blk_start[:, None]).astype(jnp.int32), axis=1),
                             N_EXPERTS - 1)
    n_used = (pend[-1:] // ROW_BLOCK).astype(jnp.int32)
    blk_valid = jnp.clip((pstart + counts)[blk_expert] - blk_start, 0, ROW_BLOCK).astype(jnp.int32)
    pad_info = jnp.stack([pstart + counts, padded - counts], axis=1).reshape(-1).astype(jnp.int32)
    ids = jnp.arange(N_EXPERTS, dtype=jnp.int32)
    later = jnp.where((ids[None, :] > ids[:, None]) & (counts[None, :] > 0), ids[None, :], N_EXPERTS)
    nxt = jnp.min(later, axis=1)
    nxt_expert = jnp.where(nxt == N_EXPERTS, -1, nxt).astype(jnp.int32)
    return dest, blk_expert, n_used, nxt_expert, blk_valid, pad_info


def _layer(x, c, ctx, c_ctx, w_ada, b_ada, g_mix, w_in, b_if, conv_w, norm_g, w_out,
           g_ffn, w_router, b_router, w_gu, b_gu, w_down, b_down, g_final):
    bsz, s, d = x.shape
    s_ctx = ctx.shape[1]

    cond = jnp.zeros((8, d), F32).at[:bsz].set(c).at[bsz].set(c_ctx)
    mod = _adaln(cond, w_ada, b_ada[None, :])
    sh_m, sc_m, gt_m, sh_f, sc_f, gt_f = [m[:, None, :] for m in jnp.split(mod, N_MOD, axis=-1)]
    lat = lambda m: m[:bsz]
    ctxm = lambda m: jnp.broadcast_to(m[bsz:bsz + 1], (bsz, 1, d))

    g0 = 2 * QK_COLS + 2 * MLSTM_WIDTH
    w_main = jnp.concatenate([w_in[:, :g0], w_in[:, g0 + N_GATE_COLS:]], axis=1).astype(BF16)
    w_gate = _pad_lanes(w_in[:, g0:g0 + N_GATE_COLS]).astype(BF16)
    b_gate = _pad_lanes(b_if[None, :])
    g_mix2 = g_mix[None, :]

    proj_c, gpre_c = _inproj(ctx, g_mix2, ctxm(sh_m), ctxm(sc_m), w_main, w_gate, min(s_ctx, 512))
    gcol_c, grow_c = _gates(gpre_c, b_gate, 512)
    zeros_state = (jnp.zeros((bsz, 2 * N_HEADS, DK, DVX), F32),
                   jnp.zeros((bsz, 2 * N_HEADS, 1, LANES), F32))
    _, _, c0, m0 = _mlstm(proj_c, gcol_c, grow_c, bsz, s_ctx, *zeros_state)

    proj, gpre = _inproj(x, g_mix2, lat(sh_m), lat(sc_m), w_main, w_gate, 512)
    gcol, grow = _gates(gpre, b_gate, 512)
    hf, hb, _, _ = _mlstm(proj, gcol, grow, bsz, s, c0, m0)
    x1, xn2, idx, tw = _mixout(
        proj, hf, hb, x.reshape(bsz * s, d), conv_w, norm_g[None, :], w_out.astype(BF16), lat(gt_m),
        g_ffn[None, :], lat(sh_f), lat(sc_f), _pad_lanes(w_router).astype(BF16),
        _pad_lanes(b_router[None, :], NEG_BIG), bsz, s, 512)

    t = bsz * s
    n_blocks = -(-(t * TOP_K) // ROW_BLOCK) + N_EXPERTS
    rank, counts = _rank(idx, 512)
    dest, blk_expert, n_used, nxt_expert, blk_valid, pad_info = _routing_tables(idx, rank, counts, n_blocks)
    xs = _dispatch(pad_info, dest, xn2, n_blocks * ROW_BLOCK, 512)
    act = _expert_gu(blk_expert, n_used, nxt_expert, blk_valid, xs, w_gu, b_gu[:, None, :], 1024)
    y = _expert_down(blk_expert, n_used, nxt_expert, blk_valid, act, w_down, b_down[:, None, :])
    out = _combine(dest, y, x1, tw, lat(gt_f), g_final[None, :], bsz, s, 256)
    return out.reshape(bsz, s, d)


def kernel(x, c, ctx, c_ctx, w_ada, b_ada, g_mix, w_in, b_if, conv_w, mlstm_norm_g, w_out,
           g_ffn, w_router, b_router, w_gu, b_gu, w_down, b_down, g_final):
    return _layer(x, c, ctx, c_ctx, w_ada[0], b_ada[0], g_mix[0], w_in[0], b_if[0], conv_w[0],
                  mlstm_norm_g[0], w_out[0], g_ffn[0], w_router[0], b_router[0], w_gu[0], b_gu[0],
                  w_down[0], b_down[0], g_final)
```

```python
import functools

import jax
import jax.numpy as jnp
from jax import lax
from jax.experimental import pallas as pl
from jax.experimental.pallas import tpu as pltpu

F32 = jnp.float32
BF16 = jnp.bfloat16

N_HEADS = 4
DK = 128
DV = 256
QK_COLS = N_HEADS * DK
MLSTM_WIDTH = N_HEADS * DV
CONV_WIDTH = 1024
CONV_HALF = CONV_WIDTH // 2
N_GATE_COLS = 4 * N_HEADS
GRID_W = 64
CHUNK = 128
GATE_SOFT_CAP = 15.0
N_EXPERTS = 32
TOP_K = 4
SWIGLU_LIMIT = 7.0
SWIGLU_ALPHA = 1.702
N_MOD = 6
EPS = 1e-6
LANES = 128
SUBLANES = 8
ROW_BLOCK = 1024
DOWN_COLS = 1024
ISSUE_UNROLL = 8
NEG_BIG = -1e30
VMEM_LIMIT = 56 * 1024 * 1024


def _cparams(sem):
    return pltpu.CompilerParams(dimension_semantics=sem, vmem_limit_bytes=VMEM_LIMIT)


def _pack_pairs(x):
    bits = lax.bitcast_convert_type(x.astype(BF16).astype(F32), jnp.uint32)
    g = x.shape[1] // 2
    return bits[:, :g] | (bits[:, g:] >> 16)


def _unpack_pairs(p, group):
    hi = lax.bitcast_convert_type(p & jnp.uint32(0xFFFF0000), F32)
    lo = lax.bitcast_convert_type(p << 16, F32)
    parts = []
    for g0 in range(0, p.shape[1], group):
        parts += [hi[:, g0:g0 + group], lo[:, g0:g0 + group]]
    return jnp.concatenate(parts, axis=1)


SLAB = 8


def _store_slab_rows(ref, r0, packed):
    rows = packed.shape[0]
    for c in range(SLAB):
        ref[pl.ds(r0 * SLAB + c, rows, stride=SLAB), :] = packed[:, c * LANES:(c + 1) * LANES]


def _load_slab_rows(ref, rows):
    return jnp.concatenate([ref[pl.ds(c, rows, stride=SLAB), :] for c in range(SLAB)], axis=1)


def _adaln_kernel(c_ref, w_ref, b_ref, o_ref):
    s = c_ref[...]
    s = s * jax.nn.sigmoid(s)
    o_ref[...] = jnp.dot(s.astype(BF16), w_ref[...].astype(BF16),
                         preferred_element_type=F32) + b_ref[...]


def _adaln(cond, w, b):
    d, n = w.shape
    tn = 1024
    return pl.pallas_call(
        _adaln_kernel,
        grid=(n // tn,),
        in_specs=[pl.BlockSpec((8, d), lambda j: (0, 0)),
                  pl.BlockSpec((d, tn), lambda j: (0, j)),
                  pl.BlockSpec((1, tn), lambda j: (0, j))],
        out_specs=pl.BlockSpec((8, tn), lambda j: (0, j)),
        out_shape=jax.ShapeDtypeStruct((8, n), F32),
        compiler_params=_cparams(("arbitrary",)),
        name="adaln",
    )(cond, w, b)


INPROJ_COLS = 1024


def _inproj_kernel(x_ref, g_ref, sh_ref, sc_ref, w_ref, wg_ref, proj_ref, gate_ref):
    x = x_ref[...]
    y = x * lax.rsqrt(jnp.mean(x * x, axis=-1, keepdims=True) + EPS) * g_ref[...]
    xn = (y * (1.0 + sc_ref[...]) + sh_ref[...]).astype(BF16)
    gate_ref[...] = jnp.dot(xn, wg_ref[...], preferred_element_type=F32)
    for j in range(w_ref.shape[1] // INPROJ_COLS):
        cols = slice(j * INPROJ_COLS, (j + 1) * INPROJ_COLS)
        proj_ref[:, cols] = jnp.dot(xn, w_ref[:, cols], preferred_element_type=F32).astype(BF16)


def _inproj(x, g, sh, sc, w, wg, tm):
    bsz, s, d = x.shape
    p = w.shape[1]
    nt = s // tm
    x2 = x.reshape(bsz * s, d)
    resident = lambda shape: pl.BlockSpec(shape, lambda b, i: (0, 0), pipeline_mode=pl.Buffered(1))
    return pl.pallas_call(
        _inproj_kernel,
        grid=(bsz, nt),
        in_specs=[pl.BlockSpec((tm, d), lambda b, i: (b * nt + i, 0)),
                  pl.BlockSpec((1, d), lambda b, i: (0, 0)),
                  pl.BlockSpec((None, 1, d), lambda b, i: (b, 0, 0)),
                  pl.BlockSpec((None, 1, d), lambda b, i: (b, 0, 0)),
                  resident((d, p)),
                  resident((d, LANES))],
        out_specs=[pl.BlockSpec((tm, p), lambda b, i: (b * nt + i, 0)),
                   pl.BlockSpec((tm, LANES), lambda b, i: (b * nt + i, 0))],
        out_shape=[jax.ShapeDtypeStruct((bsz * s, p), BF16),
                   jax.ShapeDtypeStruct((bsz * s, LANES), F32)],
        compiler_params=_cparams(("arbitrary", "arbitrary")),
        name="inproj",
    )(x2, g, sh, sc, w, wg)


def _log_sigmoid(x):
    return jnp.minimum(x, 0.0) - jnp.log1p(jnp.exp(-jnp.abs(x)))


def _gates_kernel(g_ref, b_ref, gc_ref, gr_ref):
    tm = g_ref.shape[0]
    row = lax.broadcasted_iota(jnp.int32, (tm, LANES), 0)
    lane = lax.broadcasted_iota(jnp.int32, (tm, LANES), 1)
    gp = GATE_SOFT_CAP * jnp.tanh((g_ref[...] + b_ref[...]) / GATE_SOFT_CAP)
    is_f = ((lane >> 2) & 1) == 1
    fwd_lane = lane < 2 * N_HEADS
    lf = jnp.where(is_f, _log_sigmoid(gp), 0.0)
    r2 = lax.broadcasted_iota(jnp.int32, (CHUNK, CHUNK), 0)
    c2 = lax.broadcasted_iota(jnp.int32, (CHUNK, CHUNK), 1)
    lower = (r2 >= c2).astype(F32)
    upper = (r2 <= c2).astype(F32)
    lane_c = lax.broadcasted_iota(jnp.int32, (CHUNK, LANES), 1)
    cums = []
    for c in range(tm // CHUNK):
        lf_c = lf[c * CHUNK:(c + 1) * CHUNK]
        cf = jnp.dot(lower, lf_c, precision=lax.Precision.HIGHEST, preferred_element_type=F32)
        cb = jnp.dot(upper, lf_c, precision=lax.Precision.HIGHEST, preferred_element_type=F32)
        cums.append(jnp.where(lane_c < 2 * N_HEADS, cf, cb))
    cdir = jnp.concatenate(cums, axis=0)
    a = jnp.where(is_f, cdir, gp - pltpu.roll(cdir, LANES - N_HEADS, 1))

    pos = row % CHUNK
    x = a
    k = 1
    while k < CHUNK:
        from_before = jnp.where(pos >= k, pltpu.roll(x, k, 0), -jnp.inf)
        from_after = jnp.where(pos < CHUNK - k, pltpu.roll(x, tm - k, 0), -jnp.inf)
        x = jnp.maximum(x, jnp.where(fwd_lane, from_before, from_after))
        k *= 2
    gc_ref[...] = jnp.where(is_f, a, x)

    lane_1 =lax.broadcasted_iota(jnp.int32, (1, LANES), 1)
    for c in range(tm // CHUNK):
        lo = c * CHUNK
        xc, ac = x[lo:lo + CHUNK], a[lo:lo + CHUNK]
        end_max = jnp.where(lane_1 < 2 * N_HEADS, xc[CHUNK - 1:CHUNK], xc[0:1])
        e = jnp.exp(ac - end_max)
        rows = jnp.where(((lane_c >> 2) & 1) == 1, pltpu.roll(e, N_HEADS, 1), ac)
        gr_ref[:, lo:lo + CHUNK] = rows.T[:N_GATE_COLS, :]


def _gates(gpre, b_if, tm):
    t = gpre.shape[0]
    return pl.pallas_call(
        _gates_kernel,
        grid=(t // tm,),
        in_specs=[pl.BlockSpec((tm, LANES), lambda i: (i, 0)),
                  pl.BlockSpec((1, LANES), lambda i: (0, 0))],
        out_specs=[pl.BlockSpec((tm, LANES), lambda i: (i, 0)),
                   pl.BlockSpec((N_GATE_COLS, tm), lambda i: (0, i))],
        out_shape=[jax.ShapeDtypeStruct((t, LANES), F32),
                   jax.ShapeDtypeStruct((N_GATE_COLS, t), F32)],
        compiler_params=_cparams(("arbitrary",)),
        name="gates",
    )(gpre, b_if)


DVX = DV + LANES
MLSTM_CHUNKS_PER_STEP = 2


def _mlstm_chunk(q, k, v_ext, rmax_col, b_col, r_row, e_row, b_last, rmax_last, mask, cx, m_st):
    scale = DK ** -0.5
    mb = jnp.maximum(m_st, jnp.broadcast_to(rmax_col, (CHUNK, CHUNK)))
    w_intra = jnp.exp(jnp.where(mask, r_row - mb, -jnp.inf))
    w_state = jnp.exp(m_st - mb)
    qk = lax.dot_general(q, k, (((1,), (1,)), ((), ())), preferred_element_type=F32)
    s = qk * (w_intra * scale)
    lhs = jnp.concatenate([s.astype(BF16), (q.astype(F32) * (w_state * scale)).astype(BF16)], axis=1)
    rhs = jnp.concatenate([v_ext, cx.astype(BF16)], axis=0)
    nx = jnp.dot(lhs, rhs, preferred_element_type=F32)
    denom = jnp.maximum(jnp.abs(nx[:, DV:]), jnp.exp(-(jnp.broadcast_to(b_col, (CHUNK, CHUNK)) + mb)))
    h = nx[:, :DV] / jnp.concatenate([denom, denom], axis=1)
    ke_t = (k.T.astype(F32) * e_row).astype(BF16)
    c_loc = jnp.dot(ke_t, v_ext, preferred_element_type=F32)
    m_loc = b_last + rmax_last
    m_new = jnp.maximum(b_last + m_st, m_loc)
    return h, jnp.exp(b_last + m_st - m_new) * cx + jnp.exp(m_loc - m_new) * c_loc, m_new


def _mlstm_kernel(qf_ref, kf_ref, vf_ref, gcf_ref, grf_ref, qb_ref, kb_ref, vb_ref, gcb_ref, grb_ref,
                  c0_ref, m0_ref, hf_ref, hb_ref, cout_ref, mout_ref, m_scr, *c_scrs):
    c = pl.program_id(1)

    @pl.when(c == 0)
    def _():
        for idx, c_scr in enumerate(c_scrs):
            c_scr[...] = c0_ref[idx]
        m_scr[...] = m0_ref[...]

    row = lax.broadcasted_iota(jnp.int32, (CHUNK, CHUNK), 0)
    col = lax.broadcasted_iota(jnp.int32, (CHUNK, CHUNK), 1)
    ones = jnp.ones((CHUNK, LANES), BF16)
    m_all = m_scr[...]
    dirs = ((qf_ref, kf_ref, vf_ref, gcf_ref, grf_ref, hf_ref, 0, CHUNK - 1, col <= row),
            (qb_ref, kb_ref, vb_ref, gcb_ref, grb_ref, hb_ref, 2 * N_HEADS, 0, col >= row))
    n_sub = qf_ref.shape[0] // CHUNK
    m_news = []
    for di, (q_ref, k_ref, v_ref, gc_ref, gr_ref, h_ref, off, last, mask) in enumerate(dirs):
        order = range(n_sub) if di == 0 else range(n_sub - 1, -1, -1)
        for hd in range(N_HEADS):
            idx = di * N_HEADS + hd
            lr, lb = off + hd, off + N_HEADS + hd
            cx, m_st = c_scrs[idx][...], m_all[idx][:, 0:1]
            for sub in order:
                r0 = sub * CHUNK
                rows = slice(r0, r0 + CHUNK)
                v_ext = jnp.concatenate([v_ref[rows, hd * DV:(hd + 1) * DV], ones], axis=1)
                h, cx, m_st = _mlstm_chunk(
                    q_ref[rows, hd * DK:(hd + 1) * DK], k_ref[rows, hd * DK:(hd + 1) * DK], v_ext,
                    gc_ref[rows, lr:lr + 1], gc_ref[rows, lb:lb + 1],
                    gr_ref[lr:lr + 1, rows], gr_ref[lb:lb + 1, rows],
                    gc_ref[r0 + last:r0 + last + 1, lb:lb + 1], gc_ref[r0 + last:r0 + last + 1, lr:lr + 1],
                    mask, cx, m_st)
                h_ref[rows, hd * DV:(hd + 1) * DV] = h
            c_scrs[idx][...] = cx
            m_news.append(jnp.broadcast_to(m_st, (1, LANES)))
    for idx, m_new in enumerate(m_news):
        m_scr[idx] = m_new

    @pl.when(c == pl.num_programs(1) - 1)
    def _():
        for idx, c_scr in enumerate(c_scrs):
            cout_ref[idx] = c_scr[...]
        mout_ref[...] = m_scr[...]


def _mlstm(proj, gcol, grow, bsz, s, c0, m0):
    rows = MLSTM_CHUNKS_PER_STEP * CHUNK
    nc = s // rows
    t = bsz * s
    fwd = lambda b, c: b * nc + c
    bwd = lambda b, c: b * nc + (nc - 1 - c)

    def specs(ci):
        return [pl.BlockSpec((rows, QK_COLS), lambda b, c: (ci(b, c), 0)),
                pl.BlockSpec((rows, QK_COLS), lambda b, c: (ci(b, c), 1)),
                pl.BlockSpec((rows, MLSTM_WIDTH), lambda b, c: (ci(b, c), 1)),
                pl.BlockSpec((rows, LANES), lambda b, c: (ci(b, c), 0)),
                pl.BlockSpec((N_GATE_COLS, rows), lambda b, c: (0, ci(b, c)))]

    st_specs = [pl.BlockSpec((None, 2 * N_HEADS, DK, DVX), lambda b, c: (b, 0, 0, 0)),
                pl.BlockSpec((None, 2 * N_HEADS, 1, LANES), lambda b, c: (b, 0, 0, 0))]
    return pl.pallas_call(
        _mlstm_kernel,
        grid=(bsz, nc),
        in_specs=specs(fwd) + specs(bwd) + st_specs,
        out_specs=[pl.BlockSpec((rows, MLSTM_WIDTH), lambda b, c: (fwd(b, c), 0)),
                   pl.BlockSpec((rows, MLSTM_WIDTH), lambda b, c: (bwd(b, c), 0))] + st_specs,
        out_shape=[jax.ShapeDtypeStruct((t, MLSTM_WIDTH), F32),
                   jax.ShapeDtypeStruct((t, MLSTM_WIDTH), F32),
                   jax.ShapeDtypeStruct(c0.shape, F32),
                   jax.ShapeDtypeStruct(m0.shape, F32)],
        scratch_shapes=[pltpu.VMEM((2 * N_HEADS, 1, LANES), F32)]
        + [pltpu.VMEM((DK, DVX), F32) for _ in range(2 * N_HEADS)],
        compiler_params=_cparams(("arbitrary", "arbitrary")),
        name="mlstm",
    )(proj, proj, proj, gcol, grow, proj, proj, proj, gcol, grow, c0, m0)


MIX_ROWS = 256


def _mixout_kernel(o_ref, cb_ref, cc_ref, cx_ref, ccp_ref, cxp_ref, ccn_ref, cxn_ref, hf_ref, hb_ref, x_ref,
                   cw_ref, ng_ref, wout_ref, gt_ref, gffn_ref, shf_ref, scf_ref, wr_ref, br_ref,
                   x1_ref, xn2_ref, idx_ref, tw_ref):
    i = pl.program_id(1)
    tm = x_ref.shape[0]
    cw = cw_ref[...]

    has_prev = jnp.where(i > 0, 1.0, 0.0)
    has_next = jnp.where(i < pl.num_programs(1) - 1, 1.0, 0.0)
    up = ccp_ref[...].astype(F32) * cxp_ref[...].astype(F32) * has_prev
    un = ccn_ref[...].astype(F32) * cxn_ref[...].astype(F32) * has_next
    uv = cc_ref[:, CONV_HALF:].astype(F32) * cx_ref[:, CONV_HALF:].astype(F32)
    ext = jnp.concatenate([up, uv, un], axis=0)

    pos = lax.broadcasted_iota(jnp.int32, (MIX_ROWS, CONV_HALF), 0) & (GRID_W - 1)
    lane_f = lax.broadcasted_iota(jnp.int32, (MIX_ROWS, LANES), 1).astype(F32)
    lane4 = lax.broadcasted_iota(jnp.int32, (MIX_ROWS, TOP_K), 1)

    for r0 in range(0, tm, MIX_ROWS):
        rows = slice(r0, r0 + MIX_ROWS)

        uh = cc_ref[rows, :CONV_HALF].astype(F32) * cx_ref[rows, :CONV_HALF].astype(F32)
        left = jnp.where(pos == 0, 0.0, pltpu.roll(uh, 1, 0))
        right = jnp.where(pos == GRID_W - 1, 0.0, pltpu.roll(uh, MIX_ROWS - 1, 0))
        yh = cw[0:1, :CONV_HALF] * left + cw[1:2, :CONV_HALF] * uh + cw[2:3, :CONV_HALF] * right
        yv = (cw[0:1, CONV_HALF:] * ext[r0:r0 + MIX_ROWS]
              + cw[1:2, CONV_HALF:] * ext[r0 + GRID_W:r0 + GRID_W + MIX_ROWS]
              + cw[2:3, CONV_HALF:] * ext[r0 + 2 * GRID_W:r0 + 2 * GRID_W + MIX_ROWS])
        yc = cb_ref[rows, :].astype(F32) * jnp.concatenate([yh, yv], axis=1)

        hs = hf_ref[rows, :] + hb_ref[rows, :]
        parts = []
        for hd in range(N_HEADS):
            seg = hs[:, hd * DV:(hd + 1) * DV]
            parts.append(seg * lax.rsqrt(jnp.mean(seg * seg, axis=-1, keepdims=True) + EPS))
        hm = jnp.concatenate(parts, axis=1) * ng_ref[...] * jax.nn.sigmoid(o_ref[rows, :].astype(F32))

        z = jnp.concatenate([hm.astype(BF16), yc.astype(BF16)], axis=1)
        x1 = x_ref[rows, :] + gt_ref[...] * jnp.dot(z, wout_ref[...], preferred_element_type=F32)
        x1_ref[rows, :] = x1

        y = x1 * lax.rsqrt(jnp.mean(x1 * x1, axis=-1, keepdims=True) + EPS) * gffn_ref[...]
        xn2 = y * (1.0 + scf_ref[...]) + shf_ref[...]
        _store_slab_rows(xn2_ref, r0, _pack_pairs(xn2))

        logits = jnp.dot(xn2.astype(BF16), wr_ref[...], preferred_element_type=F32) + br_ref[...]
        vals, idxs = [], []
        for _ in range(TOP_K):
            mx = jnp.max(logits, axis=-1, keepdims=True)
            ik = jnp.min(jnp.where(logits == mx, lane_f, float(LANES)), axis=-1, keepdims=True)
            vals.append(mx)
            idxs.append(ik)
            logits = jnp.where(lane_f == ik, -jnp.inf, logits)
        es = [jnp.exp(v - vals[0]) for v in vals]
        tot = es[0] + es[1] + es[2] + es[3]
        idx_out = jnp.zeros((MIX_ROWS, TOP_K), F32)
        tw_out = jnp.zeros((MIX_ROWS, TOP_K), F32)
        for kk in range(TOP_K):
            idx_out = jnp.where(lane4 == kk, idxs[kk], idx_out)
            tw_out = jnp.where(lane4 == kk, es[kk] / tot, tw_out)
        idx_ref[rows, :] = idx_out.astype(jnp.int32)
        tw_ref[rows, :] = tw_out


def _mixout(proj, hf, hb, x2, conv_w, norm_g, w_out, gt, g_ffn, sh_f, sc_f, w_r, b_r, bsz, s, tm):
    t, d = x2.shape
    nt = s // tm
    rb = tm // GRID_W
    last_rb = t // GRID_W - 1
    row = lambda b, i: b * nt + i
    w = MLSTM_WIDTH
    vec = lambda n: pl.BlockSpec((1, n), lambda b, i: (0, 0))
    per_b = pl.BlockSpec((None, 1, d), lambda b, i: (b, 0, 0))
    halo_prev = lambda cblk: pl.BlockSpec(
        (GRID_W, CONV_HALF), lambda b, i: (jnp.maximum(row(b, i) * rb - 1, 0), cblk))
    halo_next = lambda cblk: pl.BlockSpec(
        (GRID_W, CONV_HALF), lambda b, i: (jnp.minimum((row(b, i) + 1) * rb, last_rb), cblk))
    return pl.pallas_call(
        _mixout_kernel,
        grid=(bsz, nt),
        in_specs=[pl.BlockSpec((tm, w), lambda b, i: (row(b, i), 2)),
                  pl.BlockSpec((tm, w), lambda b, i: (row(b, i), 3)),
                  pl.BlockSpec((tm, w), lambda b, i: (row(b, i), 4)),
                  pl.BlockSpec((tm, w), lambda b, i: (row(b, i), 5)),
                  halo_prev(9), halo_prev(11), halo_next(9), halo_next(11),
                  pl.BlockSpec((tm, w), lambda b, i: (row(b, i), 0)),
                  pl.BlockSpec((tm, w), lambda b, i: (row(b, i), 0)),
                  pl.BlockSpec((tm, d), lambda b, i: (row(b, i), 0)),
                  pl.BlockSpec((3, CONV_WIDTH), lambda b, i: (0, 0)),
                  vec(w),
                  pl.BlockSpec((d, d), lambda b, i: (0, 0)),
                  per_b, vec(d), per_b, per_b,
                  pl.BlockSpec((d, LANES), lambda b, i: (0, 0)),
                  vec(LANES)],
        out_specs=[pl.BlockSpec((tm, d), lambda b, i: (row(b, i), 0)),
                   pl.BlockSpec((tm * SLAB, LANES), lambda b, i: (row(b, i), 0)),
                   pl.BlockSpec((tm, TOP_K), lambda b, i: (row(b, i), 0)),
                   pl.BlockSpec((tm, TOP_K), lambda b, i: (row(b, i), 0))],
        out_shape=[jax.ShapeDtypeStruct((t, d), F32),
                   jax.ShapeDtypeStruct((t * SLAB, LANES), jnp.uint32),
                   jax.ShapeDtypeStruct((t, TOP_K), jnp.int32),
                   jax.ShapeDtypeStruct((t, TOP_K), F32)],
        compiler_params=_cparams(("arbitrary", "arbitrary")),
        name="mixout",
    )(proj, proj, proj, proj, proj, proj, proj, proj, hf, hb, x2,
      conv_w, norm_g, w_out, gt, g_ffn, sh_f, sc_f, w_r, b_r)


def _rank_kernel(idx_ref, rank_ref, cnt_ref, run_scr):
    @pl.when(pl.program_id(0) == 0)
    def _():
        run_scr[...] = jnp.zeros_like(run_scr)

    tm = idx_ref.shape[0]
    idx = idx_ref[...]
    lane = lax.broadcasted_iota(jnp.int32, (tm, LANES), 1)
    hits = [lane == idx[:, kk:kk + 1] for kk in range(TOP_K)]
    onehot = jnp.zeros((tm, LANES), F32)
    for hit in hits:
        onehot = onehot + hit.astype(F32)
    r = lax.broadcasted_iota(jnp.int32, (tm, tm), 0)
    c = lax.broadcasted_iota(jnp.int32, (tm, tm), 1)
    before = jnp.dot((c < r).astype(BF16), onehot.astype(BF16), preferred_element_type=F32) + run_scr[...]
    lane4 = lax.broadcasted_iota(jnp.int32, (tm, TOP_K), 1)
    rank = jnp.zeros((tm, TOP_K), F32)
    for kk, hit in enumerate(hits):
        rk = jnp.sum(jnp.where(hit, before, 0.0), axis=-1, keepdims=True)
        rank = jnp.where(lane4 == kk, rk, rank)
    rank_ref[...] = rank.astype(jnp.int32)
    run_scr[...] = run_scr[...] + jnp.sum(onehot, axis=0, keepdims=True)
    cnt_ref[...] = run_scr[...]


def _rank(idx, tm):
    t = idx.shape[0]
    return pl.pallas_call(
        _rank_kernel,
        grid=(t // tm,),
        in_specs=[pl.BlockSpec((tm, TOP_K), lambda i: (i, 0))],
        out_specs=[pl.BlockSpec((tm, TOP_K), lambda i: (i, 0)),
                   pl.BlockSpec((1, LANES), lambda i: (0, 0))],
        out_shape=[jax.ShapeDtypeStruct((t, TOP_K), jnp.int32),
                   jax.ShapeDtypeStruct((1, LANES), F32)],
        scratch_shapes=[pltpu.VMEM((1, LANES), F32)],
        compiler_params=_cparams(("arbitrary",)),
        name="rank",
    )(idx)


def _largest_pad_piece():
    return 1 << ((ROW_BLOCK - 1).bit_length() - 1)


def _dispatch_kernel(pad_ref, dest_hbm, xn_ref, xs_hbm, dsm, zeros_scr, sem_idx, sem_rows, sem_pad):
    i = pl.program_id(0)
    tm = xn_ref.shape[0] // SLAB
    n_idx = tm * TOP_K
    idx_copy = pltpu.make_async_copy(dest_hbm.at[pl.ds(i * n_idx, n_idx)], dsm, sem_idx)
    idx_copy.start()

    def slab(ref, row, n_rows=1):
        return ref.at[pl.ds(pl.multiple_of(row * SLAB, SLAB), n_rows * SLAB), :]

    def for_each_pad_piece(fn):
        def per_expert(e, carry):
            off = pad_ref[2 * e]
            n = pad_ref[2 * e + 1]
            size = _largest_pad_piece()
            while size >= 1:
                take = (n & size) != 0

                @pl.when(take)
                def _(off=off, size=size):
                    fn(pltpu.make_async_copy(slab(zeros_scr, 0, size), slab(xs_hbm, off, size), sem_pad))

                off = off + jnp.where(take, size, 0)
                size //= 2
            return carry
        lax.fori_loop(0, N_EXPERTS, per_expert, 0)

    @pl.when(i == 0)
    def _():
        zeros_scr[...] = jnp.zeros_like(zeros_scr)
        for_each_pad_piece(lambda cp: cp.start())
        for_each_pad_piece(lambda cp: cp.wait())

    idx_copy.wait()

    def row_copy(t, kk):
        return pltpu.make_async_copy(slab(xn_ref, t), slab(xs_hbm, dsm[t * TOP_K + kk]), sem_rows)

    def issue(t, carry):
        for kk in range(TOP_K):
            row_copy(t, kk).start(priority=kk % 2)
        return carry

    lax.fori_loop(0, tm, issue, 0, unroll=ISSUE_UNROLL)
    pltpu.make_async_copy(slab(xs_hbm, 0, n_idx), slab(xs_hbm, 0, n_idx), sem_rows).wait()


def _dispatch(pad_info, dest_flat, xn2, n_rows, tm):
    t = xn2.shape[0] // SLAB
    return pl.pallas_call(
        _dispatch_kernel,
        grid_spec=pltpu.PrefetchScalarGridSpec(
            num_scalar_prefetch=1,
            grid=(t // tm,),
            in_specs=[pl.BlockSpec(memory_space=pl.ANY),
                      pl.BlockSpec((tm * SLAB, LANES), lambda i, pad: (i, 0))],
            out_specs=pl.BlockSpec(memory_space=pl.ANY),
            scratch_shapes=[pltpu.SMEM((tm * TOP_K,), jnp.int32),
                            pltpu.VMEM((_largest_pad_piece() * SLAB, LANES), xn2.dtype),
                            pltpu.SemaphoreType.DMA(()),
                            pltpu.SemaphoreType.DMA(()),
                            pltpu.SemaphoreType.DMA(())]),
        out_shape=jax.ShapeDtypeStruct((n_rows * SLAB, LANES), xn2.dtype),
        compiler_params=_cparams(("arbitrary",)),
        name="dispatch",
    )(pad_info, dest_flat, xn2)


def _new_expert(be_ref, j):
    return jnp.logical_or(j == 0, be_ref[j] != be_ref[jnp.maximum(j - 1, 0)])


GU_COLS = 512


def _full_or_half_block(valid, rows, compute):
    @pl.when(valid > rows // 2)
    def _():
        compute(rows)

    @pl.when(valid <= rows // 2)
    def _():
        compute(rows // 2)


def _expert_gu_kernel(be_ref, nu_ref, nxt_ref, bv_ref, xs_ref, w_hbm, bg_ref, bu_ref, act_ref,
                      stage_g, stage_u, wg_scr, wu_scr, sem):
    n = pl.program_id(0)
    j = pl.program_id(1)
    nt = pl.num_programs(0)
    tn = wg_scr.shape[1]

    def weight_copies(e, nn):
        col_g = pl.multiple_of(nn * tn, tn)
        col_u = pl.multiple_of((nt + nn) * tn, tn)
        return (pltpu.make_async_copy(w_hbm.at[e, :, pl.ds(col_g, tn)], stage_g, sem.at[0]),
                pltpu.make_async_copy(w_hbm.at[e, :, pl.ds(col_u, tn)], stage_u, sem.at[1]))

    @pl.when(j < nu_ref[0])
    def _():
        e = be_ref[j]

        @pl.when(_new_expert(be_ref, j))
        def _():
            @pl.when(jnp.logical_and(n == 0, j == 0))
            def _():
                for cp in weight_copies(e, n):
                    cp.start()

            for cp in weight_copies(e, n):
                cp.wait()
            wg_scr[...] = stage_g[...].astype(BF16)
            wu_scr[...] = stage_u[...].astype(BF16)

            e_next = nxt_ref[e]
            in_pass = e_next >= 0

            @pl.when(jnp.logical_or(in_pass, n + 1 < nt))
            def _():
                for cp in weight_copies(jnp.where(in_pass, e_next, be_ref[0]), jnp.where(in_pass, n, n + 1)):
                    cp.start()

        def compute(rows):
            x = _unpack_pairs(_load_slab_rows(xs_ref, rows), SLAB * LANES).astype(BF16)
            for c0 in range(0, tn, GU_COLS):
                cols = slice(c0, c0 + GU_COLS)
                g = jnp.dot(x, wg_scr[:, cols], preferred_element_type=F32) + bg_ref[:, cols]
                u = jnp.dot(x, wu_scr[:, cols], preferred_element_type=F32) + bu_ref[:, cols]
                gate = jnp.minimum(g, SWIGLU_LIMIT)
                up = jnp.clip(u, -SWIGLU_LIMIT, SWIGLU_LIMIT)
                act_ref[0:rows, cols] = ((up + 1.0) * gate * jax.nn.sigmoid(SWIGLU_ALPHA * gate)).astype(BF16)

        _full_or_half_block(bv_ref[j], xs_ref.shape[0] // SLAB, compute)


def _expert_gu(blk_expert, n_used, nxt_expert, blk_valid, xs, w_gu, b_gu, tn):
    n_rows = xs.shape[0] // SLAB
    d = w_gu.shape[1]
    dff = w_gu.shape[2] // 2
    nt = dff // tn
    nb = n_rows // ROW_BLOCK
    blk = lambda j, nu: jnp.minimum(j, nu[0] - 1)
    exp = lambda j, be, nu: be[blk(j, nu)]
    return pl.pallas_call(
        _expert_gu_kernel,
        grid_spec=pltpu.PrefetchScalarGridSpec(
            num_scalar_prefetch=4,
            grid=(nt, nb),
            in_specs=[pl.BlockSpec((ROW_BLOCK * SLAB, LANES), lambda n, j, be, nu, nx, bv: (blk(j, nu), 0)),
                      pl.BlockSpec(memory_space=pl.ANY),
                      pl.BlockSpec((None, 1, tn), lambda n, j, be, nu, nx, bv: (exp(j, be, nu), 0, n)),
                      pl.BlockSpec((None, 1, tn), lambda n, j, be, nu, nx, bv: (exp(j, be, nu), 0, nt + n))],
            out_specs=pl.BlockSpec((ROW_BLOCK, tn), lambda n, j, be, nu, nx, bv: (blk(j, nu), n)),
            scratch_shapes=[pltpu.VMEM((d, tn), F32), pltpu.VMEM((d, tn), F32),
                            pltpu.VMEM((d, tn), BF16), pltpu.VMEM((d, tn), BF16),
                            pltpu.SemaphoreType.DMA((2,))]),
        out_shape=jax.ShapeDtypeStruct((n_rows, dff), BF16),
        compiler_params=_cparams(("arbitrary", "arbitrary")),
        name="expert_gu",
    )(blk_expert, n_used, nxt_expert, blk_valid, xs, w_gu, b_gu, b_gu)


def _expert_down_kernel(be_ref, nu_ref, nxt_ref, bv_ref, act_ref, w_hbm, b_ref, y_ref, stage, w_scr, sem):
    j = pl.program_id(0)

    def weight_copy(e):
        return pltpu.make_async_copy(w_hbm.at[e], stage, sem)

    @pl.when(j < nu_ref[0])
    def _():
        e = be_ref[j]

        @pl.when(_new_expert(be_ref, j))
        def _():
            @pl.when(j == 0)
            def _():
                weight_copy(e).start()

            weight_copy(e).wait()
            w_scr[...] = stage[...].astype(BF16)
            e_next = nxt_ref[e]

            @pl.when(e_next >= 0)
            def _():
                weight_copy(e_next).start()

        def compute(rows):
            act = act_ref[0:rows, :]
            for c0 in range(0, w_scr.shape[1], DOWN_COLS):
                cols = slice(c0, c0 + DOWN_COLS)
                y = _pack_pairs(jnp.dot(act, w_scr[:, cols], preferred_element_type=F32) + b_ref[:, cols])
                for q in range(y.shape[1] // LANES):
                    chunk = c0 // 2 // LANES + q
                    y_ref[pl.ds(chunk, rows, stride=SLAB), :] = y[:, q * LANES:(q + 1) * LANES]

        _full_or_half_block(bv_ref[j], act_ref.shape[0], compute)


def _expert_down(blk_expert, n_used, nxt_expert, blk_valid, act, w_down, b_down):
    n_rows, dff = act.shape
    d = w_down.shape[2]
    nb = n_rows // ROW_BLOCK
    blk = lambda j, nu: jnp.minimum(j, nu[0] - 1)
    exp = lambda j, be, nu: be[blk(j, nu)]
    return pl.pallas_call(
        _expert_down_kernel,
        grid_spec=pltpu.PrefetchScalarGridSpec(
            num_scalar_prefetch=4,
            grid=(nb,),
            in_specs=[pl.BlockSpec((ROW_BLOCK, dff), lambda j, be, nu, nx, bv: (blk(j, nu), 0)),
                      pl.BlockSpec(memory_space=pl.ANY),
                      pl.BlockSpec((None, 1, d), lambda j, be, nu, nx, bv: (exp(j, be, nu), 0, 0))],
            out_specs=pl.BlockSpec((ROW_BLOCK * SLAB, LANES), lambda j, be, nu, nx, bv: (blk(j, nu), 0)),
            scratch_shapes=[pltpu.VMEM((dff, d), F32), pltpu.VMEM((dff, d), BF16),
                            pltpu.SemaphoreType.DMA(())]),
        out_shape=jax.ShapeDtypeStruct((n_rows * SLAB, LANES), jnp.uint32),
        compiler_params=_cparams(("arbitrary",)),
        name="expert_down",
    )(blk_expert, n_used, nxt_expert, blk_valid, act, w_down, b_down)


def _combine_kernel(dest_hbm, y_hbm, x1_ref, tw_ref, gt_ref, gfin_ref, out_ref, dsm, buf, sem_idx, sem_rows):
    i = pl.program_id(1) + pl.program_id(0) * pl.num_programs(1)
    tm = x1_ref.shape[0]
    n_idx = tm * TOP_K
    idx_copy = pltpu.make_async_copy(dest_hbm.at[pl.ds(i * n_idx, n_idx)], dsm, sem_idx)
    idx_copy.start()
    idx_copy.wait()

    def slab(ref, row, n_rows=1):
        return ref.at[pl.ds(pl.multiple_of(row * SLAB, SLAB), n_rows * SLAB), :]

    def row_copy(t, kk):
        return pltpu.make_async_copy(slab(y_hbm, dsm[t * TOP_K + kk]), slab(buf.at[kk], t), sem_rows)

    def issue(t, carry):
        for kk in range(TOP_K):
            row_copy(t, kk).start(priority=kk % 2)
        return carry

    lax.fori_loop(0, tm, issue, 0, unroll=ISSUE_UNROLL)
    for kk in range(TOP_K):
        pltpu.make_async_copy(slab(y_hbm, 0, tm), buf.at[kk], sem_rows).wait()

    tw = tw_ref[...]
    rows = lambda kk: _unpack_pairs(_load_slab_rows(buf.at[kk], tm), DOWN_COLS // 2)
    acc = rows(0) * tw[:, 0:1]
    for kk in range(1, TOP_K):
        acc = acc + rows(kk) * tw[:, kk:kk + 1]
    x2 = x1_ref[...] + gt_ref[...] * acc
    out_ref[...] = x2 * lax.rsqrt(jnp.mean(x2 * x2, axis=-1, keepdims=True) + EPS) * gfin_ref[...]


def _combine(dest_flat, y, x1, tw, gt, g_final, bsz, s, tm):
    t, d = x1.shape
    nt = s // tm
    row = lambda b, i: (b * nt + i, 0)
    return pl.pallas_call(
        _combine_kernel,
        grid=(bsz, nt),
        in_specs=[pl.BlockSpec(memory_space=pl.ANY),
                  pl.BlockSpec(memory_space=pl.ANY),
                  pl.BlockSpec((tm, d), row),
                  pl.BlockSpec((tm, TOP_K), row),
                  pl.BlockSpec((None, 1, d), lambda b, i: (b, 0, 0)),
                  pl.BlockSpec((1, d), lambda b, i: (0, 0))],
        out_specs=pl.BlockSpec((tm, d), row),
        out_shape=jax.ShapeDtypeStruct((t, d), F32),
        scratch_shapes=[pltpu.SMEM((tm * TOP_K,), jnp.int32),
                        pltpu.VMEM((TOP_K, tm * SLAB, LANES), y.dtype),
                        pltpu.SemaphoreType.DMA(()),
                        pltpu.SemaphoreType.DMA(())],
        compiler_params=_cparams(("arbitrary", "arbitrary")),
        name="combine",
    )(dest_flat, y, x1, tw, gt, g_final)


def _pad_lanes(a, value=0.0):
    return jnp.pad(a, ((0, 0), (0, LANES - a.shape[1])), constant_values=value)


def _routing_tables(idx, rank, counts_f, n_blocks):
    counts = counts_f[0, :N_EXPERTS].astype(jnp.int32)
    padded = (counts + ROW_BLOCK - 1) // ROW_BLOCK * ROW_BLOCK
    pend = jnp.cumsum(padded)
    pstart = pend - padded
    dest = (pstart[idx] + rank).reshape(-1)
    blk_start = jnp.arange(n_blocks, dtype=jnp.int32) * ROW_BLOCK
    blk_expert = jnp.minimum(jnp.sum((pend[None, :] <= blk_start[:, None]).astype(jnp.int32), axis=1),
                             N_EXPERTS - 1)
    n_used = (pend[-1:] // ROW_BLOCK).astype(jnp.int32)
    blk_valid = jnp.clip((pstart + counts)[blk_expert] - blk_start, 0, ROW_BLOCK).astype(jnp.int32)
    pad_info = jnp.stack([pstart + counts, padded - counts], axis=1).reshape(-1).astype(jnp.int32)
    ids = jnp.arange(N_EXPERTS, dtype=jnp.int32)
    later = jnp.where((ids[None, :] > ids[:, None]) & (counts[None, :] > 0), ids[None, :], N_EXPERTS)
    nxt = jnp.min(later, axis=1)
    nxt_expert = jnp.where(nxt == N_EXPERTS, -1, nxt).astype(jnp.int32)
    return dest, blk_expert, n_used, nxt_expert, blk_valid, pad_info


def _layer(x, c, ctx, c_ctx, w_ada, b_ada, g_mix, w_in, b_if, conv_w, norm_g, w_out,
           g_ffn, w_router, b_router, w_gu, b_gu, w_down, b_down, g_final):
    bsz, s, d = x.shape
    s_ctx = ctx.shape[1]

    cond = jnp.zeros((8, d), F32).at[:bsz].set(c).at[bsz].set(c_ctx)
    mod = _adaln(cond, w_ada, b_ada[None, :])
    sh_m, sc_m, gt_m, sh_f, sc_f, gt_f = [m[:, None, :] for m in jnp.split(mod, N_MOD, axis=-1)]
    lat = lambda m: m[:bsz]
    ctxm = lambda m: jnp.broadcast_to(m[bsz:bsz + 1], (bsz, 1, d))

    g0 = 2 * QK_COLS + 2 * MLSTM_WIDTH
    w_main = jnp.concatenate([w_in[:, :g0], w_in[:, g0 + N_GATE_COLS:]], axis=1).astype(BF16)
    w_gate = _pad_lanes(w_in[:, g0:g0 + N_GATE_COLS]).astype(BF16)
    b_gate = _pad_lanes(b_if[None, :])
    g_mix2 = g_mix[None, :]

    proj_c, gpre_c = _inproj(ctx, g_mix2, ctxm(sh_m), ctxm(sc_m), w_main, w_gate, min(s_ctx, 512))
    gcol_c, grow_c = _gates(gpre_c, b_gate, 512)
    zeros_state = (jnp.zeros((bsz, 2 * N_HEADS, DK, DVX), F32),
                   jnp.zeros((bsz, 2 * N_HEADS, 1, LANES), F32))
    _, _, c0, m0 = _mlstm(proj_c, gcol_c, grow_c, bsz, s_ctx, *zeros_state)

    proj, gpre = _inproj(x, g_mix2, lat(sh_m), lat(sc_m), w_main, w_gate, 512)
    gcol, grow = _gates(gpre, b_gate, 512)
    hf, hb, _, _ = _mlstm(proj, gcol, grow, bsz, s, c0, m0)
    x1, xn2, idx, tw = _mixout(
        proj, hf, hb, x.reshape(bsz * s, d), conv_w, norm_g[None, :], w_out.astype(BF16), lat(gt_m),
        g_ffn[None, :], lat(sh_f), lat(sc_f), _pad_lanes(w_router).astype(BF16),
        _pad_lanes(b_router[None, :], NEG_BIG), bsz, s, 512)

    t = bsz * s
    n_blocks = -(-(t * TOP_K) // ROW_BLOCK) + N_EXPERTS
    rank, counts = _rank(idx, 512)
    dest, blk_expert, n_used, nxt_expert, blk_valid, pad_info = _routing_tables(idx, rank, counts, n_blocks)
    xs = _dispatch(pad_info, dest, xn2, n_blocks * ROW_BLOCK, 512)
    act = _expert_gu(blk_expert, n_used, nxt_expert, blk_valid, xs, w_gu, b_gu[:, None, :], 1024)
    y = _expert_down(blk_expert, n_used, nxt_expert, blk_valid, act, w_down, b_down[:, None, :])
    out = _combine(dest, y, x1, tw, lat(gt_f), g_final[None, :], bsz, s, 256)
    return out.reshape(bsz, s, d)


def kernel(x, c, ctx, c_ctx, w_ada, b_ada, g_mix, w_in, b_if, conv_w, mlstm_norm_g, w_out,
           g_ffn, w_router, b_router, w_gu, b_gu, w_down, b_down, g_final):
    return _layer(x, c, ctx, c_ctx, w_ada[0], b_ada[0], g_mix[0], w_in[0], b_if[0], conv_w[0],
                  mlstm_norm_g[0], w_out[0], g_ffn[0], w_router[0], b_router[0], w_gu[0], b_gu[0],
                  w_down[0], b_down[0], g_final)
```

```python
import functools

import jax
import jax.numpy as jnp
from jax import lax
from jax.experimental import pallas as pl
from jax.experimental.pallas import tpu as pltpu

F32 = jnp.float32
BF16 = jnp.bfloat16

N_HEADS = 4
DK = 128
DV = 256
QK_COLS = N_HEADS * DK
MLSTM_WIDTH = N_HEADS * DV
CONV_WIDTH = 1024
CONV_HALF = CONV_WIDTH // 2
N_GATE_COLS = 4 * N_HEADS
GRID_W = 64
CHUNK = 128
GATE_SOFT_CAP = 15.0
N_EXPERTS = 32
TOP_K = 4
SWIGLU_LIMIT = 7.0
SWIGLU_ALPHA = 1.702
N_MOD = 6
EPS = 1e-6
LANES = 128
SUBLANES = 8
ROW_BLOCK = 1024
DOWN_COLS = 1024
ISSUE_UNROLL = 8
NEG_BIG = -1e30
VMEM_LIMIT = 56 * 1024 * 1024


def _cparams(sem):
    return pltpu.CompilerParams(dimension_semantics=sem, vmem_limit_bytes=VMEM_LIMIT)


def _pack_pairs(x):
    bits = lax.bitcast_convert_type(x.astype(BF16).astype(F32), jnp.uint32)
    g = x.shape[1] // 2
    return bits[:, :g] | (bits[:, g:] >> 16)


def _unpack_pairs(p, group):
    hi = lax.bitcast_convert_type(p & jnp.uint32(0xFFFF0000), F32)
    lo = lax.bitcast_convert_type(p << 16, F32)
    parts = []
    for g0 in range(0, p.shape[1], group):
        parts += [hi[:, g0:g0 + group], lo[:, g0:g0 + group]]
    return jnp.concatenate(parts, axis=1)


SLAB = 8


def _store_slab_rows(ref, r0, packed):
    rows = packed.shape[0]
    for c in range(SLAB):
        ref[pl.ds(r0 * SLAB + c, rows, stride=SLAB), :] = packed[:, c * LANES:(c + 1) * LANES]


def _load_slab_rows(ref, rows):
    return jnp.concatenate([ref[pl.ds(c, rows, stride=SLAB), :] for c in range(SLAB)], axis=1)


def _adaln_kernel(c_ref, w_ref, b_ref, o_ref):
    s = c_ref[...]
    s = s * jax.nn.sigmoid(s)
    o_ref[...] = jnp.dot(s.astype(BF16), w_ref[...].astype(BF16),
                         preferred_element_type=F32) + b_ref[...]


def _adaln(cond, w, b):
    d, n = w.shape
    tn = 1024
    return pl.pallas_call(
        _adaln_kernel,
        grid=(n // tn,),
        in_specs=[pl.BlockSpec((8, d), lambda j: (0, 0)),
                  pl.BlockSpec((d, tn), lambda j: (0, j)),
                  pl.BlockSpec((1, tn), lambda j: (0, j))],
        out_specs=pl.BlockSpec((8, tn), lambda j: (0, j)),
        out_shape=jax.ShapeDtypeStruct((8, n), F32),
        compiler_params=_cparams(("arbitrary",)),
        name="adaln",
    )(cond, w, b)


INPROJ_COLS = 1024


def _inproj_kernel(x_ref, g_ref, sh_ref, sc_ref, w_ref, wg_ref, proj_ref, gate_ref):
    x = x_ref[...]
    y = x * lax.rsqrt(jnp.mean(x * x, axis=-1, keepdims=True) + EPS) * g_ref[...]
    xn = (y * (1.0 + sc_ref[...]) + sh_ref[...]).astype(BF16)
    gate_ref[...] = jnp.dot(xn, wg_ref[...], preferred_element_type=F32)
    for j in range(w_ref.shape[1] // INPROJ_COLS):
        cols = slice(j * INPROJ_COLS, (j + 1) * INPROJ_COLS)
        proj_ref[:, cols] = jnp.dot(xn, w_ref[:, cols], preferred_element_type=F32).astype(BF16)


def _inproj(x, g, sh, sc, w, wg, tm):
    bsz, s, d = x.shape
    p = w.shape[1]
    nt = s // tm
    x2 = x.reshape(bsz * s, d)
    resident = lambda shape: pl.BlockSpec(shape, lambda b, i: (0, 0), pipeline_mode=pl.Buffered(1))
    return pl.pallas_call(
        _inproj_kernel,
        grid=(bsz, nt),
        in_specs=[pl.BlockSpec((tm, d), lambda b, i: (b * nt + i, 0)),
                  pl.BlockSpec((1, d), lambda b, i: (0, 0)),
                  pl.BlockSpec((None, 1, d), lambda b, i: (b, 0, 0)),
                  pl.BlockSpec((None, 1, d), lambda b, i: (b, 0, 0)),
                  resident((d, p)),
                  resident((d, LANES))],
        out_specs=[pl.BlockSpec((tm, p), lambda b, i: (b * nt + i, 0)),
                   pl.BlockSpec((tm, LANES), lambda b, i: (b * nt + i, 0))],
        out_shape=[jax.ShapeDtypeStruct((bsz * s, p), BF16),
                   jax.ShapeDtypeStruct((bsz * s, LANES), F32)],
        compiler_params=_cparams(("arbitrary", "arbitrary")),
        name="inproj",
    )(x2, g, sh, sc, w, wg)


def _log_sigmoid(x):
    return jnp.minimum(x, 0.0) - jnp.log1p(jnp.exp(-jnp.abs(x)))


def _gates_kernel(g_ref, b_ref, gc_ref, gr_ref):
    tm = g_ref.shape[0]
    row = lax.broadcasted_iota(jnp.int32, (tm, LANES), 0)
    lane = lax.broadcasted_iota(jnp.int32, (tm, LANES), 1)
    gp = GATE_SOFT_CAP * jnp.tanh((g_ref[...] + b_ref[...]) / GATE_SOFT_CAP)
    is_f = ((lane >> 2) & 1) == 1
    fwd_lane = lane < 2 * N_HEADS
    lf = jnp.where(is_f, _log_sigmoid(gp), 0.0)
    r2 = lax.broadcasted_iota(jnp.int32, (CHUNK, CHUNK), 0)
    c2 = lax.broadcasted_iota(jnp.int32, (CHUNK, CHUNK), 1)
    lower = (r2 >= c2).astype(F32)
    upper = (r2 <= c2).astype(F32)
    lane_c = lax.broadcasted_iota(jnp.int32, (CHUNK, LANES), 1)
    cums = []
    for c in range(tm // CHUNK):
        lf_c = lf[c * CHUNK:(c + 1) * CHUNK]
        cf = jnp.dot(lower, lf_c, precision=lax.Precision.HIGHEST, preferred_element_type=F32)
        cb = jnp.dot(upper, lf_c, precision=lax.Precision.HIGHEST, preferred_element_type=F32)
        cums.append(jnp.where(lane_c < 2 * N_HEADS, cf, cb))
    cdir = jnp.concatenate(cums, axis=0)
    a = jnp.where(is_f, cdir, gp - pltpu.roll(cdir, LANES - N_HEADS, 1))

    pos = row % CHUNK
    x = a
    k = 1
    while k < CHUNK:
        from_before = jnp.where(pos >= k, pltpu.roll(x, k, 0), -jnp.inf)
        from_after = jnp.where(pos < CHUNK - k, pltpu.roll(x, tm - k, 0), -jnp.inf)
        x = jnp.maximum(x, jnp.where(fwd_lane, from_before, from_after))
        k *= 2
    gc_ref[...] = jnp.where(is_f, a, x)

    lane_1 =lax.broadcasted_iota(jnp.int32, (1, LANES), 1)
    for c in range(tm // CHUNK):
        lo = c * CHUNK
        xc, ac = x[lo:lo + CHUNK], a[lo:lo + CHUNK]
        end_max = jnp.where(lane_1 < 2 * N_HEADS, xc[CHUNK - 1:CHUNK], xc[0:1])
        e = jnp.exp(ac - end_max)
        rows = jnp.where(((lane_c >> 2) & 1) == 1, pltpu.roll(e, N_HEADS, 1), ac)
        gr_ref[:, lo:lo + CHUNK] = rows.T[:N_GATE_COLS, :]


def _gates(gpre, b_if, tm):
    t = gpre.shape[0]
    return pl.pallas_call(
        _gates_kernel,
        grid=(t // tm,),
        in_specs=[pl.BlockSpec((tm, LANES), lambda i: (i, 0)),
                  pl.BlockSpec((1, LANES), lambda i: (0, 0))],
        out_specs=[pl.BlockSpec((tm, LANES), lambda i: (i, 0)),
                   pl.BlockSpec((N_GATE_COLS, tm), lambda i: (0, i))],
        out_shape=[jax.ShapeDtypeStruct((t, LANES), F32),
                   jax.ShapeDtypeStruct((N_GATE_COLS, t), F32)],
        compiler_params=_cparams(("arbitrary",)),
        name="gates",
    )(gpre, b_if)


DVX = DV + LANES
MLSTM_CHUNKS_PER_STEP = 2


def _mlstm_chunk(q, k, v_ext, rmax_col, b_col, r_row, e_row, b_last, rmax_last, mask, cx, m_st):
    scale = DK ** -0.5
    mb = jnp.maximum(m_st, jnp.broadcast_to(rmax_col, (CHUNK, CHUNK)))
    w_intra = jnp.exp(jnp.where(mask, r_row - mb, -jnp.inf))
    w_state = jnp.exp(m_st - mb)
    qk = lax.dot_general(q, k, (((1,), (1,)), ((), ())), preferred_element_type=F32)
    s = qk * (w_intra * scale)
    lhs = jnp.concatenate([s.astype(BF16), (q.astype(F32) * (w_state * scale)).astype(BF16)], axis=1)
    rhs = jnp.concatenate([v_ext, cx.astype(BF16)], axis=0)
    nx = jnp.dot(lhs, rhs, preferred_element_type=F32)
    denom = jnp.maximum(jnp.abs(nx[:, DV:]), jnp.exp(-(jnp.broadcast_to(b_col, (CHUNK, CHUNK)) + mb)))
    h = nx[:, :DV] / jnp.concatenate([denom, denom], axis=1)
    ke_t = (k.T.astype(F32) * e_row).astype(BF16)
    c_loc = jnp.dot(ke_t, v_ext, preferred_element_type=F32)
    m_loc = b_last + rmax_last
    m_new = jnp.maximum(b_last + m_st, m_loc)
    return h, jnp.exp(b_last + m_st - m_new) * cx + jnp.exp(m_loc - m_new) * c_loc, m_new


def _mlstm_kernel(qf_ref, kf_ref, vf_ref, gcf_ref, grf_ref, qb_ref, kb_ref, vb_ref, gcb_ref, grb_ref,
                  c0_ref, m0_ref, hf_ref, hb_ref, cout_ref, mout_ref, m_scr, *c_scrs):
    c = pl.program_id(1)

    @pl.when(c == 0)
    def _():
        for idx, c_scr in enumerate(c_scrs):
            c_scr[...] = c0_ref[idx]
        m_scr[...] = m0_ref[...]

    row = lax.broadcasted_iota(jnp.int32, (CHUNK, CHUNK), 0)
    col = lax.broadcasted_iota(jnp.int32, (CHUNK, CHUNK), 1)
    ones = jnp.ones((CHUNK, LANES), BF16)
    m_all = m_scr[...]
    dirs = ((qf_ref, kf_ref, vf_ref, gcf_ref, grf_ref, hf_ref, 0, CHUNK - 1, col <= row),
            (qb_ref, kb_ref, vb_ref, gcb_ref, grb_ref, hb_ref, 2 * N_HEADS, 0, col >= row))
    n_sub = qf_ref.shape[0] // CHUNK
    m_news = []
    for di, (q_ref, k_ref, v_ref, gc_ref, gr_ref, h_ref, off, last, mask) in enumerate(dirs):
        order = range(n_sub) if di == 0 else range(n_sub - 1, -1, -1)
        for hd in range(N_HEADS):
            idx = di * N_HEADS + hd
            lr, lb = off + hd, off + N_HEADS + hd
            cx, m_st = c_scrs[idx][...], m_all[idx][:, 0:1]
            for sub in order:
                r0 = sub * CHUNK
                rows = slice(r0, r0 + CHUNK)
                v_ext = jnp.concatenate([v_ref[rows, hd * DV:(hd + 1) * DV], ones], axis=1)
                h, cx, m_st = _mlstm_chunk(
                    q_ref[rows, hd * DK:(hd + 1) * DK], k_ref[rows, hd * DK:(hd + 1) * DK], v_ext,
                    gc_ref[rows, lr:lr + 1], gc_ref[rows, lb:lb + 1],
                    gr_ref[lr:lr + 1, rows], gr_ref[lb:lb + 1, rows],
                    gc_ref[r0 + last:r0 + last + 1, lb:lb + 1], gc_ref[r0 + last:r0 + last + 1, lr:lr + 1],
                    mask, cx, m_st)
                h_ref[rows, hd * DV:(hd + 1) * DV] = h
            c_scrs[idx][...] = cx
            m_news.append(jnp.broadcast_to(m_st, (1, LANES)))
    for idx, m_new in enumerate(m_news):
        m_scr[idx] = m_new

    @pl.when(c == pl.num_programs(1) - 1)
    def _():
        for idx, c_scr in enumerate(c_scrs):
            cout_ref[idx] = c_scr[...]
        mout_ref[...] = m_scr[...]


def _mlstm(proj, gcol, grow, bsz, s, c0, m0):
    rows = MLSTM_CHUNKS_PER_STEP * CHUNK
    nc = s // rows
    t = bsz * s
    fwd = lambda b, c: b * nc + c
    bwd = lambda b, c: b * nc + (nc - 1 - c)

    def specs(ci):
        return [pl.BlockSpec((rows, QK_COLS), lambda b, c: (ci(b, c), 0)),
                pl.BlockSpec((rows, QK_COLS), lambda b, c: (ci(b, c), 1)),
                pl.BlockSpec((rows, MLSTM_WIDTH), lambda b, c: (ci(b, c), 1)),
                pl.BlockSpec((rows, LANES), lambda b, c: (ci(b, c), 0)),
                pl.BlockSpec((N_GATE_COLS, rows), lambda b, c: (0, ci(b, c)))]

    st_specs = [pl.BlockSpec((None, 2 * N_HEADS, DK, DVX), lambda b, c: (b, 0, 0, 0)),
                pl.BlockSpec((None, 2 * N_HEADS, 1, LANES), lambda b, c: (b, 0, 0, 0))]
    return pl.pallas_call(
        _mlstm_kernel,
        grid=(bsz, nc),
        in_specs=specs(fwd) + specs(bwd) + st_specs,
        out_specs=[pl.BlockSpec((rows, MLSTM_WIDTH), lambda b, c: (fwd(b, c), 0)),
                   pl.BlockSpec((rows, MLSTM_WIDTH), lambda b, c: (bwd(b, c), 0))] + st_specs,
        out_shape=[jax.ShapeDtypeStruct((t, MLSTM_WIDTH), F32),
                   jax.ShapeDtypeStruct((t, MLSTM_WIDTH), F32),
                   jax.ShapeDtypeStruct(c0.shape, F32),
                   jax.ShapeDtypeStruct(m0.shape, F32)],
        scratch_shapes=[pltpu.VMEM((2 * N_HEADS, 1, LANES), F32)]
        + [pltpu.VMEM((DK, DVX), F32) for _ in range(2 * N_HEADS)],
        compiler_params=_cparams(("arbitrary", "arbitrary")),
        name="mlstm",
    )(proj, proj, proj, gcol, grow, proj, proj, proj, gcol, grow, c0, m0)


MIX_ROWS = 256


def _mixout_kernel(o_ref, cb_ref, cc_ref, cx_ref, ccp_ref, cxp_ref, ccn_ref, cxn_ref, hf_ref, hb_ref, x_ref,
                   cw_ref, ng_ref, wout_ref, gt_ref, gffn_ref, shf_ref, scf_ref, wr_ref, br_ref,
                   x1_ref, xn2_ref, idx_ref, tw_ref):
    i = pl.program_id(1)
    tm = x_ref.shape[0]
    cw = cw_ref[...]

    has_prev = jnp.where(i > 0, 1.0, 0.0)
    has_next = jnp.where(i < pl.num_programs(1) - 1, 1.0, 0.0)
    up = ccp_ref[...].astype(F32) * cxp_ref[...].astype(F32) * has_prev
    un = ccn_ref[...].astype(F32) * cxn_ref[...].astype(F32) * has_next
    uv = cc_ref[:, CONV_HALF:].astype(F32) * cx_ref[:, CONV_HALF:].astype(F32)
    ext = jnp.concatenate([up, uv, un], axis=0)

    pos = lax.broadcasted_iota(jnp.int32, (MIX_ROWS, CONV_HALF), 0) & (GRID_W - 1)
    lane_f = lax.broadcasted_iota(jnp.int32, (MIX_ROWS, LANES), 1).astype(F32)
    lane4 = lax.broadcasted_iota(jnp.int32, (MIX_ROWS, TOP_K), 1)

    for r0 in range(0, tm, MIX_ROWS):
        rows = slice(r0, r0 + MIX_ROWS)

        uh = cc_ref[rows, :CONV_HALF].astype(F32) * cx_ref[rows, :CONV_HALF].astype(F32)
        left = jnp.where(pos == 0, 0.0, pltpu.roll(uh, 1, 0))
        right = jnp.where(pos == GRID_W - 1, 0.0, pltpu.roll(uh, MIX_ROWS - 1, 0))
        yh = cw[0:1, :CONV_HALF] * left + cw[1:2, :CONV_HALF] * uh + cw[2:3, :CONV_HALF] * right
        yv = (cw[0:1, CONV_HALF:] * ext[r0:r0 + MIX_ROWS]
              + cw[1:2, CONV_HALF:] * ext[r0 + GRID_W:r0 + GRID_W + MIX_ROWS]
              + cw[2:3, CONV_HALF:] * ext[r0 + 2 * GRID_W:r0 + 2 * GRID_W + MIX_ROWS])
        yc = cb_ref[rows, :].astype(F32) * jnp.concatenate([yh, yv], axis=1)

        hs = hf_ref[rows, :] + hb_ref[rows, :]
        parts = []
        for hd in range(N_HEADS):
            seg = hs[:, hd * DV:(hd + 1) * DV]
            parts.append(seg * lax.rsqrt(jnp.mean(seg * seg, axis=-1, keepdims=True) + EPS))
        hm = jnp.concatenate(parts, axis=1) * ng_ref[...] * jax.nn.sigmoid(o_ref[rows, :].astype(F32))

        z = jnp.concatenate([hm.astype(BF16), yc.astype(BF16)], axis=1)
        x1 = x_ref[rows, :] + gt_ref[...] * jnp.dot(z, wout_ref[...], preferred_element_type=F32)
        x1_ref[rows, :] = x1

        y = x1 * lax.rsqrt(jnp.mean(x1 * x1, axis=-1, keepdims=True) + EPS) * gffn_ref[...]
        xn2 = y * (1.0 + scf_ref[...]) + shf_ref[...]
        _store_slab_rows(xn2_ref, r0, _pack_pairs(xn2))

        logits = jnp.dot(xn2.astype(BF16), wr_ref[...], preferred_element_type=F32) + br_ref[...]
        vals, idxs = [], []
        for _ in range(TOP_K):
            mx = jnp.max(logits, axis=-1, keepdims=True)
            ik = jnp.min(jnp.where(logits == mx, lane_f, float(LANES)), axis=-1, keepdims=True)
            vals.append(mx)
            idxs.append(ik)
            logits = jnp.where(lane_f == ik, -jnp.inf, logits)
        es = [jnp.exp(v - vals[0]) for v in vals]
        tot = es[0] + es[1] + es[2] + es[3]
        idx_out = jnp.zeros((MIX_ROWS, TOP_K), F32)
        tw_out = jnp.zeros((MIX_ROWS, TOP_K), F32)
        for kk in range(TOP_K):
            idx_out = jnp.where(lane4 == kk, idxs[kk], idx_out)
            tw_out = jnp.where(lane4 == kk, es[kk] / tot, tw_out)
        idx_ref[rows, :] = idx_out.astype(jnp.int32)
        tw_ref[rows, :] = tw_out


def _mixout(proj, hf, hb, x2, conv_w, norm_g, w_out, gt, g_ffn, sh_f, sc_f, w_r, b_r, bsz, s, tm):
    t, d = x2.shape
    nt = s // tm
    rb = tm // GRID_W
    last_rb = t // GRID_W - 1
    row = lambda b, i: b * nt + i
    w = MLSTM_WIDTH
    vec = lambda n: pl.BlockSpec((1, n), lambda b, i: (0, 0))
    per_b = pl.BlockSpec((None, 1, d), lambda b, i: (b, 0, 0))
    halo_prev = lambda cblk: pl.BlockSpec(
        (GRID_W, CONV_HALF), lambda b, i: (jnp.maximum(row(b, i) * rb - 1, 0), cblk))
    halo_next = lambda cblk: pl.BlockSpec(
        (GRID_W, CONV_HALF), lambda b, i: (jnp.minimum((row(b, i) + 1) * rb, last_rb), cblk))
    return pl.pallas_call(
        _mixout_kernel,
        grid=(bsz, nt),
        in_specs=[pl.BlockSpec((tm, w), lambda b, i: (row(b, i), 2)),
                  pl.BlockSpec((tm, w), lambda b, i: (row(b, i), 3)),
                  pl.BlockSpec((tm, w), lambda b, i: (row(b, i), 4)),
                  pl.BlockSpec((tm, w), lambda b, i: (row(b, i), 5)),
                  halo_prev(9), halo_prev(11), halo_next(9), halo_next(11),
                  pl.BlockSpec((tm, w), lambda b, i: (row(b, i), 0)),
                  pl.BlockSpec((tm, w), lambda b, i: (row(b, i), 0)),
                  pl.BlockSpec((tm, d), lambda b, i: (row(b, i), 0)),
                  pl.BlockSpec((3, CONV_WIDTH), lambda b, i: (0, 0)),
                  vec(w),
                  pl.BlockSpec((d, d), lambda b, i: (0, 0)),
                  per_b, vec(d), per_b, per_b,
                  pl.BlockSpec((d, LANES), lambda b, i: (0, 0)),
                  vec(LANES)],
        out_specs=[pl.BlockSpec((tm, d), lambda b, i: (row(b, i), 0)),
                   pl.BlockSpec((tm * SLAB, LANES), lambda b, i: (row(b, i), 0)),
                   pl.BlockSpec((tm, TOP_K), lambda b, i: (row(b, i), 0)),
                   pl.BlockSpec((tm, TOP_K), lambda b, i: (row(b, i), 0))],
        out_shape=[jax.ShapeDtypeStruct((t, d), F32),
                   jax.ShapeDtypeStruct((t * SLAB, LANES), jnp.uint32),
                   jax.ShapeDtypeStruct((t, TOP_K), jnp.int32),
                   jax.ShapeDtypeStruct((t, TOP_K), F32)],
        compiler_params=_cparams(("arbitrary", "arbitrary")),
        name="mixout",
    )(proj, proj, proj, proj, proj, proj, proj, proj, hf, hb, x2,
      conv_w, norm_g, w_out, gt, g_ffn, sh_f, sc_f, w_r, b_r)


def _rank_kernel(idx_ref, rank_ref, cnt_ref, run_scr):
    @pl.when(pl.program_id(0) == 0)
    def _():
        run_scr[...] = jnp.zeros_like(run_scr)

    tm = idx_ref.shape[0]
    idx = idx_ref[...]
    lane = lax.broadcasted_iota(jnp.int32, (tm, LANES), 1)
    hits = [lane == idx[:, kk:kk + 1] for kk in range(TOP_K)]
    onehot = jnp.zeros((tm, LANES), F32)
    for hit in hits:
        onehot = onehot + hit.astype(F32)
    r = lax.broadcasted_iota(jnp.int32, (tm, tm), 0)
    c = lax.broadcasted_iota(jnp.int32, (tm, tm), 1)
    before = jnp.dot((c < r).astype(BF16), onehot.astype(BF16), preferred_element_type=F32) + run_scr[...]
    lane4 = lax.broadcasted_iota(jnp.int32, (tm, TOP_K), 1)
    rank = jnp.zeros((tm, TOP_K), F32)
    for kk, hit in enumerate(hits):
        rk = jnp.sum(jnp.where(hit, before, 0.0), axis=-1, keepdims=True)
        rank = jnp.where(lane4 == kk, rk, rank)
    rank_ref[...] = rank.astype(jnp.int32)
    run_scr[...] = run_scr[...] + jnp.sum(onehot, axis=0, keepdims=True)
    cnt_ref[...] = run_scr[...]


def _rank(idx, tm):
    t = idx.shape[0]
    return pl.pallas_call(
        _rank_kernel,
        grid=(t // tm,),
        in_specs=[pl.BlockSpec((tm, TOP_K), lambda i: (i, 0))],
        out_specs=[pl.BlockSpec((tm, TOP_K), lambda i: (i, 0)),
                   pl.BlockSpec((1, LANES), lambda i: (0, 0))],
        out_shape=[jax.ShapeDtypeStruct((t, TOP_K), jnp.int32),
                   jax.ShapeDtypeStruct((1, LANES), F32)],
        scratch_shapes=[pltpu.VMEM((1, LANES), F32)],
        compiler_params=_cparams(("arbitrary",)),
        name="rank",
    )(idx)


def _largest_pad_piece():
    return 1 << ((ROW_BLOCK - 1).bit_length() - 1)


def _dispatch_kernel(pad_ref, dest_hbm, xn_ref, xs_hbm, dsm, zeros_scr, sem_idx, sem_rows, sem_pad):
    i = pl.program_id(0)
    tm = xn_ref.shape[0] // SLAB
    n_idx = tm * TOP_K

    def idx_copy(step):
        slot = step % 2
        return pltpu.make_async_copy(dest_hbm.at[pl.ds(step * n_idx, n_idx)], dsm.at[slot], sem_idx.at[slot])

    @pl.when(i == 0)
    def _():
        idx_copy(0).start()

    @pl.when(i + 1 < pl.num_programs(0))
    def _():
        idx_copy(i + 1).start()

    def slab(ref, row, n_rows=1):
        return ref.at[pl.ds(pl.multiple_of(row * SLAB, SLAB), n_rows * SLAB), :]

    def for_each_pad_piece(fn):
        def per_expert(e, carry):
            off = pad_ref[2 * e]
            n = pad_ref[2 * e + 1]
            size = _largest_pad_piece()
            while size >= 1:
                take = (n & size) != 0

                @pl.when(take)
                def _(off=off, size=size):
                    fn(pltpu.make_async_copy(slab(zeros_scr, 0, size), slab(xs_hbm, off, size), sem_pad))

                off = off + jnp.where(take, size, 0)
                size //= 2
            return carry
        lax.fori_loop(0, N_EXPERTS, per_expert, 0)

    @pl.when(i == 0)
    def _():
        zeros_scr[...] = jnp.zeros_like(zeros_scr)
        for_each_pad_piece(lambda cp: cp.start())
        for_each_pad_piece(lambda cp: cp.wait())

    idx_copy(i).wait()
    slot = i % 2

    def row_copy(t, kk):
        return pltpu.make_async_copy(slab(xn_ref, t), slab(xs_hbm, dsm[slot, t * TOP_K + kk]), sem_rows)

    def issue(t, carry):
        for kk in range(TOP_K):
            row_copy(t, kk).start(priority=kk % 2)
        return carry

    lax.fori_loop(0, tm, issue, 0, unroll=ISSUE_UNROLL)
    pltpu.make_async_copy(slab(xs_hbm, 0, n_idx), slab(xs_hbm, 0, n_idx), sem_rows).wait()


def _dispatch(pad_info, dest_flat, xn2, n_rows, tm):
    t = xn2.shape[0] // SLAB
    return pl.pallas_call(
        _dispatch_kernel,
        grid_spec=pltpu.PrefetchScalarGridSpec(
            num_scalar_prefetch=1,
            grid=(t // tm,),
            in_specs=[pl.BlockSpec(memory_space=pl.ANY),
                      pl.BlockSpec((tm * SLAB, LANES), lambda i, pad: (i, 0))],
            out_specs=pl.BlockSpec(memory_space=pl.ANY),
            scratch_shapes=[pltpu.SMEM((2, tm * TOP_K), jnp.int32),
                            pltpu.VMEM((_largest_pad_piece() * SLAB, LANES), xn2.dtype),
                            pltpu.SemaphoreType.DMA((2,)),
                            pltpu.SemaphoreType.DMA(()),
                            pltpu.SemaphoreType.DMA(())]),
        out_shape=jax.ShapeDtypeStruct((n_rows * SLAB, LANES), xn2.dtype),
        compiler_params=_cparams(("arbitrary",)),
        name="dispatch",
    )(pad_info, dest_flat, xn2)


def _new_expert(be_ref, j):
    return jnp.logical_or(j == 0, be_ref[j] != be_ref[jnp.maximum(j - 1, 0)])


GU_COLS = 512


def _full_or_half_block(valid, rows, compute):
    @pl.when(valid > rows // 2)
    def _():
        compute(rows)

    @pl.when(valid <= rows // 2)
    def _():
        compute(rows // 2)


def _expert_gu_kernel(be_ref, nu_ref, nxt_ref, bv_ref, xs_ref, w_hbm, bg_ref, bu_ref, act_ref,
                      stage_g, stage_u, wg_scr, wu_scr, sem):
    n = pl.program_id(0)
    j = pl.program_id(1)
    nt = pl.num_programs(0)
    tn = wg_scr.shape[1]

    def weight_copies(e, nn):
        col_g = pl.multiple_of(nn * tn, tn)
        col_u = pl.multiple_of((nt + nn) * tn, tn)
        return (pltpu.make_async_copy(w_hbm.at[e, :, pl.ds(col_g, tn)], stage_g, sem.at[0]),
                pltpu.make_async_copy(w_hbm.at[e, :, pl.ds(col_u, tn)], stage_u, sem.at[1]))

    @pl.when(j < nu_ref[0])
    def _():
        e = be_ref[j]

        @pl.when(_new_expert(be_ref, j))
        def _():
            @pl.when(jnp.logical_and(n == 0, j == 0))
            def _():
                for cp in weight_copies(e, n):
                    cp.start()

            for cp in weight_copies(e, n):
                cp.wait()
            wg_scr[...] = stage_g[...].astype(BF16)
            wu_scr[...] = stage_u[...].astype(BF16)

            e_next = nxt_ref[e]
            in_pass = e_next >= 0

            @pl.when(jnp.logical_or(in_pass, n + 1 < nt))
            def _():
                for cp in weight_copies(jnp.where(in_pass, e_next, be_ref[0]), jnp.where(in_pass, n, n + 1)):
                    cp.start()

        def compute(rows):
            x = _unpack_pairs(_load_slab_rows(xs_ref, rows), SLAB * LANES).astype(BF16)
            for c0 in range(0, tn, GU_COLS):
                cols = slice(c0, c0 + GU_COLS)
                g = jnp.dot(x, wg_scr[:, cols], preferred_element_type=F32) + bg_ref[:, cols]
                u = jnp.dot(x, wu_scr[:, cols], preferred_element_type=F32) + bu_ref[:, cols]
                gate = jnp.minimum(g, SWIGLU_LIMIT)
                up = jnp.clip(u, -SWIGLU_LIMIT, SWIGLU_LIMIT)
                act_ref[0:rows, cols] = ((up + 1.0) * gate * jax.nn.sigmoid(SWIGLU_ALPHA * gate)).astype(BF16)

        _full_or_half_block(bv_ref[j], xs_ref.shape[0] // SLAB, compute)


def _expert_gu(blk_expert, n_used, nxt_expert, blk_valid, xs, w_gu, b_gu, tn):
    n_rows = xs.shape[0] // SLAB
    d = w_gu.shape[1]
    dff = w_gu.shape[2] // 2
    nt = dff // tn
    nb = n_rows // ROW_BLOCK
    blk = lambda j, nu: jnp.minimum(j, nu[0] - 1)
    exp = lambda j, be, nu: be[blk(j, nu)]
    return pl.pallas_call(
        _expert_gu_kernel,
        grid_spec=pltpu.PrefetchScalarGridSpec(
            num_scalar_prefetch=4,
            grid=(nt, nb),
            in_specs=[pl.BlockSpec((ROW_BLOCK * SLAB, LANES), lambda n, j, be, nu, nx, bv: (blk(j, nu), 0)),
                      pl.BlockSpec(memory_space=pl.ANY),
                      pl.BlockSpec((None, 1, tn), lambda n, j, be, nu, nx, bv: (exp(j, be, nu), 0, n)),
                      pl.BlockSpec((None, 1, tn), lambda n, j, be, nu, nx, bv: (exp(j, be, nu), 0, nt + n))],
            out_specs=pl.BlockSpec((ROW_BLOCK, tn), lambda n, j, be, nu, nx, bv: (blk(j, nu), n)),
            scratch_shapes=[pltpu.VMEM((d, tn), F32), pltpu.VMEM((d, tn), F32),
                            pltpu.VMEM((d, tn), BF16), pltpu.VMEM((d, tn), BF16),
                            pltpu.SemaphoreType.DMA((2,))]),
        out_shape=jax.ShapeDtypeStruct((n_rows, dff), BF16),
        compiler_params=_cparams(("arbitrary", "arbitrary")),
        name="expert_gu",
    )(blk_expert, n_used, nxt_expert, blk_valid, xs, w_gu, b_gu, b_gu)


def _expert_down_kernel(be_ref, nu_ref, nxt_ref, bv_ref, act_ref, w_hbm, b_ref, y_ref, stage, w_scr, sem):
    j = pl.program_id(0)

    def weight_copy(e):
        return pltpu.make_async_copy(w_hbm.at[e], stage, sem)

    @pl.when(j < nu_ref[0])
    def _():
        e = be_ref[j]

        @pl.when(_new_expert(be_ref, j))
        def _():
            @pl.when(j == 0)
            def _():
                weight_copy(e).start()

            weight_copy(e).wait()
            w_scr[...] = stage[...].astype(BF16)
            e_next = nxt_ref[e]

            @pl.when(e_next >= 0)
            def _():
                weight_copy(e_next).start()

        def compute(rows):
            act = act_ref[0:rows, :]
            for c0 in range(0, w_scr.shape[1], DOWN_COLS):
                cols = slice(c0, c0 + DOWN_COLS)
                y = _pack_pairs(jnp.dot(act, w_scr[:, cols], preferred_element_type=F32) + b_ref[:, cols])
                for q in range(y.shape[1] // LANES):
                    chunk = c0 // 2 // LANES + q
                    y_ref[pl.ds(chunk, rows, stride=SLAB), :] = y[:, q * LANES:(q + 1) * LANES]

        _full_or_half_block(bv_ref[j], act_ref.shape[0], compute)


def _expert_down(blk_expert, n_used, nxt_expert, blk_valid, act, w_down, b_down):
    n_rows, dff = act.shape
    d = w_down.shape[2]
    nb = n_rows // ROW_BLOCK
    blk = lambda j, nu: jnp.minimum(j, nu[0] - 1)
    exp = lambda j, be, nu: be[blk(j, nu)]
    return pl.pallas_call(
        _expert_down_kernel,
        grid_spec=pltpu.PrefetchScalarGridSpec(
            num_scalar_prefetch=4,
            grid=(nb,),
            in_specs=[pl.BlockSpec((ROW_BLOCK, dff), lambda j, be, nu, nx, bv: (blk(j, nu), 0)),
                      pl.BlockSpec(memory_space=pl.ANY),
                      pl.BlockSpec((None, 1, d), lambda j, be, nu, nx, bv: (exp(j, be, nu), 0, 0))],
            out_specs=pl.BlockSpec((ROW_BLOCK * SLAB, LANES), lambda j, be, nu, nx, bv: (blk(j, nu), 0)),
            scratch_shapes=[pltpu.VMEM((dff, d), F32), pltpu.VMEM((dff, d), BF16),
                            pltpu.SemaphoreType.DMA(())]),
        out_shape=jax.ShapeDtypeStruct((n_rows * SLAB, LANES), jnp.uint32),
        compiler_params=_cparams(("arbitrary",)),
        name="expert_down",
    )(blk_expert, n_used, nxt_expert, blk_valid, act, w_down, b_down)


def _combine_kernel(dest_hbm, y_hbm, x1_ref, tw_ref, gt_ref, gfin_ref, out_ref, dsm, buf, sem_idx, sem_rows):
    i = pl.program_id(1) + pl.program_id(0) * pl.num_programs(1)
    n_steps = pl.num_programs(0) * pl.num_programs(1)
    tm = x1_ref.shape[0]
    n_idx = tm * TOP_K

    def slab(ref, row, n_rows=1):
        return ref.at[pl.ds(pl.multiple_of(row * SLAB, SLAB), n_rows * SLAB), :]

    def idx_copy(tile):
        slot = tile % 2
        return pltpu.make_async_copy(dest_hbm.at[pl.ds(tile * n_idx, n_idx)], dsm.at[slot], sem_idx.at[slot])

    def issue_rows(tile):
        slot = tile % 2

        def issue(t, carry):
            for kk in range(TOP_K):
                pltpu.make_async_copy(slab(y_hbm, dsm[slot, t * TOP_K + kk]), slab(buf.at[slot, kk], t),
                                      sem_rows.at[slot]).start(priority=kk % 2)
            return carry

        lax.fori_loop(0, tm, issue, 0, unroll=ISSUE_UNROLL)

    @pl.when(i == 0)
    def _():
        idx_copy(0).start()
        idx_copy(0).wait()
        issue_rows(0)

        @pl.when(n_steps > 1)
        def _():
            idx_copy(1).start()

    @pl.when(i + 2 < n_steps)
    def _():
        idx_copy(i + 2).start()

    @pl.when(i + 1 < n_steps)
    def _():
        idx_copy(i + 1).wait()
        issue_rows(i + 1)

    slot = i % 2
    for kk in range(TOP_K):
        pltpu.make_async_copy(slab(y_hbm, 0, tm), buf.at[slot, kk], sem_rows.at[slot]).wait()

    tw = tw_ref[...]
    rows = lambda kk: _unpack_pairs(_load_slab_rows(buf.at[slot, kk], tm), DOWN_COLS // 2)
    acc = rows(0) * tw[:, 0:1]
    for kk in range(1, TOP_K):
        acc = acc + rows(kk) * tw[:, kk:kk + 1]
    x2 = x1_ref[...] + gt_ref[...] * acc
    out_ref[...] = x2 * lax.rsqrt(jnp.mean(x2 * x2, axis=-1, keepdims=True) + EPS) * gfin_ref[...]


def _combine(dest_flat, y, x1, tw, gt, g_final, bsz, s, tm):
    t, d = x1.shape
    nt = s // tm
    row = lambda b, i: (b * nt + i, 0)
    return pl.pallas_call(
        _combine_kernel,
        grid=(bsz, nt),
        in_specs=[pl.BlockSpec(memory_space=pl.ANY),
                  pl.BlockSpec(memory_space=pl.ANY),
                  pl.BlockSpec((tm, d), row),
                  pl.BlockSpec((tm, TOP_K), row),
                  pl.BlockSpec((None, 1, d), lambda b, i: (b, 0, 0)),
                  pl.BlockSpec((1, d), lambda b, i: (0, 0))],
        out_specs=pl.BlockSpec((tm, d), row),
        out_shape=jax.ShapeDtypeStruct((t, d), F32),
        scratch_shapes=[pltpu.SMEM((2, tm * TOP_K), jnp.int32),
                        pltpu.VMEM((2, TOP_K, tm * SLAB, LANES), y.dtype),
                        pltpu.SemaphoreType.DMA((2,)),
                        pltpu.SemaphoreType.DMA((2,))],
        compiler_params=_cparams(("arbitrary", "arbitrary")),
        name="combine",
    )(dest_flat, y, x1, tw, gt, g_final)


def _pad_lanes(a, value=0.0):
    return jnp.pad(a, ((0, 0), (0, LANES - a.shape[1])), constant_values=value)


def _routing_tables(idx, rank, counts_f, n_blocks):
    counts = counts_f[0, :N_EXPERTS].astype(jnp.int32)
    padded = (counts + ROW_BLOCK - 1) // ROW_BLOCK * ROW_BLOCK
    pend = jnp.cumsum(padded)
    pstart = pend - padded
    dest = (pstart[idx] + rank).reshape(-1)
    blk_start = jnp.arange(n_blocks, dtype=jnp.int32) * ROW_BLOCK
    blk_expert = jnp.minimum(jnp.sum((pend[None, :] <= blk_start[:, None]).astype(jnp.int32), axis=1),
                             N_EXPERTS - 1)
    n_used = (pend[-1:] // ROW_BLOCK).astype(jnp.int32)
    blk_valid = jnp.clip((pstart + counts)[blk_expert] - blk_start, 0, ROW_BLOCK).astype(jnp.int32)
    pad_info = jnp.stack([pstart + counts, padded - counts], axis=1).reshape(-1).astype(jnp.int32)
    ids = jnp.arange(N_EXPERTS, dtype=jnp.int32)
    later = jnp.where((ids[None, :] > ids[:, None]) & (counts[None, :] > 0), ids[None, :], N_EXPERTS)
    nxt = jnp.min(later, axis=1)
    nxt_expert = jnp.where(nxt == N_EXPERTS, -1, nxt).astype(jnp.int32)
    return dest, blk_expert, n_used, nxt_expert, blk_valid, pad_info


def _layer(x, c, ctx, c_ctx, w_ada, b_ada, g_mix, w_in, b_if, conv_w, norm_g, w_out,
           g_ffn, w_router, b_router, w_gu, b_gu, w_down, b_down, g_final):
    bsz, s, d = x.shape
    s_ctx = ctx.shape[1]

    cond = jnp.zeros((8, d), F32).at[:bsz].set(c).at[bsz].set(c_ctx)
    mod = _adaln(cond, w_ada, b_ada[None, :])
    sh_m, sc_m, gt_m, sh_f, sc_f, gt_f = [m[:, None, :] for m in jnp.split(mod, N_MOD, axis=-1)]
    lat = lambda m: m[:bsz]
    ctxm = lambda m: jnp.broadcast_to(m[bsz:bsz + 1], (bsz, 1, d))

    g0 = 2 * QK_COLS + 2 * MLSTM_WIDTH
    w_main = jnp.concatenate([w_in[:, :g0], w_in[:, g0 + N_GATE_COLS:]], axis=1).astype(BF16)
    w_gate = _pad_lanes(w_in[:, g0:g0 + N_GATE_COLS]).astype(BF16)
    b_gate = _pad_lanes(b_if[None, :])
    g_mix2 = g_mix[None, :]

    proj_c, gpre_c = _inproj(ctx, g_mix2, ctxm(sh_m), ctxm(sc_m), w_main, w_gate, min(s_ctx, 512))
    gcol_c, grow_c = _gates(gpre_c, b_gate, 512)
    zeros_state = (jnp.zeros((bsz, 2 * N_HEADS, DK, DVX), F32),
                   jnp.zeros((bsz, 2 * N_HEADS, 1, LANES), F32))
    _, _, c0, m0 = _mlstm(proj_c, gcol_c, grow_c, bsz, s_ctx, *zeros_state)

    proj, gpre = _inproj(x, g_mix2, lat(sh_m), lat(sc_m), w_main, w_gate, 512)
    gcol, grow = _gates(gpre, b_gate, 512)
    hf, hb, _, _ = _mlstm(proj, gcol, grow, bsz, s, c0, m0)
    x1, xn2, idx, tw = _mixout(
        proj, hf, hb, x.reshape(bsz * s, d), conv_w, norm_g[None, :], w_out.astype(BF16), lat(gt_m),
        g_ffn[None, :], lat(sh_f), lat(sc_f), _pad_lanes(w_router).astype(BF16),
        _pad_lanes(b_router[None, :], NEG_BIG), bsz, s, 512)

    t = bsz * s
    n_blocks = -(-(t * TOP_K) // ROW_BLOCK) + N_EXPERTS
    rank, counts = _rank(idx, 512)
    dest, blk_expert, n_used, nxt_expert, blk_valid, pad_info = _routing_tables(idx, rank, counts, n_blocks)
    xs = _dispatch(pad_info, dest, xn2, n_blocks * ROW_BLOCK, 512)
    act = _expert_gu(blk_expert, n_used, nxt_expert, blk_valid, xs, w_gu, b_gu[:, None, :], 1024)
    y = _expert_down(blk_expert, n_used, nxt_expert, blk_valid, act, w_down, b_down[:, None, :])
    out = _combine(dest, y, x1, tw, lat(gt_f), g_final[None, :], bsz, s, 256)
    return out.reshape(bsz, s, d)


def kernel(x, c, ctx, c_ctx, w_ada, b_ada, g_mix, w_in, b_if, conv_w, mlstm_norm_g, w_out,
           g_ffn, w_router, b_router, w_gu, b_gu, w_down, b_down, g_final):
    return _layer(x, c, ctx, c_ctx, w_ada[0], b_ada[0], g_mix[0], w_in[0], b_if[0], conv_w[0],
                  mlstm_norm_g[0], w_out[0], g_ffn[0], w_router[0], b_router[0], w_gu[0], b_gu[0],
                  w_down[0], b_down[0], g_final)
```

```python
import functools

import jax
import jax.numpy as jnp
from jax import lax
from jax.experimental import pallas as pl
from jax.experimental.pallas import tpu as pltpu

F32 = jnp.float32
BF16 = jnp.bfloat16

N_HEADS = 4
DK = 128
DV = 256
QK_COLS = N_HEADS * DK
MLSTM_WIDTH = N_HEADS * DV
CONV_WIDTH = 1024
CONV_HALF = CONV_WIDTH // 2
N_GATE_COLS = 4 * N_HEADS
GRID_W = 64
CHUNK = 128
GATE_SOFT_CAP = 15.0
N_EXPERTS = 32
TOP_K = 4
SWIGLU_LIMIT = 7.0
SWIGLU_ALPHA = 1.702
N_MOD = 6
EPS = 1e-6
LANES = 128
SUBLANES = 8
ROW_BLOCK = 1024
DOWN_COLS = 1024
ISSUE_UNROLL = 8
NEG_BIG = -1e30
VMEM_LIMIT = 56 * 1024 * 1024


def _cparams(sem):
    return pltpu.CompilerParams(dimension_semantics=sem, vmem_limit_bytes=VMEM_LIMIT)


def _pack_pairs(x):
    bits = lax.bitcast_convert_type(x.astype(BF16).astype(F32), jnp.uint32)
    g = x.shape[1] // 2
    return bits[:, :g] | (bits[:, g:] >> 16)


def _unpack_pairs(p, group):
    hi = lax.bitcast_convert_type(p & jnp.uint32(0xFFFF0000), F32)
    lo = lax.bitcast_convert_type(p << 16, F32)
    parts = []
    for g0 in range(0, p.shape[1], group):
        parts += [hi[:, g0:g0 + group], lo[:, g0:g0 + group]]
    return jnp.concatenate(parts, axis=1)


SLAB = 8


def _store_slab_rows(ref, r0, packed):
    rows = packed.shape[0]
    for c in range(SLAB):
        ref[pl.ds(r0 * SLAB + c, rows, stride=SLAB), :] = packed[:, c * LANES:(c + 1) * LANES]


def _load_slab_rows(ref, rows):
    return jnp.concatenate([ref[pl.ds(c, rows, stride=SLAB), :] for c in range(SLAB)], axis=1)


def _adaln_kernel(c_ref, w_ref, b_ref, o_ref):
    s = c_ref[...]
    s = s * jax.nn.sigmoid(s)
    o_ref[...] = jnp.dot(s.astype(BF16), w_ref[...].astype(BF16),
                         preferred_element_type=F32) + b_ref[...]


def _adaln(cond, w, b):
    d, n = w.shape
    tn = 1024
    return pl.pallas_call(
        _adaln_kernel,
        grid=(n // tn,),
        in_specs=[pl.BlockSpec((8, d), lambda j: (0, 0)),
                  pl.BlockSpec((d, tn), lambda j: (0, j)),
                  pl.BlockSpec((1, tn), lambda j: (0, j))],
        out_specs=pl.BlockSpec((8, tn), lambda j: (0, j)),
        out_shape=jax.ShapeDtypeStruct((8, n), F32),
        compiler_params=_cparams(("arbitrary",)),
        name="adaln",
    )(cond, w, b)


INPROJ_COLS = 1024


def _inproj_kernel(x_ref, g_ref, sh_ref, sc_ref, w_ref, wg_ref, proj_ref, gate_ref):
    x = x_ref[...]
    y = x * lax.rsqrt(jnp.mean(x * x, axis=-1, keepdims=True) + EPS) * g_ref[...]
    xn = (y * (1.0 + sc_ref[...]) + sh_ref[...]).astype(BF16)
    gate_ref[...] = jnp.dot(xn, wg_ref[...], preferred_element_type=F32)
    for j in range(w_ref.shape[1] // INPROJ_COLS):
        cols = slice(j * INPROJ_COLS, (j + 1) * INPROJ_COLS)
        proj_ref[:, cols] = jnp.dot(xn, w_ref[:, cols], preferred_element_type=F32).astype(BF16)


def _inproj(x, g, sh, sc, w, wg, tm):
    bsz, s, d = x.shape
    p = w.shape[1]
    nt = s // tm
    x2 = x.reshape(bsz * s, d)
    resident = lambda shape: pl.BlockSpec(shape, lambda b, i: (0, 0), pipeline_mode=pl.Buffered(1))
    return pl.pallas_call(
        _inproj_kernel,
        grid=(bsz, nt),
        in_specs=[pl.BlockSpec((tm, d), lambda b, i: (b * nt + i, 0)),
                  pl.BlockSpec((1, d), lambda b, i: (0, 0)),
                  pl.BlockSpec((None, 1, d), lambda b, i: (b, 0, 0)),
                  pl.BlockSpec((None, 1, d), lambda b, i: (b, 0, 0)),
                  resident((d, p)),
                  resident((d, LANES))],
        out_specs=[pl.BlockSpec((tm, p), lambda b, i: (b * nt + i, 0)),
                   pl.BlockSpec((tm, LANES), lambda b, i: (b * nt + i, 0))],
        out_shape=[jax.ShapeDtypeStruct((bsz * s, p), BF16),
                   jax.ShapeDtypeStruct((bsz * s, LANES), F32)],
        compiler_params=_cparams(("arbitrary", "arbitrary")),
        name="inproj",
    )(x2, g, sh, sc, w, wg)


def _log_sigmoid(x):
    return jnp.minimum(x, 0.0) - jnp.log1p(jnp.exp(-jnp.abs(x)))


def _gates_kernel(g_ref, b_ref, gc_ref, gr_ref):
    tm = g_ref.shape[0]
    row = lax.broadcasted_iota(jnp.int32, (tm, LANES), 0)
    lane = lax.broadcasted_iota(jnp.int32, (tm, LANES), 1)
    gp = GATE_SOFT_CAP * jnp.tanh((g_ref[...] + b_ref[...]) / GATE_SOFT_CAP)
    is_f = ((lane >> 2) & 1) == 1
    fwd_lane = lane < 2 * N_HEADS
    lf = jnp.where(is_f, _log_sigmoid(gp), 0.0)
    r2 = lax.broadcasted_iota(jnp.int32, (CHUNK, CHUNK), 0)
    c2 = lax.broadcasted_iota(jnp.int32, (CHUNK, CHUNK), 1)
    lower = (r2 >= c2).astype(F32)
    upper = (r2 <= c2).astype(F32)
    lane_c = lax.broadcasted_iota(jnp.int32, (CHUNK, LANES), 1)
    cums = []
    for c in range(tm // CHUNK):
        lf_c = lf[c * CHUNK:(c + 1) * CHUNK]
        cf = jnp.dot(lower, lf_c, precision=lax.Precision.HIGHEST, preferred_element_type=F32)
        cb = jnp.dot(upper, lf_c, precision=lax.Precision.HIGHEST, preferred_element_type=F32)
        cums.append(jnp.where(lane_c < 2 * N_HEADS, cf, cb))
    cdir = jnp.concatenate(cums, axis=0)
    a = jnp.where(is_f, cdir, gp - pltpu.roll(cdir, LANES - N_HEADS, 1))

    pos = row % CHUNK
    x = a
    k = 1
    while k < CHUNK:
        from_before = jnp.where(pos >= k, pltpu.roll(x, k, 0), -jnp.inf)
        from_after = jnp.where(pos < CHUNK - k, pltpu.roll(x, tm - k, 0), -jnp.inf)
        x = jnp.maximum(x, jnp.where(fwd_lane, from_before, from_after))
        k *= 2
    gc_ref[...] = jnp.where(is_f, a, x)

    lane_1 =lax.broadcasted_iota(jnp.int32, (1, LANES), 1)
    for c in range(tm // CHUNK):
        lo = c * CHUNK
        xc, ac = x[lo:lo + CHUNK], a[lo:lo + CHUNK]
        end_max = jnp.where(lane_1 < 2 * N_HEADS, xc[CHUNK - 1:CHUNK], xc[0:1])
        e = jnp.exp(ac - end_max)
        rows = jnp.where(((lane_c >> 2) & 1) == 1, pltpu.roll(e, N_HEADS, 1), ac)
        gr_ref[:, lo:lo + CHUNK] = rows.T[:N_GATE_COLS, :]


def _gates(gpre, b_if, tm):
    t = gpre.shape[0]
    return pl.pallas_call(
        _gates_kernel,
        grid=(t // tm,),
        in_specs=[pl.BlockSpec((tm, LANES), lambda i: (i, 0)),
                  pl.BlockSpec((1, LANES), lambda i: (0, 0))],
        out_specs=[pl.BlockSpec((tm, LANES), lambda i: (i, 0)),
                   pl.BlockSpec((N_GATE_COLS, tm), lambda i: (0, i))],
        out_shape=[jax.ShapeDtypeStruct((t, LANES), F32),
                   jax.ShapeDtypeStruct((N_GATE_COLS, t), F32)],
        compiler_params=_cparams(("arbitrary",)),
        name="gates",
    )(gpre, b_if)


DVX = DV + LANES
MLSTM_CHUNKS_PER_STEP = 2


def _mlstm_chunk(q, k, v_ext, rmax_col, b_col, r_row, e_row, b_last, rmax_last, mask, cx, m_st):
    scale = DK ** -0.5
    mb = jnp.maximum(m_st, jnp.broadcast_to(rmax_col, (CHUNK, CHUNK)))
    w_intra = jnp.exp(jnp.where(mask, r_row - mb, -jnp.inf))
    w_state = jnp.exp(m_st - mb)
    qk = lax.dot_general(q, k, (((1,), (1,)), ((), ())), preferred_element_type=F32)
    s = qk * (w_intra * scale)
    lhs = jnp.concatenate([s.astype(BF16), (q.astype(F32) * (w_state * scale)).astype(BF16)], axis=1)
    rhs = jnp.concatenate([v_ext, cx.astype(BF16)], axis=0)
    nx = jnp.dot(lhs, rhs, preferred_element_type=F32)
    denom = jnp.maximum(jnp.abs(nx[:, DV:]), jnp.exp(-(jnp.broadcast_to(b_col, (CHUNK, CHUNK)) + mb)))
    h = nx[:, :DV] / jnp.concatenate([denom, denom], axis=1)
    ke_t = (k.T.astype(F32) * e_row).astype(BF16)
    c_loc = jnp.dot(ke_t, v_ext, preferred_element_type=F32)
    m_loc = b_last + rmax_last
    m_new = jnp.maximum(b_last + m_st, m_loc)
    return h, jnp.exp(b_last + m_st - m_new) * cx + jnp.exp(m_loc - m_new) * c_loc, m_new


def _mlstm_kernel(qf_ref, kf_ref, vf_ref, gcf_ref, grf_ref, qb_ref, kb_ref, vb_ref, gcb_ref, grb_ref,
                  c0_ref, m0_ref, hf_ref, hb_ref, cout_ref, mout_ref, m_scr, *c_scrs):
    c = pl.program_id(1)

    @pl.when(c == 0)
    def _():
        for idx, c_scr in enumerate(c_scrs):
            c_scr[...] = c0_ref[idx]
        m_scr[...] = m0_ref[...]

    row = lax.broadcasted_iota(jnp.int32, (CHUNK, CHUNK), 0)
    col = lax.broadcasted_iota(jnp.int32, (CHUNK, CHUNK), 1)
    ones = jnp.ones((CHUNK, LANES), BF16)
    m_all = m_scr[...]
    dirs = ((qf_ref, kf_ref, vf_ref, gcf_ref, grf_ref, hf_ref, 0, CHUNK - 1, col <= row),
            (qb_ref, kb_ref, vb_ref, gcb_ref, grb_ref, hb_ref, 2 * N_HEADS, 0, col >= row))
    n_sub = qf_ref.shape[0] // CHUNK
    m_news = []
    for di, (q_ref, k_ref, v_ref, gc_ref, gr_ref, h_ref, off, last, mask) in enumerate(dirs):
        order = range(n_sub) if di == 0 else range(n_sub - 1, -1, -1)
        for hd in range(N_HEADS):
            idx = di * N_HEADS + hd
            lr, lb = off + hd, off + N_HEADS + hd
            cx, m_st = c_scrs[idx][...], m_all[idx][:, 0:1]
            for sub in order:
                r0 = sub * CHUNK
                rows = slice(r0, r0 + CHUNK)
                v_ext = jnp.concatenate([v_ref[rows, hd * DV:(hd + 1) * DV], ones], axis=1)
                h, cx, m_st = _mlstm_chunk(
                    q_ref[rows, hd * DK:(hd + 1) * DK], k_ref[rows, hd * DK:(hd + 1) * DK], v_ext,
                    gc_ref[rows, lr:lr + 1], gc_ref[rows, lb:lb + 1],
                    gr_ref[lr:lr + 1, rows], gr_ref[lb:lb + 1, rows],
                    gc_ref[r0 + last:r0 + last + 1, lb:lb + 1], gc_ref[r0 + last:r0 + last + 1, lr:lr + 1],
                    mask, cx, m_st)
                h_ref[rows, hd * DV:(hd + 1) * DV] = h
            c_scrs[idx][...] = cx
            m_news.append(jnp.broadcast_to(m_st, (1, LANES)))
    for idx, m_new in enumerate(m_news):
        m_scr[idx] = m_new

    @pl.when(c == pl.num_programs(1) - 1)
    def _():
        for idx, c_scr in enumerate(c_scrs):
            cout_ref[idx] = c_scr[...]
        mout_ref[...] = m_scr[...]


def _mlstm(proj, gcol, grow, bsz, s, c0, m0):
    rows = MLSTM_CHUNKS_PER_STEP * CHUNK
    nc = s // rows
    t = bsz * s
    fwd = lambda b, c: b * nc + c
    bwd = lambda b, c: b * nc + (nc - 1 - c)

    def specs(ci):
        return [pl.BlockSpec((rows, QK_COLS), lambda b, c: (ci(b, c), 0)),
                pl.BlockSpec((rows, QK_COLS), lambda b, c: (ci(b, c), 1)),
                pl.BlockSpec((rows, MLSTM_WIDTH), lambda b, c: (ci(b, c), 1)),
                pl.BlockSpec((rows, LANES), lambda b, c: (ci(b, c), 0)),
                pl.BlockSpec((N_GATE_COLS, rows), lambda b, c: (0, ci(b, c)))]

    st_specs = [pl.BlockSpec((None, 2 * N_HEADS, DK, DVX), lambda b, c: (b, 0, 0, 0)),
                pl.BlockSpec((None, 2 * N_HEADS, 1, LANES), lambda b, c: (b, 0, 0, 0))]
    return pl.pallas_call(
        _mlstm_kernel,
        grid=(bsz, nc),
        in_specs=specs(fwd) + specs(bwd) + st_specs,
        out_specs=[pl.BlockSpec((rows, MLSTM_WIDTH), lambda b, c: (fwd(b, c), 0)),
                   pl.BlockSpec((rows, MLSTM_WIDTH), lambda b, c: (bwd(b, c), 0))] + st_specs,
        out_shape=[jax.ShapeDtypeStruct((t, MLSTM_WIDTH), F32),
                   jax.ShapeDtypeStruct((t, MLSTM_WIDTH), F32),
                   jax.ShapeDtypeStruct(c0.shape, F32),
                   jax.ShapeDtypeStruct(m0.shape, F32)],
        scratch_shapes=[pltpu.VMEM((2 * N_HEADS, 1, LANES), F32)]
        + [pltpu.VMEM((DK, DVX), F32) for _ in range(2 * N_HEADS)],
        compiler_params=_cparams(("arbitrary", "arbitrary")),
        name="mlstm",
    )(proj, proj, proj, gcol, grow, proj, proj, proj, gcol, grow, c0, m0)


MIX_ROWS = 256


def _mixout_kernel(o_ref, cb_ref, cc_ref, cx_ref, ccp_ref, cxp_ref, ccn_ref, cxn_ref, hf_ref, hb_ref, x_ref,
                   cw_ref, ng_ref, wout_ref, gt_ref, gffn_ref, shf_ref, scf_ref, wr_ref, br_ref,
                   x1_ref, xn2_ref, idx_ref, tw_ref):
    i = pl.program_id(1)
    tm = x_ref.shape[0]
    cw = cw_ref[...]

    has_prev = jnp.where(i > 0, 1.0, 0.0)
    has_next = jnp.where(i < pl.num_programs(1) - 1, 1.0, 0.0)
    up = ccp_ref[...].astype(F32) * cxp_ref[...].astype(F32) * has_prev
    un = ccn_ref[...].astype(F32) * cxn_ref[...].astype(F32) * has_next
    uv = cc_ref[:, CONV_HALF:].astype(F32) * cx_ref[:, CONV_HALF:].astype(F32)
    ext = jnp.concatenate([up, uv, un], axis=0)

    pos = lax.broadcasted_iota(jnp.int32, (MIX_ROWS, CONV_HALF), 0) & (GRID_W - 1)
    lane_f = lax.broadcasted_iota(jnp.int32, (MIX_ROWS, LANES), 1).astype(F32)
    lane4 = lax.broadcasted_iota(jnp.int32, (MIX_ROWS, TOP_K), 1)

    for r0 in range(0, tm, MIX_ROWS):
        rows = slice(r0, r0 + MIX_ROWS)

        uh = cc_ref[rows, :CONV_HALF].astype(F32) * cx_ref[rows, :CONV_HALF].astype(F32)
        left = jnp.where(pos == 0, 0.0, pltpu.roll(uh, 1, 0))
        right = jnp.where(pos == GRID_W - 1, 0.0, pltpu.roll(uh, MIX_ROWS - 1, 0))
        yh = cw[0:1, :CONV_HALF] * left + cw[1:2, :CONV_HALF] * uh + cw[2:3, :CONV_HALF] * right
        yv = (cw[0:1, CONV_HALF:] * ext[r0:r0 + MIX_ROWS]
              + cw[1:2, CONV_HALF:] * ext[r0 + GRID_W:r0 + GRID_W + MIX_ROWS]
              + cw[2:3, CONV_HALF:] * ext[r0 + 2 * GRID_W:r0 + 2 * GRID_W + MIX_ROWS])
        yc = cb_ref[rows, :].astype(F32) * jnp.concatenate([yh, yv], axis=1)

        hs = hf_ref[rows, :] + hb_ref[rows, :]
        parts = []
        for hd in range(N_HEADS):
            seg = hs[:, hd * DV:(hd + 1) * DV]
            parts.append(seg * lax.rsqrt(jnp.mean(seg * seg, axis=-1, keepdims=True) + EPS))
        hm = jnp.concatenate(parts, axis=1) * ng_ref[...] * jax.nn.sigmoid(o_ref[rows, :].astype(F32))

        z = jnp.concatenate([hm.astype(BF16), yc.astype(BF16)], axis=1)
        x1 = x_ref[rows, :] + gt_ref[...] * jnp.dot(z, wout_ref[...], preferred_element_type=F32)
        x1_ref[rows, :] = x1

        y = x1 * lax.rsqrt(jnp.mean(x1 * x1, axis=-1, keepdims=True) + EPS) * gffn_ref[...]
        xn2 = y * (1.0 + scf_ref[...]) + shf_ref[...]
        _store_slab_rows(xn2_ref, r0, _pack_pairs(xn2))

        logits = jnp.dot(xn2.astype(BF16), wr_ref[...], preferred_element_type=F32) + br_ref[...]
        vals, idxs = [], []
        for _ in range(TOP_K):
            mx = jnp.max(logits, axis=-1, keepdims=True)
            ik = jnp.min(jnp.where(logits == mx, lane_f, float(LANES)), axis=-1, keepdims=True)
            vals.append(mx)
            idxs.append(ik)
            logits = jnp.where(lane_f == ik, -jnp.inf, logits)
        es = [jnp.exp(v - vals[0]) for v in vals]
        tot = es[0] + es[1] + es[2] + es[3]
        idx_out = jnp.zeros((MIX_ROWS, TOP_K), F32)
        tw_out = jnp.zeros((MIX_ROWS, TOP_K), F32)
        for kk in range(TOP_K):
            idx_out = jnp.where(lane4 == kk, idxs[kk], idx_out)
            tw_out = jnp.where(lane4 == kk, es[kk] / tot, tw_out)
        idx_ref[rows, :] = idx_out.astype(jnp.int32)
        tw_ref[rows, :] = tw_out


def _mixout(proj, hf, hb, x2, conv_w, norm_g, w_out, gt, g_ffn, sh_f, sc_f, w_r, b_r, bsz, s, tm):
    t, d = x2.shape
    nt = s // tm
    rb = tm // GRID_W
    last_rb = t // GRID_W - 1
    row = lambda b, i: b * nt + i
    w = MLSTM_WIDTH
    vec = lambda n: pl.BlockSpec((1, n), lambda b, i: (0, 0))
    per_b = pl.BlockSpec((None, 1, d), lambda b, i: (b, 0, 0))
    halo_prev = lambda cblk: pl.BlockSpec(
        (GRID_W, CONV_HALF), lambda b, i: (jnp.maximum(row(b, i) * rb - 1, 0), cblk))
    halo_next = lambda cblk: pl.BlockSpec(
        (GRID_W, CONV_HALF), lambda b, i: (jnp.minimum((row(b, i) + 1) * rb, last_rb), cblk))
    return pl.pallas_call(
        _mixout_kernel,
        grid=(bsz, nt),
        in_specs=[pl.BlockSpec((tm, w), lambda b, i: (row(b, i), 2)),
                  pl.BlockSpec((tm, w), lambda b, i: (row(b, i), 3)),
                  pl.BlockSpec((tm, w), lambda b, i: (row(b, i), 4)),
                  pl.BlockSpec((tm, w), lambda b, i: (row(b, i), 5)),
                  halo_prev(9), halo_prev(11), halo_next(9), halo_next(11),
                  pl.BlockSpec((tm, w), lambda b, i: (row(b, i), 0)),
                  pl.BlockSpec((tm, w), lambda b, i: (row(b, i), 0)),
                  pl.BlockSpec((tm, d), lambda b, i: (row(b, i), 0)),
                  pl.BlockSpec((3, CONV_WIDTH), lambda b, i: (0, 0)),
                  vec(w),
                  pl.BlockSpec((d, d), lambda b, i: (0, 0)),
                  per_b, vec(d), per_b, per_b,
                  pl.BlockSpec((d, LANES), lambda b, i: (0, 0)),
                  vec(LANES)],
        out_specs=[pl.BlockSpec((tm, d), lambda b, i: (row(b, i), 0)),
                   pl.BlockSpec((tm * SLAB, LANES), lambda b, i: (row(b, i), 0)),
                   pl.BlockSpec((tm, TOP_K), lambda b, i: (row(b, i), 0)),
                   pl.BlockSpec((tm, TOP_K), lambda b, i: (row(b, i), 0))],
        out_shape=[jax.ShapeDtypeStruct((t, d), F32),
                   jax.ShapeDtypeStruct((t * SLAB, LANES), jnp.uint32),
                   jax.ShapeDtypeStruct((t, TOP_K), jnp.int32),
                   jax.ShapeDtypeStruct((t, TOP_K), F32)],
        compiler_params=_cparams(("arbitrary", "arbitrary")),
        name="mixout",
    )(proj, proj, proj, proj, proj, proj, proj, proj, hf, hb, x2,
      conv_w, norm_g, w_out, gt, g_ffn, sh_f, sc_f, w_r, b_r)


def _rank_kernel(idx_ref, rank_ref, cnt_ref, run_scr):
    @pl.when(pl.program_id(0) == 0)
    def _():
        run_scr[...] = jnp.zeros_like(run_scr)

    tm = idx_ref.shape[0]
    idx = idx_ref[...]
    lane = lax.broadcasted_iota(jnp.int32, (tm, LANES), 1)
    hits = [lane == idx[:, kk:kk + 1] for kk in range(TOP_K)]
    onehot = jnp.zeros((tm, LANES), F32)
    for hit in hits:
        onehot = onehot + hit.astype(F32)
    r = lax.broadcasted_iota(jnp.int32, (tm, tm), 0)
    c = lax.broadcasted_iota(jnp.int32, (tm, tm), 1)
    before = jnp.dot((c < r).astype(BF16), onehot.astype(BF16), preferred_element_type=F32) + run_scr[...]
    lane4 = lax.broadcasted_iota(jnp.int32, (tm, TOP_K), 1)
    rank = jnp.zeros((tm, TOP_K), F32)
    for kk, hit in enumerate(hits):
        rk = jnp.sum(jnp.where(hit, before, 0.0), axis=-1, keepdims=True)
        rank = jnp.where(lane4 == kk, rk, rank)
    rank_ref[...] = rank.astype(jnp.int32)
    run_scr[...] = run_scr[...] + jnp.sum(onehot, axis=0, keepdims=True)
    cnt_ref[...] = run_scr[...]


def _rank(idx, tm):
    t = idx.shape[0]
    return pl.pallas_call(
        _rank_kernel,
        grid=(t // tm,),
        in_specs=[pl.BlockSpec((tm, TOP_K), lambda i: (i, 0))],
        out_specs=[pl.BlockSpec((tm, TOP_K), lambda i: (i, 0)),
                   pl.BlockSpec((1, LANES), lambda i: (0, 0))],
        out_shape=[jax.ShapeDtypeStruct((t, TOP_K), jnp.int32),
                   jax.ShapeDtypeStruct((1, LANES), F32)],
        scratch_shapes=[pltpu.VMEM((1, LANES), F32)],
        compiler_params=_cparams(("arbitrary",)),
        name="rank",
    )(idx)


def _largest_pad_piece():
    return 1 << ((ROW_BLOCK - 1).bit_length() - 1)


def _dispatch_kernel(pad_ref, dest_hbm, xn_ref, xs_hbm, dsm, zeros_scr, sem_idx, sem_rows, sem_pad):
    i = pl.program_id(0)
    tm = xn_ref.shape[0] // SLAB
    n_idx = tm * TOP_K
    idx_copy = pltpu.make_async_copy(dest_hbm.at[pl.ds(i * n_idx, n_idx)], dsm, sem_idx)
    idx_copy.start()

    def slab(ref, row, n_rows=1):
        return ref.at[pl.ds(pl.multiple_of(row * SLAB, SLAB), n_rows * SLAB), :]

    def for_each_pad_piece(fn):
        def per_expert(e, carry):
            off = pad_ref[2 * e]
            n = pad_ref[2 * e + 1]
            size = _largest_pad_piece()
            while size >= 1:
                take = (n & size) != 0

                @pl.when(take)
                def _(off=off, size=size):
                    fn(pltpu.make_async_copy(slab(zeros_scr, 0, size), slab(xs_hbm, off, size), sem_pad))

                off = off + jnp.where(take, size, 0)
                size //= 2
            return carry
        lax.fori_loop(0, N_EXPERTS, per_expert, 0)

    @pl.when(i == 0)
    def _():
        zeros_scr[...] = jnp.zeros_like(zeros_scr)
        for_each_pad_piece(lambda cp: cp.start())
        for_each_pad_piece(lambda cp: cp.wait())

    idx_copy.wait()

    def row_copy(t, kk):
        return pltpu.make_async_copy(slab(xn_ref, t), slab(xs_hbm, dsm[t * TOP_K + kk]), sem_rows)

    def issue(t, carry):
        for kk in range(TOP_K):
            row_copy(t, kk).start(priority=kk % 2)
        return carry

    lax.fori_loop(0, tm, issue, 0, unroll=ISSUE_UNROLL)
    pltpu.make_async_copy(slab(xs_hbm, 0, n_idx), slab(xs_hbm, 0, n_idx), sem_rows).wait()


def _dispatch(pad_info, dest_flat, xn2, n_rows, tm):
    t = xn2.shape[0] // SLAB
    return pl.pallas_call(
        _dispatch_kernel,
        grid_spec=pltpu.PrefetchScalarGridSpec(
            num_scalar_prefetch=1,
            grid=(t // tm,),
            in_specs=[pl.BlockSpec(memory_space=pl.ANY),
                      pl.BlockSpec((tm * SLAB, LANES), lambda i, pad: (i, 0))],
            out_specs=pl.BlockSpec(memory_space=pl.ANY),
            scratch_shapes=[pltpu.SMEM((tm * TOP_K,), jnp.int32),
                            pltpu.VMEM((_largest_pad_piece() * SLAB, LANES), xn2.dtype),
                            pltpu.SemaphoreType.DMA(()),
                            pltpu.SemaphoreType.DMA(()),
                            pltpu.SemaphoreType.DMA(())]),
        out_shape=jax.ShapeDtypeStruct((n_rows * SLAB, LANES), xn2.dtype),
        compiler_params=_cparams(("arbitrary",)),
        name="dispatch",
    )(pad_info, dest_flat, xn2)


def _new_expert(be_ref, j):
    return jnp.logical_or(j == 0, be_ref[j] != be_ref[jnp.maximum(j - 1, 0)])


GU_COLS = 512


def _full_or_half_block(valid, rows, compute):
    @pl.when(valid > rows // 2)
    def _():
        compute(rows)

    @pl.when(valid <= rows // 2)
    def _():
        compute(rows // 2)


def _expert_gu_kernel(be_ref, nu_ref, nxt_ref, bv_ref, xs_ref, w_hbm, bg_ref, bu_ref, act_ref,
                      stage_g, stage_u, wg_scr, wu_scr, sem):
    n = pl.program_id(0)
    j = pl.program_id(1)
    nt = pl.num_programs(0)
    tn = wg_scr.shape[1]

    def weight_copies(e, nn):
        col_g = pl.multiple_of(nn * tn, tn)
        col_u = pl.multiple_of((nt + nn) * tn, tn)
        return (pltpu.make_async_copy(w_hbm.at[e, :, pl.ds(col_g, tn)], stage_g, sem.at[0]),
                pltpu.make_async_copy(w_hbm.at[e, :, pl.ds(col_u, tn)], stage_u, sem.at[1]))

    @pl.when(j < nu_ref[0])
    def _():
        e = be_ref[j]

        @pl.when(_new_expert(be_ref, j))
        def _():
            @pl.when(jnp.logical_and(n == 0, j == 0))
            def _():
                for cp in weight_copies(e, n):
                    cp.start()

            for cp in weight_copies(e, n):
                cp.wait()
            wg_scr[...] = stage_g[...].astype(BF16)
            wu_scr[...] = stage_u[...].astype(BF16)

            e_next = nxt_ref[e]
            in_pass = e_next >= 0

            @pl.when(jnp.logical_or(in_pass, n + 1 < nt))
            def _():
                for cp in weight_copies(jnp.where(in_pass, e_next, be_ref[0]), jnp.where(in_pass, n, n + 1)):
                    cp.start()

        def compute(rows):
            x = _unpack_pairs(_load_slab_rows(xs_ref, rows), SLAB * LANES).astype(BF16)
            for c0 in range(0, tn, GU_COLS):
                cols = slice(c0, c0 + GU_COLS)
                g = jnp.dot(x, wg_scr[:, cols], preferred_element_type=F32) + bg_ref[:, cols]
                u = jnp.dot(x, wu_scr[:, cols], preferred_element_type=F32) + bu_ref[:, cols]
                gate = jnp.minimum(g, SWIGLU_LIMIT)
                up = jnp.clip(u, -SWIGLU_LIMIT, SWIGLU_LIMIT)
                act_ref[0:rows, cols] = ((up + 1.0) * gate * jax.nn.sigmoid(SWIGLU_ALPHA * gate)).astype(BF16)

        _full_or_half_block(bv_ref[j], xs_ref.shape[0] // SLAB, compute)


def _expert_gu(blk_expert, n_used, nxt_expert, blk_valid, xs, w_gu, b_gu, tn):
    n_rows = xs.shape[0] // SLAB
    d = w_gu.shape[1]
    dff = w_gu.shape[2] // 2
    nt = dff // tn
    nb = n_rows // ROW_BLOCK
    blk = lambda j, nu: jnp.minimum(j, nu[0] - 1)
    exp = lambda j, be, nu: be[blk(j, nu)]
    return pl.pallas_call(
        _expert_gu_kernel,
        grid_spec=pltpu.PrefetchScalarGridSpec(
            num_scalar_prefetch=4,
            grid=(nt, nb),
            in_specs=[pl.BlockSpec((ROW_BLOCK * SLAB, LANES), lambda n, j, be, nu, nx, bv: (blk(j, nu), 0)),
                      pl.BlockSpec(memory_space=pl.ANY),
                      pl.BlockSpec((None, 1, tn), lambda n, j, be, nu, nx, bv: (exp(j, be, nu), 0, n)),
                      pl.BlockSpec((None, 1, tn), lambda n, j, be, nu, nx, bv: (exp(j, be, nu), 0, nt + n))],
            out_specs=pl.BlockSpec((ROW_BLOCK, tn), lambda n, j, be, nu, nx, bv: (blk(j, nu), n)),
            scratch_shapes=[pltpu.VMEM((d, tn), F32), pltpu.VMEM((d, tn), F32),
                            pltpu.VMEM((d, tn), BF16), pltpu.VMEM((d, tn), BF16),
                            pltpu.SemaphoreType.DMA((2,))]),
        out_shape=jax.ShapeDtypeStruct((n_rows, dff), BF16),
        compiler_params=_cparams(("arbitrary", "arbitrary")),
        name="expert_gu",
    )(blk_expert, n_used, nxt_expert, blk_valid, xs, w_gu, b_gu, b_gu)


def _expert_down_kernel(be_ref, nu_ref, nxt_ref, bv_ref, act_ref, w_hbm, b_ref, y_ref, stage, w_scr, sem):
    j = pl.program_id(0)

    def weight_copy(e):
        return pltpu.make_async_copy(w_hbm.at[e], stage, sem)

    @pl.when(j < nu_ref[0])
    def _():
        e = be_ref[j]

        @pl.when(_new_expert(be_ref, j))
        def _():
            @pl.when(j == 0)
            def _():
                weight_copy(e).start()

            weight_copy(e).wait()
            w_scr[...] = stage[...].astype(BF16)
            e_next = nxt_ref[e]

            @pl.when(e_next >= 0)
            def _():
                weight_copy(e_next).start()

        def compute(rows):
            act = act_ref[0:rows, :]
            for c0 in range(0, w_scr.shape[1], DOWN_COLS):
                cols = slice(c0, c0 + DOWN_COLS)
                y = _pack_pairs(jnp.dot(act, w_scr[:, cols], preferred_element_type=F32) + b_ref[:, cols])
                for q in range(y.shape[1] // LANES):
                    chunk = c0 // 2 // LANES + q
                    y_ref[pl.ds(chunk, rows, stride=SLAB), :] = y[:, q * LANES:(q + 1) * LANES]

        _full_or_half_block(bv_ref[j], act_ref.shape[0], compute)


def _expert_down(blk_expert, n_used, nxt_expert, blk_valid, act, w_down, b_down):
    n_rows, dff = act.shape
    d = w_down.shape[2]
    nb = n_rows // ROW_BLOCK
    blk = lambda j, nu: jnp.minimum(j, nu[0] - 1)
    exp = lambda j, be, nu: be[blk(j, nu)]
    return pl.pallas_call(
        _expert_down_kernel,
        grid_spec=pltpu.PrefetchScalarGridSpec(
            num_scalar_prefetch=4,
            grid=(nb,),
            in_specs=[pl.BlockSpec((ROW_BLOCK, dff), lambda j, be, nu, nx, bv: (blk(j, nu), 0)),
                      pl.BlockSpec(memory_space=pl.ANY),
                      pl.BlockSpec((None, 1, d), lambda j, be, nu, nx, bv: (exp(j, be, nu), 0, 0))],
            out_specs=pl.BlockSpec((ROW_BLOCK * SLAB, LANES), lambda j, be, nu, nx, bv: (blk(j, nu), 0)),
            scratch_shapes=[pltpu.VMEM((dff, d), F32), pltpu.VMEM((dff, d), BF16),
                            pltpu.SemaphoreType.DMA(())]),
        out_shape=jax.ShapeDtypeStruct((n_rows * SLAB, LANES), jnp.uint32),
        compiler_params=_cparams(("arbitrary",)),
        name="expert_down",
    )(blk_expert, n_used, nxt_expert, blk_valid, act, w_down, b_down)


def _combine_kernel(dest_hbm, y_hbm, x1_ref, tw_ref, gt_ref, gfin_ref, out_ref, dsm, buf, sem_idx, sem_rows):
    i = pl.program_id(1) + pl.program_id(0) * pl.num_programs(1)
    n_steps = pl.num_programs(0) * pl.num_programs(1)
    tm = x1_ref.shape[0]
    n_idx = tm * TOP_K

    def slab(ref, row, n_rows=1):
        return ref.at[pl.ds(pl.multiple_of(row * SLAB, SLAB), n_rows * SLAB), :]

    def idx_copy(tile):
        slot = tile % 2
        return pltpu.make_async_copy(dest_hbm.at[pl.ds(tile * n_idx, n_idx)], dsm.at[slot], sem_idx.at[slot])

    def issue_rows(tile):
        slot = tile % 2

        def issue(t, carry):
            for kk in range(TOP_K):
                pltpu.make_async_copy(slab(y_hbm, dsm[slot, t * TOP_K + kk]), slab(buf.at[slot, kk], t),
                                      sem_rows.at[slot]).start(priority=kk % 2)
            return carry

        lax.fori_loop(0, tm, issue, 0, unroll=ISSUE_UNROLL)

    @pl.when(i == 0)
    def _():
        idx_copy(0).start()
        idx_copy(0).wait()
        issue_rows(0)

        @pl.when(n_steps > 1)
        def _():
            idx_copy(1).start()

    @pl.when(i + 2 < n_steps)
    def _():
        idx_copy(i + 2).start()

    @pl.when(i + 1 < n_steps)
    def _():
        idx_copy(i + 1).wait()
        issue_rows(i + 1)

    slot = i % 2
    for kk in range(TOP_K):
        pltpu.make_async_copy(slab(y_hbm, 0, tm), buf.at[slot, kk], sem_rows.at[slot]).wait()

    tw = tw_ref[...]
    rows = lambda kk: _unpack_pairs(_load_slab_rows(buf.at[slot, kk], tm), DOWN_COLS // 2)
    acc = rows(0) * tw[:, 0:1]
    for kk in range(1, TOP_K):
        acc = acc + rows(kk) * tw[:, kk:kk + 1]
    x2 = x1_ref[...] + gt_ref[...] * acc
    out_ref[...] = x2 * lax.rsqrt(jnp.mean(x2 * x2, axis=-1, keepdims=True) + EPS) * gfin_ref[...]


def _combine(dest_flat, y, x1, tw, gt, g_final, bsz, s, tm):
    t, d = x1.shape
    nt = s // tm
    row = lambda b, i: (b * nt + i, 0)
    return pl.pallas_call(
        _combine_kernel,
        grid=(bsz, nt),
        in_specs=[pl.BlockSpec(memory_space=pl.ANY),
                  pl.BlockSpec(memory_space=pl.ANY),
                  pl.BlockSpec((tm, d), row),
                  pl.BlockSpec((tm, TOP_K), row),
                  pl.BlockSpec((None, 1, d), lambda b, i: (b, 0, 0)),
                  pl.BlockSpec((1, d), lambda b, i: (0, 0))],
        out_specs=pl.BlockSpec((tm, d), row),
        out_shape=jax.ShapeDtypeStruct((t, d), F32),
        scratch_shapes=[pltpu.SMEM((2, tm * TOP_K), jnp.int32),
                        pltpu.VMEM((2, TOP_K, tm * SLAB, LANES), y.dtype),
                        pltpu.SemaphoreType.DMA((2,)),
                        pltpu.SemaphoreType.DMA((2,))],
        compiler_params=_cparams(("arbitrary", "arbitrary")),
        name="combine",
    )(dest_flat, y, x1, tw, gt, g_final)


def _pad_lanes(a, value=0.0):
    return jnp.pad(a, ((0, 0), (0, LANES - a.shape[1])), constant_values=value)


def _routing_tables(idx, rank, counts_f, n_blocks):
    counts = counts_f[0, :N_EXPERTS].astype(jnp.int32)
    padded = (counts + ROW_BLOCK - 1) // ROW_BLOCK * ROW_BLOCK
    pend = jnp.cumsum(padded)
    pstart = pend - padded
    dest = (pstart[idx] + rank).reshape(-1)
    blk_start = jnp.arange(n_blocks, dtype=jnp.int32) * ROW_BLOCK
    blk_expert = jnp.minimum(jnp.sum((pend[None, :] <= blk_start[:, None]).astype(jnp.int32), axis=1),
                             N_EXPERTS - 1)
    n_used = (pend[-1:] // ROW_BLOCK).astype(jnp.int32)
    blk_valid = jnp.clip((pstart + counts)[blk_expert] - blk_start, 0, ROW_BLOCK).astype(jnp.int32)
    pad_info = jnp.stack([pstart + counts, padded - counts], axis=1).reshape(-1).astype(jnp.int32)
    ids = jnp.arange(N_EXPERTS, dtype=jnp.int32)
    later = jnp.where((ids[None, :] > ids[:, None]) & (counts[None, :] > 0), ids[None, :], N_EXPERTS)
    nxt = jnp.min(later, axis=1)
    nxt_expert = jnp.where(nxt == N_EXPERTS, -1, nxt).astype(jnp.int32)
    return dest, blk_expert, n_used, nxt_expert, blk_valid, pad_info


def _layer(x, c, ctx, c_ctx, w_ada, b_ada, g_mix, w_in, b_if, conv_w, norm_g, w_out,
           g_ffn, w_router, b_router, w_gu, b_gu, w_down, b_down, g_final):
    bsz, s, d = x.shape
    s_ctx = ctx.shape[1]

    cond = jnp.zeros((8, d), F32).at[:bsz].set(c).at[bsz].set(c_ctx)
    mod = _adaln(cond, w_ada, b_ada[None, :])
    sh_m, sc_m, gt_m, sh_f, sc_f, gt_f = [m[:, None, :] for m in jnp.split(mod, N_MOD, axis=-1)]
    lat = lambda m: m[:bsz]
    ctxm = lambda m: jnp.broadcast_to(m[bsz:bsz + 1], (bsz, 1, d))

    g0 = 2 * QK_COLS + 2 * MLSTM_WIDTH
    w_main = jnp.concatenate([w_in[:, :g0], w_in[:, g0 + N_GATE_COLS:]], axis=1).astype(BF16)
    w_gate = _pad_lanes(w_in[:, g0:g0 + N_GATE_COLS]).astype(BF16)
    b_gate = _pad_lanes(b_if[None, :])
    g_mix2 = g_mix[None, :]

    proj_c, gpre_c = _inproj(ctx, g_mix2, ctxm(sh_m), ctxm(sc_m), w_main, w_gate, min(s_ctx, 512))
    gcol_c, grow_c = _gates(gpre_c, b_gate, 512)
    zeros_state = (jnp.zeros((bsz, 2 * N_HEADS, DK, DVX), F32),
                   jnp.zeros((bsz, 2 * N_HEADS, 1, LANES), F32))
    _, _, c0, m0 = _mlstm(proj_c, gcol_c, grow_c, bsz, s_ctx, *zeros_state)

    proj, gpre = _inproj(x, g_mix2, lat(sh_m), lat(sc_m), w_main, w_gate, 512)
    gcol, grow = _gates(gpre, b_gate, 512)
    hf, hb, _, _ = _mlstm(proj, gcol, grow, bsz, s, c0, m0)
    x1, xn2, idx, tw = _mixout(
        proj, hf, hb, x.reshape(bsz * s, d), conv_w, norm_g[None, :], w_out.astype(BF16), lat(gt_m),
        g_ffn[None, :], lat(sh_f), lat(sc_f), _pad_lanes(w_router).astype(BF16),
        _pad_lanes(b_router[None, :], NEG_BIG), bsz, s, 512)

    t = bsz * s
    n_blocks = -(-(t * TOP_K) // ROW_BLOCK) + N_EXPERTS
    rank, counts = _rank(idx, 512)
    dest, blk_expert, n_used, nxt_expert, blk_valid, pad_info = _routing_tables(idx, rank, counts, n_blocks)
    xs = _dispatch(pad_info, dest, xn2, n_blocks * ROW_BLOCK, 1024)
    act = _expert_gu(blk_expert, n_used, nxt_expert, blk_valid, xs, w_gu, b_gu[:, None, :], 1024)
    y = _expert_down(blk_expert, n_used, nxt_expert, blk_valid, act, w_down, b_down[:, None, :])
    out = _combine(dest, y, x1, tw, lat(gt_f), g_final[None, :], bsz, s, 512)
    return out.reshape(bsz, s, d)


def kernel(x, c, ctx, c_ctx, w_ada, b_ada, g_mix, w_in, b_if, conv_w, mlstm_norm_g, w_out,
           g_ffn, w_router, b_router, w_gu, b_gu, w_down, b_down, g_final):
    return _layer(x, c, ctx, c_ctx, w_ada[0], b_ada[0], g_mix[0], w_in[0], b_if[0], conv_w[0],
                  mlstm_norm_g[0], w_out[0], g_ffn[0], w_router[0], b_router[0], w_gu[0], b_gu[0],
                  w_down[0], b_down[0], g_final)
```

```python
import functools

import jax
import jax.numpy as jnp
from jax import lax
from jax.experimental import pallas as pl
from jax.experimental.pallas import tpu as pltpu

F32 = jnp.float32
BF16 = jnp.bfloat16

N_HEADS = 4
DK = 128
DV = 256
QK_COLS = N_HEADS * DK
MLSTM_WIDTH = N_HEADS * DV
CONV_WIDTH = 1024
CONV_HALF = CONV_WIDTH // 2
N_GATE_COLS = 4 * N_HEADS
GRID_W = 64
CHUNK = 128
GATE_SOFT_CAP = 15.0
N_EXPERTS = 32
TOP_K = 4
SWIGLU_LIMIT = 7.0
SWIGLU_ALPHA = 1.702
N_MOD = 6
EPS = 1e-6
LANES = 128
SUBLANES = 8
ROW_BLOCK = 1024
DOWN_COLS = 1024
ISSUE_UNROLL = 8
NEG_BIG = -1e30
VMEM_LIMIT = 56 * 1024 * 1024


def _cparams(sem):
    return pltpu.CompilerParams(dimension_semantics=sem, vmem_limit_bytes=VMEM_LIMIT)


def _pack_pairs(x):
    bits = lax.bitcast_convert_type(x.astype(BF16).astype(F32), jnp.uint32)
    g = x.shape[1] // 2
    return bits[:, :g] | (bits[:, g:] >> 16)


def _unpack_pairs(p, group):
    hi = lax.bitcast_convert_type(p & jnp.uint32(0xFFFF0000), F32)
    lo = lax.bitcast_convert_type(p << 16, F32)
    parts = []
    for g0 in range(0, p.shape[1], group):
        parts += [hi[:, g0:g0 + group], lo[:, g0:g0 + group]]
    return jnp.concatenate(parts, axis=1)


SLAB = 8


def _store_slab_rows(ref, r0, packed):
    rows = packed.shape[0]
    for c in range(SLAB):
        ref[pl.ds(r0 * SLAB + c, rows, stride=SLAB), :] = packed[:, c * LANES:(c + 1) * LANES]


def _load_slab_rows(ref, rows):
    return jnp.concatenate([ref[pl.ds(c, rows, stride=SLAB), :] for c in range(SLAB)], axis=1)


def _adaln_kernel(c_ref, w_ref, b_ref, o_ref):
    s = c_ref[...]
    s = s * jax.nn.sigmoid(s)
    o_ref[...] = jnp.dot(s.astype(BF16), w_ref[...].astype(BF16),
                         preferred_element_type=F32) + b_ref[...]


def _adaln(cond, w, b):
    d, n = w.shape
    tn = 1024
    return pl.pallas_call(
        _adaln_kernel,
        grid=(n // tn,),
        in_specs=[pl.BlockSpec((8, d), lambda j: (0, 0)),
                  pl.BlockSpec((d, tn), lambda j: (0, j)),
                  pl.BlockSpec((1, tn), lambda j: (0, j))],
        out_specs=pl.BlockSpec((8, tn), lambda j: (0, j)),
        out_shape=jax.ShapeDtypeStruct((8, n), F32),
        compiler_params=_cparams(("arbitrary",)),
        name="adaln",
    )(cond, w, b)


INPROJ_COLS = 1024


def _inproj_kernel(x_ref, g_ref, sh_ref, sc_ref, w_ref, wg_ref, proj_ref, gate_ref):
    x = x_ref[...]
    y = x * lax.rsqrt(jnp.mean(x * x, axis=-1, keepdims=True) + EPS) * g_ref[...]
    xn = (y * (1.0 + sc_ref[...]) + sh_ref[...]).astype(BF16)
    gate_ref[...] = jnp.dot(xn, wg_ref[...], preferred_element_type=F32)
    for j in range(w_ref.shape[1] // INPROJ_COLS):
        cols = slice(j * INPROJ_COLS, (j + 1) * INPROJ_COLS)
        proj_ref[:, cols] = jnp.dot(xn, w_ref[:, cols], preferred_element_type=F32).astype(BF16)


def _inproj(x, g, sh, sc, w, wg, tm):
    bsz, s, d = x.shape
    p = w.shape[1]
    nt = s // tm
    x2 = x.reshape(bsz * s, d)
    resident = lambda shape: pl.BlockSpec(shape, lambda b, i: (0, 0), pipeline_mode=pl.Buffered(1))
    return pl.pallas_call(
        _inproj_kernel,
        grid=(bsz, nt),
        in_specs=[pl.BlockSpec((tm, d), lambda b, i: (b * nt + i, 0)),
                  pl.BlockSpec((1, d), lambda b, i: (0, 0)),
                  pl.BlockSpec((None, 1, d), lambda b, i: (b, 0, 0)),
                  pl.BlockSpec((None, 1, d), lambda b, i: (b, 0, 0)),
                  resident((d, p)),
                  resident((d, LANES))],
        out_specs=[pl.BlockSpec((tm, p), lambda b, i: (b * nt + i, 0)),
                   pl.BlockSpec((tm, LANES), lambda b, i: (b * nt + i, 0))],
        out_shape=[jax.ShapeDtypeStruct((bsz * s, p), BF16),
                   jax.ShapeDtypeStruct((bsz * s, LANES), F32)],
        compiler_params=_cparams(("arbitrary", "arbitrary")),
        name="inproj",
    )(x2, g, sh, sc, w, wg)


def _log_sigmoid(x):
    return jnp.minimum(x, 0.0) - jnp.log1p(jnp.exp(-jnp.abs(x)))


def _gates_kernel(g_ref, b_ref, gc_ref, gr_ref):
    tm = g_ref.shape[0]
    row = lax.broadcasted_iota(jnp.int32, (tm, LANES), 0)
    lane = lax.broadcasted_iota(jnp.int32, (tm, LANES), 1)
    gp = GATE_SOFT_CAP * jnp.tanh((g_ref[...] + b_ref[...]) / GATE_SOFT_CAP)
    is_f = ((lane >> 2) & 1) == 1
    fwd_lane = lane < 2 * N_HEADS
    lf = jnp.where(is_f, _log_sigmoid(gp), 0.0)
    r2 = lax.broadcasted_iota(jnp.int32, (CHUNK, CHUNK), 0)
    c2 = lax.broadcasted_iota(jnp.int32, (CHUNK, CHUNK), 1)
    lower = (r2 >= c2).astype(F32)
    upper = (r2 <= c2).astype(F32)
    lane_c = lax.broadcasted_iota(jnp.int32, (CHUNK, LANES), 1)
    cums = []
    for c in range(tm // CHUNK):
        lf_c = lf[c * CHUNK:(c + 1) * CHUNK]
        cf = jnp.dot(lower, lf_c, precision=lax.Precision.HIGHEST, preferred_element_type=F32)
        cb = jnp.dot(upper, lf_c, precision=lax.Precision.HIGHEST, preferred_element_type=F32)
        cums.append(jnp.where(lane_c < 2 * N_HEADS, cf, cb))
    cdir = jnp.concatenate(cums, axis=0)
    a = jnp.where(is_f, cdir, gp - pltpu.roll(cdir, LANES - N_HEADS, 1))

    pos = row % CHUNK
    x = a
    k = 1
    while k < CHUNK:
        from_before = jnp.where(pos >= k, pltpu.roll(x, k, 0), -jnp.inf)
        from_after = jnp.where(pos < CHUNK - k, pltpu.roll(x, tm - k, 0), -jnp.inf)
        x = jnp.maximum(x, jnp.where(fwd_lane, from_before, from_after))
        k *= 2
    gc_ref[...] = jnp.where(is_f, a, x)

    lane_1 =lax.broadcasted_iota(jnp.int32, (1, LANES), 1)
    for c in range(tm // CHUNK):
        lo = c * CHUNK
        xc, ac = x[lo:lo + CHUNK], a[lo:lo + CHUNK]
        end_max = jnp.where(lane_1 < 2 * N_HEADS, xc[CHUNK - 1:CHUNK], xc[0:1])
        e = jnp.exp(ac - end_max)
        rows = jnp.where(((lane_c >> 2) & 1) == 1, pltpu.roll(e, N_HEADS, 1), ac)
        gr_ref[:, lo:lo + CHUNK] = rows.T[:N_GATE_COLS, :]


def _gates(gpre, b_if, tm):
    t = gpre.shape[0]
    return pl.pallas_call(
        _gates_kernel,
        grid=(t // tm,),
        in_specs=[pl.BlockSpec((tm, LANES), lambda i: (i, 0)),
                  pl.BlockSpec((1, LANES), lambda i: (0, 0))],
        out_specs=[pl.BlockSpec((tm, LANES), lambda i: (i, 0)),
                   pl.BlockSpec((N_GATE_COLS, tm), lambda i: (0, i))],
        out_shape=[jax.ShapeDtypeStruct((t, LANES), F32),
                   jax.ShapeDtypeStruct((N_GATE_COLS, t), F32)],
        compiler_params=_cparams(("arbitrary",)),
        name="gates",
    )(gpre, b_if)


DVX = DV + LANES
MLSTM_CHUNKS_PER_STEP = 4


def _mlstm_chunk(q, k, v_ext, rmax_col, b_col, r_row, e_row, b_last, rmax_last, mask, cx, m_st):
    scale = DK ** -0.5
    mb = jnp.maximum(m_st, jnp.broadcast_to(rmax_col, (CHUNK, CHUNK)))
    w_intra = jnp.exp(jnp.where(mask, r_row - mb, -jnp.inf))
    w_state = jnp.exp(m_st - mb)
    qk = lax.dot_general(q, k, (((1,), (1,)), ((), ())), preferred_element_type=F32)
    s = qk * (w_intra * scale)
    lhs = jnp.concatenate([s.astype(BF16), (q.astype(F32) * (w_state * scale)).astype(BF16)], axis=1)
    rhs = jnp.concatenate([v_ext, cx.astype(BF16)], axis=0)
    nx = jnp.dot(lhs, rhs, preferred_element_type=F32)
    denom = jnp.maximum(jnp.abs(nx[:, DV:]), jnp.exp(-(jnp.broadcast_to(b_col, (CHUNK, CHUNK)) + mb)))
    h = nx[:, :DV] / jnp.concatenate([denom, denom], axis=1)
    ke_t = (k.T.astype(F32) * e_row).astype(BF16)
    c_loc = jnp.dot(ke_t, v_ext, preferred_element_type=F32)
    m_loc = b_last + rmax_last
    m_new = jnp.maximum(b_last + m_st, m_loc)
    return h, jnp.exp(b_last + m_st - m_new) * cx + jnp.exp(m_loc - m_new) * c_loc, m_new


def _mlstm_kernel(qf_ref, kf_ref, vf_ref, gcf_ref, grf_ref, qb_ref, kb_ref, vb_ref, gcb_ref, grb_ref,
                  c0_ref, m0_ref, hf_ref, hb_ref, cout_ref, mout_ref, m_scr, *c_scrs):
    c = pl.program_id(1)

    @pl.when(c == 0)
    def _():
        for idx, c_scr in enumerate(c_scrs):
            c_scr[...] = c0_ref[idx]
        m_scr[...] = m0_ref[...]

    row = lax.broadcasted_iota(jnp.int32, (CHUNK, CHUNK), 0)
    col = lax.broadcasted_iota(jnp.int32, (CHUNK, CHUNK), 1)
    ones = jnp.ones((CHUNK, LANES), BF16)
    m_all = m_scr[...]
    dirs = ((qf_ref, kf_ref, vf_ref, gcf_ref, grf_ref, hf_ref, 0, CHUNK - 1, col <= row),
            (qb_ref, kb_ref, vb_ref, gcb_ref, grb_ref, hb_ref, 2 * N_HEADS, 0, col >= row))
    n_sub = qf_ref.shape[0] // CHUNK
    m_news = []
    for di, (q_ref, k_ref, v_ref, gc_ref, gr_ref, h_ref, off, last, mask) in enumerate(dirs):
        order = range(n_sub) if di == 0 else range(n_sub - 1, -1, -1)
        for hd in range(N_HEADS):
            idx = di * N_HEADS + hd
            lr, lb = off + hd, off + N_HEADS + hd
            cx, m_st = c_scrs[idx][...], m_all[idx][:, 0:1]
            for sub in order:
                r0 = sub * CHUNK
                rows = slice(r0, r0 + CHUNK)
                v_ext = jnp.concatenate([v_ref[rows, hd * DV:(hd + 1) * DV], ones], axis=1)
                h, cx, m_st = _mlstm_chunk(
                    q_ref[rows, hd * DK:(hd + 1) * DK], k_ref[rows, hd * DK:(hd + 1) * DK], v_ext,
                    gc_ref[rows, lr:lr + 1], gc_ref[rows, lb:lb + 1],
                    gr_ref[lr:lr + 1, rows], gr_ref[lb:lb + 1, rows],
                    gc_ref[r0 + last:r0 + last + 1, lb:lb + 1], gc_ref[r0 + last:r0 + last + 1, lr:lr + 1],
                    mask, cx, m_st)
                h_ref[rows, hd * DV:(hd + 1) * DV] = h
            c_scrs[idx][...] = cx
            m_news.append(jnp.broadcast_to(m_st, (1, LANES)))
    for idx, m_new in enumerate(m_news):
        m_scr[idx] = m_new

    @pl.when(c == pl.num_programs(1) - 1)
    def _():
        for idx, c_scr in enumerate(c_scrs):
            cout_ref[idx] = c_scr[...]
        mout_ref[...] = m_scr[...]


def _mlstm(proj, gcol, grow, bsz, s, c0, m0):
    rows = min(MLSTM_CHUNKS_PER_STEP * CHUNK, s)
    nc = s // rows
    t = bsz * s
    fwd = lambda b, c: b * nc + c
    bwd = lambda b, c: b * nc + (nc - 1 - c)

    def specs(ci):
        return [pl.BlockSpec((rows, QK_COLS), lambda b, c: (ci(b, c), 0)),
                pl.BlockSpec((rows, QK_COLS), lambda b, c: (ci(b, c), 1)),
                pl.BlockSpec((rows, MLSTM_WIDTH), lambda b, c: (ci(b, c), 1)),
                pl.BlockSpec((rows, LANES), lambda b, c: (ci(b, c), 0)),
                pl.BlockSpec((N_GATE_COLS, rows), lambda b, c: (0, ci(b, c)))]

    st_specs = [pl.BlockSpec((None, 2 * N_HEADS, DK, DVX), lambda b, c: (b, 0, 0, 0)),
                pl.BlockSpec((None, 2 * N_HEADS, 1, LANES), lambda b, c: (b, 0, 0, 0))]
    return pl.pallas_call(
        _mlstm_kernel,
        grid=(bsz, nc),
        in_specs=specs(fwd) + specs(bwd) + st_specs,
        out_specs=[pl.BlockSpec((rows, MLSTM_WIDTH), lambda b, c: (fwd(b, c), 0)),
                   pl.BlockSpec((rows, MLSTM_WIDTH), lambda b, c: (bwd(b, c), 0))] + st_specs,
        out_shape=[jax.ShapeDtypeStruct((t, MLSTM_WIDTH), F32),
                   jax.ShapeDtypeStruct((t, MLSTM_WIDTH), F32),
                   jax.ShapeDtypeStruct(c0.shape, F32),
                   jax.ShapeDtypeStruct(m0.shape, F32)],
        scratch_shapes=[pltpu.VMEM((2 * N_HEADS, 1, LANES), F32)]
        + [pltpu.VMEM((DK, DVX), F32) for _ in range(2 * N_HEADS)],
        compiler_params=_cparams(("arbitrary", "arbitrary")),
        name="mlstm",
    )(proj, proj, proj, gcol, grow, proj, proj, proj, gcol, grow, c0, m0)


MIX_ROWS = 256


def _mixout_kernel(o_ref, cb_ref, cc_ref, cx_ref, ccp_ref, cxp_ref, ccn_ref, cxn_ref, hf_ref, hb_ref, x_ref,
                   cw_ref, ng_ref, wout_ref, gt_ref, gffn_ref, shf_ref, scf_ref, wr_ref, br_ref,
                   x1_ref, xn2_ref, idx_ref, tw_ref):
    i = pl.program_id(1)
    tm = x_ref.shape[0]
    cw = cw_ref[...]

    has_prev = jnp.where(i > 0, 1.0, 0.0)
    has_next = jnp.where(i < pl.num_programs(1) - 1, 1.0, 0.0)
    up = ccp_ref[...].astype(F32) * cxp_ref[...].astype(F32) * has_prev
    un = ccn_ref[...].astype(F32) * cxn_ref[...].astype(F32) * has_next
    uv = cc_ref[:, CONV_HALF:].astype(F32) * cx_ref[:, CONV_HALF:].astype(F32)
    ext = jnp.concatenate([up, uv, un], axis=0)

    pos = lax.broadcasted_iota(jnp.int32, (MIX_ROWS, CONV_HALF), 0) & (GRID_W - 1)
    lane_f = lax.broadcasted_iota(jnp.int32, (MIX_ROWS, LANES), 1).astype(F32)
    lane4 = lax.broadcasted_iota(jnp.int32, (MIX_ROWS, TOP_K), 1)

    for r0 in range(0, tm, MIX_ROWS):
        rows = slice(r0, r0 + MIX_ROWS)

        uh = cc_ref[rows, :CONV_HALF].astype(F32) * cx_ref[rows, :CONV_HALF].astype(F32)
        left = jnp.where(pos == 0, 0.0, pltpu.roll(uh, 1, 0))
        right = jnp.where(pos == GRID_W - 1, 0.0, pltpu.roll(uh, MIX_ROWS - 1, 0))
        yh = cw[0:1, :CONV_HALF] * left + cw[1:2, :CONV_HALF] * uh + cw[2:3, :CONV_HALF] * right
        yv = (cw[0:1, CONV_HALF:] * ext[r0:r0 + MIX_ROWS]
              + cw[1:2, CONV_HALF:] * ext[r0 + GRID_W:r0 + GRID_W + MIX_ROWS]
              + cw[2:3, CONV_HALF:] * ext[r0 + 2 * GRID_W:r0 + 2 * GRID_W + MIX_ROWS])
        yc = cb_ref[rows, :].astype(F32) * jnp.concatenate([yh, yv], axis=1)

        hs = hf_ref[rows, :] + hb_ref[rows, :]
        parts = []
        for hd in range(N_HEADS):
            seg = hs[:, hd * DV:(hd + 1) * DV]
            parts.append(seg * lax.rsqrt(jnp.mean(seg * seg, axis=-1, keepdims=True) + EPS))
        hm = jnp.concatenate(parts, axis=1) * ng_ref[...] * jax.nn.sigmoid(o_ref[rows, :].astype(F32))

        z = jnp.concatenate([hm.astype(BF16), yc.astype(BF16)], axis=1)
        x1 = x_ref[rows, :] + gt_ref[...] * jnp.dot(z, wout_ref[...], preferred_element_type=F32)
        x1_ref[rows, :] = x1

        y = x1 * lax.rsqrt(jnp.mean(x1 * x1, axis=-1, keepdims=True) + EPS) * gffn_ref[...]
        xn2 = y * (1.0 + scf_ref[...]) + shf_ref[...]
        _store_slab_rows(xn2_ref, r0, _pack_pairs(xn2))

        logits = jnp.dot(xn2.astype(BF16), wr_ref[...], preferred_element_type=F32) + br_ref[...]
        vals, idxs = [], []
        for _ in range(TOP_K):
            mx = jnp.max(logits, axis=-1, keepdims=True)
            ik = jnp.min(jnp.where(logits == mx, lane_f, float(LANES)), axis=-1, keepdims=True)
            vals.append(mx)
            idxs.append(ik)
            logits = jnp.where(lane_f == ik, -jnp.inf, logits)
        es = [jnp.exp(v - vals[0]) for v in vals]
        tot = es[0] + es[1] + es[2] + es[3]
        idx_out = jnp.zeros((MIX_ROWS, TOP_K), F32)
        tw_out = jnp.zeros((MIX_ROWS, TOP_K), F32)
        for kk in range(TOP_K):
            idx_out = jnp.where(lane4 == kk, idxs[kk], idx_out)
            tw_out = jnp.where(lane4 == kk, es[kk] / tot, tw_out)
        idx_ref[rows, :] = idx_out.astype(jnp.int32)
        tw_ref[rows, :] = tw_out


def _mixout(proj, hf, hb, x2, conv_w, norm_g, w_out, gt, g_ffn, sh_f, sc_f, w_r, b_r, bsz, s, tm):
    t, d = x2.shape
    nt = s // tm
    rb = tm // GRID_W
    last_rb = t // GRID_W - 1
    row = lambda b, i: b * nt + i
    w = MLSTM_WIDTH
    vec = lambda n: pl.BlockSpec((1, n), lambda b, i: (0, 0))
    per_b = pl.BlockSpec((None, 1, d), lambda b, i: (b, 0, 0))
    halo_prev = lambda cblk: pl.BlockSpec(
        (GRID_W, CONV_HALF), lambda b, i: (jnp.maximum(row(b, i) * rb - 1, 0), cblk))
    halo_next = lambda cblk: pl.BlockSpec(
        (GRID_W, CONV_HALF), lambda b, i: (jnp.minimum((row(b, i) + 1) * rb, last_rb), cblk))
    return pl.pallas_call(
        _mixout_kernel,
        grid=(bsz, nt),
        in_specs=[pl.BlockSpec((tm, w), lambda b, i: (row(b, i), 2)),
                  pl.BlockSpec((tm, w), lambda b, i: (row(b, i), 3)),
                  pl.BlockSpec((tm, w), lambda b, i: (row(b, i), 4)),
                  pl.BlockSpec((tm, w), lambda b, i: (row(b, i), 5)),
                  halo_prev(9), halo_prev(11), halo_next(9), halo_next(11),
                  pl.BlockSpec((tm, w), lambda b, i: (row(b, i), 0)),
                  pl.BlockSpec((tm, w), lambda b, i: (row(b, i), 0)),
                  pl.BlockSpec((tm, d), lambda b, i: (row(b, i), 0)),
                  pl.BlockSpec((3, CONV_WIDTH), lambda b, i: (0, 0)),
                  vec(w),
                  pl.BlockSpec((d, d), lambda b, i: (0, 0)),
                  per_b, vec(d), per_b, per_b,
                  pl.BlockSpec((d, LANES), lambda b, i: (0, 0)),
                  vec(LANES)],
        out_specs=[pl.BlockSpec((tm, d), lambda b, i: (row(b, i), 0)),
                   pl.BlockSpec((tm * SLAB, LANES), lambda b, i: (row(b, i), 0)),
                   pl.BlockSpec((tm, TOP_K), lambda b, i: (row(b, i), 0)),
                   pl.BlockSpec((tm, TOP_K), lambda b, i: (row(b, i), 0))],
        out_shape=[jax.ShapeDtypeStruct((t, d), F32),
                   jax.ShapeDtypeStruct((t * SLAB, LANES), jnp.uint32),
                   jax.ShapeDtypeStruct((t, TOP_K), jnp.int32),
                   jax.ShapeDtypeStruct((t, TOP_K), F32)],
        compiler_params=_cparams(("arbitrary", "arbitrary")),
        name="mixout",
    )(proj, proj, proj, proj, proj, proj, proj, proj, hf, hb, x2,
      conv_w, norm_g, w_out, gt, g_ffn, sh_f, sc_f, w_r, b_r)


def _rank_kernel(idx_ref, rank_ref, cnt_ref, run_scr):
    @pl.when(pl.program_id(0) == 0)
    def _():
        run_scr[...] = jnp.zeros_like(run_scr)

    tm = idx_ref.shape[0]
    idx = idx_ref[...]
    lane = lax.broadcasted_iota(jnp.int32, (tm, LANES), 1)
    hits = [lane == idx[:, kk:kk + 1] for kk in range(TOP_K)]
    onehot = jnp.zeros((tm, LANES), F32)
    for hit in hits:
        onehot = onehot + hit.astype(F32)
    r = lax.broadcasted_iota(jnp.int32, (tm, tm), 0)
    c = lax.broadcasted_iota(jnp.int32, (tm, tm), 1)
    before = jnp.dot((c < r).astype(BF16), onehot.astype(BF16), preferred_element_type=F32) + run_scr[...]
    lane4 = lax.broadcasted_iota(jnp.int32, (tm, TOP_K), 1)
    rank = jnp.zeros((tm, TOP_K), F32)
    for kk, hit in enumerate(hits):
        rk = jnp.sum(jnp.where(hit, before, 0.0), axis=-1, keepdims=True)
        rank = jnp.where(lane4 == kk, rk, rank)
    rank_ref[...] = rank.astype(jnp.int32)
    run_scr[...] = run_scr[...] + jnp.sum(onehot, axis=0, keepdims=True)
    cnt_ref[...] = run_scr[...]


def _rank(idx, tm):
    t = idx.shape[0]
    return pl.pallas_call(
        _rank_kernel,
        grid=(t // tm,),
        in_specs=[pl.BlockSpec((tm, TOP_K), lambda i: (i, 0))],
        out_specs=[pl.BlockSpec((tm, TOP_K), lambda i: (i, 0)),
                   pl.BlockSpec((1, LANES), lambda i: (0, 0))],
        out_shape=[jax.ShapeDtypeStruct((t, TOP_K), jnp.int32),
                   jax.ShapeDtypeStruct((1, LANES), F32)],
        scratch_shapes=[pltpu.VMEM((1, LANES), F32)],
        compiler_params=_cparams(("arbitrary",)),
        name="rank",
    )(idx)


def _largest_pad_piece():
    return 1 << ((ROW_BLOCK - 1).bit_length() - 1)


def _dispatch_kernel(pad_ref, dest_hbm, xn_ref, xs_hbm, dsm, zeros_scr, sem_idx, sem_rows, sem_pad):
    i = pl.program_id(0)
    tm = xn_ref.shape[0] // SLAB
    n_idx = tm * TOP_K
    idx_copy = pltpu.make_async_copy(dest_hbm.at[pl.ds(i * n_idx, n_idx)], dsm, sem_idx)
    idx_copy.start()

    def slab(ref, row, n_rows=1):
        return ref.at[pl.ds(pl.multiple_of(row * SLAB, SLAB), n_rows * SLAB), :]

    def for_each_pad_piece(fn):
        def per_expert(e, carry):
            off = pad_ref[2 * e]
            n = pad_ref[2 * e + 1]
            size = _largest_pad_piece()
            while size >= 1:
                take = (n & size) != 0

                @pl.when(take)
                def _(off=off, size=size):
                    fn(pltpu.make_async_copy(slab(zeros_scr, 0, size), slab(xs_hbm, off, size), sem_pad))

                off = off + jnp.where(take, size, 0)
                size //= 2
            return carry
        lax.fori_loop(0, N_EXPERTS, per_expert, 0)

    @pl.when(i == 0)
    def _():
        zeros_scr[...] = jnp.zeros_like(zeros_scr)
        for_each_pad_piece(lambda cp: cp.start())
        for_each_pad_piece(lambda cp: cp.wait())

    idx_copy.wait()

    def row_copy(t, kk):
        return pltpu.make_async_copy(slab(xn_ref, t), slab(xs_hbm, dsm[t * TOP_K + kk]), sem_rows)

    def issue(t, carry):
        for kk in range(TOP_K):
            row_copy(t, kk).start(priority=kk % 2)
        return carry

    lax.fori_loop(0, tm, issue, 0, unroll=ISSUE_UNROLL)
    pltpu.make_async_copy(slab(xs_hbm, 0, n_idx), slab(xs_hbm, 0, n_idx), sem_rows).wait()


def _dispatch(pad_info, dest_flat, xn2, n_rows, tm):
    t = xn2.shape[0] // SLAB
    return pl.pallas_call(
        _dispatch_kernel,
        grid_spec=pltpu.PrefetchScalarGridSpec(
            num_scalar_prefetch=1,
            grid=(t // tm,),
            in_specs=[pl.BlockSpec(memory_space=pl.ANY),
                      pl.BlockSpec((tm * SLAB, LANES), lambda i, pad: (i, 0))],
            out_specs=pl.BlockSpec(memory_space=pl.ANY),
            scratch_shapes=[pltpu.SMEM((tm * TOP_K,), jnp.int32),
                            pltpu.VMEM((_largest_pad_piece() * SLAB, LANES), xn2.dtype),
                            pltpu.SemaphoreType.DMA(()),
                            pltpu.SemaphoreType.DMA(()),
                            pltpu.SemaphoreType.DMA(())]),
        out_shape=jax.ShapeDtypeStruct((n_rows * SLAB, LANES), xn2.dtype),
        compiler_params=_cparams(("arbitrary",)),
        name="dispatch",
    )(pad_info, dest_flat, xn2)


def _new_expert(be_ref, j):
    return jnp.logical_or(j == 0, be_ref[j] != be_ref[jnp.maximum(j - 1, 0)])


GU_COLS = 512


def _full_or_half_block(valid, rows, compute):
    @pl.when(valid > rows // 2)
    def _():
        compute(rows)

    @pl.when(valid <= rows // 2)
    def _():
        compute(rows // 2)


def _expert_gu_kernel(be_ref, nu_ref, nxt_ref, bv_ref, xs_ref, w_hbm, bg_ref, bu_ref, act_ref,
                      stage_g, stage_u, wg_scr, wu_scr, sem):
    n = pl.program_id(0)
    j = pl.program_id(1)
    nt = pl.num_programs(0)
    tn = wg_scr.shape[1]

    def weight_copies(e, nn):
        col_g = pl.multiple_of(nn * tn, tn)
        col_u = pl.multiple_of((nt + nn) * tn, tn)
        return (pltpu.make_async_copy(w_hbm.at[e, :, pl.ds(col_g, tn)], stage_g, sem.at[0]),
                pltpu.make_async_copy(w_hbm.at[e, :, pl.ds(col_u, tn)], stage_u, sem.at[1]))

    @pl.when(j < nu_ref[0])
    def _():
        e = be_ref[j]

        @pl.when(_new_expert(be_ref, j))
        def _():
            @pl.when(jnp.logical_and(n == 0, j == 0))
            def _():
                for cp in weight_copies(e, n):
                    cp.start()

            for cp in weight_copies(e, n):
                cp.wait()
            wg_scr[...] = stage_g[...].astype(BF16)
            wu_scr[...] = stage_u[...].astype(BF16)

            e_next = nxt_ref[e]
            in_pass = e_next >= 0

            @pl.when(jnp.logical_or(in_pass, n + 1 < nt))
            def _():
                for cp in weight_copies(jnp.where(in_pass, e_next, be_ref[0]), jnp.where(in_pass, n, n + 1)):
                    cp.start()

        def compute(rows):
            x = _unpack_pairs(_load_slab_rows(xs_ref, rows), SLAB * LANES).astype(BF16)
            for c0 in range(0, tn, GU_COLS):
                cols = slice(c0, c0 + GU_COLS)
                g = jnp.dot(x, wg_scr[:, cols], preferred_element_type=F32) + bg_ref[:, cols]
                u = jnp.dot(x, wu_scr[:, cols], preferred_element_type=F32) + bu_ref[:, cols]
                gate = jnp.minimum(g, SWIGLU_LIMIT)
                up = jnp.clip(u, -SWIGLU_LIMIT, SWIGLU_LIMIT)
                act_ref[0:rows, cols] = ((up + 1.0) * gate * jax.nn.sigmoid(SWIGLU_ALPHA * gate)).astype(BF16)

        _full_or_half_block(bv_ref[j], xs_ref.shape[0] // SLAB, compute)


def _expert_gu(blk_expert, n_used, nxt_expert, blk_valid, xs, w_gu, b_gu, tn):
    n_rows = xs.shape[0] // SLAB
    d = w_gu.shape[1]
    dff = w_gu.shape[2] // 2
    nt = dff // tn
    nb = n_rows // ROW_BLOCK
    blk = lambda j, nu: jnp.minimum(j, nu[0] - 1)
    exp = lambda j, be, nu: be[blk(j, nu)]
    return pl.pallas_call(
        _expert_gu_kernel,
        grid_spec=pltpu.PrefetchScalarGridSpec(
            num_scalar_prefetch=4,
            grid=(nt, nb),
            in_specs=[pl.BlockSpec((ROW_BLOCK * SLAB, LANES), lambda n, j, be, nu, nx, bv: (blk(j, nu), 0)),
                      pl.BlockSpec(memory_space=pl.ANY),
                      pl.BlockSpec((None, 1, tn), lambda n, j, be, nu, nx, bv: (exp(j, be, nu), 0, n)),
                      pl.BlockSpec((None, 1, tn), lambda n, j, be, nu, nx, bv: (exp(j, be, nu), 0, nt + n))],
            out_specs=pl.BlockSpec((ROW_BLOCK, tn), lambda n, j, be, nu, nx, bv: (blk(j, nu), n)),
            scratch_shapes=[pltpu.VMEM((d, tn), F32), pltpu.VMEM((d, tn), F32),
                            pltpu.VMEM((d, tn), BF16), pltpu.VMEM((d, tn), BF16),
                            pltpu.SemaphoreType.DMA((2,))]),
        out_shape=jax.ShapeDtypeStruct((n_rows, dff), BF16),
        compiler_params=_cparams(("arbitrary", "arbitrary")),
        name="expert_gu",
    )(blk_expert, n_used, nxt_expert, blk_valid, xs, w_gu, b_gu, b_gu)


def _expert_down_kernel(be_ref, nu_ref, nxt_ref, bv_ref, act_ref, w_hbm, b_ref, y_ref, stage, w_scr, sem):
    j = pl.program_id(0)

    def weight_copy(e):
        return pltpu.make_async_copy(w_hbm.at[e], stage, sem)

    @pl.when(j < nu_ref[0])
    def _():
        e = be_ref[j]

        @pl.when(_new_expert(be_ref, j))
        def _():
            @pl.when(j == 0)
            def _():
                weight_copy(e).start()

            weight_copy(e).wait()
            w_scr[...] = stage[...].astype(BF16)
            e_next = nxt_ref[e]

            @pl.when(e_next >= 0)
            def _():
                weight_copy(e_next).start()

        def compute(rows):
            act = act_ref[0:rows, :]
            for c0 in range(0, w_scr.shape[1], DOWN_COLS):
                cols = slice(c0, c0 + DOWN_COLS)
                y = _pack_pairs(jnp.dot(act, w_scr[:, cols], preferred_element_type=F32) + b_ref[:, cols])
                for q in range(y.shape[1] // LANES):
                    chunk = c0 // 2 // LANES + q
                    y_ref[pl.ds(chunk, rows, stride=SLAB), :] = y[:, q * LANES:(q + 1) * LANES]

        _full_or_half_block(bv_ref[j], act_ref.shape[0], compute)


def _expert_down(blk_expert, n_used, nxt_expert, blk_valid, act, w_down, b_down):
    n_rows, dff = act.shape
    d = w_down.shape[2]
    nb = n_rows // ROW_BLOCK
    blk = lambda j, nu: jnp.minimum(j, nu[0] - 1)
    exp = lambda j, be, nu: be[blk(j, nu)]
    return pl.pallas_call(
        _expert_down_kernel,
        grid_spec=pltpu.PrefetchScalarGridSpec(
            num_scalar_prefetch=4,
            grid=(nb,),
            in_specs=[pl.BlockSpec((ROW_BLOCK, dff), lambda j, be, nu, nx, bv: (blk(j, nu), 0)),
                      pl.BlockSpec(memory_space=pl.ANY),
                      pl.BlockSpec((None, 1, d), lambda j, be, nu, nx, bv: (exp(j, be, nu), 0, 0))],
            out_specs=pl.BlockSpec((ROW_BLOCK * SLAB, LANES), lambda j, be, nu, nx, bv: (blk(j, nu), 0)),
            scratch_shapes=[pltpu.VMEM((dff, d), F32), pltpu.VMEM((dff, d), BF16),
                            pltpu.SemaphoreType.DMA(())]),
        out_shape=jax.ShapeDtypeStruct((n_rows * SLAB, LANES), jnp.uint32),
        compiler_params=_cparams(("arbitrary",)),
        name="expert_down",
    )(blk_expert, n_used, nxt_expert, blk_valid, act, w_down, b_down)


def _combine_kernel(dest_hbm, y_hbm, x1_ref, tw_ref, gt_ref, gfin_ref, out_ref, dsm, buf, sem_idx, sem_rows):
    i = pl.program_id(1) + pl.program_id(0) * pl.num_programs(1)
    n_steps = pl.num_programs(0) * pl.num_programs(1)
    tm = x1_ref.shape[0]
    n_idx = tm * TOP_K

    def slab(ref, row, n_rows=1):
        return ref.at[pl.ds(pl.multiple_of(row * SLAB, SLAB), n_rows * SLAB), :]

    def idx_copy(tile):
        slot = tile % 2
        return pltpu.make_async_copy(dest_hbm.at[pl.ds(tile * n_idx, n_idx)], dsm.at[slot], sem_idx.at[slot])

    def issue_rows(tile):
        slot = tile % 2

        def issue(t, carry):
            for kk in range(TOP_K):
                pltpu.make_async_copy(slab(y_hbm, dsm[slot, t * TOP_K + kk]), slab(buf.at[slot, kk], t),
                                      sem_rows.at[slot]).start(priority=kk % 2)
            return carry

        lax.fori_loop(0, tm, issue, 0, unroll=ISSUE_UNROLL)

    @pl.when(i == 0)
    def _():
        idx_copy(0).start()
        idx_copy(0).wait()
        issue_rows(0)

        @pl.when(n_steps > 1)
        def _():
            idx_copy(1).start()

    @pl.when(i + 2 < n_steps)
    def _():
        idx_copy(i + 2).start()

    @pl.when(i + 1 < n_steps)
    def _():
        idx_copy(i + 1).wait()
        issue_rows(i + 1)

    slot = i % 2
    for kk in range(TOP_K):
        pltpu.make_async_copy(slab(y_hbm, 0, tm), buf.at[slot, kk], sem_rows.at[slot]).wait()

    tw = tw_ref[...]
    rows = lambda kk: _unpack_pairs(_load_slab_rows(buf.at[slot, kk], tm), DOWN_COLS // 2)
    acc = rows(0) * tw[:, 0:1]
    for kk in range(1, TOP_K):
        acc = acc + rows(kk) * tw[:, kk:kk + 1]
    x2 = x1_ref[...] + gt_ref[...] * acc
    out_ref[...] = x2 * lax.rsqrt(jnp.mean(x2 * x2, axis=-1, keepdims=True) + EPS) * gfin_ref[...]


def _combine(dest_flat, y, x1, tw, gt, g_final, bsz, s, tm):
    t, d = x1.shape
    nt = s // tm
    row = lambda b, i: (b * nt + i, 0)
    return pl.pallas_call(
        _combine_kernel,
        grid=(bsz, nt),
        in_specs=[pl.BlockSpec(memory_space=pl.ANY),
                  pl.BlockSpec(memory_space=pl.ANY),
                  pl.BlockSpec((tm, d), row),
                  pl.BlockSpec((tm, TOP_K), row),
                  pl.BlockSpec((None, 1, d), lambda b, i: (b, 0, 0)),
                  pl.BlockSpec((1, d), lambda b, i: (0, 0))],
        out_specs=pl.BlockSpec((tm, d), row),
        out_shape=jax.ShapeDtypeStruct((t, d), F32),
        scratch_shapes=[pltpu.SMEM((2, tm * TOP_K), jnp.int32),
                        pltpu.VMEM((2, TOP_K, tm * SLAB, LANES), y.dtype),
                        pltpu.SemaphoreType.DMA((2,)),
                        pltpu.SemaphoreType.DMA((2,))],
        compiler_params=_cparams(("arbitrary", "arbitrary")),
        name="combine",
    )(dest_flat, y, x1, tw, gt, g_final)


def _pad_lanes(a, value=0.0):
    return jnp.pad(a, ((0, 0), (0, LANES - a.shape[1])), constant_values=value)


def _routing_tables(idx, rank, counts_f, n_blocks):
    counts = counts_f[0, :N_EXPERTS].astype(jnp.int32)
    padded = (counts + ROW_BLOCK - 1) // ROW_BLOCK * ROW_BLOCK
    pend = jnp.cumsum(padded)
    pstart = pend - padded
    dest = (pstart[idx] + rank).reshape(-1)
    blk_start = jnp.arange(n_blocks, dtype=jnp.int32) * ROW_BLOCK
    blk_expert = jnp.minimum(jnp.sum((pend[None, :] <= blk_start[:, None]).astype(jnp.int32), axis=1),
                             N_EXPERTS - 1)
    n_used = (pend[-1:] // ROW_BLOCK).astype(jnp.int32)
    blk_valid = jnp.clip((pstart + counts)[blk_expert] - blk_start, 0, ROW_BLOCK).astype(jnp.int32)
    pad_info = jnp.stack([pstart + counts, padded - counts], axis=1).reshape(-1).astype(jnp.int32)
    ids = jnp.arange(N_EXPERTS, dtype=jnp.int32)
    later = jnp.where((ids[None, :] > ids[:, None]) & (counts[None, :] > 0), ids[None, :], N_EXPERTS)
    nxt = jnp.min(later, axis=1)
    nxt_expert = jnp.where(nxt == N_EXPERTS, -1, nxt).astype(jnp.int32)
    return dest, blk_expert, n_used, nxt_expert, blk_valid, pad_info


def _layer(x, c, ctx, c_ctx, w_ada, b_ada, g_mix, w_in, b_if, conv_w, norm_g, w_out,
           g_ffn, w_router, b_router, w_gu, b_gu, w_down, b_down, g_final):
    bsz, s, d = x.shape
    s_ctx = ctx.shape[1]

    cond = jnp.zeros((8, d), F32).at[:bsz].set(c).at[bsz].set(c_ctx)
    mod = _adaln(cond, w_ada, b_ada[None, :])
    sh_m, sc_m, gt_m, sh_f, sc_f, gt_f = [m[:, None, :] for m in jnp.split(mod, N_MOD, axis=-1)]
    lat = lambda m: m[:bsz]
    ctxm = lambda m: jnp.broadcast_to(m[bsz:bsz + 1], (bsz, 1, d))

    g0 = 2 * QK_COLS + 2 * MLSTM_WIDTH
    w_main = jnp.concatenate([w_in[:, :g0], w_in[:, g0 + N_GATE_COLS:]], axis=1).astype(BF16)
    w_gate = _pad_lanes(w_in[:, g0:g0 + N_GATE_COLS]).astype(BF16)
    b_gate = _pad_lanes(b_if[None, :])
    g_mix2 = g_mix[None, :]

    proj_c, gpre_c = _inproj(ctx, g_mix2, ctxm(sh_m), ctxm(sc_m), w_main, w_gate, min(s_ctx, 512))
    gcol_c, grow_c = _gates(gpre_c, b_gate, 512)
    zeros_state = (jnp.zeros((bsz, 2 * N_HEADS, DK, DVX), F32),
                   jnp.zeros((bsz, 2 * N_HEADS, 1, LANES), F32))
    _, _, c0, m0 = _mlstm(proj_c, gcol_c, grow_c, bsz, s_ctx, *zeros_state)

    proj, gpre = _inproj(x, g_mix2, lat(sh_m), lat(sc_m), w_main, w_gate, 512)
    gcol, grow = _gates(gpre, b_gate, 512)
    hf, hb, _, _ = _mlstm(proj, gcol, grow, bsz, s, c0, m0)
    x1, xn2, idx, tw = _mixout(
        proj, hf, hb, x.reshape(bsz * s, d), conv_w, norm_g[None, :], w_out.astype(BF16), lat(gt_m),
        g_ffn[None, :], lat(sh_f), lat(sc_f), _pad_lanes(w_router).astype(BF16),
        _pad_lanes(b_router[None, :], NEG_BIG), bsz, s, 512)

    t = bsz * s
    n_blocks = -(-(t * TOP_K) // ROW_BLOCK) + N_EXPERTS
    rank, counts = _rank(idx, 512)
    dest, blk_expert, n_used, nxt_expert, blk_valid, pad_info = _routing_tables(idx, rank, counts, n_blocks)
    xs = _dispatch(pad_info, dest, xn2, n_blocks * ROW_BLOCK, 2048)
    act = _expert_gu(blk_expert, n_used, nxt_expert, blk_valid, xs, w_gu, b_gu[:, None, :], 1024)
    y = _expert_down(blk_expert, n_used, nxt_expert, blk_valid, act, w_down, b_down[:, None, :])
    out = _combine(dest, y, x1, tw, lat(gt_f), g_final[None, :], bsz, s, 256)
    return out.reshape(bsz, s, d)


def kernel(x, c, ctx, c_ctx, w_ada, b_ada, g_mix, w_in, b_if, conv_w, mlstm_norm_g, w_out,
           g_ffn, w_router, b_router, w_gu, b_gu, w_down, b_down, g_final):
    return _layer(x, c, ctx, c_ctx, w_ada[0], b_ada[0], g_mix[0], w_in[0], b_if[0], conv_w[0],
                  mlstm_norm_g[0], w_out[0], g_ffn[0], w_router[0], b_router[0], w_gu[0], b_gu[0],
                  w_down[0], b_down[0], g_final)
```

```python
import functools

import jax
import jax.numpy as jnp
from jax import lax
from jax.experimental import pallas as pl
from jax.experimental.pallas import tpu as pltpu

F32 = jnp.float32
BF16 = jnp.bfloat16

N_HEADS = 4
DK = 128
DV = 256
QK_COLS = N_HEADS * DK
MLSTM_WIDTH = N_HEADS * DV
CONV_WIDTH = 1024
CONV_HALF = CONV_WIDTH // 2
N_GATE_COLS = 4 * N_HEADS
GRID_W = 64
CHUNK = 128
GATE_SOFT_CAP = 15.0
N_EXPERTS = 32
TOP_K = 4
SWIGLU_LIMIT = 7.0
SWIGLU_ALPHA = 1.702
N_MOD = 6
EPS = 1e-6
LANES = 128
SUBLANES = 8
ROW_BLOCK = 1024
DOWN_COLS = 1024
ISSUE_UNROLL = 8
NEG_BIG = -1e30
VMEM_LIMIT = 56 * 1024 * 1024


def _cparams(sem):
    return pltpu.CompilerParams(dimension_semantics=sem, vmem_limit_bytes=VMEM_LIMIT)


def _pack_pairs(x):
    bits = lax.bitcast_convert_type(x.astype(BF16).astype(F32), jnp.uint32)
    g = x.shape[1] // 2
    return bits[:, :g] | (bits[:, g:] >> 16)


def _unpack_pairs(p, group):
    hi = lax.bitcast_convert_type(p & jnp.uint32(0xFFFF0000), F32)
    lo = lax.bitcast_convert_type(p << 16, F32)
    parts = []
    for g0 in range(0, p.shape[1], group):
        parts += [hi[:, g0:g0 + group], lo[:, g0:g0 + group]]
    return jnp.concatenate(parts, axis=1)


SLAB = 8


def _store_slab_rows(ref, r0, packed):
    rows = packed.shape[0]
    for c in range(SLAB):
        ref[pl.ds(r0 * SLAB + c, rows, stride=SLAB), :] = packed[:, c * LANES:(c + 1) * LANES]


def _load_slab_rows(ref, rows):
    return jnp.concatenate([ref[pl.ds(c, rows, stride=SLAB), :] for c in range(SLAB)], axis=1)


def _adaln_kernel(c_ref, w_ref, b_ref, o_ref):
    s = c_ref[...]
    s = s * jax.nn.sigmoid(s)
    o_ref[...] = jnp.dot(s.astype(BF16), w_ref[...].astype(BF16),
                         preferred_element_type=F32) + b_ref[...]


def _adaln(cond, w, b):
    d, n = w.shape
    tn = 1024
    return pl.pallas_call(
        _adaln_kernel,
        grid=(n // tn,),
        in_specs=[pl.BlockSpec((8, d), lambda j: (0, 0)),
                  pl.BlockSpec((d, tn), lambda j: (0, j)),
                  pl.BlockSpec((1, tn), lambda j: (0, j))],
        out_specs=pl.BlockSpec((8, tn), lambda j: (0, j)),
        out_shape=jax.ShapeDtypeStruct((8, n), F32),
        compiler_params=_cparams(("arbitrary",)),
        name="adaln",
    )(cond, w, b)


INPROJ_COLS = 1024


def _inproj_kernel(x_ref, g_ref, sh_ref, sc_ref, w_ref, wg_ref, proj_ref, gate_ref):
    x = x_ref[...]
    y = x * lax.rsqrt(jnp.mean(x * x, axis=-1, keepdims=True) + EPS) * g_ref[...]
    xn = (y * (1.0 + sc_ref[...]) + sh_ref[...]).astype(BF16)
    gate_ref[...] = jnp.dot(xn, wg_ref[...], preferred_element_type=F32)
    for j in range(w_ref.shape[1] // INPROJ_COLS):
        cols = slice(j * INPROJ_COLS, (j + 1) * INPROJ_COLS)
        proj_ref[:, cols] = jnp.dot(xn, w_ref[:, cols], preferred_element_type=F32).astype(BF16)


def _inproj(x, g, sh, sc, w, wg, tm):
    bsz, s, d = x.shape
    p = w.shape[1]
    nt = s // tm
    x2 = x.reshape(bsz * s, d)
    resident = lambda shape: pl.BlockSpec(shape, lambda b, i: (0, 0), pipeline_mode=pl.Buffered(1))
    return pl.pallas_call(
        _inproj_kernel,
        grid=(bsz, nt),
        in_specs=[pl.BlockSpec((tm, d), lambda b, i: (b * nt + i, 0)),
                  pl.BlockSpec((1, d), lambda b, i: (0, 0)),
                  pl.BlockSpec((None, 1, d), lambda b, i: (b, 0, 0)),
                  pl.BlockSpec((None, 1, d), lambda b, i: (b, 0, 0)),
                  resident((d, p)),
                  resident((d, LANES))],
        out_specs=[pl.BlockSpec((tm, p), lambda b, i: (b * nt + i, 0)),
                   pl.BlockSpec((tm, LANES), lambda b, i: (b * nt + i, 0))],
        out_shape=[jax.ShapeDtypeStruct((bsz * s, p), BF16),
                   jax.ShapeDtypeStruct((bsz * s, LANES), F32)],
        compiler_params=_cparams(("arbitrary", "arbitrary")),
        name="inproj",
    )(x2, g, sh, sc, w, wg)


def _log_sigmoid(x):
    return jnp.minimum(x, 0.0) - jnp.log1p(jnp.exp(-jnp.abs(x)))


def _gates_kernel(g_ref, b_ref, gc_ref, gr_ref):
    tm = g_ref.shape[0]
    row = lax.broadcasted_iota(jnp.int32, (tm, LANES), 0)
    lane = lax.broadcasted_iota(jnp.int32, (tm, LANES), 1)
    gp = GATE_SOFT_CAP * jnp.tanh((g_ref[...] + b_ref[...]) / GATE_SOFT_CAP)
    is_f = ((lane >> 2) & 1) == 1
    fwd_lane = lane < 2 * N_HEADS
    lf = jnp.where(is_f, _log_sigmoid(gp), 0.0)
    r2 = lax.broadcasted_iota(jnp.int32, (CHUNK, CHUNK), 0)
    c2 = lax.broadcasted_iota(jnp.int32, (CHUNK, CHUNK), 1)
    lower = (r2 >= c2).astype(F32)
    upper = (r2 <= c2).astype(F32)
    lane_c = lax.broadcasted_iota(jnp.int32, (CHUNK, LANES), 1)
    cums = []
    for c in range(tm // CHUNK):
        lf_c = lf[c * CHUNK:(c + 1) * CHUNK]
        cf = jnp.dot(lower, lf_c, precision=lax.Precision.HIGHEST, preferred_element_type=F32)
        cb = jnp.dot(upper, lf_c, precision=lax.Precision.HIGHEST, preferred_element_type=F32)
        cums.append(jnp.where(lane_c < 2 * N_HEADS, cf, cb))
    cdir = jnp.concatenate(cums, axis=0)
    a = jnp.where(is_f, cdir, gp - pltpu.roll(cdir, LANES - N_HEADS, 1))

    pos = row % CHUNK
    x = a
    k = 1
    while k < CHUNK:
        from_before = jnp.where(pos >= k, pltpu.roll(x, k, 0), -jnp.inf)
        from_after = jnp.where(pos < CHUNK - k, pltpu.roll(x, tm - k, 0), -jnp.inf)
        x = jnp.maximum(x, jnp.where(fwd_lane, from_before, from_after))
        k *= 2
    gc_ref[...] = jnp.where(is_f, a, x)

    lane_1 =lax.broadcasted_iota(jnp.int32, (1, LANES), 1)
    for c in range(tm // CHUNK):
        lo = c * CHUNK
        xc, ac = x[lo:lo + CHUNK], a[lo:lo + CHUNK]
        end_max = jnp.where(lane_1 < 2 * N_HEADS, xc[CHUNK - 1:CHUNK], xc[0:1])
        e = jnp.exp(ac - end_max)
        rows = jnp.where(((lane_c >> 2) & 1) == 1, pltpu.roll(e, N_HEADS, 1), ac)
        gr_ref[:, lo:lo + CHUNK] = rows.T[:N_GATE_COLS, :]


def _gates(gpre, b_if, tm):
    t = gpre.shape[0]
    return pl.pallas_call(
        _gates_kernel,
        grid=(t // tm,),
        in_specs=[pl.BlockSpec((tm, LANES), lambda i: (i, 0)),
                  pl.BlockSpec((1, LANES), lambda i: (0, 0))],
        out_specs=[pl.BlockSpec((tm, LANES), lambda i: (i, 0)),
                   pl.BlockSpec((N_GATE_COLS, tm), lambda i: (0, i))],
        out_shape=[jax.ShapeDtypeStruct((t, LANES), F32),
                   jax.ShapeDtypeStruct((N_GATE_COLS, t), F32)],
        compiler_params=_cparams(("arbitrary",)),
        name="gates",
    )(gpre, b_if)


DVX = DV + LANES
MLSTM_CHUNKS_PER_STEP = 4


def _mlstm_chunk(q, k, v_ext, rmax_col, b_col, r_row, e_row, b_last, rmax_last, mask, cx, m_st):
    scale = DK ** -0.5
    mb = jnp.maximum(m_st, jnp.broadcast_to(rmax_col, (CHUNK, CHUNK)))
    w_intra = jnp.exp(jnp.where(mask, r_row - mb, -jnp.inf))
    w_state = jnp.exp(m_st - mb)
    qk = lax.dot_general(q, k, (((1,), (1,)), ((), ())), preferred_element_type=F32)
    s = qk * (w_intra * scale)
    lhs = jnp.concatenate([s.astype(BF16), (q.astype(F32) * (w_state * scale)).astype(BF16)], axis=1)
    rhs = jnp.concatenate([v_ext, cx.astype(BF16)], axis=0)
    nx = jnp.dot(lhs, rhs, preferred_element_type=F32)
    denom = jnp.maximum(jnp.abs(nx[:, DV:]), jnp.exp(-(jnp.broadcast_to(b_col, (CHUNK, CHUNK)) + mb)))
    h = nx[:, :DV] / jnp.concatenate([denom, denom], axis=1)
    ke_t = (k.T.astype(F32) * e_row).astype(BF16)
    c_loc = jnp.dot(ke_t, v_ext, preferred_element_type=F32)
    m_loc = b_last + rmax_last
    m_new = jnp.maximum(b_last + m_st, m_loc)
    return h, jnp.exp(b_last + m_st - m_new) * cx + jnp.exp(m_loc - m_new) * c_loc, m_new


def _mlstm_kernel(qf_ref, kf_ref, vf_ref, gcf_ref, grf_ref, qb_ref, kb_ref, vb_ref, gcb_ref, grb_ref,
                  c0_ref, m0_ref, hf_ref, hb_ref, cout_ref, mout_ref, m_scr, *c_scrs):
    c = pl.program_id(1)

    @pl.when(c == 0)
    def _():
        for idx, c_scr in enumerate(c_scrs):
            c_scr[...] = c0_ref[idx]
        m_scr[...] = m0_ref[...]

    row = lax.broadcasted_iota(jnp.int32, (CHUNK, CHUNK), 0)
    col = lax.broadcasted_iota(jnp.int32, (CHUNK, CHUNK), 1)
    ones = jnp.ones((CHUNK, LANES), BF16)
    m_all = m_scr[...]
    dirs = ((qf_ref, kf_ref, vf_ref, gcf_ref, grf_ref, hf_ref, 0, CHUNK - 1, col <= row),
            (qb_ref, kb_ref, vb_ref, gcb_ref, grb_ref, hb_ref, 2 * N_HEADS, 0, col >= row))
    n_sub = qf_ref.shape[0] // CHUNK
    m_news = []
    for di, (q_ref, k_ref, v_ref, gc_ref, gr_ref, h_ref, off, last, mask) in enumerate(dirs):
        order = range(n_sub) if di == 0 else range(n_sub - 1, -1, -1)
        for hd in range(N_HEADS):
            idx = di * N_HEADS + hd
            lr, lb = off + hd, off + N_HEADS + hd
            cx, m_st = c_scrs[idx][...], m_all[idx][:, 0:1]
            for sub in order:
                r0 = sub * CHUNK
                rows = slice(r0, r0 + CHUNK)
                v_ext = jnp.concatenate([v_ref[rows, hd * DV:(hd + 1) * DV], ones], axis=1)
                h, cx, m_st = _mlstm_chunk(
                    q_ref[rows, hd * DK:(hd + 1) * DK], k_ref[rows, hd * DK:(hd + 1) * DK], v_ext,
                    gc_ref[rows, lr:lr + 1], gc_ref[rows, lb:lb + 1],
                    gr_ref[lr:lr + 1, rows], gr_ref[lb:lb + 1, rows],
                    gc_ref[r0 + last:r0 + last + 1, lb:lb + 1], gc_ref[r0 + last:r0 + last + 1, lr:lr + 1],
                    mask, cx, m_st)
                h_ref[rows, hd * DV:(hd + 1) * DV] = h
            c_scrs[idx][...] = cx
            m_news.append(jnp.broadcast_to(m_st, (1, LANES)))
    for idx, m_new in enumerate(m_news):
        m_scr[idx] = m_new

    @pl.when(c == pl.num_programs(1) - 1)
    def _():
        for idx, c_scr in enumerate(c_scrs):
            cout_ref[idx] = c_scr[...]
        mout_ref[...] = m_scr[...]


def _mlstm(proj, gcol, grow, bsz, s, c0, m0):
    rows = min(MLSTM_CHUNKS_PER_STEP * CHUNK, s)
    nc = s // rows
    t = bsz * s
    fwd = lambda b, c: b * nc + c
    bwd = lambda b, c: b * nc + (nc - 1 - c)

    def specs(ci):
        return [pl.BlockSpec((rows, QK_COLS), lambda b, c: (ci(b, c), 0)),
                pl.BlockSpec((rows, QK_COLS), lambda b, c: (ci(b, c), 1)),
                pl.BlockSpec((rows, MLSTM_WIDTH), lambda b, c: (ci(b, c), 1)),
                pl.BlockSpec((rows, LANES), lambda b, c: (ci(b, c), 0)),
                pl.BlockSpec((N_GATE_COLS, rows), lambda b, c: (0, ci(b, c)))]

    st_specs = [pl.BlockSpec((None, 2 * N_HEADS, DK, DVX), lambda b, c: (b, 0, 0, 0)),
                pl.BlockSpec((None, 2 * N_HEADS, 1, LANES), lambda b, c: (b, 0, 0, 0))]
    return pl.pallas_call(
        _mlstm_kernel,
        grid=(bsz, nc),
        in_specs=specs(fwd) + specs(bwd) + st_specs,
        out_specs=[pl.BlockSpec((rows, MLSTM_WIDTH), lambda b, c: (fwd(b, c), 0)),
                   pl.BlockSpec((rows, MLSTM_WIDTH), lambda b, c: (bwd(b, c), 0))] + st_specs,
        out_shape=[jax.ShapeDtypeStruct((t, MLSTM_WIDTH), F32),
                   jax.ShapeDtypeStruct((t, MLSTM_WIDTH), F32),
                   jax.ShapeDtypeStruct(c0.shape, F32),
                   jax.ShapeDtypeStruct(m0.shape, F32)],
        scratch_shapes=[pltpu.VMEM((2 * N_HEADS, 1, LANES), F32)]
        + [pltpu.VMEM((DK, DVX), F32) for _ in range(2 * N_HEADS)],
        compiler_params=_cparams(("arbitrary", "arbitrary")),
        name="mlstm",
    )(proj, proj, proj, gcol, grow, proj, proj, proj, gcol, grow, c0, m0)


MIX_ROWS = 256


def _mixout_kernel(o_ref, cb_ref, cc_ref, cx_ref, ccp_ref, cxp_ref, ccn_ref, cxn_ref, hf_ref, hb_ref, x_ref,
                   cw_ref, ng_ref, wout_ref, gt_ref, gffn_ref, shf_ref, scf_ref, wr_ref, br_ref,
                   x1_ref, xn2_ref, idx_ref, tw_ref):
    i = pl.program_id(1)
    tm = x_ref.shape[0]
    cw = cw_ref[...]

    has_prev = jnp.where(i > 0, 1.0, 0.0)
    has_next = jnp.where(i < pl.num_programs(1) - 1, 1.0, 0.0)
    up = ccp_ref[...].astype(F32) * cxp_ref[...].astype(F32) * has_prev
    un = ccn_ref[...].astype(F32) * cxn_ref[...].astype(F32) * has_next
    uv = cc_ref[:, CONV_HALF:].astype(F32) * cx_ref[:, CONV_HALF:].astype(F32)
    ext = jnp.concatenate([up, uv, un], axis=0)

    pos = lax.broadcasted_iota(jnp.int32, (MIX_ROWS, CONV_HALF), 0) & (GRID_W - 1)
    lane_f = lax.broadcasted_iota(jnp.int32, (MIX_ROWS, LANES), 1).astype(F32)
    lane4 = lax.broadcasted_iota(jnp.int32, (MIX_ROWS, TOP_K), 1)

    for r0 in range(0, tm, MIX_ROWS):
        rows = slice(r0, r0 + MIX_ROWS)

        uh = cc_ref[rows, :CONV_HALF].astype(F32) * cx_ref[rows, :CONV_HALF].astype(F32)
        left = jnp.where(pos == 0, 0.0, pltpu.roll(uh, 1, 0))
        right = jnp.where(pos == GRID_W - 1, 0.0, pltpu.roll(uh, MIX_ROWS - 1, 0))
        yh = cw[0:1, :CONV_HALF] * left + cw[1:2, :CONV_HALF] * uh + cw[2:3, :CONV_HALF] * right
        yv = (cw[0:1, CONV_HALF:] * ext[r0:r0 + MIX_ROWS]
              + cw[1:2, CONV_HALF:] * ext[r0 + GRID_W:r0 + GRID_W + MIX_ROWS]
              + cw[2:3, CONV_HALF:] * ext[r0 + 2 * GRID_W:r0 + 2 * GRID_W + MIX_ROWS])
        yc = cb_ref[rows, :].astype(F32) * jnp.concatenate([yh, yv], axis=1)

        hs = hf_ref[rows, :] + hb_ref[rows, :]
        parts = []
        for hd in range(N_HEADS):
            seg = hs[:, hd * DV:(hd + 1) * DV]
            parts.append(seg * lax.rsqrt(jnp.mean(seg * seg, axis=-1, keepdims=True) + EPS))
        hm = jnp.concatenate(parts, axis=1) * ng_ref[...] * jax.nn.sigmoid(o_ref[rows, :].astype(F32))

        z = jnp.concatenate([hm.astype(BF16), yc.astype(BF16)], axis=1)
        x1 = x_ref[rows, :] + gt_ref[...] * jnp.dot(z, wout_ref[...], preferred_element_type=F32)
        x1_ref[rows, :] = x1

        y = x1 * lax.rsqrt(jnp.mean(x1 * x1, axis=-1, keepdims=True) + EPS) * gffn_ref[...]
        xn2 = y * (1.0 + scf_ref[...]) + shf_ref[...]
        _store_slab_rows(xn2_ref, r0, _pack_pairs(xn2))

        logits = jnp.dot(xn2.astype(BF16), wr_ref[...], preferred_element_type=F32) + br_ref[...]
        vals, idxs = [], []
        for _ in range(TOP_K):
            mx = jnp.max(logits, axis=-1, keepdims=True)
            ik = jnp.min(jnp.where(logits == mx, lane_f, float(LANES)), axis=-1, keepdims=True)
            vals.append(mx)
            idxs.append(ik)
            logits = jnp.where(lane_f == ik, -jnp.inf, logits)
        es = [jnp.exp(v - vals[0]) for v in vals]
        tot = es[0] + es[1] + es[2] + es[3]
        idx_out = jnp.zeros((MIX_ROWS, TOP_K), F32)
        tw_out = jnp.zeros((MIX_ROWS, TOP_K), F32)
        for kk in range(TOP_K):
            idx_out = jnp.where(lane4 == kk, idxs[kk], idx_out)
            tw_out = jnp.where(lane4 == kk, es[kk] / tot, tw_out)
        idx_ref[rows, :] = idx_out.astype(jnp.int32)
        tw_ref[rows, :] = tw_out


def _mixout(proj, hf, hb, x2, conv_w, norm_g, w_out, gt, g_ffn, sh_f, sc_f, w_r, b_r, bsz, s, tm):
    t, d = x2.shape
    nt = s // tm
    rb = tm // GRID_W
    last_rb = t // GRID_W - 1
    row = lambda b, i: b * nt + i
    w = MLSTM_WIDTH
    vec = lambda n: pl.BlockSpec((1, n), lambda b, i: (0, 0))
    per_b = pl.BlockSpec((None, 1, d), lambda b, i: (b, 0, 0))
    halo_prev = lambda cblk: pl.BlockSpec(
        (GRID_W, CONV_HALF), lambda b, i: (jnp.maximum(row(b, i) * rb - 1, 0), cblk))
    halo_next = lambda cblk: pl.BlockSpec(
        (GRID_W, CONV_HALF), lambda b, i: (jnp.minimum((row(b, i) + 1) * rb, last_rb), cblk))
    return pl.pallas_call(
        _mixout_kernel,
        grid=(bsz, nt),
        in_specs=[pl.BlockSpec((tm, w), lambda b, i: (row(b, i), 2)),
                  pl.BlockSpec((tm, w), lambda b, i: (row(b, i), 3)),
                  pl.BlockSpec((tm, w), lambda b, i: (row(b, i), 4)),
                  pl.BlockSpec((tm, w), lambda b, i: (row(b, i), 5)),
                  halo_prev(9), halo_prev(11), halo_next(9), halo_next(11),
                  pl.BlockSpec((tm, w), lambda b, i: (row(b, i), 0)),
                  pl.BlockSpec((tm, w), lambda b, i: (row(b, i), 0)),
                  pl.BlockSpec((tm, d), lambda b, i: (row(b, i), 0)),
                  pl.BlockSpec((3, CONV_WIDTH), lambda b, i: (0, 0)),
                  vec(w),
                  pl.BlockSpec((d, d), lambda b, i: (0, 0)),
                  per_b, vec(d), per_b, per_b,
                  pl.BlockSpec((d, LANES), lambda b, i: (0, 0)),
                  vec(LANES)],
        out_specs=[pl.BlockSpec((tm, d), lambda b, i: (row(b, i), 0)),
                   pl.BlockSpec((tm * SLAB, LANES), lambda b, i: (row(b, i), 0)),
                   pl.BlockSpec((tm, TOP_K), lambda b, i: (row(b, i), 0)),
                   pl.BlockSpec((tm, TOP_K), lambda b, i: (row(b, i), 0))],
        out_shape=[jax.ShapeDtypeStruct((t, d), F32),
                   jax.ShapeDtypeStruct((t * SLAB, LANES), jnp.uint32),
                   jax.ShapeDtypeStruct((t, TOP_K), jnp.int32),
                   jax.ShapeDtypeStruct((t, TOP_K), F32)],
        compiler_params=_cparams(("arbitrary", "arbitrary")),
        name="mixout",
    )(proj, proj, proj, proj, proj, proj, proj, proj, hf, hb, x2,
      conv_w, norm_g, w_out, gt, g_ffn, sh_f, sc_f, w_r, b_r)


def _rank_kernel(idx_ref, rank_ref, cnt_ref, run_scr):
    @pl.when(pl.program_id(0) == 0)
    def _():
        run_scr[...] = jnp.zeros_like(run_scr)

    tm = idx_ref.shape[0]
    idx = idx_ref[...]
    lane = lax.broadcasted_iota(jnp.int32, (tm, LANES), 1)
    hits = [lane == idx[:, kk:kk + 1] for kk in range(TOP_K)]
    onehot = jnp.zeros((tm, LANES), F32)
    for hit in hits:
        onehot = onehot + hit.astype(F32)
    r = lax.broadcasted_iota(jnp.int32, (tm, tm), 0)
    c = lax.broadcasted_iota(jnp.int32, (tm, tm), 1)
    before = jnp.dot((c < r).astype(BF16), onehot.astype(BF16), preferred_element_type=F32) + run_scr[...]
    lane4 = lax.broadcasted_iota(jnp.int32, (tm, TOP_K), 1)
    rank = jnp.zeros((tm, TOP_K), F32)
    for kk, hit in enumerate(hits):
        rk = jnp.sum(jnp.where(hit, before, 0.0), axis=-1, keepdims=True)
        rank = jnp.where(lane4 == kk, rk, rank)
    rank_ref[...] = rank.astype(jnp.int32)
    run_scr[...] = run_scr[...] + jnp.sum(onehot, axis=0, keepdims=True)
    cnt_ref[...] = run_scr[...]


def _rank(idx, tm):
    t = idx.shape[0]
    return pl.pallas_call(
        _rank_kernel,
        grid=(t // tm,),
        in_specs=[pl.BlockSpec((tm, TOP_K), lambda i: (i, 0))],
        out_specs=[pl.BlockSpec((tm, TOP_K), lambda i: (i, 0)),
                   pl.BlockSpec((1, LANES), lambda i: (0, 0))],
        out_shape=[jax.ShapeDtypeStruct((t, TOP_K), jnp.int32),
                   jax.ShapeDtypeStruct((1, LANES), F32)],
        scratch_shapes=[pltpu.VMEM((1, LANES), F32)],
        compiler_params=_cparams(("arbitrary",)),
        name="rank",
    )(idx)


def _largest_pad_piece():
    return 1 << ((ROW_BLOCK - 1).bit_length() - 1)


def _dispatch_kernel(pad_ref, dest_hbm, xn_ref, xs_hbm, dsm, zeros_scr, sem_idx, sem_rows, sem_pad):
    i = pl.program_id(0)
    tm = xn_ref.shape[0] // SLAB
    n_idx = tm * TOP_K
    idx_copy = pltpu.make_async_copy(dest_hbm.at[pl.ds(i * n_idx, n_idx)], dsm, sem_idx)
    idx_copy.start()

    def slab(ref, row, n_rows=1):
        return ref.at[pl.ds(pl.multiple_of(row * SLAB, SLAB), n_rows * SLAB), :]

    def for_each_pad_piece(fn):
        def per_expert(e, carry):
            off = pad_ref[2 * e]
            n = pad_ref[2 * e + 1]
            size = _largest_pad_piece()
            while size >= 1:
                take = (n & size) != 0

                @pl.when(take)
                def _(off=off, size=size):
                    fn(pltpu.make_async_copy(slab(zeros_scr, 0, size), slab(xs_hbm, off, size), sem_pad))

                off = off + jnp.where(take, size, 0)
                size //= 2
            return carry
        lax.fori_loop(0, N_EXPERTS, per_expert, 0)

    @pl.when(i == 0)
    def _():
        zeros_scr[...] = jnp.zeros_like(zeros_scr)
        for_each_pad_piece(lambda cp: cp.start())
        for_each_pad_piece(lambda cp: cp.wait())

    idx_copy.wait()

    def row_copy(t, kk):
        return pltpu.make_async_copy(slab(xn_ref, t), slab(xs_hbm, dsm[t * TOP_K + kk]), sem_rows)

    def issue(t, carry):
        for kk in range(TOP_K):
            row_copy(t, kk).start(priority=kk % 2)
        return carry

    lax.fori_loop(0, tm, issue, 0, unroll=ISSUE_UNROLL)
    pltpu.make_async_copy(slab(xs_hbm, 0, n_idx), slab(xs_hbm, 0, n_idx), sem_rows).wait()


def _dispatch(pad_info, dest_flat, xn2, n_rows, tm):
    t = xn2.shape[0] // SLAB
    return pl.pallas_call(
        _dispatch_kernel,
        grid_spec=pltpu.PrefetchScalarGridSpec(
            num_scalar_prefetch=1,
            grid=(t // tm,),
            in_specs=[pl.BlockSpec(memory_space=pl.ANY),
                      pl.BlockSpec((tm * SLAB, LANES), lambda i, pad: (i, 0))],
            out_specs=pl.BlockSpec(memory_space=pl.ANY),
            scratch_shapes=[pltpu.SMEM((tm * TOP_K,), jnp.int32),
                            pltpu.VMEM((_largest_pad_piece() * SLAB, LANES), xn2.dtype),
                            pltpu.SemaphoreType.DMA(()),
                            pltpu.SemaphoreType.DMA(()),
                            pltpu.SemaphoreType.DMA(())]),
        out_shape=jax.ShapeDtypeStruct((n_rows * SLAB, LANES), xn2.dtype),
        compiler_params=_cparams(("arbitrary",)),
        name="dispatch",
    )(pad_info, dest_flat, xn2)


def _new_expert(be_ref, j):
    return jnp.logical_or(j == 0, be_ref[j] != be_ref[jnp.maximum(j - 1, 0)])


GU_COLS = 512


def _full_or_half_block(valid, rows, compute):
    half, quarter = rows // 2, rows // 4

    @pl.when(valid > half)
    def _():
        compute(rows)

    @pl.when(jnp.logical_and(valid > quarter, valid <= half))
    def _():
        compute(half)

    @pl.when(valid <= quarter)
    def _():
        compute(quarter)


def _expert_gu_kernel(be_ref, nu_ref, nxt_ref, bv_ref, xs_ref, w_hbm, bg_ref, bu_ref, act_ref,
                      stage_g, stage_u, wg_scr, wu_scr, sem):
    n = pl.program_id(0)
    j = pl.program_id(1)
    nt = pl.num_programs(0)
    tn = wg_scr.shape[1]

    def weight_copies(e, nn):
        col_g = pl.multiple_of(nn * tn, tn)
        col_u = pl.multiple_of((nt + nn) * tn, tn)
        return (pltpu.make_async_copy(w_hbm.at[e, :, pl.ds(col_g, tn)], stage_g, sem.at[0]),
                pltpu.make_async_copy(w_hbm.at[e, :, pl.ds(col_u, tn)], stage_u, sem.at[1]))

    @pl.when(j < nu_ref[0])
    def _():
        e = be_ref[j]

        @pl.when(_new_expert(be_ref, j))
        def _():
            @pl.when(jnp.logical_and(n == 0, j == 0))
            def _():
                for cp in weight_copies(e, n):
                    cp.start()

            for cp in weight_copies(e, n):
                cp.wait()
            wg_scr[...] = stage_g[...].astype(BF16)
            wu_scr[...] = stage_u[...].astype(BF16)

            e_next = nxt_ref[e]
            in_pass = e_next >= 0

            @pl.when(jnp.logical_or(in_pass, n + 1 < nt))
            def _():
                for cp in weight_copies(jnp.where(in_pass, e_next, be_ref[0]), jnp.where(in_pass, n, n + 1)):
                    cp.start()

        def compute(rows):
            x = _unpack_pairs(_load_slab_rows(xs_ref, rows), SLAB * LANES).astype(BF16)
            for c0 in range(0, tn, GU_COLS):
                cols = slice(c0, c0 + GU_COLS)
                g = jnp.dot(x, wg_scr[:, cols], preferred_element_type=F32) + bg_ref[:, cols]
                u = jnp.dot(x, wu_scr[:, cols], preferred_element_type=F32) + bu_ref[:, cols]
                gate = jnp.minimum(g, SWIGLU_LIMIT)
                up = jnp.clip(u, -SWIGLU_LIMIT, SWIGLU_LIMIT)
                act_ref[0:rows, cols] = ((up + 1.0) * gate * jax.nn.sigmoid(SWIGLU_ALPHA * gate)).astype(BF16)

        _full_or_half_block(bv_ref[j], xs_ref.shape[0] // SLAB, compute)


def _expert_gu(blk_expert, n_used, nxt_expert, blk_valid, xs, w_gu, b_gu, tn):
    n_rows = xs.shape[0] // SLAB
    d = w_gu.shape[1]
    dff = w_gu.shape[2] // 2
    nt = dff // tn
    nb = n_rows // ROW_BLOCK
    blk = lambda j, nu: jnp.minimum(j, nu[0] - 1)
    exp = lambda j, be, nu: be[blk(j, nu)]
    return pl.pallas_call(
        _expert_gu_kernel,
        grid_spec=pltpu.PrefetchScalarGridSpec(
            num_scalar_prefetch=4,
            grid=(nt, nb),
            in_specs=[pl.BlockSpec((ROW_BLOCK * SLAB, LANES), lambda n, j, be, nu, nx, bv: (blk(j, nu), 0)),
                      pl.BlockSpec(memory_space=pl.ANY),
                      pl.BlockSpec((None, 1, tn), lambda n, j, be, nu, nx, bv: (exp(j, be, nu), 0, n)),
                      pl.BlockSpec((None, 1, tn), lambda n, j, be, nu, nx, bv: (exp(j, be, nu), 0, nt + n))],
            out_specs=pl.BlockSpec((ROW_BLOCK, tn), lambda n, j, be, nu, nx, bv: (blk(j, nu), n)),
            scratch_shapes=[pltpu.VMEM((d, tn), F32), pltpu.VMEM((d, tn), F32),
                            pltpu.VMEM((d, tn), BF16), pltpu.VMEM((d, tn), BF16),
                            pltpu.SemaphoreType.DMA((2,))]),
        out_shape=jax.ShapeDtypeStruct((n_rows, dff), BF16),
        compiler_params=_cparams(("arbitrary", "arbitrary")),
        name="expert_gu",
    )(blk_expert, n_used, nxt_expert, blk_valid, xs, w_gu, b_gu, b_gu)


def _expert_down_kernel(be_ref, nu_ref, nxt_ref, bv_ref, act_ref, w_hbm, b_ref, y_ref, stage, w_scr, sem):
    j = pl.program_id(0)

    def weight_copy(e):
        return pltpu.make_async_copy(w_hbm.at[e], stage, sem)

    @pl.when(j < nu_ref[0])
    def _():
        e = be_ref[j]

        @pl.when(_new_expert(be_ref, j))
        def _():
            @pl.when(j == 0)
            def _():
                weight_copy(e).start()

            weight_copy(e).wait()
            w_scr[...] = stage[...].astype(BF16)
            e_next = nxt_ref[e]

            @pl.when(e_next >= 0)
            def _():
                weight_copy(e_next).start()

        def compute(rows):
            act = act_ref[0:rows, :]
            for c0 in range(0, w_scr.shape[1], DOWN_COLS):
                cols = slice(c0, c0 + DOWN_COLS)
                y = _pack_pairs(jnp.dot(act, w_scr[:, cols], preferred_element_type=F32) + b_ref[:, cols])
                for q in range(y.shape[1] // LANES):
                    chunk = c0 // 2 // LANES + q
                    y_ref[pl.ds(chunk, rows, stride=SLAB), :] = y[:, q * LANES:(q + 1) * LANES]

        _full_or_half_block(bv_ref[j], act_ref.shape[0], compute)


def _expert_down(blk_expert, n_used, nxt_expert, blk_valid, act, w_down, b_down):
    n_rows, dff = act.shape
    d = w_down.shape[2]
    nb = n_rows // ROW_BLOCK
    blk = lambda j, nu: jnp.minimum(j, nu[0] - 1)
    exp = lambda j, be, nu: be[blk(j, nu)]
    return pl.pallas_call(
        _expert_down_kernel,
        grid_spec=pltpu.PrefetchScalarGridSpec(
            num_scalar_prefetch=4,
            grid=(nb,),
            in_specs=[pl.BlockSpec((ROW_BLOCK, dff), lambda j, be, nu, nx, bv: (blk(j, nu), 0)),
                      pl.BlockSpec(memory_space=pl.ANY),
                      pl.BlockSpec((None, 1, d), lambda j, be, nu, nx, bv: (exp(j, be, nu), 0, 0))],
            out_specs=pl.BlockSpec((ROW_BLOCK * SLAB, LANES), lambda j, be, nu, nx, bv: (blk(j, nu), 0)),
            scratch_shapes=[pltpu.VMEM((dff, d), F32), pltpu.VMEM((dff, d), BF16),
                            pltpu.SemaphoreType.DMA(())]),
        out_shape=jax.ShapeDtypeStruct((n_rows * SLAB, LANES), jnp.uint32),
        compiler_params=_cparams(("arbitrary",)),
        name="expert_down",
    )(blk_expert, n_used, nxt_expert, blk_valid, act, w_down, b_down)


def _combine_kernel(dest_hbm, y_hbm, x1_ref, tw_ref, gt_ref, gfin_ref, out_ref, dsm, buf, sem_idx, sem_rows):
    i = pl.program_id(1) + pl.program_id(0) * pl.num_programs(1)
    n_steps = pl.num_programs(0) * pl.num_programs(1)
    tm = x1_ref.shape[0]
    n_idx = tm * TOP_K

    def slab(ref, row, n_rows=1):
        return ref.at[pl.ds(pl.multiple_of(row * SLAB, SLAB), n_rows * SLAB), :]

    def idx_copy(tile):
        slot = tile % 2
        return pltpu.make_async_copy(dest_hbm.at[pl.ds(tile * n_idx, n_idx)], dsm.at[slot], sem_idx.at[slot])

    def issue_rows(tile):
        slot = tile % 2

        def issue(t, carry):
            for kk in range(TOP_K):
                pltpu.make_async_copy(slab(y_hbm, dsm[slot, t * TOP_K + kk]), slab(buf.at[slot, kk], t),
                                      sem_rows.at[slot]).start(priority=kk % 2)
            return carry

        lax.fori_loop(0, tm, issue, 0, unroll=ISSUE_UNROLL)

    @pl.when(i == 0)
    def _():
        idx_copy(0).start()
        idx_copy(0).wait()
        issue_rows(0)

        @pl.when(n_steps > 1)
        def _():
            idx_copy(1).start()

    @pl.when(i + 2 < n_steps)
    def _():
        idx_copy(i + 2).start()

    @pl.when(i + 1 < n_steps)
    def _():
        idx_copy(i + 1).wait()
        issue_rows(i + 1)

    slot = i % 2
    for kk in range(TOP_K):
        pltpu.make_async_copy(slab(y_hbm, 0, tm), buf.at[slot, kk], sem_rows.at[slot]).wait()

    tw = tw_ref[...]
    rows = lambda kk: _unpack_pairs(_load_slab_rows(buf.at[slot, kk], tm), DOWN_COLS // 2)
    acc = rows(0) * tw[:, 0:1]
    for kk in range(1, TOP_K):
        acc = acc + rows(kk) * tw[:, kk:kk + 1]
    x2 = x1_ref[...] + gt_ref[...] * acc
    out_ref[...] = x2 * lax.rsqrt(jnp.mean(x2 * x2, axis=-1, keepdims=True) + EPS) * gfin_ref[...]


def _combine(dest_flat, y, x1, tw, gt, g_final, bsz, s, tm):
    t, d = x1.shape
    nt = s // tm
    row = lambda b, i: (b * nt + i, 0)
    return pl.pallas_call(
        _combine_kernel,
        grid=(bsz, nt),
        in_specs=[pl.BlockSpec(memory_space=pl.ANY),
                  pl.BlockSpec(memory_space=pl.ANY),
                  pl.BlockSpec((tm, d), row),
                  pl.BlockSpec((tm, TOP_K), row),
                  pl.BlockSpec((None, 1, d), lambda b, i: (b, 0, 0)),
                  pl.BlockSpec((1, d), lambda b, i: (0, 0))],
        out_specs=pl.BlockSpec((tm, d), row),
        out_shape=jax.ShapeDtypeStruct((t, d), F32),
        scratch_shapes=[pltpu.SMEM((2, tm * TOP_K), jnp.int32),
                        pltpu.VMEM((2, TOP_K, tm * SLAB, LANES), y.dtype),
                        pltpu.SemaphoreType.DMA((2,)),
                        pltpu.SemaphoreType.DMA((2,))],
        compiler_params=_cparams(("arbitrary", "arbitrary")),
        name="combine",
    )(dest_flat, y, x1, tw, gt, g_final)


def _pad_lanes(a, value=0.0):
    return jnp.pad(a, ((0, 0), (0, LANES - a.shape[1])), constant_values=value)


def _routing_tables(idx, rank, counts_f, n_blocks):
    counts = counts_f[0, :N_EXPERTS].astype(jnp.int32)
    padded = (counts + ROW_BLOCK - 1) // ROW_BLOCK * ROW_BLOCK
    pend = jnp.cumsum(padded)
    pstart = pend - padded
    dest = (pstart[idx] + rank).reshape(-1)
    blk_start = jnp.arange(n_blocks, dtype=jnp.int32) * ROW_BLOCK
    blk_expert = jnp.minimum(jnp.sum((pend[None, :] <= blk_start[:, None]).astype(jnp.int32), axis=1),
                             N_EXPERTS - 1)
    n_used = (pend[-1:] // ROW_BLOCK).astype(jnp.int32)
    blk_valid = jnp.clip((pstart + counts)[blk_expert] - blk_start, 0, ROW_BLOCK).astype(jnp.int32)
    pad_info = jnp.stack([pstart + counts, padded - counts], axis=1).reshape(-1).astype(jnp.int32)
    ids = jnp.arange(N_EXPERTS, dtype=jnp.int32)
    later = jnp.where((ids[None, :] > ids[:, None]) & (counts[None, :] > 0), ids[None, :], N_EXPERTS)
    nxt = jnp.min(later, axis=1)
    nxt_expert = jnp.where(nxt == N_EXPERTS, -1, nxt).astype(jnp.int32)
    return dest, blk_expert, n_used, nxt_expert, blk_valid, pad_info


def _layer(x, c, ctx, c_ctx, w_ada, b_ada, g_mix, w_in, b_if, conv_w, norm_g, w_out,
           g_ffn, w_router, b_router, w_gu, b_gu, w_down, b_down, g_final):
    bsz, s, d = x.shape
    s_ctx = ctx.shape[1]

    cond = jnp.zeros((8, d), F32).at[:bsz].set(c).at[bsz].set(c_ctx)
    mod = _adaln(cond, w_ada, b_ada[None, :])
    sh_m, sc_m, gt_m, sh_f, sc_f, gt_f = [m[:, None, :] for m in jnp.split(mod, N_MOD, axis=-1)]
    lat = lambda m: m[:bsz]
    ctxm = lambda m: jnp.broadcast_to(m[bsz:bsz + 1], (bsz, 1, d))

    g0 = 2 * QK_COLS + 2 * MLSTM_WIDTH
    w_main = jnp.concatenate([w_in[:, :g0], w_in[:, g0 + N_GATE_COLS:]], axis=1).astype(BF16)
    w_gate = _pad_lanes(w_in[:, g0:g0 + N_GATE_COLS]).astype(BF16)
    b_gate = _pad_lanes(b_if[None, :])
    g_mix2 = g_mix[None, :]

    proj_c, gpre_c = _inproj(ctx, g_mix2, ctxm(sh_m), ctxm(sc_m), w_main, w_gate, min(s_ctx, 512))
    gcol_c, grow_c = _gates(gpre_c, b_gate, 512)
    zeros_state = (jnp.zeros((bsz, 2 * N_HEADS, DK, DVX), F32),
                   jnp.zeros((bsz, 2 * N_HEADS, 1, LANES), F32))
    _, _, c0, m0 = _mlstm(proj_c, gcol_c, grow_c, bsz, s_ctx, *zeros_state)

    proj, gpre = _inproj(x, g_mix2, lat(sh_m), lat(sc_m), w_main, w_gate, 512)
    gcol, grow = _gates(gpre, b_gate, 512)
    hf, hb, _, _ = _mlstm(proj, gcol, grow, bsz, s, c0, m0)
    x1, xn2, idx, tw = _mixout(
        proj, hf, hb, x.reshape(bsz * s, d), conv_w, norm_g[None, :], w_out.astype(BF16), lat(gt_m),
        g_ffn[None, :], lat(sh_f), lat(sc_f), _pad_lanes(w_router).astype(BF16),
        _pad_lanes(b_router[None, :], NEG_BIG), bsz, s, 512)

    t = bsz * s
    n_blocks = -(-(t * TOP_K) // ROW_BLOCK) + N_EXPERTS
    rank, counts = _rank(idx, 512)
    dest, blk_expert, n_used, nxt_expert, blk_valid, pad_info = _routing_tables(idx, rank, counts, n_blocks)
    xs = _dispatch(pad_info, dest, xn2, n_blocks * ROW_BLOCK, 2048)
    act = _expert_gu(blk_expert, n_used, nxt_expert, blk_valid, xs, w_gu, b_gu[:, None, :], 1024)
    y = _expert_down(blk_expert, n_used, nxt_expert, blk_valid, act, w_down, b_down[:, None, :])
    out = _combine(dest, y, x1, tw, lat(gt_f), g_final[None, :], bsz, s, 256)
    return out.reshape(bsz, s, d)


def kernel(x, c, ctx, c_ctx, w_ada, b_ada, g_mix, w_in, b_if, conv_w, mlstm_norm_g, w_out,
           g_ffn, w_router, b_router, w_gu, b_gu, w_down, b_down, g_final):
    return _layer(x, c, ctx, c_ctx, w_ada[0], b_ada[0], g_mix[0], w_in[0], b_if[0], conv_w[0],
                  mlstm_norm_g[0], w_out[0], g_ffn[0], w_router[0], b_router[0], w_gu[0], b_gu[0],
                  w_down[0], b_down[0], g_final)
```

```python
import functools

import jax
import jax.numpy as jnp
from jax import lax
from jax.experimental import pallas as pl
from jax.experimental.pallas import tpu as pltpu

F32 = jnp.float32
BF16 = jnp.bfloat16

N_HEADS = 4
DK = 128
DV = 256
QK_COLS = N_HEADS * DK
MLSTM_WIDTH = N_HEADS * DV
CONV_WIDTH = 1024
CONV_HALF = CONV_WIDTH // 2
N_GATE_COLS = 4 * N_HEADS
GRID_W = 64
CHUNK = 128
GATE_SOFT_CAP = 15.0
N_EXPERTS = 32
TOP_K = 4
SWIGLU_LIMIT = 7.0
SWIGLU_ALPHA = 1.702
N_MOD = 6
EPS = 1e-6
LANES = 128
SUBLANES = 8
ROW_BLOCK = 1024
DOWN_COLS = 1024
ISSUE_UNROLL = 16
NEG_BIG = -1e30
VMEM_LIMIT = 56 * 1024 * 1024


def _cparams(sem):
    return pltpu.CompilerParams(dimension_semantics=sem, vmem_limit_bytes=VMEM_LIMIT)


def _pack_pairs(x):
    bits = lax.bitcast_convert_type(x.astype(BF16).astype(F32), jnp.uint32)
    g = x.shape[1] // 2
    return bits[:, :g] | (bits[:, g:] >> 16)


def _unpack_pairs(p, group):
    hi = lax.bitcast_convert_type(p & jnp.uint32(0xFFFF0000), F32)
    lo = lax.bitcast_convert_type(p << 16, F32)
    parts = []
    for g0 in range(0, p.shape[1], group):
        parts += [hi[:, g0:g0 + group], lo[:, g0:g0 + group]]
    return jnp.concatenate(parts, axis=1)


SLAB = 8


def _store_slab_rows(ref, r0, packed):
    rows = packed.shape[0]
    for c in range(SLAB):
        ref[pl.ds(r0 * SLAB + c, rows, stride=SLAB), :] = packed[:, c * LANES:(c + 1) * LANES]


def _load_slab_rows(ref, rows):
    return jnp.concatenate([ref[pl.ds(c, rows, stride=SLAB), :] for c in range(SLAB)], axis=1)


def _adaln_kernel(c_ref, w_ref, b_ref, o_ref):
    s = c_ref[...]
    s = s * jax.nn.sigmoid(s)
    o_ref[...] = jnp.dot(s.astype(BF16), w_ref[...].astype(BF16),
                         preferred_element_type=F32) + b_ref[...]


def _adaln(cond, w, b):
    d, n = w.shape
    tn = 1024
    return pl.pallas_call(
        _adaln_kernel,
        grid=(n // tn,),
        in_specs=[pl.BlockSpec((8, d), lambda j: (0, 0)),
                  pl.BlockSpec((d, tn), lambda j: (0, j)),
                  pl.BlockSpec((1, tn), lambda j: (0, j))],
        out_specs=pl.BlockSpec((8, tn), lambda j: (0, j)),
        out_shape=jax.ShapeDtypeStruct((8, n), F32),
        compiler_params=_cparams(("arbitrary",)),
        name="adaln",
    )(cond, w, b)


INPROJ_COLS = 1024


def _inproj_kernel(x_ref, g_ref, sh_ref, sc_ref, w_ref, wg_ref, proj_ref, gate_ref):
    x = x_ref[...]
    y = x * lax.rsqrt(jnp.mean(x * x, axis=-1, keepdims=True) + EPS) * g_ref[...]
    xn = (y * (1.0 + sc_ref[...]) + sh_ref[...]).astype(BF16)
    gate_ref[...] = jnp.dot(xn, wg_ref[...], preferred_element_type=F32)
    for j in range(w_ref.shape[1] // INPROJ_COLS):
        cols = slice(j * INPROJ_COLS, (j + 1) * INPROJ_COLS)
        proj_ref[:, cols] = jnp.dot(xn, w_ref[:, cols], preferred_element_type=F32).astype(BF16)


def _inproj(x, g, sh, sc, w, wg, tm):
    bsz, s, d = x.shape
    p = w.shape[1]
    nt = s // tm
    x2 = x.reshape(bsz * s, d)
    resident = lambda shape: pl.BlockSpec(shape, lambda b, i: (0, 0), pipeline_mode=pl.Buffered(1))
    return pl.pallas_call(
        _inproj_kernel,
        grid=(bsz, nt),
        in_specs=[pl.BlockSpec((tm, d), lambda b, i: (b * nt + i, 0)),
                  pl.BlockSpec((1, d), lambda b, i: (0, 0)),
                  pl.BlockSpec((None, 1, d), lambda b, i: (b, 0, 0)),
                  pl.BlockSpec((None, 1, d), lambda b, i: (b, 0, 0)),
                  resident((d, p)),
                  resident((d, LANES))],
        out_specs=[pl.BlockSpec((tm, p), lambda b, i: (b * nt + i, 0)),
                   pl.BlockSpec((tm, LANES), lambda b, i: (b * nt + i, 0))],
        out_shape=[jax.ShapeDtypeStruct((bsz * s, p), BF16),
                   jax.ShapeDtypeStruct((bsz * s, LANES), F32)],
        compiler_params=_cparams(("arbitrary", "arbitrary")),
        name="inproj",
    )(x2, g, sh, sc, w, wg)


def _log_sigmoid(x):
    return jnp.minimum(x, 0.0) - jnp.log1p(jnp.exp(-jnp.abs(x)))


def _gates_kernel(g_ref, b_ref, gc_ref, gr_ref):
    tm = g_ref.shape[0]
    row = lax.broadcasted_iota(jnp.int32, (tm, LANES), 0)
    lane = lax.broadcasted_iota(jnp.int32, (tm, LANES), 1)
    gp = GATE_SOFT_CAP * jnp.tanh((g_ref[...] + b_ref[...]) / GATE_SOFT_CAP)
    is_f = ((lane >> 2) & 1) == 1
    fwd_lane = lane < 2 * N_HEADS
    lf = jnp.where(is_f, _log_sigmoid(gp), 0.0)
    r2 = lax.broadcasted_iota(jnp.int32, (CHUNK, CHUNK), 0)
    c2 = lax.broadcasted_iota(jnp.int32, (CHUNK, CHUNK), 1)
    lower = (r2 >= c2).astype(F32)
    upper = (r2 <= c2).astype(F32)
    lane_c = lax.broadcasted_iota(jnp.int32, (CHUNK, LANES), 1)
    cums = []
    for c in range(tm // CHUNK):
        lf_c = lf[c * CHUNK:(c + 1) * CHUNK]
        cf = jnp.dot(lower, lf_c, precision=lax.Precision.HIGHEST, preferred_element_type=F32)
        cb = jnp.dot(upper, lf_c, precision=lax.Precision.HIGHEST, preferred_element_type=F32)
        cums.append(jnp.where(lane_c < 2 * N_HEADS, cf, cb))
    cdir = jnp.concatenate(cums, axis=0)
    a = jnp.where(is_f, cdir, gp - pltpu.roll(cdir, LANES - N_HEADS, 1))

    pos = row % CHUNK
    x = a
    k = 1
    while k < CHUNK:
        from_before = jnp.where(pos >= k, pltpu.roll(x, k, 0), -jnp.inf)
        from_after = jnp.where(pos < CHUNK - k, pltpu.roll(x, tm - k, 0), -jnp.inf)
        x = jnp.maximum(x, jnp.where(fwd_lane, from_before, from_after))
        k *= 2
    gc_ref[...] = jnp.where(is_f, a, x)

    lane_1 =lax.broadcasted_iota(jnp.int32, (1, LANES), 1)
    for c in range(tm // CHUNK):
        lo = c * CHUNK
        xc, ac = x[lo:lo + CHUNK], a[lo:lo + CHUNK]
        end_max = jnp.where(lane_1 < 2 * N_HEADS, xc[CHUNK - 1:CHUNK], xc[0:1])
        e = jnp.exp(ac - end_max)
        rows = jnp.where(((lane_c >> 2) & 1) == 1, pltpu.roll(e, N_HEADS, 1), ac)
        gr_ref[:, lo:lo + CHUNK] = rows.T[:N_GATE_COLS, :]


def _gates(gpre, b_if, tm):
    t = gpre.shape[0]
    return pl.pallas_call(
        _gates_kernel,
        grid=(t // tm,),
        in_specs=[pl.BlockSpec((tm, LANES), lambda i: (i, 0)),
                  pl.BlockSpec((1, LANES), lambda i: (0, 0))],
        out_specs=[pl.BlockSpec((tm, LANES), lambda i: (i, 0)),
                   pl.BlockSpec((N_GATE_COLS, tm), lambda i: (0, i))],
        out_shape=[jax.ShapeDtypeStruct((t, LANES), F32),
                   jax.ShapeDtypeStruct((N_GATE_COLS, t), F32)],
        compiler_params=_cparams(("arbitrary",)),
        name="gates",
    )(gpre, b_if)


DVX = DV + LANES
MLSTM_CHUNKS_PER_STEP = 4


def _mlstm_chunk(q, k, v_ext, rmax_col, b_col, r_row, e_row, b_last, rmax_last, mask, cx, m_st):
    scale = DK ** -0.5
    mb = jnp.maximum(m_st, jnp.broadcast_to(rmax_col, (CHUNK, CHUNK)))
    w_intra = jnp.exp(jnp.where(mask, r_row - mb, -jnp.inf))
    w_state = jnp.exp(m_st - mb)
    qk = lax.dot_general(q, k, (((1,), (1,)), ((), ())), preferred_element_type=F32)
    s = qk * (w_intra * scale)
    lhs = jnp.concatenate([s.astype(BF16), (q.astype(F32) * (w_state * scale)).astype(BF16)], axis=1)
    rhs = jnp.concatenate([v_ext, cx.astype(BF16)], axis=0)
    nx = jnp.dot(lhs, rhs, preferred_element_type=F32)
    denom = jnp.maximum(jnp.abs(nx[:, DV:]), jnp.exp(-(jnp.broadcast_to(b_col, (CHUNK, CHUNK)) + mb)))
    h = nx[:, :DV] / jnp.concatenate([denom, denom], axis=1)
    ke_t = (k.T.astype(F32) * e_row).astype(BF16)
    c_loc = jnp.dot(ke_t, v_ext, preferred_element_type=F32)
    m_loc = b_last + rmax_last
    m_new = jnp.maximum(b_last + m_st, m_loc)
    return h, jnp.exp(b_last + m_st - m_new) * cx + jnp.exp(m_loc - m_new) * c_loc, m_new


def _mlstm_kernel(qf_ref, kf_ref, vf_ref, gcf_ref, grf_ref, qb_ref, kb_ref, vb_ref, gcb_ref, grb_ref,
                  c0_ref, m0_ref, hf_ref, hb_ref, cout_ref, mout_ref, m_scr, *c_scrs):
    c = pl.program_id(1)

    @pl.when(c == 0)
    def _():
        for idx, c_scr in enumerate(c_scrs):
            c_scr[...] = c0_ref[idx]
        m_scr[...] = m0_ref[...]

    row = lax.broadcasted_iota(jnp.int32, (CHUNK, CHUNK), 0)
    col = lax.broadcasted_iota(jnp.int32, (CHUNK, CHUNK), 1)
    ones = jnp.ones((CHUNK, LANES), BF16)
    m_all = m_scr[...]
    dirs = ((qf_ref, kf_ref, vf_ref, gcf_ref, grf_ref, hf_ref, 0, CHUNK - 1, col <= row),
            (qb_ref, kb_ref, vb_ref, gcb_ref, grb_ref, hb_ref, 2 * N_HEADS, 0, col >= row))
    n_sub = qf_ref.shape[0] // CHUNK
    m_news = []
    for di, (q_ref, k_ref, v_ref, gc_ref, gr_ref, h_ref, off, last, mask) in enumerate(dirs):
        order = range(n_sub) if di == 0 else range(n_sub - 1, -1, -1)
        for hd in range(N_HEADS):
            idx = di * N_HEADS + hd
            lr, lb = off + hd, off + N_HEADS + hd
            cx, m_st = c_scrs[idx][...], m_all[idx][:, 0:1]
            for sub in order:
                r0 = sub * CHUNK
                rows = slice(r0, r0 + CHUNK)
                v_ext = jnp.concatenate([v_ref[rows, hd * DV:(hd + 1) * DV], ones], axis=1)
                h, cx, m_st = _mlstm_chunk(
                    q_ref[rows, hd * DK:(hd + 1) * DK], k_ref[rows, hd * DK:(hd + 1) * DK], v_ext,
                    gc_ref[rows, lr:lr + 1], gc_ref[rows, lb:lb + 1],
                    gr_ref[lr:lr + 1, rows], gr_ref[lb:lb + 1, rows],
                    gc_ref[r0 + last:r0 + last + 1, lb:lb + 1], gc_ref[r0 + last:r0 + last + 1, lr:lr + 1],
                    mask, cx, m_st)
                h_ref[rows, hd * DV:(hd + 1) * DV] = h
            c_scrs[idx][...] = cx
            m_news.append(jnp.broadcast_to(m_st, (1, LANES)))
    for idx, m_new in enumerate(m_news):
        m_scr[idx] = m_new

    @pl.when(c == pl.num_programs(1) - 1)
    def _():
        for idx, c_scr in enumerate(c_scrs):
            cout_ref[idx] = c_scr[...]
        mout_ref[...] = m_scr[...]


def _mlstm(proj, gcol, grow, bsz, s, c0, m0):
    rows = min(MLSTM_CHUNKS_PER_STEP * CHUNK, s)
    nc = s // rows
    t = bsz * s
    fwd = lambda b, c: b * nc + c
    bwd = lambda b, c: b * nc + (nc - 1 - c)

    def specs(ci):
        return [pl.BlockSpec((rows, QK_COLS), lambda b, c: (ci(b, c), 0)),
                pl.BlockSpec((rows, QK_COLS), lambda b, c: (ci(b, c), 1)),
                pl.BlockSpec((rows, MLSTM_WIDTH), lambda b, c: (ci(b, c), 1)),
                pl.BlockSpec((rows, LANES), lambda b, c: (ci(b, c), 0)),
                pl.BlockSpec((N_GATE_COLS, rows), lambda b, c: (0, ci(b, c)))]

    st_specs = [pl.BlockSpec((None, 2 * N_HEADS, DK, DVX), lambda b, c: (b, 0, 0, 0)),
                pl.BlockSpec((None, 2 * N_HEADS, 1, LANES), lambda b, c: (b, 0, 0, 0))]
    return pl.pallas_call(
        _mlstm_kernel,
        grid=(bsz, nc),
        in_specs=specs(fwd) + specs(bwd) + st_specs,
        out_specs=[pl.BlockSpec((rows, MLSTM_WIDTH), lambda b, c: (fwd(b, c), 0)),
                   pl.BlockSpec((rows, MLSTM_WIDTH), lambda b, c: (bwd(b, c), 0))] + st_specs,
        out_shape=[jax.ShapeDtypeStruct((t, MLSTM_WIDTH), F32),
                   jax.ShapeDtypeStruct((t, MLSTM_WIDTH), F32),
                   jax.ShapeDtypeStruct(c0.shape, F32),
                   jax.ShapeDtypeStruct(m0.shape, F32)],
        scratch_shapes=[pltpu.VMEM((2 * N_HEADS, 1, LANES), F32)]
        + [pltpu.VMEM((DK, DVX), F32) for _ in range(2 * N_HEADS)],
        compiler_params=_cparams(("arbitrary", "arbitrary")),
        name="mlstm",
    )(proj, proj, proj, gcol, grow, proj, proj, proj, gcol, grow, c0, m0)


MIX_ROWS = 256


def _mixout_kernel(o_ref, cb_ref, cc_ref, cx_ref, ccp_ref, cxp_ref, ccn_ref, cxn_ref, hf_ref, hb_ref, x_ref,
                   cw_ref, ng_ref, wout_ref, gt_ref, gffn_ref, shf_ref, scf_ref, wr_ref, br_ref,
                   x1_ref, xn2_ref, idx_ref, tw_ref):
    i = pl.program_id(1)
    tm = x_ref.shape[0]
    cw = cw_ref[...]

    has_prev = jnp.where(i > 0, 1.0, 0.0)
    has_next = jnp.where(i < pl.num_programs(1) - 1, 1.0, 0.0)
    up = ccp_ref[...].astype(F32) * cxp_ref[...].astype(F32) * has_prev
    un = ccn_ref[...].astype(F32) * cxn_ref[...].astype(F32) * has_next
    uv = cc_ref[:, CONV_HALF:].astype(F32) * cx_ref[:, CONV_HALF:].astype(F32)
    ext = jnp.concatenate([up, uv, un], axis=0)

    pos = lax.broadcasted_iota(jnp.int32, (MIX_ROWS, CONV_HALF), 0) & (GRID_W - 1)
    lane_f = lax.broadcasted_iota(jnp.int32, (MIX_ROWS, LANES), 1).astype(F32)
    lane4 = lax.broadcasted_iota(jnp.int32, (MIX_ROWS, TOP_K), 1)

    for r0 in range(0, tm, MIX_ROWS):
        rows = slice(r0, r0 + MIX_ROWS)

        uh = cc_ref[rows, :CONV_HALF].astype(F32) * cx_ref[rows, :CONV_HALF].astype(F32)
        left = jnp.where(pos == 0, 0.0, pltpu.roll(uh, 1, 0))
        right = jnp.where(pos == GRID_W - 1, 0.0, pltpu.roll(uh, MIX_ROWS - 1, 0))
        yh = cw[0:1, :CONV_HALF] * left + cw[1:2, :CONV_HALF] * uh + cw[2:3, :CONV_HALF] * right
        yv = (cw[0:1, CONV_HALF:] * ext[r0:r0 + MIX_ROWS]
              + cw[1:2, CONV_HALF:] * ext[r0 + GRID_W:r0 + GRID_W + MIX_ROWS]
              + cw[2:3, CONV_HALF:] * ext[r0 + 2 * GRID_W:r0 + 2 * GRID_W + MIX_ROWS])
        yc = cb_ref[rows, :].astype(F32) * jnp.concatenate([yh, yv], axis=1)

        hs = hf_ref[rows, :] + hb_ref[rows, :]
        parts = []
        for hd in range(N_HEADS):
            seg = hs[:, hd * DV:(hd + 1) * DV]
            parts.append(seg * lax.rsqrt(jnp.mean(seg * seg, axis=-1, keepdims=True) + EPS))
        hm = jnp.concatenate(parts, axis=1) * ng_ref[...] * jax.nn.sigmoid(o_ref[rows, :].astype(F32))

        z = jnp.concatenate([hm.astype(BF16), yc.astype(BF16)], axis=1)
        x1 = x_ref[rows, :] + gt_ref[...] * jnp.dot(z, wout_ref[...], preferred_element_type=F32)
        x1_ref[rows, :] = x1

        y = x1 * lax.rsqrt(jnp.mean(x1 * x1, axis=-1, keepdims=True) + EPS) * gffn_ref[...]
        xn2 = y * (1.0 + scf_ref[...]) + shf_ref[...]
        _store_slab_rows(xn2_ref, r0, _pack_pairs(xn2))

        logits = jnp.dot(xn2.astype(BF16), wr_ref[...], preferred_element_type=F32) + br_ref[...]
        vals, idxs = [], []
        for _ in range(TOP_K):
            mx = jnp.max(logits, axis=-1, keepdims=True)
            ik = jnp.min(jnp.where(logits == mx, lane_f, float(LANES)), axis=-1, keepdims=True)
            vals.append(mx)
            idxs.append(ik)
            logits = jnp.where(lane_f == ik, -jnp.inf, logits)
        es = [jnp.exp(v - vals[0]) for v in vals]
        tot = es[0] + es[1] + es[2] + es[3]
        idx_out = jnp.zeros((MIX_ROWS, TOP_K), F32)
        tw_out = jnp.zeros((MIX_ROWS, TOP_K), F32)
        for kk in range(TOP_K):
            idx_out = jnp.where(lane4 == kk, idxs[kk], idx_out)
            tw_out = jnp.where(lane4 == kk, es[kk] / tot, tw_out)
        idx_ref[rows, :] = idx_out.astype(jnp.int32)
        tw_ref[rows, :] = tw_out


def _mixout(proj, hf, hb, x2, conv_w, norm_g, w_out, gt, g_ffn, sh_f, sc_f, w_r, b_r, bsz, s, tm):
    t, d = x2.shape
    nt = s // tm
    rb = tm // GRID_W
    last_rb = t // GRID_W - 1
    row = lambda b, i: b * nt + i
    w = MLSTM_WIDTH
    vec = lambda n: pl.BlockSpec((1, n), lambda b, i: (0, 0))
    per_b = pl.BlockSpec((None, 1, d), lambda b, i: (b, 0, 0))
    halo_prev = lambda cblk: pl.BlockSpec(
        (GRID_W, CONV_HALF), lambda b, i: (jnp.maximum(row(b, i) * rb - 1, 0), cblk))
    halo_next = lambda cblk: pl.BlockSpec(
        (GRID_W, CONV_HALF), lambda b, i: (jnp.minimum((row(b, i) + 1) * rb, last_rb), cblk))
    return pl.pallas_call(
        _mixout_kernel,
        grid=(bsz, nt),
        in_specs=[pl.BlockSpec((tm, w), lambda b, i: (row(b, i), 2)),
                  pl.BlockSpec((tm, w), lambda b, i: (row(b, i), 3)),
                  pl.BlockSpec((tm, w), lambda b, i: (row(b, i), 4)),
                  pl.BlockSpec((tm, w), lambda b, i: (row(b, i), 5)),
                  halo_prev(9), halo_prev(11), halo_next(9), halo_next(11),
                  pl.BlockSpec((tm, w), lambda b, i: (row(b, i), 0)),
                  pl.BlockSpec((tm, w), lambda b, i: (row(b, i), 0)),
                  pl.BlockSpec((tm, d), lambda b, i: (row(b, i), 0)),
                  pl.BlockSpec((3, CONV_WIDTH), lambda b, i: (0, 0)),
                  vec(w),
                  pl.BlockSpec((d, d), lambda b, i: (0, 0)),
                  per_b, vec(d), per_b, per_b,
                  pl.BlockSpec((d, LANES), lambda b, i: (0, 0)),
                  vec(LANES)],
        out_specs=[pl.BlockSpec((tm, d), lambda b, i: (row(b, i), 0)),
                   pl.BlockSpec((tm * SLAB, LANES), lambda b, i: (row(b, i), 0)),
                   pl.BlockSpec((tm, TOP_K), lambda b, i: (row(b, i), 0)),
                   pl.BlockSpec((tm, TOP_K), lambda b, i: (row(b, i), 0))],
        out_shape=[jax.ShapeDtypeStruct((t, d), F32),
                   jax.ShapeDtypeStruct((t * SLAB, LANES), jnp.uint32),
                   jax.ShapeDtypeStruct((t, TOP_K), jnp.int32),
                   jax.ShapeDtypeStruct((t, TOP_K), F32)],
        compiler_params=_cparams(("arbitrary", "arbitrary")),
        name="mixout",
    )(proj, proj, proj, proj, proj, proj, proj, proj, hf, hb, x2,
      conv_w, norm_g, w_out, gt, g_ffn, sh_f, sc_f, w_r, b_r)


def _rank_kernel(idx_ref, rank_ref, cnt_ref, run_scr):
    @pl.when(pl.program_id(0) == 0)
    def _():
        run_scr[...] = jnp.zeros_like(run_scr)

    tm = idx_ref.shape[0]
    idx = idx_ref[...]
    lane = lax.broadcasted_iota(jnp.int32, (tm, LANES), 1)
    hits = [lane == idx[:, kk:kk + 1] for kk in range(TOP_K)]
    onehot = jnp.zeros((tm, LANES), F32)
    for hit in hits:
        onehot = onehot + hit.astype(F32)
    r = lax.broadcasted_iota(jnp.int32, (tm, tm), 0)
    c = lax.broadcasted_iota(jnp.int32, (tm, tm), 1)
    before = jnp.dot((c < r).astype(BF16), onehot.astype(BF16), preferred_element_type=F32) + run_scr[...]
    lane4 = lax.broadcasted_iota(jnp.int32, (tm, TOP_K), 1)
    rank = jnp.zeros((tm, TOP_K), F32)
    for kk, hit in enumerate(hits):
        rk = jnp.sum(jnp.where(hit, before, 0.0), axis=-1, keepdims=True)
        rank = jnp.where(lane4 == kk, rk, rank)
    rank_ref[...] = rank.astype(jnp.int32)
    run_scr[...] = run_scr[...] + jnp.sum(onehot, axis=0, keepdims=True)
    cnt_ref[...] = run_scr[...]


def _rank(idx, tm):
    t = idx.shape[0]
    return pl.pallas_call(
        _rank_kernel,
        grid=(t // tm,),
        in_specs=[pl.BlockSpec((tm, TOP_K), lambda i: (i, 0))],
        out_specs=[pl.BlockSpec((tm, TOP_K), lambda i: (i, 0)),
                   pl.BlockSpec((1, LANES), lambda i: (0, 0))],
        out_shape=[jax.ShapeDtypeStruct((t, TOP_K), jnp.int32),
                   jax.ShapeDtypeStruct((1, LANES), F32)],
        scratch_shapes=[pltpu.VMEM((1, LANES), F32)],
        compiler_params=_cparams(("arbitrary",)),
        name="rank",
    )(idx)


def _largest_pad_piece():
    return 1 << ((ROW_BLOCK - 1).bit_length() - 1)


def _dispatch_kernel(pad_ref, dest_hbm, xn_ref, xs_hbm, dsm, zeros_scr, sem_idx, sem_rows, sem_pad):
    i = pl.program_id(0)
    tm = xn_ref.shape[0] // SLAB
    n_idx = tm * TOP_K
    idx_copy = pltpu.make_async_copy(dest_hbm.at[pl.ds(i * n_idx, n_idx)], dsm, sem_idx)
    idx_copy.start()

    def slab(ref, row, n_rows=1):
        return ref.at[pl.ds(pl.multiple_of(row * SLAB, SLAB), n_rows * SLAB), :]

    def for_each_pad_piece(fn):
        def per_expert(e, carry):
            off = pad_ref[2 * e]
            n = pad_ref[2 * e + 1]
            size = _largest_pad_piece()
            while size >= 1:
                take = (n & size) != 0

                @pl.when(take)
                def _(off=off, size=size):
                    fn(pltpu.make_async_copy(slab(zeros_scr, 0, size), slab(xs_hbm, off, size), sem_pad))

                off = off + jnp.where(take, size, 0)
                size //= 2
            return carry
        lax.fori_loop(0, N_EXPERTS, per_expert, 0)

    @pl.when(i == 0)
    def _():
        zeros_scr[...] = jnp.zeros_like(zeros_scr)
        for_each_pad_piece(lambda cp: cp.start())
        for_each_pad_piece(lambda cp: cp.wait())

    idx_copy.wait()

    def row_copy(t, kk):
        return pltpu.make_async_copy(slab(xn_ref, t), slab(xs_hbm, dsm[t * TOP_K + kk]), sem_rows)

    def issue(t, carry):
        for kk in range(TOP_K):
            row_copy(t, kk).start(priority=kk % 2)
        return carry

    lax.fori_loop(0, tm, issue, 0, unroll=ISSUE_UNROLL)
    pltpu.make_async_copy(slab(xs_hbm, 0, n_idx), slab(xs_hbm, 0, n_idx), sem_rows).wait()


def _dispatch(pad_info, dest_flat, xn2, n_rows, tm):
    t = xn2.shape[0] // SLAB
    return pl.pallas_call(
        _dispatch_kernel,
        grid_spec=pltpu.PrefetchScalarGridSpec(
            num_scalar_prefetch=1,
            grid=(t // tm,),
            in_specs=[pl.BlockSpec(memory_space=pl.ANY),
                      pl.BlockSpec((tm * SLAB, LANES), lambda i, pad: (i, 0))],
            out_specs=pl.BlockSpec(memory_space=pl.ANY),
            scratch_shapes=[pltpu.SMEM((tm * TOP_K,), jnp.int32),
                            pltpu.VMEM((_largest_pad_piece() * SLAB, LANES), xn2.dtype),
                            pltpu.SemaphoreType.DMA(()),
                            pltpu.SemaphoreType.DMA(()),
                            pltpu.SemaphoreType.DMA(())]),
        out_shape=jax.ShapeDtypeStruct((n_rows * SLAB, LANES), xn2.dtype),
        compiler_params=_cparams(("arbitrary",)),
        name="dispatch",
    )(pad_info, dest_flat, xn2)


def _new_expert(be_ref, j):
    return jnp.logical_or(j == 0, be_ref[j] != be_ref[jnp.maximum(j - 1, 0)])


GU_COLS = 512


def _full_or_half_block(valid, rows, compute):
    half, quarter = rows // 2, rows // 4

    @pl.when(valid > half)
    def _():
        compute(rows)

    @pl.when(jnp.logical_and(valid > quarter, valid <= half))
    def _():
        compute(half)

    @pl.when(valid <= quarter)
    def _():
        compute(quarter)


def _expert_gu_kernel(be_ref, nu_ref, nxt_ref, bv_ref, xs_ref, w_hbm, bg_ref, bu_ref, act_ref,
                      stage_g, stage_u, wg_scr, wu_scr, sem):
    n = pl.program_id(0)
    j = pl.program_id(1)
    nt = pl.num_programs(0)
    tn = wg_scr.shape[1]

    def weight_copies(e, nn):
        col_g = pl.multiple_of(nn * tn, tn)
        col_u = pl.multiple_of((nt + nn) * tn, tn)
        return (pltpu.make_async_copy(w_hbm.at[e, :, pl.ds(col_g, tn)], stage_g, sem.at[0]),
                pltpu.make_async_copy(w_hbm.at[e, :, pl.ds(col_u, tn)], stage_u, sem.at[1]))

    @pl.when(j < nu_ref[0])
    def _():
        e = be_ref[j]

        @pl.when(_new_expert(be_ref, j))
        def _():
            @pl.when(jnp.logical_and(n == 0, j == 0))
            def _():
                for cp in weight_copies(e, n):
                    cp.start()

            for cp in weight_copies(e, n):
                cp.wait()
            wg_scr[...] = stage_g[...].astype(BF16)
            wu_scr[...] = stage_u[...].astype(BF16)

            e_next = nxt_ref[e]
            in_pass = e_next >= 0

            @pl.when(jnp.logical_or(in_pass, n + 1 < nt))
            def _():
                for cp in weight_copies(jnp.where(in_pass, e_next, be_ref[0]), jnp.where(in_pass, n, n + 1)):
                    cp.start()

        def compute(rows):
            x = _unpack_pairs(_load_slab_rows(xs_ref, rows), SLAB * LANES).astype(BF16)
            for c0 in range(0, tn, GU_COLS):
                cols = slice(c0, c0 + GU_COLS)
                g = jnp.dot(x, wg_scr[:, cols], preferred_element_type=F32) + bg_ref[:, cols]
                u = jnp.dot(x, wu_scr[:, cols], preferred_element_type=F32) + bu_ref[:, cols]
                gate = jnp.minimum(g, SWIGLU_LIMIT)
                up = jnp.clip(u, -SWIGLU_LIMIT, SWIGLU_LIMIT)
                act_ref[0:rows, cols] = ((up + 1.0) * gate * jax.nn.sigmoid(SWIGLU_ALPHA * gate)).astype(BF16)

        _full_or_half_block(bv_ref[j], xs_ref.shape[0] // SLAB, compute)


def _expert_gu(blk_expert, n_used, nxt_expert, blk_valid, xs, w_gu, b_gu, tn):
    n_rows = xs.shape[0] // SLAB
    d = w_gu.shape[1]
    dff = w_gu.shape[2] // 2
    nt = dff // tn
    nb = n_rows // ROW_BLOCK
    blk = lambda j, nu: jnp.minimum(j, nu[0] - 1)
    exp = lambda j, be, nu: be[blk(j, nu)]
    return pl.pallas_call(
        _expert_gu_kernel,
        grid_spec=pltpu.PrefetchScalarGridSpec(
            num_scalar_prefetch=4,
            grid=(nt, nb),
            in_specs=[pl.BlockSpec((ROW_BLOCK * SLAB, LANES), lambda n, j, be, nu, nx, bv: (blk(j, nu), 0)),
                      pl.BlockSpec(memory_space=pl.ANY),
                      pl.BlockSpec((None, 1, tn), lambda n, j, be, nu, nx, bv: (exp(j, be, nu), 0, n)),
                      pl.BlockSpec((None, 1, tn), lambda n, j, be, nu, nx, bv: (exp(j, be, nu), 0, nt + n))],
            out_specs=pl.BlockSpec((ROW_BLOCK, tn), lambda n, j, be, nu, nx, bv: (blk(j, nu), n)),
            scratch_shapes=[pltpu.VMEM((d, tn), F32), pltpu.VMEM((d, tn), F32),
                            pltpu.VMEM((d, tn), BF16), pltpu.VMEM((d, tn), BF16),
                            pltpu.SemaphoreType.DMA((2,))]),
        out_shape=jax.ShapeDtypeStruct((n_rows, dff), BF16),
        compiler_params=_cparams(("arbitrary", "arbitrary")),
        name="expert_gu",
    )(blk_expert, n_used, nxt_expert, blk_valid, xs, w_gu, b_gu, b_gu)


def _expert_down_kernel(be_ref, nu_ref, nxt_ref, bv_ref, act_ref, w_hbm, b_ref, y_ref, stage, w_scr, sem):
    j = pl.program_id(0)

    def weight_copy(e):
        return pltpu.make_async_copy(w_hbm.at[e], stage, sem)

    @pl.when(j < nu_ref[0])
    def _():
        e = be_ref[j]

        @pl.when(_new_expert(be_ref, j))
        def _():
            @pl.when(j == 0)
            def _():
                weight_copy(e).start()

            weight_copy(e).wait()
            w_scr[...] = stage[...].astype(BF16)
            e_next = nxt_ref[e]

            @pl.when(e_next >= 0)
            def _():
                weight_copy(e_next).start()

        def compute(rows):
            act = act_ref[0:rows, :]
            for c0 in range(0, w_scr.shape[1], DOWN_COLS):
                cols = slice(c0, c0 + DOWN_COLS)
                y = _pack_pairs(jnp.dot(act, w_scr[:, cols], preferred_element_type=F32) + b_ref[:, cols])
                for q in range(y.shape[1] // LANES):
                    chunk = c0 // 2 // LANES + q
                    y_ref[pl.ds(chunk, rows, stride=SLAB), :] = y[:, q * LANES:(q + 1) * LANES]

        _full_or_half_block(bv_ref[j], act_ref.shape[0], compute)


def _expert_down(blk_expert, n_used, nxt_expert, blk_valid, act, w_down, b_down):
    n_rows, dff = act.shape
    d = w_down.shape[2]
    nb = n_rows // ROW_BLOCK
    blk = lambda j, nu: jnp.minimum(j, nu[0] - 1)
    exp = lambda j, be, nu: be[blk(j, nu)]
    return pl.pallas_call(
        _expert_down_kernel,
        grid_spec=pltpu.PrefetchScalarGridSpec(
            num_scalar_prefetch=4,
            grid=(nb,),
            in_specs=[pl.BlockSpec((ROW_BLOCK, dff), lambda j, be, nu, nx, bv: (blk(j, nu), 0)),
                      pl.BlockSpec(memory_space=pl.ANY),
                      pl.BlockSpec((None, 1, d), lambda j, be, nu, nx, bv: (exp(j, be, nu), 0, 0))],
            out_specs=pl.BlockSpec((ROW_BLOCK * SLAB, LANES), lambda j, be, nu, nx, bv: (blk(j, nu), 0)),
            scratch_shapes=[pltpu.VMEM((dff, d), F32), pltpu.VMEM((dff, d), BF16),
                            pltpu.SemaphoreType.DMA(())]),
        out_shape=jax.ShapeDtypeStruct((n_rows * SLAB, LANES), jnp.uint32),
        compiler_params=_cparams(("arbitrary",)),
        name="expert_down",
    )(blk_expert, n_used, nxt_expert, blk_valid, act, w_down, b_down)


def _combine_kernel(dest_hbm, y_hbm, x1_ref, tw_ref, gt_ref, gfin_ref, out_ref, dsm, buf, sem_idx, sem_rows):
    i = pl.program_id(1) + pl.program_id(0) * pl.num_programs(1)
    n_steps = pl.num_programs(0) * pl.num_programs(1)
    tm = x1_ref.shape[0]
    n_idx = tm * TOP_K

    def slab(ref, row, n_rows=1):
        return ref.at[pl.ds(pl.multiple_of(row * SLAB, SLAB), n_rows * SLAB), :]

    def idx_copy(tile):
        slot = tile % 2
        return pltpu.make_async_copy(dest_hbm.at[pl.ds(tile * n_idx, n_idx)], dsm.at[slot], sem_idx.at[slot])

    def issue_rows(tile):
        slot = tile % 2

        def issue(t, carry):
            for kk in range(TOP_K):
                pltpu.make_async_copy(slab(y_hbm, dsm[slot, t * TOP_K + kk]), slab(buf.at[slot, kk], t),
                                      sem_rows.at[slot]).start(priority=kk % 2)
            return carry

        lax.fori_loop(0, tm, issue, 0, unroll=ISSUE_UNROLL)

    @pl.when(i == 0)
    def _():
        idx_copy(0).start()
        idx_copy(0).wait()
        issue_rows(0)

        @pl.when(n_steps > 1)
        def _():
            idx_copy(1).start()

    @pl.when(i + 2 < n_steps)
    def _():
        idx_copy(i + 2).start()

    @pl.when(i + 1 < n_steps)
    def _():
        idx_copy(i + 1).wait()
        issue_rows(i + 1)

    slot = i % 2
    for kk in range(TOP_K):
        pltpu.make_async_copy(slab(y_hbm, 0, tm), buf.at[slot, kk], sem_rows.at[slot]).wait()

    tw = tw_ref[...]
    rows = lambda kk: _unpack_pairs(_load_slab_rows(buf.at[slot, kk], tm), DOWN_COLS // 2)
    acc = rows(0) * tw[:, 0:1]
    for kk in range(1, TOP_K):
        acc = acc + rows(kk) * tw[:, kk:kk + 1]
    x2 = x1_ref[...] + gt_ref[...] * acc
    out_ref[...] = x2 * lax.rsqrt(jnp.mean(x2 * x2, axis=-1, keepdims=True) + EPS) * gfin_ref[...]


def _combine(dest_flat, y, x1, tw, gt, g_final, bsz, s, tm):
    t, d = x1.shape
    nt = s // tm
    row = lambda b, i: (b * nt + i, 0)
    return pl.pallas_call(
        _combine_kernel,
        grid=(bsz, nt),
        in_specs=[pl.BlockSpec(memory_space=pl.ANY),
                  pl.BlockSpec(memory_space=pl.ANY),
                  pl.BlockSpec((tm, d), row),
                  pl.BlockSpec((tm, TOP_K), row),
                  pl.BlockSpec((None, 1, d), lambda b, i: (b, 0, 0)),
                  pl.BlockSpec((1, d), lambda b, i: (0, 0))],
        out_specs=pl.BlockSpec((tm, d), row),
        out_shape=jax.ShapeDtypeStruct((t, d), F32),
        scratch_shapes=[pltpu.SMEM((2, tm * TOP_K), jnp.int32),
                        pltpu.VMEM((2, TOP_K, tm * SLAB, LANES), y.dtype),
                        pltpu.SemaphoreType.DMA((2,)),
                        pltpu.SemaphoreType.DMA((2,))],
        compiler_params=_cparams(("arbitrary", "arbitrary")),
        name="combine",
    )(dest_flat, y, x1, tw, gt, g_final)


def _pad_lanes(a, value=0.0):
    return jnp.pad(a, ((0, 0), (0, LANES - a.shape[1])), constant_values=value)


def _routing_tables(idx, rank, counts_f, n_blocks):
    counts = counts_f[0, :N_EXPERTS].astype(jnp.int32)
    padded = (counts + ROW_BLOCK - 1) // ROW_BLOCK * ROW_BLOCK
    pend = jnp.cumsum(padded)
    pstart = pend - padded
    dest = (pstart[idx] + rank).reshape(-1)
    blk_start = jnp.arange(n_blocks, dtype=jnp.int32) * ROW_BLOCK
    blk_expert = jnp.minimum(jnp.sum((pend[None, :] <= blk_start[:, None]).astype(jnp.int32), axis=1),
                             N_EXPERTS - 1)
    n_used = (pend[-1:] // ROW_BLOCK).astype(jnp.int32)
    blk_valid = jnp.clip((pstart + counts)[blk_expert] - blk_start, 0, ROW_BLOCK).astype(jnp.int32)
    pad_info = jnp.stack([pstart + counts, padded - counts], axis=1).reshape(-1).astype(jnp.int32)
    ids = jnp.arange(N_EXPERTS, dtype=jnp.int32)
    later = jnp.where((ids[None, :] > ids[:, None]) & (counts[None, :] > 0), ids[None, :], N_EXPERTS)
    nxt = jnp.min(later, axis=1)
    nxt_expert = jnp.where(nxt == N_EXPERTS, -1, nxt).astype(jnp.int32)
    return dest, blk_expert, n_used, nxt_expert, blk_valid, pad_info


def _layer(x, c, ctx, c_ctx, w_ada, b_ada, g_mix, w_in, b_if, conv_w, norm_g, w_out,
           g_ffn, w_router, b_router, w_gu, b_gu, w_down, b_down, g_final):
    bsz, s, d = x.shape
    s_ctx = ctx.shape[1]

    cond = jnp.zeros((8, d), F32).at[:bsz].set(c).at[bsz].set(c_ctx)
    mod = _adaln(cond, w_ada, b_ada[None, :])
    sh_m, sc_m, gt_m, sh_f, sc_f, gt_f = [m[:, None, :] for m in jnp.split(mod, N_MOD, axis=-1)]
    lat = lambda m: m[:bsz]
    ctxm = lambda m: jnp.broadcast_to(m[bsz:bsz + 1], (bsz, 1, d))

    g0 = 2 * QK_COLS + 2 * MLSTM_WIDTH
    w_main = jnp.concatenate([w_in[:, :g0], w_in[:, g0 + N_GATE_COLS:]], axis=1).astype(BF16)
    w_gate = _pad_lanes(w_in[:, g0:g0 + N_GATE_COLS]).astype(BF16)
    b_gate = _pad_lanes(b_if[None, :])
    g_mix2 = g_mix[None, :]

    proj_c, gpre_c = _inproj(ctx, g_mix2, ctxm(sh_m), ctxm(sc_m), w_main, w_gate, min(s_ctx, 512))
    gcol_c, grow_c = _gates(gpre_c, b_gate, 512)
    zeros_state = (jnp.zeros((bsz, 2 * N_HEADS, DK, DVX), F32),
                   jnp.zeros((bsz, 2 * N_HEADS, 1, LANES), F32))
    _, _, c0, m0 = _mlstm(proj_c, gcol_c, grow_c, bsz, s_ctx, *zeros_state)

    proj, gpre = _inproj(x, g_mix2, lat(sh_m), lat(sc_m), w_main, w_gate, 512)
    gcol, grow = _gates(gpre, b_gate, 512)
    hf, hb, _, _ = _mlstm(proj, gcol, grow, bsz, s, c0, m0)
    x1, xn2, idx, tw = _mixout(
        proj, hf, hb, x.reshape(bsz * s, d), conv_w, norm_g[None, :], w_out.astype(BF16), lat(gt_m),
        g_ffn[None, :], lat(sh_f), lat(sc_f), _pad_lanes(w_router).astype(BF16),
        _pad_lanes(b_router[None, :], NEG_BIG), bsz, s, 512)

    t = bsz * s
    n_blocks = -(-(t * TOP_K) // ROW_BLOCK) + N_EXPERTS
    rank, counts = _rank(idx, 512)
    dest, blk_expert, n_used, nxt_expert, blk_valid, pad_info = _routing_tables(idx, rank, counts, n_blocks)
    xs = _dispatch(pad_info, dest, xn2, n_blocks * ROW_BLOCK, 4096)
    act = _expert_gu(blk_expert, n_used, nxt_expert, blk_valid, xs, w_gu, b_gu[:, None, :], 1024)
    y = _expert_down(blk_expert, n_used, nxt_expert, blk_valid, act, w_down, b_down[:, None, :])
    out = _combine(dest, y, x1, tw, lat(gt_f), g_final[None, :], bsz, s, 256)
    return out.reshape(bsz, s, d)


def kernel(x, c, ctx, c_ctx, w_ada, b_ada, g_mix, w_in, b_if, conv_w, mlstm_norm_g, w_out,
           g_ffn, w_router, b_router, w_gu, b_gu, w_down, b_down, g_final):
    return _layer(x, c, ctx, c_ctx, w_ada[0], b_ada[0], g_mix[0], w_in[0], b_if[0], conv_w[0],
                  mlstm_norm_g[0], w_out[0], g_ffn[0], w_router[0], b_router[0], w_gu[0], b_gu[0],
                  w_down[0], b_down[0], g_final)
```

```python
import functools

import jax
import jax.numpy as jnp
from jax import lax
from jax.experimental import pallas as pl
from jax.experimental.pallas import tpu as pltpu

F32 = jnp.float32
BF16 = jnp.bfloat16

N_HEADS = 4
DK = 128
DV = 256
QK_COLS = N_HEADS * DK
MLSTM_WIDTH = N_HEADS * DV
CONV_WIDTH = 1024
CONV_HALF = CONV_WIDTH // 2
N_GATE_COLS = 4 * N_HEADS
GRID_W = 64
CHUNK = 128
GATE_SOFT_CAP = 15.0
N_EXPERTS = 32
TOP_K = 4
SWIGLU_LIMIT = 7.0
SWIGLU_ALPHA = 1.702
N_MOD = 6
EPS = 1e-6
LANES = 128
SUBLANES = 8
ROW_BLOCK = 1024
DOWN_COLS = 1024
ISSUE_UNROLL = 16
NEG_BIG = -1e30
VMEM_LIMIT = 56 * 1024 * 1024


def _cparams(sem):
    return pltpu.CompilerParams(dimension_semantics=sem, vmem_limit_bytes=VMEM_LIMIT)


def _pack_pairs(x):
    bits = lax.bitcast_convert_type(x.astype(BF16).astype(F32), jnp.uint32)
    g = x.shape[1] // 2
    return bits[:, :g] | (bits[:, g:] >> 16)


def _unpack_pairs(p, group):
    hi = lax.bitcast_convert_type(p & jnp.uint32(0xFFFF0000), F32)
    lo = lax.bitcast_convert_type(p << 16, F32)
    parts = []
    for g0 in range(0, p.shape[1], group):
        parts += [hi[:, g0:g0 + group], lo[:, g0:g0 + group]]
    return jnp.concatenate(parts, axis=1)


SLAB = 8


def _store_slab_rows(ref, r0, packed):
    rows = packed.shape[0]
    for c in range(SLAB):
        ref[pl.ds(r0 * SLAB + c, rows, stride=SLAB), :] = packed[:, c * LANES:(c + 1) * LANES]


def _load_slab_rows(ref, rows):
    return jnp.concatenate([ref[pl.ds(c, rows, stride=SLAB), :] for c in range(SLAB)], axis=1)


def _adaln_kernel(c_ref, w_ref, b_ref, o_ref):
    s = c_ref[...]
    s = s * jax.nn.sigmoid(s)
    o_ref[...] = jnp.dot(s.astype(BF16), w_ref[...].astype(BF16),
                         preferred_element_type=F32) + b_ref[...]


def _adaln(cond, w, b):
    d, n = w.shape
    tn = 1024
    return pl.pallas_call(
        _adaln_kernel,
        grid=(n // tn,),
        in_specs=[pl.BlockSpec((8, d), lambda j: (0, 0)),
                  pl.BlockSpec((d, tn), lambda j: (0, j)),
                  pl.BlockSpec((1, tn), lambda j: (0, j))],
        out_specs=pl.BlockSpec((8, tn), lambda j: (0, j)),
        out_shape=jax.ShapeDtypeStruct((8, n), F32),
        compiler_params=_cparams(("arbitrary",)),
        name="adaln",
    )(cond, w, b)


INPROJ_COLS = 1024


def _inproj_kernel(x_ref, g_ref, sh_ref, sc_ref, w_ref, wg_ref, proj_ref, gate_ref):
    x = x_ref[...]
    y = x * lax.rsqrt(jnp.mean(x * x, axis=-1, keepdims=True) + EPS) * g_ref[...]
    xn = (y * (1.0 + sc_ref[...]) + sh_ref[...]).astype(BF16)
    gate_ref[...] = jnp.dot(xn, wg_ref[...], preferred_element_type=F32)
    for j in range(w_ref.shape[1] // INPROJ_COLS):
        cols = slice(j * INPROJ_COLS, (j + 1) * INPROJ_COLS)
        proj_ref[:, cols] = jnp.dot(xn, w_ref[:, cols], preferred_element_type=F32).astype(BF16)


def _inproj(x, g, sh, sc, w, wg, tm):
    bsz, s, d = x.shape
    p = w.shape[1]
    nt = s // tm
    x2 = x.reshape(bsz * s, d)
    resident = lambda shape: pl.BlockSpec(shape, lambda b, i: (0, 0), pipeline_mode=pl.Buffered(1))
    return pl.pallas_call(
        _inproj_kernel,
        grid=(bsz, nt),
        in_specs=[pl.BlockSpec((tm, d), lambda b, i: (b * nt + i, 0)),
                  pl.BlockSpec((1, d), lambda b, i: (0, 0)),
                  pl.BlockSpec((None, 1, d), lambda b, i: (b, 0, 0)),
                  pl.BlockSpec((None, 1, d), lambda b, i: (b, 0, 0)),
                  resident((d, p)),
                  resident((d, LANES))],
        out_specs=[pl.BlockSpec((tm, p), lambda b, i: (b * nt + i, 0)),
                   pl.BlockSpec((tm, LANES), lambda b, i: (b * nt + i, 0))],
        out_shape=[jax.ShapeDtypeStruct((bsz * s, p), BF16),
                   jax.ShapeDtypeStruct((bsz * s, LANES), F32)],
        compiler_params=_cparams(("arbitrary", "arbitrary")),
        name="inproj",
    )(x2, g, sh, sc, w, wg)


def _log_sigmoid(x):
    return jnp.minimum(x, 0.0) - jnp.log1p(jnp.exp(-jnp.abs(x)))


def _gates_kernel(g_ref, b_ref, gc_ref, gr_ref):
    tm = g_ref.shape[0]
    row = lax.broadcasted_iota(jnp.int32, (tm, LANES), 0)
    lane = lax.broadcasted_iota(jnp.int32, (tm, LANES), 1)
    gp = GATE_SOFT_CAP * jnp.tanh((g_ref[...] + b_ref[...]) / GATE_SOFT_CAP)
    is_f = ((lane >> 2) & 1) == 1
    fwd_lane = lane < 2 * N_HEADS
    lf = jnp.where(is_f, _log_sigmoid(gp), 0.0)
    r2 = lax.broadcasted_iota(jnp.int32, (CHUNK, CHUNK), 0)
    c2 = lax.broadcasted_iota(jnp.int32, (CHUNK, CHUNK), 1)
    lower = (r2 >= c2).astype(F32)
    upper = (r2 <= c2).astype(F32)
    lane_c = lax.broadcasted_iota(jnp.int32, (CHUNK, LANES), 1)
    cums = []
    for c in range(tm // CHUNK):
        lf_c = lf[c * CHUNK:(c + 1) * CHUNK]
        cf = jnp.dot(lower, lf_c, precision=lax.Precision.HIGHEST, preferred_element_type=F32)
        cb = jnp.dot(upper, lf_c, precision=lax.Precision.HIGHEST, preferred_element_type=F32)
        cums.append(jnp.where(lane_c < 2 * N_HEADS, cf, cb))
    cdir = jnp.concatenate(cums, axis=0)
    a = jnp.where(is_f, cdir, gp - pltpu.roll(cdir, LANES - N_HEADS, 1))

    pos = row % CHUNK
    x = a
    k = 1
    while k < CHUNK:
        from_before = jnp.where(pos >= k, pltpu.roll(x, k, 0), -jnp.inf)
        from_after = jnp.where(pos < CHUNK - k, pltpu.roll(x, tm - k, 0), -jnp.inf)
        x = jnp.maximum(x, jnp.where(fwd_lane, from_before, from_after))
        k *= 2
    gc_ref[...] = jnp.where(is_f, a, x)

    lane_1 =lax.broadcasted_iota(jnp.int32, (1, LANES), 1)
    for c in range(tm // CHUNK):
        lo = c * CHUNK
        xc, ac = x[lo:lo + CHUNK], a[lo:lo + CHUNK]
        end_max = jnp.where(lane_1 < 2 * N_HEADS, xc[CHUNK - 1:CHUNK], xc[0:1])
        e = jnp.exp(ac - end_max)
        rows = jnp.where(((lane_c >> 2) & 1) == 1, pltpu.roll(e, N_HEADS, 1), ac)
        gr_ref[:, lo:lo + CHUNK] = rows.T[:N_GATE_COLS, :]


def _gates(gpre, b_if, tm):
    t = gpre.shape[0]
    return pl.pallas_call(
        _gates_kernel,
        grid=(t // tm,),
        in_specs=[pl.BlockSpec((tm, LANES), lambda i: (i, 0)),
                  pl.BlockSpec((1, LANES), lambda i: (0, 0))],
        out_specs=[pl.BlockSpec((tm, LANES), lambda i: (i, 0)),
                   pl.BlockSpec((N_GATE_COLS, tm), lambda i: (0, i))],
        out_shape=[jax.ShapeDtypeStruct((t, LANES), F32),
                   jax.ShapeDtypeStruct((N_GATE_COLS, t), F32)],
        compiler_params=_cparams(("arbitrary",)),
        name="gates",
    )(gpre, b_if)


DVX = DV + LANES
MLSTM_CHUNKS_PER_STEP = 4


def _mlstm_chunk(q, k, v_ext, rmax_col, b_col, r_row, e_row, b_last, rmax_last, mask, cx, m_st):
    scale = DK ** -0.5
    mb = jnp.maximum(m_st, jnp.broadcast_to(rmax_col, (CHUNK, CHUNK)))
    w_intra = jnp.exp(jnp.where(mask, r_row - mb, -jnp.inf))
    w_state = jnp.exp(m_st - mb)
    qk = lax.dot_general(q, k, (((1,), (1,)), ((), ())), preferred_element_type=F32)
    s = qk * (w_intra * scale)
    lhs = jnp.concatenate([s.astype(BF16), (q.astype(F32) * (w_state * scale)).astype(BF16)], axis=1)
    rhs = jnp.concatenate([v_ext, cx.astype(BF16)], axis=0)
    nx = jnp.dot(lhs, rhs, preferred_element_type=F32)
    denom = jnp.maximum(jnp.abs(nx[:, DV:]), jnp.exp(-(jnp.broadcast_to(b_col, (CHUNK, CHUNK)) + mb)))
    h = nx[:, :DV] / jnp.concatenate([denom, denom], axis=1)
    ke_t = (k.T.astype(F32) * e_row).astype(BF16)
    c_loc = jnp.dot(ke_t, v_ext, preferred_element_type=F32)
    m_loc = b_last + rmax_last
    m_new = jnp.maximum(b_last + m_st, m_loc)
    return h, jnp.exp(b_last + m_st - m_new) * cx + jnp.exp(m_loc - m_new) * c_loc, m_new


def _mlstm_kernel(qf_ref, kf_ref, vf_ref, gcf_ref, grf_ref, qb_ref, kb_ref, vb_ref, gcb_ref, grb_ref,
                  c0_ref, m0_ref, hf_ref, hb_ref, cout_ref, mout_ref, m_scr, *c_scrs):
    c = pl.program_id(1)

    @pl.when(c == 0)
    def _():
        for idx, c_scr in enumerate(c_scrs):
            c_scr[...] = c0_ref[idx]
        m_scr[...] = m0_ref[...]

    row = lax.broadcasted_iota(jnp.int32, (CHUNK, CHUNK), 0)
    col = lax.broadcasted_iota(jnp.int32, (CHUNK, CHUNK), 1)
    ones = jnp.ones((CHUNK, LANES), BF16)
    m_all = m_scr[...]
    dirs = ((qf_ref, kf_ref, vf_ref, gcf_ref, grf_ref, hf_ref, 0, CHUNK - 1, col <= row),
            (qb_ref, kb_ref, vb_ref, gcb_ref, grb_ref, hb_ref, 2 * N_HEADS, 0, col >= row))
    n_sub = qf_ref.shape[0] // CHUNK
    m_news = []
    for di, (q_ref, k_ref, v_ref, gc_ref, gr_ref, h_ref, off, last, mask) in enumerate(dirs):
        order = range(n_sub) if di == 0 else range(n_sub - 1, -1, -1)
        for hd in range(N_HEADS):
            idx = di * N_HEADS + hd
            lr, lb = off + hd, off + N_HEADS + hd
            cx, m_st = c_scrs[idx][...], m_all[idx][:, 0:1]
            for sub in order:
                r0 = sub * CHUNK
                rows = slice(r0, r0 + CHUNK)
                v_ext = jnp.concatenate([v_ref[rows, hd * DV:(hd + 1) * DV], ones], axis=1)
                h, cx, m_st = _mlstm_chunk(
                    q_ref[rows, hd * DK:(hd + 1) * DK], k_ref[rows, hd * DK:(hd + 1) * DK], v_ext,
                    gc_ref[rows, lr:lr + 1], gc_ref[rows, lb:lb + 1],
                    gr_ref[lr:lr + 1, rows], gr_ref[lb:lb + 1, rows],
                    gc_ref[r0 + last:r0 + last + 1, lb:lb + 1], gc_ref[r0 + last:r0 + last + 1, lr:lr + 1],
                    mask, cx, m_st)
                h_ref[rows, hd * DV:(hd + 1) * DV] = h
            c_scrs[idx][...] = cx
            m_news.append(jnp.broadcast_to(m_st, (1, LANES)))
    for idx, m_new in enumerate(m_news):
        m_scr[idx] = m_new

    @pl.when(c == pl.num_programs(1) - 1)
    def _():
        for idx, c_scr in enumerate(c_scrs):
            cout_ref[idx] = c_scr[...]
        mout_ref[...] = m_scr[...]


def _mlstm(proj, gcol, grow, bsz, s, c0, m0):
    rows = min(MLSTM_CHUNKS_PER_STEP * CHUNK, s)
    nc = s // rows
    t = bsz * s
    fwd = lambda b, c: b * nc + c
    bwd = lambda b, c: b * nc + (nc - 1 - c)

    def specs(ci):
        return [pl.BlockSpec((rows, QK_COLS), lambda b, c: (ci(b, c), 0)),
                pl.BlockSpec((rows, QK_COLS), lambda b, c: (ci(b, c), 1)),
                pl.BlockSpec((rows, MLSTM_WIDTH), lambda b, c: (ci(b, c), 1)),
                pl.BlockSpec((rows, LANES), lambda b, c: (ci(b, c), 0)),
                pl.BlockSpec((N_GATE_COLS, rows), lambda b, c: (0, ci(b, c)))]

    st_specs = [pl.BlockSpec((None, 2 * N_HEADS, DK, DVX), lambda b, c: (b, 0, 0, 0)),
                pl.BlockSpec((None, 2 * N_HEADS, 1, LANES), lambda b, c: (b, 0, 0, 0))]
    return pl.pallas_call(
        _mlstm_kernel,
        grid=(bsz, nc),
        in_specs=specs(fwd) + specs(bwd) + st_specs,
        out_specs=[pl.BlockSpec((rows, MLSTM_WIDTH), lambda b, c: (fwd(b, c), 0)),
                   pl.BlockSpec((rows, MLSTM_WIDTH), lambda b, c: (bwd(b, c), 0))] + st_specs,
        out_shape=[jax.ShapeDtypeStruct((t, MLSTM_WIDTH), F32),
                   jax.ShapeDtypeStruct((t, MLSTM_WIDTH), F32),
                   jax.ShapeDtypeStruct(c0.shape, F32),
                   jax.ShapeDtypeStruct(m0.shape, F32)],
        scratch_shapes=[pltpu.VMEM((2 * N_HEADS, 1, LANES), F32)]
        + [pltpu.VMEM((DK, DVX), F32) for _ in range(2 * N_HEADS)],
        compiler_params=_cparams(("arbitrary", "arbitrary")),
        name="mlstm",
    )(proj, proj, proj, gcol, grow, proj, proj, proj, gcol, grow, c0, m0)


MIX_ROWS = 256


def _mixout_kernel(o_ref, cb_ref, cc_ref, cx_ref, ccp_ref, cxp_ref, ccn_ref, cxn_ref, hf_ref, hb_ref, x_ref,
                   cw_ref, ng_ref, wout_ref, gt_ref, gffn_ref, shf_ref, scf_ref, wr_ref, br_ref,
                   x1_ref, xn2_ref, idx_ref, tw_ref):
    i = pl.program_id(1)
    tm = x_ref.shape[0]
    cw = cw_ref[...]

    has_prev = jnp.where(i > 0, 1.0, 0.0)
    has_next = jnp.where(i < pl.num_programs(1) - 1, 1.0, 0.0)
    up = ccp_ref[...].astype(F32) * cxp_ref[...].astype(F32) * has_prev
    un = ccn_ref[...].astype(F32) * cxn_ref[...].astype(F32) * has_next
    uv = cc_ref[:, CONV_HALF:].astype(F32) * cx_ref[:, CONV_HALF:].astype(F32)
    ext = jnp.concatenate([up, uv, un], axis=0)

    pos = lax.broadcasted_iota(jnp.int32, (MIX_ROWS, CONV_HALF), 0) & (GRID_W - 1)
    lane_f = lax.broadcasted_iota(jnp.int32, (MIX_ROWS, LANES), 1).astype(F32)
    lane4 = lax.broadcasted_iota(jnp.int32, (MIX_ROWS, TOP_K), 1)

    for r0 in range(0, tm, MIX_ROWS):
        rows = slice(r0, r0 + MIX_ROWS)

        uh = cc_ref[rows, :CONV_HALF].astype(F32) * cx_ref[rows, :CONV_HALF].astype(F32)
        left = jnp.where(pos == 0, 0.0, pltpu.roll(uh, 1, 0))
        right = jnp.where(pos == GRID_W - 1, 0.0, pltpu.roll(uh, MIX_ROWS - 1, 0))
        yh = cw[0:1, :CONV_HALF] * left + cw[1:2, :CONV_HALF] * uh + cw[2:3, :CONV_HALF] * right
        yv = (cw[0:1, CONV_HALF:] * ext[r0:r0 + MIX_ROWS]
              + cw[1:2, CONV_HALF:] * ext[r0 + GRID_W:r0 + GRID_W + MIX_ROWS]
              + cw[2:3, CONV_HALF:] * ext[r0 + 2 * GRID_W:r0 + 2 * GRID_W + MIX_ROWS])
        yc = cb_ref[rows, :].astype(F32) * jnp.concatenate([yh, yv], axis=1)

        hs = hf_ref[rows, :] + hb_ref[rows, :]
        parts = []
        for hd in range(N_HEADS):
            seg = hs[:, hd * DV:(hd + 1) * DV]
            parts.append(seg * lax.rsqrt(jnp.mean(seg * seg, axis=-1, keepdims=True) + EPS))
        hm = jnp.concatenate(parts, axis=1) * ng_ref[...] * jax.nn.sigmoid(o_ref[rows, :].astype(F32))

        z = jnp.concatenate([hm.astype(BF16), yc.astype(BF16)], axis=1)
        x1 = x_ref[rows, :] + gt_ref[...] * jnp.dot(z, wout_ref[...], preferred_element_type=F32)
        x1_ref[rows, :] = x1

        y = x1 * lax.rsqrt(jnp.mean(x1 * x1, axis=-1, keepdims=True) + EPS) * gffn_ref[...]
        xn2 = y * (1.0 + scf_ref[...]) + shf_ref[...]
        _store_slab_rows(xn2_ref, r0, _pack_pairs(xn2))

        logits = jnp.dot(xn2.astype(BF16), wr_ref[...], preferred_element_type=F32) + br_ref[...]
        vals, idxs = [], []
        for _ in range(TOP_K):
            mx = jnp.max(logits, axis=-1, keepdims=True)
            ik = jnp.min(jnp.where(logits == mx, lane_f, float(LANES)), axis=-1, keepdims=True)
            vals.append(mx)
            idxs.append(ik)
            logits = jnp.where(lane_f == ik, -jnp.inf, logits)
        es = [jnp.exp(v - vals[0]) for v in vals]
        tot = es[0] + es[1] + es[2] + es[3]
        idx_out = jnp.zeros((MIX_ROWS, TOP_K), F32)
        tw_out = jnp.zeros((MIX_ROWS, TOP_K), F32)
        for kk in range(TOP_K):
            idx_out = jnp.where(lane4 == kk, idxs[kk], idx_out)
            tw_out = jnp.where(lane4 == kk, es[kk] / tot, tw_out)
        idx_ref[rows, :] = idx_out.astype(jnp.int32)
        tw_ref[rows, :] = tw_out


def _mixout(proj, hf, hb, x2, conv_w, norm_g, w_out, gt, g_ffn, sh_f, sc_f, w_r, b_r, bsz, s, tm):
    t, d = x2.shape
    nt = s // tm
    rb = tm // GRID_W
    last_rb = t // GRID_W - 1
    row = lambda b, i: b * nt + i
    w = MLSTM_WIDTH
    vec = lambda n: pl.BlockSpec((1, n), lambda b, i: (0, 0))
    per_b = pl.BlockSpec((None, 1, d), lambda b, i: (b, 0, 0))
    halo_prev = lambda cblk: pl.BlockSpec(
        (GRID_W, CONV_HALF), lambda b, i: (jnp.maximum(row(b, i) * rb - 1, 0), cblk))
    halo_next = lambda cblk: pl.BlockSpec(
        (GRID_W, CONV_HALF), lambda b, i: (jnp.minimum((row(b, i) + 1) * rb, last_rb), cblk))
    return pl.pallas_call(
        _mixout_kernel,
        grid=(bsz, nt),
        in_specs=[pl.BlockSpec((tm, w), lambda b, i: (row(b, i), 2)),
                  pl.BlockSpec((tm, w), lambda b, i: (row(b, i), 3)),
                  pl.BlockSpec((tm, w), lambda b, i: (row(b, i), 4)),
                  pl.BlockSpec((tm, w), lambda b, i: (row(b, i), 5)),
                  halo_prev(9), halo_prev(11), halo_next(9), halo_next(11),
                  pl.BlockSpec((tm, w), lambda b, i: (row(b, i), 0)),
                  pl.BlockSpec((tm, w), lambda b, i: (row(b, i), 0)),
                  pl.BlockSpec((tm, d), lambda b, i: (row(b, i), 0)),
                  pl.BlockSpec((3, CONV_WIDTH), lambda b, i: (0, 0)),
                  vec(w),
                  pl.BlockSpec((d, d), lambda b, i: (0, 0)),
                  per_b, vec(d), per_b, per_b,
                  pl.BlockSpec((d, LANES), lambda b, i: (0, 0)),
                  vec(LANES)],
        out_specs=[pl.BlockSpec((tm, d), lambda b, i: (row(b, i), 0)),
                   pl.BlockSpec((tm * SLAB, LANES), lambda b, i: (row(b, i), 0)),
                   pl.BlockSpec((tm, TOP_K), lambda b, i: (row(b, i), 0)),
                   pl.BlockSpec((tm, TOP_K), lambda b, i: (row(b, i), 0))],
        out_shape=[jax.ShapeDtypeStruct((t, d), F32),
                   jax.ShapeDtypeStruct((t * SLAB, LANES), jnp.uint32),
                   jax.ShapeDtypeStruct((t, TOP_K), jnp.int32),
                   jax.ShapeDtypeStruct((t, TOP_K), F32)],
        compiler_params=_cparams(("arbitrary", "arbitrary")),
        name="mixout",
    )(proj, proj, proj, proj, proj, proj, proj, proj, hf, hb, x2,
      conv_w, norm_g, w_out, gt, g_ffn, sh_f, sc_f, w_r, b_r)


def _rank_kernel(idx_ref, rank_ref, cnt_ref, run_scr):
    @pl.when(pl.program_id(0) == 0)
    def _():
        run_scr[...] = jnp.zeros_like(run_scr)

    tm = idx_ref.shape[0]
    idx = idx_ref[...]
    lane = lax.broadcasted_iota(jnp.int32, (tm, LANES), 1)
    hits = [lane == idx[:, kk:kk + 1] for kk in range(TOP_K)]
    onehot = jnp.zeros((tm, LANES), F32)
    for hit in hits:
        onehot = onehot + hit.astype(F32)
    r = lax.broadcasted_iota(jnp.int32, (tm, tm), 0)
    c = lax.broadcasted_iota(jnp.int32, (tm, tm), 1)
    before = jnp.dot((c < r).astype(BF16), onehot.astype(BF16), preferred_element_type=F32) + run_scr[...]
    lane4 = lax.broadcasted_iota(jnp.int32, (tm, TOP_K), 1)
    rank = jnp.zeros((tm, TOP_K), F32)
    for kk, hit in enumerate(hits):
        rk = jnp.sum(jnp.where(hit, before, 0.0), axis=-1, keepdims=True)
        rank = jnp.where(lane4 == kk, rk, rank)
    rank_ref[...] = rank.astype(jnp.int32)
    run_scr[...] = run_scr[...] + jnp.sum(onehot, axis=0, keepdims=True)
    cnt_ref[...] = run_scr[...]


def _rank(idx, tm):
    t = idx.shape[0]
    return pl.pallas_call(
        _rank_kernel,
        grid=(t // tm,),
        in_specs=[pl.BlockSpec((tm, TOP_K), lambda i: (i, 0))],
        out_specs=[pl.BlockSpec((tm, TOP_K), lambda i: (i, 0)),
                   pl.BlockSpec((1, LANES), lambda i: (0, 0))],
        out_shape=[jax.ShapeDtypeStruct((t, TOP_K), jnp.int32),
                   jax.ShapeDtypeStruct((1, LANES), F32)],
        scratch_shapes=[pltpu.VMEM((1, LANES), F32)],
        compiler_params=_cparams(("arbitrary",)),
        name="rank",
    )(idx)


def _largest_pad_piece():
    return 1 << ((ROW_BLOCK - 1).bit_length() - 1)


def _dispatch_kernel(pad_ref, dest_hbm, xn_ref, xs_hbm, dsm, zeros_scr, sem_idx, sem_rows, sem_pad):
    i = pl.program_id(0)
    tm = xn_ref.shape[0] // SLAB
    n_idx = tm * TOP_K
    idx_copy = pltpu.make_async_copy(dest_hbm.at[pl.ds(i * n_idx, n_idx)], dsm, sem_idx)
    idx_copy.start()

    def slab(ref, row, n_rows=1):
        return ref.at[pl.ds(pl.multiple_of(row * SLAB, SLAB), n_rows * SLAB), :]

    def for_each_pad_piece(fn):
        def per_expert(e, carry):
            off = pad_ref[2 * e]
            n = pad_ref[2 * e + 1]
            size = _largest_pad_piece()
            while size >= 1:
                take = (n & size) != 0

                @pl.when(take)
                def _(off=off, size=size):
                    fn(pltpu.make_async_copy(slab(zeros_scr, 0, size), slab(xs_hbm, off, size), sem_pad))

                off = off + jnp.where(take, size, 0)
                size //= 2
            return carry
        lax.fori_loop(0, N_EXPERTS, per_expert, 0)

    @pl.when(i == 0)
    def _():
        zeros_scr[...] = jnp.zeros_like(zeros_scr)
        for_each_pad_piece(lambda cp: cp.start())
        for_each_pad_piece(lambda cp: cp.wait())

    idx_copy.wait()

    def row_copy(t, kk):
        return pltpu.make_async_copy(slab(xn_ref, t), slab(xs_hbm, dsm[t * TOP_K + kk]), sem_rows)

    def issue(t, carry):
        for kk in range(TOP_K):
            row_copy(t, kk).start(priority=kk % 2)
        return carry

    lax.fori_loop(0, tm, issue, 0, unroll=ISSUE_UNROLL)
    pltpu.make_async_copy(slab(xs_hbm, 0, n_idx), slab(xs_hbm, 0, n_idx), sem_rows).wait()


def _dispatch(pad_info, dest_flat, xn2, n_rows, tm):
    t = xn2.shape[0] // SLAB
    return pl.pallas_call(
        _dispatch_kernel,
        grid_spec=pltpu.PrefetchScalarGridSpec(
            num_scalar_prefetch=1,
            grid=(t // tm,),
            in_specs=[pl.BlockSpec(memory_space=pl.ANY),
                      pl.BlockSpec((tm * SLAB, LANES), lambda i, pad: (i, 0))],
            out_specs=pl.BlockSpec(memory_space=pl.ANY),
            scratch_shapes=[pltpu.SMEM((tm * TOP_K,), jnp.int32),
                            pltpu.VMEM((_largest_pad_piece() * SLAB, LANES), xn2.dtype),
                            pltpu.SemaphoreType.DMA(()),
                            pltpu.SemaphoreType.DMA(()),
                            pltpu.SemaphoreType.DMA(())]),
        out_shape=jax.ShapeDtypeStruct((n_rows * SLAB, LANES), xn2.dtype),
        compiler_params=_cparams(("arbitrary",)),
        name="dispatch",
    )(pad_info, dest_flat, xn2)


def _new_expert(be_ref, j):
    return jnp.logical_or(j == 0, be_ref[j] != be_ref[jnp.maximum(j - 1, 0)])


GU_COLS = 512


def _full_or_half_block(valid, rows, compute):
    half, quarter = rows // 2, rows // 4

    @pl.when(valid > half)
    def _():
        compute(rows)

    @pl.when(jnp.logical_and(valid > quarter, valid <= half))
    def _():
        compute(half)

    @pl.when(valid <= quarter)
    def _():
        compute(quarter)


def _expert_gu_kernel(be_ref, nu_ref, nxt_ref, bv_ref, xs_ref, w_hbm, bg_ref, bu_ref, act_ref,
                      stage_g, stage_u, wg_scr, wu_scr, sem):
    n = pl.program_id(0)
    j = pl.program_id(1)
    nt = pl.num_programs(0)
    tn = wg_scr.shape[1]

    def weight_copies(e, nn):
        col_g = pl.multiple_of(nn * tn, tn)
        col_u = pl.multiple_of((nt + nn) * tn, tn)
        return (pltpu.make_async_copy(w_hbm.at[e, :, pl.ds(col_g, tn)], stage_g, sem.at[0]),
                pltpu.make_async_copy(w_hbm.at[e, :, pl.ds(col_u, tn)], stage_u, sem.at[1]))

    @pl.when(j < nu_ref[0])
    def _():
        e = be_ref[j]

        @pl.when(_new_expert(be_ref, j))
        def _():
            @pl.when(jnp.logical_and(n == 0, j == 0))
            def _():
                for cp in weight_copies(e, n):
                    cp.start()

            for cp in weight_copies(e, n):
                cp.wait()
            wg_scr[...] = stage_g[...].astype(BF16)
            wu_scr[...] = stage_u[...].astype(BF16)

            e_next = nxt_ref[e]
            in_pass = e_next >= 0

            @pl.when(jnp.logical_or(in_pass, n + 1 < nt))
            def _():
                for cp in weight_copies(jnp.where(in_pass, e_next, be_ref[0]), jnp.where(in_pass, n, n + 1)):
                    cp.start()

        def compute(rows):
            x = _unpack_pairs(_load_slab_rows(xs_ref, rows), SLAB * LANES).astype(BF16)
            for c0 in range(0, tn, GU_COLS):
                cols = slice(c0, c0 + GU_COLS)
                g = jnp.dot(x, wg_scr[:, cols], preferred_element_type=F32) + bg_ref[:, cols]
                u = jnp.dot(x, wu_scr[:, cols], preferred_element_type=F32) + bu_ref[:, cols]
                gate = jnp.minimum(g, SWIGLU_LIMIT)
                up = jnp.clip(u, -SWIGLU_LIMIT, SWIGLU_LIMIT)
                act_ref[0:rows, cols] = ((up + 1.0) * gate * jax.nn.sigmoid(SWIGLU_ALPHA * gate)).astype(BF16)

        _full_or_half_block(bv_ref[j], xs_ref.shape[0] // SLAB, compute)


def _expert_gu(blk_expert, n_used, nxt_expert, blk_valid, xs, w_gu, b_gu, tn):
    n_rows = xs.shape[0] // SLAB
    d = w_gu.shape[1]
    dff = w_gu.shape[2] // 2
    nt = dff // tn
    nb = n_rows // ROW_BLOCK
    blk = lambda j, nu: jnp.minimum(j, nu[0] - 1)
    exp = lambda j, be, nu: be[blk(j, nu)]
    return pl.pallas_call(
        _expert_gu_kernel,
        grid_spec=pltpu.PrefetchScalarGridSpec(
            num_scalar_prefetch=4,
            grid=(nt, nb),
            in_specs=[pl.BlockSpec((ROW_BLOCK * SLAB, LANES), lambda n, j, be, nu, nx, bv: (blk(j, nu), 0)),
                      pl.BlockSpec(memory_space=pl.ANY),
                      pl.BlockSpec((None, 1, tn), lambda n, j, be, nu, nx, bv: (exp(j, be, nu), 0, n)),
                      pl.BlockSpec((None, 1, tn), lambda n, j, be, nu, nx, bv: (exp(j, be, nu), 0, nt + n))],
            out_specs=pl.BlockSpec((ROW_BLOCK, tn), lambda n, j, be, nu, nx, bv: (blk(j, nu), n)),
            scratch_shapes=[pltpu.VMEM((d, tn), F32), pltpu.VMEM((d, tn), F32),
                            pltpu.VMEM((d, tn), BF16), pltpu.VMEM((d, tn), BF16),
                            pltpu.SemaphoreType.DMA((2,))]),
        out_shape=jax.ShapeDtypeStruct((n_rows, dff), BF16),
        compiler_params=_cparams(("arbitrary", "arbitrary")),
        name="expert_gu",
    )(blk_expert, n_used, nxt_expert, blk_valid, xs, w_gu, b_gu, b_gu)


def _expert_down_kernel(be_ref, nu_ref, nxt_ref, bv_ref, act_ref, w_hbm, b_ref, y_ref, stage, w_scr, sem):
    j = pl.program_id(0)

    def weight_copy(e):
        return pltpu.make_async_copy(w_hbm.at[e], stage, sem)

    @pl.when(j < nu_ref[0])
    def _():
        e = be_ref[j]

        @pl.when(_new_expert(be_ref, j))
        def _():
            @pl.when(j == 0)
            def _():
                weight_copy(e).start()

            weight_copy(e).wait()
            w_scr[...] = stage[...].astype(BF16)
            e_next = nxt_ref[e]

            @pl.when(e_next >= 0)
            def _():
                weight_copy(e_next).start()

        def compute(rows):
            act = act_ref[0:rows, :]
            for c0 in range(0, w_scr.shape[1], DOWN_COLS):
                cols = slice(c0, c0 + DOWN_COLS)
                y = _pack_pairs(jnp.dot(act, w_scr[:, cols], preferred_element_type=F32) + b_ref[:, cols])
                for q in range(y.shape[1] // LANES):
                    chunk = c0 // 2 // LANES + q
                    y_ref[pl.ds(chunk, rows, stride=SLAB), :] = y[:, q * LANES:(q + 1) * LANES]

        _full_or_half_block(bv_ref[j], act_ref.shape[0], compute)


def _expert_down(blk_expert, n_used, nxt_expert, blk_valid, act, w_down, b_down):
    n_rows, dff = act.shape
    d = w_down.shape[2]
    nb = n_rows // ROW_BLOCK
    blk = lambda j, nu: jnp.minimum(j, nu[0] - 1)
    exp = lambda j, be, nu: be[blk(j, nu)]
    return pl.pallas_call(
        _expert_down_kernel,
        grid_spec=pltpu.PrefetchScalarGridSpec(
            num_scalar_prefetch=4,
            grid=(nb,),
            in_specs=[pl.BlockSpec((ROW_BLOCK, dff), lambda j, be, nu, nx, bv: (blk(j, nu), 0)),
                      pl.BlockSpec(memory_space=pl.ANY),
                      pl.BlockSpec((None, 1, d), lambda j, be, nu, nx, bv: (exp(j, be, nu), 0, 0))],
            out_specs=pl.BlockSpec((ROW_BLOCK * SLAB, LANES), lambda j, be, nu, nx, bv: (blk(j, nu), 0)),
            scratch_shapes=[pltpu.VMEM((dff, d), F32), pltpu.VMEM((dff, d), BF16),
                            pltpu.SemaphoreType.DMA(())]),
        out_shape=jax.ShapeDtypeStruct((n_rows * SLAB, LANES), jnp.uint32),
        compiler_params=_cparams(("arbitrary",)),
        name="expert_down",
    )(blk_expert, n_used, nxt_expert, blk_valid, act, w_down, b_down)


def _combine_kernel(dest_hbm, y_hbm, x1_ref, tw_ref, gt_ref, gfin_ref, out_ref, dsm, buf, sem_idx, sem_rows):
    i = pl.program_id(1) + pl.program_id(0) * pl.num_programs(1)
    n_steps = pl.num_programs(0) * pl.num_programs(1)
    tm = x1_ref.shape[0]
    n_idx = tm * TOP_K

    def slab(ref, row, n_rows=1):
        return ref.at[pl.ds(pl.multiple_of(row * SLAB, SLAB), n_rows * SLAB), :]

    def idx_copy(tile):
        slot = tile % 2
        return pltpu.make_async_copy(dest_hbm.at[pl.ds(tile * n_idx, n_idx)], dsm.at[slot], sem_idx.at[slot])

    def issue_rows(tile):
        slot = tile % 2

        def issue(t, carry):
            for kk in range(TOP_K):
                pltpu.make_async_copy(slab(y_hbm, dsm[slot, t * TOP_K + kk]), slab(buf.at[slot, kk], t),
                                      sem_rows.at[slot]).start(priority=kk % 2)
            return carry

        lax.fori_loop(0, tm, issue, 0, unroll=ISSUE_UNROLL)

    @pl.when(i == 0)
    def _():
        idx_copy(0).start()
        idx_copy(0).wait()
        issue_rows(0)

        @pl.when(n_steps > 1)
        def _():
            idx_copy(1).start()

    @pl.when(i + 2 < n_steps)
    def _():
        idx_copy(i + 2).start()

    @pl.when(i + 1 < n_steps)
    def _():
        idx_copy(i + 1).wait()
        issue_rows(i + 1)

    slot = i % 2
    for kk in range(TOP_K):
        pltpu.make_async_copy(slab(y_hbm, 0, tm), buf.at[slot, kk], sem_rows.at[slot]).wait()

    tw = tw_ref[...]
    rows = lambda kk: _unpack_pairs(_load_slab_rows(buf.at[slot, kk], tm), DOWN_COLS // 2)
    acc = rows(0) * tw[:, 0:1]
    for kk in range(1, TOP_K):
        acc = acc + rows(kk) * tw[:, kk:kk + 1]
    x2 = x1_ref[...] + gt_ref[...] * acc
    out_ref[...] = x2 * lax.rsqrt(jnp.mean(x2 * x2, axis=-1, keepdims=True) + EPS) * gfin_ref[...]


def _combine(dest_flat, y, x1, tw, gt, g_final, bsz, s, tm):
    t, d = x1.shape
    nt = s // tm
    row = lambda b, i: (b * nt + i, 0)
    return pl.pallas_call(
        _combine_kernel,
        grid=(bsz, nt),
        in_specs=[pl.BlockSpec(memory_space=pl.ANY),
                  pl.BlockSpec(memory_space=pl.ANY),
                  pl.BlockSpec((tm, d), row),
                  pl.BlockSpec((tm, TOP_K), row),
                  pl.BlockSpec((None, 1, d), lambda b, i: (b, 0, 0)),
                  pl.BlockSpec((1, d), lambda b, i: (0, 0))],
        out_specs=pl.BlockSpec((tm, d), row),
        out_shape=jax.ShapeDtypeStruct((t, d), F32),
        scratch_shapes=[pltpu.SMEM((2, tm * TOP_K), jnp.int32),
                        pltpu.VMEM((2, TOP_K, tm * SLAB, LANES), y.dtype),
                        pltpu.SemaphoreType.DMA((2,)),
                        pltpu.SemaphoreType.DMA((2,))],
        compiler_params=_cparams(("arbitrary", "arbitrary")),
        name="combine",
    )(dest_flat, y, x1, tw, gt, g_final)


def _pad_lanes(a, value=0.0):
    return jnp.pad(a, ((0, 0), (0, LANES - a.shape[1])), constant_values=value)


def _routing_tables(idx, rank, counts_f, n_blocks):
    counts = counts_f[0, :N_EXPERTS].astype(jnp.int32)
    padded = (counts + ROW_BLOCK - 1) // ROW_BLOCK * ROW_BLOCK
    pend = jnp.cumsum(padded)
    pstart = pend - padded
    ids = jnp.arange(N_EXPERTS, dtype=jnp.int32)
    seg_start = jnp.sum(jnp.where(idx[..., None] == ids, pstart, 0), axis=-1)
    dest = (seg_start + rank).reshape(-1)
    blk_start = jnp.arange(n_blocks, dtype=jnp.int32) * ROW_BLOCK
    blk_expert = jnp.minimum(jnp.sum((pend[None, :] <= blk_start[:, None]).astype(jnp.int32), axis=1),
                             N_EXPERTS - 1)
    n_used = (pend[-1:] // ROW_BLOCK).astype(jnp.int32)
    blk_valid = jnp.clip((pstart + counts)[blk_expert] - blk_start, 0, ROW_BLOCK).astype(jnp.int32)
    pad_info = jnp.stack([pstart + counts, padded - counts], axis=1).reshape(-1).astype(jnp.int32)
    later = jnp.where((ids[None, :] > ids[:, None]) & (counts[None, :] > 0), ids[None, :], N_EXPERTS)
    nxt = jnp.min(later, axis=1)
    nxt_expert = jnp.where(nxt == N_EXPERTS, -1, nxt).astype(jnp.int32)
    return dest, blk_expert, n_used, nxt_expert, blk_valid, pad_info


def _layer(x, c, ctx, c_ctx, w_ada, b_ada, g_mix, w_in, b_if, conv_w, norm_g, w_out,
           g_ffn, w_router, b_router, w_gu, b_gu, w_down, b_down, g_final):
    bsz, s, d = x.shape
    s_ctx = ctx.shape[1]

    cond = jnp.zeros((8, d), F32).at[:bsz].set(c).at[bsz].set(c_ctx)
    mod = _adaln(cond, w_ada, b_ada[None, :])
    sh_m, sc_m, gt_m, sh_f, sc_f, gt_f = [m[:, None, :] for m in jnp.split(mod, N_MOD, axis=-1)]
    lat = lambda m: m[:bsz]
    ctxm = lambda m: jnp.broadcast_to(m[bsz:bsz + 1], (bsz, 1, d))

    g0 = 2 * QK_COLS + 2 * MLSTM_WIDTH
    w_main = jnp.concatenate([w_in[:, :g0], w_in[:, g0 + N_GATE_COLS:]], axis=1).astype(BF16)
    w_gate = _pad_lanes(w_in[:, g0:g0 + N_GATE_COLS]).astype(BF16)
    b_gate = _pad_lanes(b_if[None, :])
    g_mix2 = g_mix[None, :]

    proj_c, gpre_c = _inproj(ctx, g_mix2, ctxm(sh_m), ctxm(sc_m), w_main, w_gate, min(s_ctx, 512))
    gcol_c, grow_c = _gates(gpre_c, b_gate, 512)
    zeros_state = (jnp.zeros((bsz, 2 * N_HEADS, DK, DVX), F32),
                   jnp.zeros((bsz, 2 * N_HEADS, 1, LANES), F32))
    _, _, c0, m0 = _mlstm(proj_c, gcol_c, grow_c, bsz, s_ctx, *zeros_state)

    proj, gpre = _inproj(x, g_mix2, lat(sh_m), lat(sc_m), w_main, w_gate, 512)
    gcol, grow = _gates(gpre, b_gate, 512)
    hf, hb, _, _ = _mlstm(proj, gcol, grow, bsz, s, c0, m0)
    x1, xn2, idx, tw = _mixout(
        proj, hf, hb, x.reshape(bsz * s, d), conv_w, norm_g[None, :], w_out.astype(BF16), lat(gt_m),
        g_ffn[None, :], lat(sh_f), lat(sc_f), _pad_lanes(w_router).astype(BF16),
        _pad_lanes(b_router[None, :], NEG_BIG), bsz, s, 512)

    t = bsz * s
    n_blocks = -(-(t * TOP_K) // ROW_BLOCK) + N_EXPERTS
    rank, counts = _rank(idx, 512)
    dest, blk_expert, n_used, nxt_expert, blk_valid, pad_info = _routing_tables(idx, rank, counts, n_blocks)
    xs = _dispatch(pad_info, dest, xn2, n_blocks * ROW_BLOCK, 4096)
    act = _expert_gu(blk_expert, n_used, nxt_expert, blk_valid, xs, w_gu, b_gu[:, None, :], 1024)
    y = _expert_down(blk_expert, n_used, nxt_expert, blk_valid, act, w_down, b_down[:, None, :])
    out = _combine(dest, y, x1, tw, lat(gt_f), g_final[None, :], bsz, s, 256)
    return out.reshape(bsz, s, d)


def kernel(x, c, ctx, c_ctx, w_ada, b_ada, g_mix, w_in, b_if, conv_w, mlstm_norm_g, w_out,
           g_ffn, w_router, b_router, w_gu, b_gu, w_down, b_down, g_final):
    return _layer(x, c, ctx, c_ctx, w_ada[0], b_ada[0], g_mix[0], w_in[0], b_if[0], conv_w[0],
                  mlstm_norm_g[0], w_out[0], g_ffn[0], w_router[0], b_router[0], w_gu[0], b_gu[0],
                  w_down[0], b_down[0], g_final)
```

```python
import functools

import jax
import jax.numpy as jnp
from jax import lax
from jax.experimental import pallas as pl
from jax.experimental.pallas import tpu as pltpu

F32 = jnp.float32
BF16 = jnp.bfloat16

N_HEADS = 4
DK = 128
DV = 256
QK_COLS = N_HEADS * DK
MLSTM_WIDTH = N_HEADS * DV
CONV_WIDTH = 1024
CONV_HALF = CONV_WIDTH // 2
N_GATE_COLS = 4 * N_HEADS
GRID_W = 64
CHUNK = 128
GATE_SOFT_CAP = 15.0
N_EXPERTS = 32
TOP_K = 4
SWIGLU_LIMIT = 7.0
SWIGLU_ALPHA = 1.702
N_MOD = 6
EPS = 1e-6
LANES = 128
SUBLANES = 8
ROW_BLOCK = 1024
DOWN_COLS = 1024
ISSUE_UNROLL = 16
NEG_BIG = -1e30
VMEM_LIMIT = 56 * 1024 * 1024


def _cparams(sem):
    return pltpu.CompilerParams(dimension_semantics=sem, vmem_limit_bytes=VMEM_LIMIT)


def _pack_pairs(x):
    bits = lax.bitcast_convert_type(x.astype(BF16).astype(F32), jnp.uint32)
    g = x.shape[1] // 2
    return bits[:, :g] | (bits[:, g:] >> 16)


def _unpack_pairs(p, group):
    hi = lax.bitcast_convert_type(p & jnp.uint32(0xFFFF0000), F32)
    lo = lax.bitcast_convert_type(p << 16, F32)
    parts = []
    for g0 in range(0, p.shape[1], group):
        parts += [hi[:, g0:g0 + group], lo[:, g0:g0 + group]]
    return jnp.concatenate(parts, axis=1)


SLAB = 8


def _store_slab_rows(ref, r0, packed):
    rows = packed.shape[0]
    for c in range(SLAB):
        ref[pl.ds(r0 * SLAB + c, rows, stride=SLAB), :] = packed[:, c * LANES:(c + 1) * LANES]


def _load_slab_rows(ref, rows):
    return jnp.concatenate([ref[pl.ds(c, rows, stride=SLAB), :] for c in range(SLAB)], axis=1)


def _adaln_kernel(c_ref, w_ref, b_ref, o_ref):
    s = c_ref[...]
    s = s * jax.nn.sigmoid(s)
    o_ref[...] = jnp.dot(s.astype(BF16), w_ref[...].astype(BF16),
                         preferred_element_type=F32) + b_ref[...]


def _adaln(cond, w, b):
    d, n = w.shape
    tn = 1024
    return pl.pallas_call(
        _adaln_kernel,
        grid=(n // tn,),
        in_specs=[pl.BlockSpec((8, d), lambda j: (0, 0)),
                  pl.BlockSpec((d, tn), lambda j: (0, j)),
                  pl.BlockSpec((1, tn), lambda j: (0, j))],
        out_specs=pl.BlockSpec((8, tn), lambda j: (0, j)),
        out_shape=jax.ShapeDtypeStruct((8, n), F32),
        compiler_params=_cparams(("arbitrary",)),
        name="adaln",
    )(cond, w, b)


INPROJ_COLS = 1024


def _inproj_kernel(x_ref, g_ref, sh_ref, sc_ref, wa_ref, wb_ref, wg_ref, proj_ref, gate_ref):
    x = x_ref[...]
    y = x * lax.rsqrt(jnp.mean(x * x, axis=-1, keepdims=True) + EPS) * g_ref[...]
    xn = (y * (1.0 + sc_ref[...]) + sh_ref[...]).astype(BF16)
    gate_ref[...] = jnp.dot(xn, wg_ref[...], preferred_element_type=F32)
    base = 0
    for w_ref in (wa_ref, wb_ref):
        for c0 in range(0, w_ref.shape[1], INPROJ_COLS):
            y = jnp.dot(xn, w_ref[:, c0:c0 + INPROJ_COLS], preferred_element_type=F32)
            proj_ref[:, base + c0:base + c0 + INPROJ_COLS] = y.astype(BF16)
        base += w_ref.shape[1]


def _inproj(x, g, sh, sc, wa, wb, wg, tm):
    bsz, s, d = x.shape
    p = wa.shape[1] + wb.shape[1]
    nt = s // tm
    x2 = x.reshape(bsz * s, d)
    resident = lambda shape: pl.BlockSpec(shape, lambda b, i: (0, 0), pipeline_mode=pl.Buffered(1))
    return pl.pallas_call(
        _inproj_kernel,
        grid=(bsz, nt),
        in_specs=[pl.BlockSpec((tm, d), lambda b, i: (b * nt + i, 0)),
                  pl.BlockSpec((1, d), lambda b, i: (0, 0)),
                  pl.BlockSpec((None, 1, d), lambda b, i: (b, 0, 0)),
                  pl.BlockSpec((None, 1, d), lambda b, i: (b, 0, 0)),
                  resident(wa.shape),
                  resident(wb.shape),
                  resident((d, LANES))],
        out_specs=[pl.BlockSpec((tm, p), lambda b, i: (b * nt + i, 0)),
                   pl.BlockSpec((tm, LANES), lambda b, i: (b * nt + i, 0))],
        out_shape=[jax.ShapeDtypeStruct((bsz * s, p), BF16),
                   jax.ShapeDtypeStruct((bsz * s, LANES), F32)],
        compiler_params=_cparams(("arbitrary", "arbitrary")),
        name="inproj",
    )(x2, g, sh, sc, wa, wb, wg)


def _log_sigmoid(x):
    return jnp.minimum(x, 0.0) - jnp.log1p(jnp.exp(-jnp.abs(x)))


def _gates_kernel(g_ref, b_ref, gc_ref, gr_ref):
    tm = g_ref.shape[0]
    row = lax.broadcasted_iota(jnp.int32, (tm, LANES), 0)
    lane = lax.broadcasted_iota(jnp.int32, (tm, LANES), 1)
    gp = GATE_SOFT_CAP * jnp.tanh((g_ref[...] + b_ref[...]) / GATE_SOFT_CAP)
    is_f = ((lane >> 2) & 1) == 1
    fwd_lane = lane < 2 * N_HEADS
    lf = jnp.where(is_f, _log_sigmoid(gp), 0.0)
    r2 = lax.broadcasted_iota(jnp.int32, (CHUNK, CHUNK), 0)
    c2 = lax.broadcasted_iota(jnp.int32, (CHUNK, CHUNK), 1)
    lower = (r2 >= c2).astype(F32)
    upper = (r2 <= c2).astype(F32)
    lane_c = lax.broadcasted_iota(jnp.int32, (CHUNK, LANES), 1)
    cums = []
    for c in range(tm // CHUNK):
        lf_c = lf[c * CHUNK:(c + 1) * CHUNK]
        cf = jnp.dot(lower, lf_c, precision=lax.Precision.HIGHEST, preferred_element_type=F32)
        cb = jnp.dot(upper, lf_c, precision=lax.Precision.HIGHEST, preferred_element_type=F32)
        cums.append(jnp.where(lane_c < 2 * N_HEADS, cf, cb))
    cdir = jnp.concatenate(cums, axis=0)
    a = jnp.where(is_f, cdir, gp - pltpu.roll(cdir, LANES - N_HEADS, 1))

    pos = row % CHUNK
    x = a
    k = 1
    while k < CHUNK:
        from_before = jnp.where(pos >= k, pltpu.roll(x, k, 0), -jnp.inf)
        from_after = jnp.where(pos < CHUNK - k, pltpu.roll(x, tm - k, 0), -jnp.inf)
        x = jnp.maximum(x, jnp.where(fwd_lane, from_before, from_after))
        k *= 2
    gc_ref[...] = jnp.where(is_f, a, x)

    lane_1 =lax.broadcasted_iota(jnp.int32, (1, LANES), 1)
    for c in range(tm // CHUNK):
        lo = c * CHUNK
        xc, ac = x[lo:lo + CHUNK], a[lo:lo + CHUNK]
        end_max = jnp.where(lane_1 < 2 * N_HEADS, xc[CHUNK - 1:CHUNK], xc[0:1])
        e = jnp.exp(ac - end_max)
        rows = jnp.where(((lane_c >> 2) & 1) == 1, pltpu.roll(e, N_HEADS, 1), ac)
        gr_ref[:, lo:lo + CHUNK] = rows.T[:N_GATE_COLS, :]


def _gates(gpre, b_if, tm):
    t = gpre.shape[0]
    return pl.pallas_call(
        _gates_kernel,
        grid=(t // tm,),
        in_specs=[pl.BlockSpec((tm, LANES), lambda i: (i, 0)),
                  pl.BlockSpec((1, LANES), lambda i: (0, 0))],
        out_specs=[pl.BlockSpec((tm, LANES), lambda i: (i, 0)),
                   pl.BlockSpec((N_GATE_COLS, tm), lambda i: (0, i))],
        out_shape=[jax.ShapeDtypeStruct((t, LANES), F32),
                   jax.ShapeDtypeStruct((N_GATE_COLS, t), F32)],
        compiler_params=_cparams(("arbitrary",)),
        name="gates",
    )(gpre, b_if)


DVX = DV + LANES
MLSTM_CHUNKS_PER_STEP = 4


def _mlstm_chunk(q, k, v_ext, rmax_col, b_col, r_row, e_row, b_last, rmax_last, mask, cx, m_st):
    scale = DK ** -0.5
    mb = jnp.maximum(m_st, jnp.broadcast_to(rmax_col, (CHUNK, CHUNK)))
    w_intra = jnp.exp(jnp.where(mask, r_row - mb, -jnp.inf))
    w_state = jnp.exp(m_st - mb)
    qk = lax.dot_general(q, k, (((1,), (1,)), ((), ())), preferred_element_type=F32)
    s = qk * (w_intra * scale)
    lhs = jnp.concatenate([s.astype(BF16), (q.astype(F32) * (w_state * scale)).astype(BF16)], axis=1)
    rhs = jnp.concatenate([v_ext, cx.astype(BF16)], axis=0)
    nx = jnp.dot(lhs, rhs, preferred_element_type=F32)
    denom = jnp.maximum(jnp.abs(nx[:, DV:]), jnp.exp(-(jnp.broadcast_to(b_col, (CHUNK, CHUNK)) + mb)))
    h = nx[:, :DV] / jnp.concatenate([denom, denom], axis=1)
    ke_t = (k.T.astype(F32) * e_row).astype(BF16)
    c_loc = jnp.dot(ke_t, v_ext, preferred_element_type=F32)
    m_loc = b_last + rmax_last
    m_new = jnp.maximum(b_last + m_st, m_loc)
    return h, jnp.exp(b_last + m_st - m_new) * cx + jnp.exp(m_loc - m_new) * c_loc, m_new


def _mlstm_kernel(qf_ref, kf_ref, vf_ref, gcf_ref, grf_ref, qb_ref, kb_ref, vb_ref, gcb_ref, grb_ref,
                  c0_ref, m0_ref, hf_ref, hb_ref, cout_ref, mout_ref, m_scr, *c_scrs):
    c = pl.program_id(1)

    @pl.when(c == 0)
    def _():
        for idx, c_scr in enumerate(c_scrs):
            c_scr[...] = c0_ref[idx]
        m_scr[...] = m0_ref[...]

    row = lax.broadcasted_iota(jnp.int32, (CHUNK, CHUNK), 0)
    col = lax.broadcasted_iota(jnp.int32, (CHUNK, CHUNK), 1)
    ones = jnp.ones((CHUNK, LANES), BF16)
    m_all = m_scr[...]
    dirs = ((qf_ref, kf_ref, vf_ref, gcf_ref, grf_ref, hf_ref, 0, CHUNK - 1, col <= row),
            (qb_ref, kb_ref, vb_ref, gcb_ref, grb_ref, hb_ref, 2 * N_HEADS, 0, col >= row))
    n_sub = qf_ref.shape[0] // CHUNK
    m_news = []
    for di, (q_ref, k_ref, v_ref, gc_ref, gr_ref, h_ref, off, last, mask) in enumerate(dirs):
        order = range(n_sub) if di == 0 else range(n_sub - 1, -1, -1)
        for hd in range(N_HEADS):
            idx = di * N_HEADS + hd
            lr, lb = off + hd, off + N_HEADS + hd
            cx, m_st = c_scrs[idx][...], m_all[idx][:, 0:1]
            for sub in order:
                r0 = sub * CHUNK
                rows = slice(r0, r0 + CHUNK)
                v_ext = jnp.concatenate([v_ref[rows, hd * DV:(hd + 1) * DV], ones], axis=1)
                h, cx, m_st = _mlstm_chunk(
                    q_ref[rows, hd * DK:(hd + 1) * DK], k_ref[rows, hd * DK:(hd + 1) * DK], v_ext,
                    gc_ref[rows, lr:lr + 1], gc_ref[rows, lb:lb + 1],
                    gr_ref[lr:lr + 1, rows], gr_ref[lb:lb + 1, rows],
                    gc_ref[r0 + last:r0 + last + 1, lb:lb + 1], gc_ref[r0 + last:r0 + last + 1, lr:lr + 1],
                    mask, cx, m_st)
                h_ref[rows, hd * DV:(hd + 1) * DV] = h
            c_scrs[idx][...] = cx
            m_news.append(jnp.broadcast_to(m_st, (1, LANES)))
    for idx, m_new in enumerate(m_news):
        m_scr[idx] = m_new

    @pl.when(c == pl.num_programs(1) - 1)
    def _():
        for idx, c_scr in enumerate(c_scrs):
            cout_ref[idx] = c_scr[...]
        mout_ref[...] = m_scr[...]


def _mlstm(proj, gcol, grow, bsz, s, c0, m0):
    rows = min(MLSTM_CHUNKS_PER_STEP * CHUNK, s)
    nc = s // rows
    t = bsz * s
    fwd = lambda b, c: b * nc + c
    bwd = lambda b, c: b * nc + (nc - 1 - c)

    def specs(ci):
        return [pl.BlockSpec((rows, QK_COLS), lambda b, c: (ci(b, c), 0)),
                pl.BlockSpec((rows, QK_COLS), lambda b, c: (ci(b, c), 1)),
                pl.BlockSpec((rows, MLSTM_WIDTH), lambda b, c: (ci(b, c), 1)),
                pl.BlockSpec((rows, LANES), lambda b, c: (ci(b, c), 0)),
                pl.BlockSpec((N_GATE_COLS, rows), lambda b, c: (0, ci(b, c)))]

    st_specs = [pl.BlockSpec((None, 2 * N_HEADS, DK, DVX), lambda b, c: (b, 0, 0, 0)),
                pl.BlockSpec((None, 2 * N_HEADS, 1, LANES), lambda b, c: (b, 0, 0, 0))]
    return pl.pallas_call(
        _mlstm_kernel,
        grid=(bsz, nc),
        in_specs=specs(fwd) + specs(bwd) + st_specs,
        out_specs=[pl.BlockSpec((rows, MLSTM_WIDTH), lambda b, c: (fwd(b, c), 0)),
                   pl.BlockSpec((rows, MLSTM_WIDTH), lambda b, c: (bwd(b, c), 0))] + st_specs,
        out_shape=[jax.ShapeDtypeStruct((t, MLSTM_WIDTH), F32),
                   jax.ShapeDtypeStruct((t, MLSTM_WIDTH), F32),
                   jax.ShapeDtypeStruct(c0.shape, F32),
                   jax.ShapeDtypeStruct(m0.shape, F32)],
        scratch_shapes=[pltpu.VMEM((2 * N_HEADS, 1, LANES), F32)]
        + [pltpu.VMEM((DK, DVX), F32) for _ in range(2 * N_HEADS)],
        compiler_params=_cparams(("arbitrary", "arbitrary")),
        name="mlstm",
    )(proj, proj, proj, gcol, grow, proj, proj, proj, gcol, grow, c0, m0)


MIX_ROWS = 256


def _mixout_kernel(o_ref, cb_ref, cc_ref, cx_ref, ccp_ref, cxp_ref, ccn_ref, cxn_ref, hf_ref, hb_ref, x_ref,
                   cw_ref, ng_ref, wout_ref, gt_ref, gffn_ref, shf_ref, scf_ref, wr_ref, br_ref,
                   x1_ref, xn2_ref, idx_ref, tw_ref):
    i = pl.program_id(1)
    tm = x_ref.shape[0]
    cw = cw_ref[...]

    has_prev = jnp.where(i > 0, 1.0, 0.0)
    has_next = jnp.where(i < pl.num_programs(1) - 1, 1.0, 0.0)
    up = ccp_ref[...].astype(F32) * cxp_ref[...].astype(F32) * has_prev
    un = ccn_ref[...].astype(F32) * cxn_ref[...].astype(F32) * has_next
    uv = cc_ref[:, CONV_HALF:].astype(F32) * cx_ref[:, CONV_HALF:].astype(F32)
    ext = jnp.concatenate([up, uv, un], axis=0)

    pos = lax.broadcasted_iota(jnp.int32, (MIX_ROWS, CONV_HALF), 0) & (GRID_W - 1)
    lane_f = lax.broadcasted_iota(jnp.int32, (MIX_ROWS, LANES), 1).astype(F32)
    lane4 = lax.broadcasted_iota(jnp.int32, (MIX_ROWS, TOP_K), 1)

    for r0 in range(0, tm, MIX_ROWS):
        rows = slice(r0, r0 + MIX_ROWS)

        uh = cc_ref[rows, :CONV_HALF].astype(F32) * cx_ref[rows, :CONV_HALF].astype(F32)
        left = jnp.where(pos == 0, 0.0, pltpu.roll(uh, 1, 0))
        right = jnp.where(pos == GRID_W - 1, 0.0, pltpu.roll(uh, MIX_ROWS - 1, 0))
        yh = cw[0:1, :CONV_HALF] * left + cw[1:2, :CONV_HALF] * uh + cw[2:3, :CONV_HALF] * right
        yv = (cw[0:1, CONV_HALF:] * ext[r0:r0 + MIX_ROWS]
              + cw[1:2, CONV_HALF:] * ext[r0 + GRID_W:r0 + GRID_W + MIX_ROWS]
              + cw[2:3, CONV_HALF:] * ext[r0 + 2 * GRID_W:r0 + 2 * GRID_W + MIX_ROWS])
        yc = cb_ref[rows, :].astype(F32) * jnp.concatenate([yh, yv], axis=1)

        hs = hf_ref[rows, :] + hb_ref[rows, :]
        parts = []
        for hd in range(N_HEADS):
            seg = hs[:, hd * DV:(hd + 1) * DV]
            parts.append(seg * lax.rsqrt(jnp.mean(seg * seg, axis=-1, keepdims=True) + EPS))
        hm = jnp.concatenate(parts, axis=1) * ng_ref[...] * jax.nn.sigmoid(o_ref[rows, :].astype(F32))

        z = jnp.concatenate([hm.astype(BF16), yc.astype(BF16)], axis=1)
        x1 = x_ref[rows, :] + gt_ref[...] * jnp.dot(z, wout_ref[...], preferred_element_type=F32)
        x1_ref[rows, :] = x1

        y = x1 * lax.rsqrt(jnp.mean(x1 * x1, axis=-1, keepdims=True) + EPS) * gffn_ref[...]
        xn2 = y * (1.0 + scf_ref[...]) + shf_ref[...]
        _store_slab_rows(xn2_ref, r0, _pack_pairs(xn2))

        logits = jnp.dot(xn2.astype(BF16), wr_ref[...], preferred_element_type=F32) + br_ref[...]
        vals, idxs = [], []
        for _ in range(TOP_K):
            mx = jnp.max(logits, axis=-1, keepdims=True)
            ik = jnp.min(jnp.where(logits == mx, lane_f, float(LANES)), axis=-1, keepdims=True)
            vals.append(mx)
            idxs.append(ik)
            logits = jnp.where(lane_f == ik, -jnp.inf, logits)
        es = [jnp.exp(v - vals[0]) for v in vals]
        tot = es[0] + es[1] + es[2] + es[3]
        idx_out = jnp.zeros((MIX_ROWS, TOP_K), F32)
        tw_out = jnp.zeros((MIX_ROWS, TOP_K), F32)
        for kk in range(TOP_K):
            idx_out = jnp.where(lane4 == kk, idxs[kk], idx_out)
            tw_out = jnp.where(lane4 == kk, es[kk] / tot, tw_out)
        idx_ref[rows, :] = idx_out.astype(jnp.int32)
        tw_ref[rows, :] = tw_out


def _mixout(proj, hf, hb, x2, conv_w, norm_g, w_out, gt, g_ffn, sh_f, sc_f, w_r, b_r, bsz, s, tm):
    t, d = x2.shape
    nt = s // tm
    rb = tm // GRID_W
    last_rb = t // GRID_W - 1
    row = lambda b, i: b * nt + i
    w = MLSTM_WIDTH
    vec = lambda n: pl.BlockSpec((1, n), lambda b, i: (0, 0))
    per_b = pl.BlockSpec((None, 1, d), lambda b, i: (b, 0, 0))
    halo_prev = lambda cblk: pl.BlockSpec(
        (GRID_W, CONV_HALF), lambda b, i: (jnp.maximum(row(b, i) * rb - 1, 0), cblk))
    halo_next = lambda cblk: pl.BlockSpec(
        (GRID_W, CONV_HALF), lambda b, i: (jnp.minimum((row(b, i) + 1) * rb, last_rb), cblk))
    return pl.pallas_call(
        _mixout_kernel,
        grid=(bsz, nt),
        in_specs=[pl.BlockSpec((tm, w), lambda b, i: (row(b, i), 2)),
                  pl.BlockSpec((tm, w), lambda b, i: (row(b, i), 3)),
                  pl.BlockSpec((tm, w), lambda b, i: (row(b, i), 4)),
                  pl.BlockSpec((tm, w), lambda b, i: (row(b, i), 5)),
                  halo_prev(9), halo_prev(11), halo_next(9), halo_next(11),
                  pl.BlockSpec((tm, w), lambda b, i: (row(b, i), 0)),
                  pl.BlockSpec((tm, w), lambda b, i: (row(b, i), 0)),
                  pl.BlockSpec((tm, d), lambda b, i: (row(b, i), 0)),
                  pl.BlockSpec((3, CONV_WIDTH), lambda b, i: (0, 0)),
                  vec(w),
                  pl.BlockSpec((d, d), lambda b, i: (0, 0)),
                  per_b, vec(d), per_b, per_b,
                  pl.BlockSpec((d, LANES), lambda b, i: (0, 0)),
                  vec(LANES)],
        out_specs=[pl.BlockSpec((tm, d), lambda b, i: (row(b, i), 0)),
                   pl.BlockSpec((tm * SLAB, LANES), lambda b, i: (row(b, i), 0)),
                   pl.BlockSpec((tm, TOP_K), lambda b, i: (row(b, i), 0)),
                   pl.BlockSpec((tm, TOP_K), lambda b, i: (row(b, i), 0))],
        out_shape=[jax.ShapeDtypeStruct((t, d), F32),
                   jax.ShapeDtypeStruct((t * SLAB, LANES), jnp.uint32),
                   jax.ShapeDtypeStruct((t, TOP_K), jnp.int32),
                   jax.ShapeDtypeStruct((t, TOP_K), F32)],
        compiler_params=_cparams(("arbitrary", "arbitrary")),
        name="mixout",
    )(proj, proj, proj, proj, proj, proj, proj, proj, hf, hb, x2,
      conv_w, norm_g, w_out, gt, g_ffn, sh_f, sc_f, w_r, b_r)


def _rank_kernel(idx_ref, rank_ref, cnt_ref, run_scr):
    @pl.when(pl.program_id(0) == 0)
    def _():
        run_scr[...] = jnp.zeros_like(run_scr)

    tm = idx_ref.shape[0]
    idx = idx_ref[...]
    lane = lax.broadcasted_iota(jnp.int32, (tm, LANES), 1)
    hits = [lane == idx[:, kk:kk + 1] for kk in range(TOP_K)]
    onehot = jnp.zeros((tm, LANES), F32)
    for hit in hits:
        onehot = onehot + hit.astype(F32)
    r = lax.broadcasted_iota(jnp.int32, (tm, tm), 0)
    c = lax.broadcasted_iota(jnp.int32, (tm, tm), 1)
    before = jnp.dot((c < r).astype(BF16), onehot.astype(BF16), preferred_element_type=F32) + run_scr[...]
    lane4 = lax.broadcasted_iota(jnp.int32, (tm, TOP_K), 1)
    rank = jnp.zeros((tm, TOP_K), F32)
    for kk, hit in enumerate(hits):
        rk = jnp.sum(jnp.where(hit, before, 0.0), axis=-1, keepdims=True)
        rank = jnp.where(lane4 == kk, rk, rank)
    rank_ref[...] = rank.astype(jnp.int32)
    run_scr[...] = run_scr[...] + jnp.sum(onehot, axis=0, keepdims=True)
    cnt_ref[...] = run_scr[...]


def _rank(idx, tm):
    t = idx.shape[0]
    return pl.pallas_call(
        _rank_kernel,
        grid=(t // tm,),
        in_specs=[pl.BlockSpec((tm, TOP_K), lambda i: (i, 0))],
        out_specs=[pl.BlockSpec((tm, TOP_K), lambda i: (i, 0)),
                   pl.BlockSpec((1, LANES), lambda i: (0, 0))],
        out_shape=[jax.ShapeDtypeStruct((t, TOP_K), jnp.int32),
                   jax.ShapeDtypeStruct((1, LANES), F32)],
        scratch_shapes=[pltpu.VMEM((1, LANES), F32)],
        compiler_params=_cparams(("arbitrary",)),
        name="rank",
    )(idx)


def _largest_pad_piece():
    return 1 << ((ROW_BLOCK - 1).bit_length() - 1)


def _dispatch_kernel(pad_ref, dest_hbm, xn_ref, xs_hbm, dsm, zeros_scr, sem_idx, sem_rows, sem_pad):
    i = pl.program_id(0)
    tm = xn_ref.shape[0] // SLAB
    n_idx = tm * TOP_K
    idx_copy = pltpu.make_async_copy(dest_hbm.at[pl.ds(i * n_idx, n_idx)], dsm, sem_idx)
    idx_copy.start()

    def slab(ref, row, n_rows=1):
        return ref.at[pl.ds(pl.multiple_of(row * SLAB, SLAB), n_rows * SLAB), :]

    def for_each_pad_piece(fn):
        def per_expert(e, carry):
            off = pad_ref[2 * e]
            n = pad_ref[2 * e + 1]
            size = _largest_pad_piece()
            while size >= 1:
                take = (n & size) != 0

                @pl.when(take)
                def _(off=off, size=size):
                    fn(pltpu.make_async_copy(slab(zeros_scr, 0, size), slab(xs_hbm, off, size), sem_pad))

                off = off + jnp.where(take, size, 0)
                size //= 2
            return carry
        lax.fori_loop(0, N_EXPERTS, per_expert, 0)

    @pl.when(i == 0)
    def _():
        zeros_scr[...] = jnp.zeros_like(zeros_scr)
        for_each_pad_piece(lambda cp: cp.start())
        for_each_pad_piece(lambda cp: cp.wait())

    idx_copy.wait()

    def row_copy(t, kk):
        return pltpu.make_async_copy(slab(xn_ref, t), slab(xs_hbm, dsm[t * TOP_K + kk]), sem_rows)

    def issue(t, carry):
        for kk in range(TOP_K):
            row_copy(t, kk).start(priority=kk % 2)
        return carry

    lax.fori_loop(0, tm, issue, 0, unroll=ISSUE_UNROLL)
    pltpu.make_async_copy(slab(xs_hbm, 0, n_idx), slab(xs_hbm, 0, n_idx), sem_rows).wait()


def _dispatch(pad_info, dest_flat, xn2, n_rows, tm):
    t = xn2.shape[0] // SLAB
    return pl.pallas_call(
        _dispatch_kernel,
        grid_spec=pltpu.PrefetchScalarGridSpec(
            num_scalar_prefetch=1,
            grid=(t // tm,),
            in_specs=[pl.BlockSpec(memory_space=pl.ANY),
                      pl.BlockSpec((tm * SLAB, LANES), lambda i, pad: (i, 0))],
            out_specs=pl.BlockSpec(memory_space=pl.ANY),
            scratch_shapes=[pltpu.SMEM((tm * TOP_K,), jnp.int32),
                            pltpu.VMEM((_largest_pad_piece() * SLAB, LANES), xn2.dtype),
                            pltpu.SemaphoreType.DMA(()),
                            pltpu.SemaphoreType.DMA(()),
                            pltpu.SemaphoreType.DMA(())]),
        out_shape=jax.ShapeDtypeStruct((n_rows * SLAB, LANES), xn2.dtype),
        compiler_params=_cparams(("arbitrary",)),
        name="dispatch",
    )(pad_info, dest_flat, xn2)


def _new_expert(be_ref, j):
    return jnp.logical_or(j == 0, be_ref[j] != be_ref[jnp.maximum(j - 1, 0)])


GU_COLS = 512


def _full_or_half_block(valid, rows, compute):
    half, quarter = rows // 2, rows // 4

    @pl.when(valid > half)
    def _():
        compute(rows)

    @pl.when(jnp.logical_and(valid > quarter, valid <= half))
    def _():
        compute(half)

    @pl.when(valid <= quarter)
    def _():
        compute(quarter)


def _expert_gu_kernel(be_ref, nu_ref, nxt_ref, bv_ref, xs_ref, w_hbm, bg_ref, bu_ref, act_ref,
                      stage_g, stage_u, wg_scr, wu_scr, sem):
    n = pl.program_id(0)
    j = pl.program_id(1)
    nt = pl.num_programs(0)
    tn = wg_scr.shape[1]

    def weight_copies(e, nn):
        col_g = pl.multiple_of(nn * tn, tn)
        col_u = pl.multiple_of((nt + nn) * tn, tn)
        return (pltpu.make_async_copy(w_hbm.at[e, :, pl.ds(col_g, tn)], stage_g, sem.at[0]),
                pltpu.make_async_copy(w_hbm.at[e, :, pl.ds(col_u, tn)], stage_u, sem.at[1]))

    @pl.when(j < nu_ref[0])
    def _():
        e = be_ref[j]

        @pl.when(_new_expert(be_ref, j))
        def _():
            @pl.when(jnp.logical_and(n == 0, j == 0))
            def _():
                for cp in weight_copies(e, n):
                    cp.start()

            for cp in weight_copies(e, n):
                cp.wait()
            wg_scr[...] = stage_g[...].astype(BF16)
            wu_scr[...] = stage_u[...].astype(BF16)

            e_next = nxt_ref[e]
            in_pass = e_next >= 0

            @pl.when(jnp.logical_or(in_pass, n + 1 < nt))
            def _():
                for cp in weight_copies(jnp.where(in_pass, e_next, be_ref[0]), jnp.where(in_pass, n, n + 1)):
                    cp.start()

        def compute(rows):
            x = _unpack_pairs(_load_slab_rows(xs_ref, rows), SLAB * LANES).astype(BF16)
            for c0 in range(0, tn, GU_COLS):
                cols = slice(c0, c0 + GU_COLS)
                g = jnp.dot(x, wg_scr[:, cols], preferred_element_type=F32) + bg_ref[:, cols]
                u = jnp.dot(x, wu_scr[:, cols], preferred_element_type=F32) + bu_ref[:, cols]
                gate = jnp.minimum(g, SWIGLU_LIMIT)
                up = jnp.clip(u, -SWIGLU_LIMIT, SWIGLU_LIMIT)
                act_ref[0:rows, cols] = ((up + 1.0) * gate * jax.nn.sigmoid(SWIGLU_ALPHA * gate)).astype(BF16)

        _full_or_half_block(bv_ref[j], xs_ref.shape[0] // SLAB, compute)


def _expert_gu(blk_expert, n_used, nxt_expert, blk_valid, xs, w_gu, b_gu, tn):
    n_rows = xs.shape[0] // SLAB
    d = w_gu.shape[1]
    dff = w_gu.shape[2] // 2
    nt = dff // tn
    nb = n_rows // ROW_BLOCK
    blk = lambda j, nu: jnp.minimum(j, nu[0] - 1)
    exp = lambda j, be, nu: be[blk(j, nu)]
    return pl.pallas_call(
        _expert_gu_kernel,
        grid_spec=pltpu.PrefetchScalarGridSpec(
            num_scalar_prefetch=4,
            grid=(nt, nb),
            in_specs=[pl.BlockSpec((ROW_BLOCK * SLAB, LANES), lambda n, j, be, nu, nx, bv: (blk(j, nu), 0)),
                      pl.BlockSpec(memory_space=pl.ANY),
                      pl.BlockSpec((None, 1, tn), lambda n, j, be, nu, nx, bv: (exp(j, be, nu), 0, n)),
                      pl.BlockSpec((None, 1, tn), lambda n, j, be, nu, nx, bv: (exp(j, be, nu), 0, nt + n))],
            out_specs=pl.BlockSpec((ROW_BLOCK, tn), lambda n, j, be, nu, nx, bv: (blk(j, nu), n)),
            scratch_shapes=[pltpu.VMEM((d, tn), F32), pltpu.VMEM((d, tn), F32),
                            pltpu.VMEM((d, tn), BF16), pltpu.VMEM((d, tn), BF16),
                            pltpu.SemaphoreType.DMA((2,))]),
        out_shape=jax.ShapeDtypeStruct((n_rows, dff), BF16),
        compiler_params=_cparams(("arbitrary", "arbitrary")),
        name="expert_gu",
    )(blk_expert, n_used, nxt_expert, blk_valid, xs, w_gu, b_gu, b_gu)


def _expert_down_kernel(be_ref, nu_ref, nxt_ref, bv_ref, act_ref, w_hbm, b_ref, y_ref, stage, w_scr, sem):
    j = pl.program_id(0)

    def weight_copy(e):
        return pltpu.make_async_copy(w_hbm.at[e], stage, sem)

    @pl.when(j < nu_ref[0])
    def _():
        e = be_ref[j]

        @pl.when(_new_expert(be_ref, j))
        def _():
            @pl.when(j == 0)
            def _():
                weight_copy(e).start()

            weight_copy(e).wait()
            w_scr[...] = stage[...].astype(BF16)
            e_next = nxt_ref[e]

            @pl.when(e_next >= 0)
            def _():
                weight_copy(e_next).start()

        def compute(rows):
            act = act_ref[0:rows, :]
            for c0 in range(0, w_scr.shape[1], DOWN_COLS):
                cols = slice(c0, c0 + DOWN_COLS)
                y = _pack_pairs(jnp.dot(act, w_scr[:, cols], preferred_element_type=F32) + b_ref[:, cols])
                for q in range(y.shape[1] // LANES):
                    chunk = c0 // 2 // LANES + q
                    y_ref[pl.ds(chunk, rows, stride=SLAB), :] = y[:, q * LANES:(q + 1) * LANES]

        _full_or_half_block(bv_ref[j], act_ref.shape[0], compute)


def _expert_down(blk_expert, n_used, nxt_expert, blk_valid, act, w_down, b_down):
    n_rows, dff = act.shape
    d = w_down.shape[2]
    nb = n_rows // ROW_BLOCK
    blk = lambda j, nu: jnp.minimum(j, nu[0] - 1)
    exp = lambda j, be, nu: be[blk(j, nu)]
    return pl.pallas_call(
        _expert_down_kernel,
        grid_spec=pltpu.PrefetchScalarGridSpec(
            num_scalar_prefetch=4,
            grid=(nb,),
            in_specs=[pl.BlockSpec((ROW_BLOCK, dff), lambda j, be, nu, nx, bv: (blk(j, nu), 0)),
                      pl.BlockSpec(memory_space=pl.ANY),
                      pl.BlockSpec((None, 1, d), lambda j, be, nu, nx, bv: (exp(j, be, nu), 0, 0))],
            out_specs=pl.BlockSpec((ROW_BLOCK * SLAB, LANES), lambda j, be, nu, nx, bv: (blk(j, nu), 0)),
            scratch_shapes=[pltpu.VMEM((dff, d), F32), pltpu.VMEM((dff, d), BF16),
                            pltpu.SemaphoreType.DMA(())]),
        out_shape=jax.ShapeDtypeStruct((n_rows * SLAB, LANES), jnp.uint32),
        compiler_params=_cparams(("arbitrary",)),
        name="expert_down",
    )(blk_expert, n_used, nxt_expert, blk_valid, act, w_down, b_down)


def _combine_kernel(dest_hbm, y_hbm, x1_ref, tw_ref, gt_ref, gfin_ref, out_ref, dsm, buf, sem_idx, sem_rows):
    i = pl.program_id(1) + pl.program_id(0) * pl.num_programs(1)
    n_steps = pl.num_programs(0) * pl.num_programs(1)
    tm = x1_ref.shape[0]
    n_idx = tm * TOP_K

    def slab(ref, row, n_rows=1):
        return ref.at[pl.ds(pl.multiple_of(row * SLAB, SLAB), n_rows * SLAB), :]

    def idx_copy(tile):
        slot = tile % 2
        return pltpu.make_async_copy(dest_hbm.at[pl.ds(tile * n_idx, n_idx)], dsm.at[slot], sem_idx.at[slot])

    def issue_rows(tile):
        slot = tile % 2

        def issue(t, carry):
            for kk in range(TOP_K):
                pltpu.make_async_copy(slab(y_hbm, dsm[slot, t * TOP_K + kk]), slab(buf.at[slot, kk], t),
                                      sem_rows.at[slot]).start(priority=kk % 2)
            return carry

        lax.fori_loop(0, tm, issue, 0, unroll=ISSUE_UNROLL)

    @pl.when(i == 0)
    def _():
        idx_copy(0).start()
        idx_copy(0).wait()
        issue_rows(0)

        @pl.when(n_steps > 1)
        def _():
            idx_copy(1).start()

    @pl.when(i + 2 < n_steps)
    def _():
        idx_copy(i + 2).start()

    @pl.when(i + 1 < n_steps)
    def _():
        idx_copy(i + 1).wait()
        issue_rows(i + 1)

    slot = i % 2
    for kk in range(TOP_K):
        pltpu.make_async_copy(slab(y_hbm, 0, tm), buf.at[slot, kk], sem_rows.at[slot]).wait()

    tw = tw_ref[...]
    rows = lambda kk: _unpack_pairs(_load_slab_rows(buf.at[slot, kk], tm), DOWN_COLS // 2)
    acc = rows(0) * tw[:, 0:1]
    for kk in range(1, TOP_K):
        acc = acc + rows(kk) * tw[:, kk:kk + 1]
    x2 = x1_ref[...] + gt_ref[...] * acc
    out_ref[...] = x2 * lax.rsqrt(jnp.mean(x2 * x2, axis=-1, keepdims=True) + EPS) * gfin_ref[...]


def _combine(dest_flat, y, x1, tw, gt, g_final, bsz, s, tm):
    t, d = x1.shape
    nt = s // tm
    row = lambda b, i: (b * nt + i, 0)
    return pl.pallas_call(
        _combine_kernel,
        grid=(bsz, nt),
        in_specs=[pl.BlockSpec(memory_space=pl.ANY),
                  pl.BlockSpec(memory_space=pl.ANY),
                  pl.BlockSpec((tm, d), row),
                  pl.BlockSpec((tm, TOP_K), row),
                  pl.BlockSpec((None, 1, d), lambda b, i: (b, 0, 0)),
                  pl.BlockSpec((1, d), lambda b, i: (0, 0))],
        out_specs=pl.BlockSpec((tm, d), row),
        out_shape=jax.ShapeDtypeStruct((t, d), F32),
        scratch_shapes=[pltpu.SMEM((2, tm * TOP_K), jnp.int32),
                        pltpu.VMEM((2, TOP_K, tm * SLAB, LANES), y.dtype),
                        pltpu.SemaphoreType.DMA((2,)),
                        pltpu.SemaphoreType.DMA((2,))],
        compiler_params=_cparams(("arbitrary", "arbitrary")),
        name="combine",
    )(dest_flat, y, x1, tw, gt, g_final)


def _pad_lanes(a, value=0.0):
    return jnp.pad(a, ((0, 0), (0, LANES - a.shape[1])), constant_values=value)


def _routing_tables(idx, rank, counts_f, n_blocks):
    counts = counts_f[0, :N_EXPERTS].astype(jnp.int32)
    padded = (counts + ROW_BLOCK - 1) // ROW_BLOCK * ROW_BLOCK
    pend = jnp.cumsum(padded)
    pstart = pend - padded
    ids = jnp.arange(N_EXPERTS, dtype=jnp.int32)
    seg_start = jnp.sum(jnp.where(idx[..., None] == ids, pstart, 0), axis=-1)
    dest = (seg_start + rank).reshape(-1)
    blk_start = jnp.arange(n_blocks, dtype=jnp.int32) * ROW_BLOCK
    blk_expert = jnp.minimum(jnp.sum((pend[None, :] <= blk_start[:, None]).astype(jnp.int32), axis=1),
                             N_EXPERTS - 1)
    n_used = (pend[-1:] // ROW_BLOCK).astype(jnp.int32)
    blk_valid = jnp.clip((pstart + counts)[blk_expert] - blk_start, 0, ROW_BLOCK).astype(jnp.int32)
    pad_info = jnp.stack([pstart + counts, padded - counts], axis=1).reshape(-1).astype(jnp.int32)
    later = jnp.where((ids[None, :] > ids[:, None]) & (counts[None, :] > 0), ids[None, :], N_EXPERTS)
    nxt = jnp.min(later, axis=1)
    nxt_expert = jnp.where(nxt == N_EXPERTS, -1, nxt).astype(jnp.int32)
    return dest, blk_expert, n_used, nxt_expert, blk_valid, pad_info


def _layer(x, c, ctx, c_ctx, w_ada, b_ada, g_mix, w_in, b_if, conv_w, norm_g, w_out,
           g_ffn, w_router, b_router, w_gu, b_gu, w_down, b_down, g_final):
    bsz, s, d = x.shape
    s_ctx = ctx.shape[1]

    cond = jnp.zeros((8, d), F32).at[:bsz].set(c).at[bsz].set(c_ctx)
    mod = _adaln(cond, w_ada, b_ada[None, :])
    sh_m, sc_m, gt_m, sh_f, sc_f, gt_f = [m[:, None, :] for m in jnp.split(mod, N_MOD, axis=-1)]
    lat = lambda m: m[:bsz]
    ctxm = lambda m: jnp.broadcast_to(m[bsz:bsz + 1], (bsz, 1, d))

    g0 = 2 * QK_COLS + 2 * MLSTM_WIDTH
    w_a = w_in[:, :g0].astype(BF16)
    w_b = w_in[:, g0 + N_GATE_COLS:].astype(BF16)
    w_gate = _pad_lanes(w_in[:, g0:g0 + N_GATE_COLS]).astype(BF16)
    b_gate = _pad_lanes(b_if[None, :])
    g_mix2 = g_mix[None, :]

    proj_c, gpre_c = _inproj(ctx, g_mix2, ctxm(sh_m), ctxm(sc_m), w_a, w_b, w_gate, min(s_ctx, 512))
    gcol_c, grow_c = _gates(gpre_c, b_gate, 512)
    zeros_state = (jnp.zeros((bsz, 2 * N_HEADS, DK, DVX), F32),
                   jnp.zeros((bsz, 2 * N_HEADS, 1, LANES), F32))
    _, _, c0, m0 = _mlstm(proj_c, gcol_c, grow_c, bsz, s_ctx, *zeros_state)

    proj, gpre = _inproj(x, g_mix2, lat(sh_m), lat(sc_m), w_a, w_b, w_gate, 512)
    gcol, grow = _gates(gpre, b_gate, 512)
    hf, hb, _, _ = _mlstm(proj, gcol, grow, bsz, s, c0, m0)
    x1, xn2, idx, tw = _mixout(
        proj, hf, hb, x.reshape(bsz * s, d), conv_w, norm_g[None, :], w_out.astype(BF16), lat(gt_m),
        g_ffn[None, :], lat(sh_f), lat(sc_f), _pad_lanes(w_router).astype(BF16),
        _pad_lanes(b_router[None, :], NEG_BIG), bsz, s, 512)

    t = bsz * s
    n_blocks = -(-(t * TOP_K) // ROW_BLOCK) + N_EXPERTS
    rank, counts = _rank(idx, 512)
    dest, blk_expert, n_used, nxt_expert, blk_valid, pad_info = _routing_tables(idx, rank, counts, n_blocks)
    xs = _dispatch(pad_info, dest, xn2, n_blocks * ROW_BLOCK, 4096)
    act = _expert_gu(blk_expert, n_used, nxt_expert, blk_valid, xs, w_gu, b_gu[:, None, :], 1024)
    y = _expert_down(blk_expert, n_used, nxt_expert, blk_valid, act, w_down, b_down[:, None, :])
    out = _combine(dest, y, x1, tw, lat(gt_f), g_final[None, :], bsz, s, 256)
    return out.reshape(bsz, s, d)


def kernel(x, c, ctx, c_ctx, w_ada, b_ada, g_mix, w_in, b_if, conv_w, mlstm_norm_g, w_out,
           g_ffn, w_router, b_router, w_gu, b_gu, w_down, b_down, g_final):
    return _layer(x, c, ctx, c_ctx, w_ada[0], b_ada[0], g_mix[0], w_in[0], b_if[0], conv_w[0],
                  mlstm_norm_g[0], w_out[0], g_ffn[0], w_router[0], b_router[0], w_gu[0], b_gu[0],
                  w_down[0], b_down[0], g_final)
```
